```python
import math
import jax, jax.numpy as jnp
from jax import lax
import numpy as np

D_MODEL = 1024
BATCH = 4
SEQ = 4096
DEPTH = 2

D_MIX = D_MODEL
RET_WIDTH = D_MIX // 4
SSM_WIDTH = D_MIX // 4
ATT_WIDTH = D_MIX // 2
RET_DK = 64
RET_DV = 64
RET_HEADS = RET_WIDTH // RET_DV
RET_CHUNK = 128
SSM_CPG = 16
SSM_GROUPS = SSM_WIDTH // SSM_CPG
SSM_STATE = 64
ATT_HEAD_DIM = 64
ATT_HEADS = ATT_WIDTH // ATT_HEAD_DIM
ATT_KV_HEADS = 2
ATT_GQ = ATT_HEADS // ATT_KV_HEADS
KV_WIDTH = ATT_KV_HEADS * ATT_HEAD_DIM
ATT_WINDOW = 128
ATT_BLOCK = 128
D_IN = 4 * RET_WIDTH + SSM_WIDTH + ATT_WIDTH + 2 * KV_WIDTH
N_EXPERTS = 16
EXPERT_FF = 2 * D_MODEL
EC_FACTOR = 2
DEEPNORM_ALPHA = (2.0 * DEPTH) ** 0.25
DEEPNORM_BETA = (8.0 * DEPTH) ** -0.25
LN_EPS = 1e-5
NEG_INF = -1e30

kernel_name = "hybrid_retention_s5_swa_ec_encoder"


def _split_points():
    sizes = (RET_WIDTH, RET_WIDTH, RET_WIDTH, RET_WIDTH, SSM_WIDTH, ATT_WIDTH, KV_WIDTH, KV_WIDTH)
    points, acc = [], 0
    for s in sizes[:-1]:
        acc += s
        points.append(acc)
    return points


def layer_norm(x, g, b):
    xf = x.astype(jnp.float32)
    mu = jnp.mean(xf, axis=-1, keepdims=True)
    var = jnp.mean(jnp.square(xf - mu), axis=-1, keepdims=True)
    y = (xf - mu) * lax.rsqrt(var + LN_EPS) * g.astype(jnp.float32) + b.astype(jnp.float32)
    return y.astype(x.dtype)


def retention(q, k, v, g, theta):
    f32 = jnp.float32
    bsz, seq, _ = q.shape
    nc = seq // RET_CHUNK
    log_gamma = jax.nn.log_sigmoid(theta.astype(f32))
    lg_f, lg_b = log_gamma[0], log_gamma[1]
    shp = (bsz, nc, RET_CHUNK, RET_HEADS, RET_DK)
    qc = q.astype(f32).reshape(shp)
    kc = k.astype(f32).reshape(shp) * RET_DK ** -0.5
    vc = v.astype(f32).reshape(bsz, nc, RET_CHUNK, RET_HEADS, RET_DV)
    pos = jnp.arange(RET_CHUNK, dtype=f32)
    dist = pos[:, None] - pos[None, :]
    adist = jnp.abs(dist)
    dmat = jnp.where(dist >= 0, jnp.exp(lg_f[:, None, None] * adist),
                     jnp.exp(lg_b[:, None, None] * adist))
    scores = jnp.einsum('bcihd,bcjhd->bchij', qc, kc) * dmat
    inner = jnp.einsum('bchij,bcjhe->bcihe', scores, vc)
    w_f = jnp.exp(lg_f[None, :] * (RET_CHUNK - 1 - pos)[:, None])
    w_b = jnp.exp(lg_b[None, :] * pos[:, None])
    kv_f = jnp.einsum('bcjhd,bcjhe->cbhde', kc * w_f[:, :, None], vc)
    kv_b = jnp.einsum('bcjhd,bcjhe->cbhde', kc * w_b[:, :, None], vc)
    dec_f = jnp.exp(lg_f * RET_CHUNK)[None, :, None, None]
    dec_b = jnp.exp(lg_b * RET_CHUNK)[None, :, None, None]
    zero = jnp.zeros((bsz, RET_HEADS, RET_DK, RET_DV), f32)
    _, state_f = lax.scan(lambda s, kv: (dec_f * s + kv, s), zero, kv_f)
    _, state_b = lax.scan(lambda s, kv: (dec_b * s + kv, s), zero, kv_b, reverse=True)
    q_f = qc * jnp.exp(lg_f[None, :] * (pos + 1.0)[:, None])[:, :, None]
    q_b = qc * jnp.exp(lg_b[None, :] * (RET_CHUNK - pos)[:, None])[:, :, None]
    out = (inner + jnp.einsum('bcihd,cbhde->bcihe', q_f, state_f)
           + jnp.einsum('bcihd,cbhde->bcihe', q_b, state_b))
    mu = jnp.mean(out, axis=-1, keepdims=True)
    var = jnp.mean(jnp.square(out - mu), axis=-1, keepdims=True)
    out = ((out - mu) * lax.rsqrt(var + LN_EPS)).reshape(bsz, seq, RET_WIDTH)
    return (jax.nn.silu(g.astype(f32)) * out).astype(q.dtype)


def _ssm_combine(e1, e2):
    a1, b1 = e1
    a2, b2 = e2
    return a2 * a1, a2 * b1 + b2


def s5_mixer(u, lam_re, lam_im, log_step, b_re, b_im, c_re, c_im, d, w_glu, b_glu):
    f32 = jnp.float32
    bsz, seq, _ = u.shape
    lam = lax.complex(lam_re.astype(f32), lam_im.astype(f32))
    step = jnp.exp(log_step.astype(f32))[..., None]
    lam_bar = jnp.exp(lam * step)
    b = lax.complex(b_re.astype(f32), b_im.astype(f32))
    b_bar = ((lam_bar - 1.0) / lam)[..., None] * b[None]
    uf = u.astype(f32)
    ug = uf.reshape(bsz, seq, SSM_GROUPS, SSM_CPG).astype(jnp.complex64)
    bu = jnp.einsum('blgc,rgpc->rblgp', ug, b_bar)
    bu = jnp.stack([bu[0], bu[1][:, ::-1]])
    a = jnp.broadcast_to(lam_bar[:, None, None], bu.shape)
    _, states = lax.associative_scan(_ssm_combine, (a, bu), axis=2)
    states = jnp.stack([states[0], states[1][:, ::-1]])
    c = lax.complex(c_re.astype(f32), c_im.astype(f32))
    y = jnp.einsum('rgcp,rblgp->blgc', c, states).real.reshape(bsz, seq, SSM_WIDTH)
    y = jax.nn.gelu(y + d.astype(f32) * uf)
    gate = jax.nn.sigmoid(y @ w_glu.astype(f32) + b_glu.astype(f32))
    return (y * gate).astype(u.dtype)


def window_sink_attention(q, k, v, sink):
    f32 = jnp.float32
    bsz, seq, _ = q.shape
    nc = seq // ATT_BLOCK
    qb = q.reshape(bsz, nc, ATT_BLOCK, ATT_KV_HEADS, ATT_GQ, ATT_HEAD_DIM) * ATT_HEAD_DIM ** -0.5
    pad = ((0, 0), (1, 1), (0, 0), (0, 0), (0, 0))
    kp = jnp.pad(k.reshape(bsz, nc, ATT_BLOCK, ATT_KV_HEADS, ATT_HEAD_DIM), pad)
    vp = jnp.pad(v.reshape(bsz, nc, ATT_BLOCK, ATT_KV_HEADS, ATT_HEAD_DIM), pad)
    kb = jnp.concatenate([kp[:, :-2], kp[:, 1:-1], kp[:, 2:]], axis=2)
    vb = jnp.concatenate([vp[:, :-2], vp[:, 1:-1], vp[:, 2:]], axis=2)
    scores = jnp.einsum('bcqkgd,bcskd->bckgqs', qb, kb).astype(f32)
    span = jnp.arange(3 * ATT_BLOCK)
    rel = span[None, :] - ATT_BLOCK - jnp.arange(ATT_BLOCK)[:, None]
    spos = jnp.arange(nc)[:, None] * ATT_BLOCK - ATT_BLOCK + span[None, :]
    valid = (jnp.abs(rel) <= ATT_WINDOW)[None] & ((spos >= 0) & (spos < seq))[:, None, :]
    slopes = jnp.exp2(-8.0 * jnp.arange(1, ATT_HEADS + 1, dtype=f32) / ATT_HEADS)
    slopes = slopes.reshape(ATT_KV_HEADS, ATT_GQ)[None, None, :, :, None, None]
    scores = scores - slopes * jnp.abs(rel).astype(f32)
    scores = jnp.where(valid[None, :, None, None], scores, NEG_INF)
    sink_f = sink.astype(f32).reshape(ATT_KV_HEADS, ATT_GQ)[None, None, :, :, None, None]
    m = jnp.maximum(jnp.max(scores, axis=-1, keepdims=True), sink_f)
    p = jnp.exp(scores - m)
    denom = jnp.sum(p, axis=-1, keepdims=True) + jnp.exp(sink_f - m)
    probs = (p / denom).astype(v.dtype)
    out = jnp.einsum('bckgqs,bcskd->bcqkgd', probs, vb)
    return out.reshape(bsz, seq, ATT_WIDTH)


def expert_choice_ffn(x, router_w, w_gate, w_up, w_down):
    bsz, seq, _ = x.shape
    cap = EC_FACTOR * seq // N_EXPERTS
    logits = jnp.einsum('bld,de->ble', x, router_w).astype(jnp.float32)
    aff = jax.nn.softmax(logits, axis=-1)
    gates, idx = lax.top_k(jnp.swapaxes(aff, 1, 2), cap)
    bidx = jnp.arange(bsz)[:, None, None]
    xs = x[bidx, idx]
    hdn = jax.nn.silu(jnp.einsum('becd,edf->becf', xs, w_gate)) * jnp.einsum('becd,edf->becf', xs, w_up)
    out = jnp.einsum('becf,efd->becd', hdn, w_down) * gates[..., None]
    return jnp.zeros_like(x).at[bidx, idx].add(out.astype(x.dtype))


def setup_inputs(seed: int = 0) -> dict:
    key = jax.random.key(seed)
    ks = jax.random.split(key, 26)
    f32 = jnp.float32

    def nrm(k, shape, scale):
        return scale * jax.random.normal(k, shape, f32)

    x = nrm(ks[0], (BATCH, SEQ, D_MODEL), 1.0)
    ln_in_g = 1.0 + nrm(ks[1], (D_MODEL,), 0.02)
    ln_in_b = nrm(ks[2], (D_MODEL,), 0.02)
    col_scale = jnp.concatenate([
        jnp.ones((2 * RET_WIDTH,), f32), jnp.full((RET_WIDTH,), DEEPNORM_BETA, f32),
        jnp.ones((RET_WIDTH + SSM_WIDTH + ATT_WIDTH + KV_WIDTH,), f32),
        jnp.full((KV_WIDTH,), DEEPNORM_BETA, f32)])
    w_in = nrm(ks[3], (DEPTH, D_MODEL, D_IN), D_MODEL ** -0.5) * col_scale
    p_decay = 1.0 - jnp.exp2(-5.0 - jnp.arange(RET_HEADS, dtype=f32))
    ret_theta = jnp.log(p_decay / (1.0 - p_decay))[None, None, :] + nrm(ks[4], (DEPTH, 2, RET_HEADS), 0.05)
    ssm_lambda_re = -0.5 + nrm(ks[5], (DEPTH, 2, SSM_GROUPS, SSM_STATE), 0.01)
    ssm_lambda_im = jnp.pi * jnp.arange(SSM_STATE, dtype=f32) + nrm(ks[6], (DEPTH, 2, SSM_GROUPS, SSM_STATE), 0.01)
    ssm_log_step = jax.random.uniform(ks[7], (DEPTH, 2, SSM_GROUPS), f32, math.log(1e-3), math.log(1e-1))
    ssm_b_re = nrm(ks[8], (DEPTH, SSM_GROUPS, SSM_STATE, SSM_CPG), (2.0 * SSM_CPG) ** -0.5)
    ssm_b_im = nrm(ks[9], (DEPTH, SSM_GROUPS, SSM_STATE, SSM_CPG), (2.0 * SSM_CPG) ** -0.5)
    ssm_c_re = nrm(ks[10], (DEPTH, 2, SSM_GROUPS, SSM_CPG, SSM_STATE), (4.0 * SSM_STATE) ** -0.5)
    ssm_c_im = nrm(ks[11], (DEPTH, 2, SSM_GROUPS, SSM_CPG, SSM_STATE), (4.0 * SSM_STATE) ** -0.5)
    ssm_d = nrm(ks[12], (DEPTH, SSM_WIDTH), 1.0)
    ssm_w_glu = nrm(ks[13], (DEPTH, SSM_WIDTH, SSM_WIDTH), SSM_WIDTH ** -0.5)
    ssm_b_glu = nrm(ks[14], (DEPTH, SSM_WIDTH), 0.02)
    attn_sink = nrm(ks[15], (DEPTH, ATT_HEADS), 1.0)
    w_out = nrm(ks[16], (DEPTH, D_MIX, D_MODEL), D_MIX ** -0.5 * DEEPNORM_BETA)
    ln1_g = 1.0 + nrm(ks[17], (DEPTH, D_MODEL), 0.02)
    ln1_b = nrm(ks[18], (DEPTH, D_MODEL), 0.02)
    router_w = nrm(ks[19], (DEPTH, D_MODEL, N_EXPERTS), D_MODEL ** -0.5)
    exp_w_gate = nrm(ks[20], (DEPTH, N_EXPERTS, D_MODEL, EXPERT_FF), D_MODEL ** -0.5)
    exp_w_up = nrm(ks[21], (DEPTH, N_EXPERTS, D_MODEL, EXPERT_FF), D_MODEL ** -0.5 * DEEPNORM_BETA)
    exp_w_down = nrm(ks[22], (DEPTH, N_EXPERTS, EXPERT_FF, D_MODEL), EXPERT_FF ** -0.5 * DEEPNORM_BETA)
    ln2_g = 1.0 + nrm(ks[23], (DEPTH, D_MODEL), 0.02)
    ln2_b = nrm(ks[24], (DEPTH, D_MODEL), 0.02)
    return {"x": x, "ln_in_g": ln_in_g, "ln_in_b": ln_in_b, "w_in": w_in, "ret_theta": ret_theta,
            "ssm_lambda_re": ssm_lambda_re, "ssm_lambda_im": ssm_lambda_im, "ssm_log_step": ssm_log_step,
            "ssm_b_re": ssm_b_re, "ssm_b_im": ssm_b_im, "ssm_c_re": ssm_c_re, "ssm_c_im": ssm_c_im,
            "ssm_d": ssm_d, "ssm_w_glu": ssm_w_glu, "ssm_b_glu": ssm_b_glu, "attn_sink": attn_sink,
            "w_out": w_out, "ln1_g": ln1_g, "ln1_b": ln1_b, "router_w": router_w,
            "exp_w_gate": exp_w_gate, "exp_w_up": exp_w_up, "exp_w_down": exp_w_down,
            "ln2_g": ln2_g, "ln2_b": ln2_b}


def reference(x, ln_in_g, ln_in_b, w_in, ret_theta, ssm_lambda_re, ssm_lambda_im, ssm_log_step,
              ssm_b_re, ssm_b_im, ssm_c_re, ssm_c_im, ssm_d, ssm_w_glu, ssm_b_glu, attn_sink,
              w_out, ln1_g, ln1_b, router_w, exp_w_gate, exp_w_up, exp_w_down, ln2_g, ln2_b):
    h = layer_norm(x, ln_in_g, ln_in_b)
    for l in range(DEPTH):
        proj = jnp.einsum('bld,de->ble', h, w_in[l])
        rq, rk, rv, rg, su, aq, ak, av = jnp.split(proj, _split_points(), axis=-1)
        y_ret = retention(rq, rk, rv, rg, ret_theta[l])
        y_ssm = s5_mixer(su, ssm_lambda_re[l], ssm_lambda_im[l], ssm_log_step[l], ssm_b_re[l], ssm_b_im[l],
                         ssm_c_re[l], ssm_c_im[l], ssm_d[l], ssm_w_glu[l], ssm_b_glu[l])
        y_att = window_sink_attention(aq, ak, av, attn_sink[l])
        mix = jnp.einsum('ble,ed->bld', jnp.concatenate([y_ret, y_ssm, y_att], axis=-1), w_out[l])
        h = layer_norm(DEEPNORM_ALPHA * h + mix, ln1_g[l], ln1_b[l])
        ffn = expert_choice_ffn(h, router_w[l], exp_w_gate[l], exp_w_up[l], exp_w_down[l])
        h = layer_norm(DEEPNORM_ALPHA * h + ffn, ln2_g[l], ln2_b[l])
    return h
```

```python
import functools
import math

import jax
import jax.numpy as jnp
from jax import lax
from jax.experimental import pallas as pl
from jax.experimental.pallas import tpu as pltpu

F32 = jnp.float32
BF16 = jnp.bfloat16

RET_HEADS = 4
RET_DK = 64
RET_CHUNK = 128
SSM_CPG = 16
SSM_GROUPS = 16
SSM_STATE = 64
ATT_HEADS = 8
ATT_KV_HEADS = 2
ATT_GQ = ATT_HEADS // ATT_KV_HEADS
ATT_HEAD_DIM = 64
ATT_BLOCK = 128
N_EXPERTS = 16
EC_FACTOR = 2
DEPTH = 2
DEEPNORM_ALPHA = (2.0 * DEPTH) ** 0.25
LN_EPS = 1e-5
NEG_INF = -1e30

COL_RQ, COL_RK, COL_RV, COL_RG, COL_SU, COL_AQ, COL_AK, COL_AV = 0, 256, 512, 768, 1024, 1280, 1792, 1920
D_IN = 2048

SSM_TC = 64
SLOT_ALIGN = 16
WIN = ATT_BLOCK + SLOT_ALIGN
VMEM_LIMIT = 56 * 1024 * 1024


def _cparams(sem):
    return pltpu.CompilerParams(dimension_semantics=sem, vmem_limit_bytes=VMEM_LIMIT)


def _layer_norm(x, g, b):
    mu = jnp.mean(x, axis=-1, keepdims=True)
    xc = x - mu
    var = jnp.mean(xc * xc, axis=-1, keepdims=True)
    return xc * lax.rsqrt(var + LN_EPS) * g + b


def _dot(a, b):
    return jnp.dot(a, b, preferred_element_type=F32)


def _dot_nt(a, b):
    return lax.dot_general(a, b, (((1,), (1,)), ((), ())), preferred_element_type=F32)


def _dot_tn(a, b):
    return lax.dot_general(a, b, (((0,), (0,)), ((), ())), preferred_element_type=F32)


def _inproj_ln_kernel(x_ref, g_ref, b_ref, w_ref, h_ref, proj_ref):
    h = _layer_norm(x_ref[...], g_ref[...], b_ref[...])
    h_ref[...] = h
    proj_ref[...] = _dot(h.astype(BF16), w_ref[...])


def _inproj_kernel(h_ref, w_ref, proj_ref):
    proj_ref[...] = _dot(h_ref[...].astype(BF16), w_ref[...])


def _inproj(x2, w_bf, ln=None, tm=512):
    t, d = x2.shape
    n = w_bf.shape[1]
    row = pl.BlockSpec((tm, d), lambda i: (i, 0))
    wspec = pl.BlockSpec((d, n), lambda i: (0, 0))
    pspec = pl.BlockSpec((tm, n), lambda i: (i, 0))
    if ln is None:
        return pl.pallas_call(
            _inproj_kernel, grid=(t // tm,), in_specs=[row, wspec], out_specs=pspec,
            out_shape=jax.ShapeDtypeStruct((t, n), F32), compiler_params=_cparams(("parallel",)),
        )(x2, w_bf)
    g, b = ln
    vec = pl.BlockSpec((1, d), lambda i: (0, 0))
    return pl.pallas_call(
        _inproj_ln_kernel, grid=(t // tm,), in_specs=[row, vec, vec, wspec], out_specs=[row, pspec],
        out_shape=[jax.ShapeDtypeStruct((t, d), F32), jax.ShapeDtypeStruct((t, n), F32)],
        compiler_params=_cparams(("parallel",)),
    )(x2, g.reshape(1, d), b.reshape(1, d), w_bf)


def _ssm_tables(lam_re, lam_im, log_step, b_re, b_im, c_re, c_im):
    tc, p, cpg, ng = SSM_TC, SSM_STATE, SSM_CPG, SSM_GROUPS
    hp = lax.Precision.HIGHEST
    lam = lax.complex(lam_re.astype(F32), lam_im.astype(F32))
    step = jnp.exp(log_step.astype(F32))[..., None]
    lam_bar = jnp.exp(lam * step)
    b = lax.complex(b_re.astype(F32), b_im.astype(F32))
    b_bar = ((lam_bar - 1.0) / lam)[..., None] * b[None]
    c = lax.complex(c_re.astype(F32), c_im.astype(F32))
    ones = jnp.ones((1,) + lam_bar.shape, lam_bar.dtype)
    pw = jnp.concatenate([ones, jnp.cumprod(jnp.broadcast_to(lam_bar[None], (tc,) + lam_bar.shape), axis=0)], 0)
    ca = c[None] * pw[:, :, :, None, :]
    k = jnp.einsum('drgcp,rgpi->rgdci', ca[:tc], b_bar, precision=hp).real
    kf, kb = k[0], k[1]
    lagtab = jnp.concatenate([kb[:, :0:-1], (kf[:, :1] + kb[:, :1]), kf[:, 1:]], axis=1)
    s_idx = jnp.arange(tc)[:, None]
    t_idx = jnp.arange(tc)[None, :]
    m = lagtab[:, t_idx - s_idx + tc - 1]
    m = jnp.transpose(m, (0, 1, 4, 2, 3)).reshape(ng, tc * cpg, tc * cpg)
    ab_f = pw[tc - 1::-1, 0][:, :, :, None] * b_bar[0][None]
    ab_b = pw[:tc, 1][:, :, :, None] * b_bar[1][None]

    def smap(ab):
        return jnp.transpose(ab, (1, 0, 3, 2)).reshape(ng, tc * cpg, p)
    w1 = jnp.concatenate([m, smap(ab_f.real), smap(ab_f.imag), smap(ab_b.real), smap(ab_b.imag)], axis=-1)
    ca_f = ca[1:tc + 1, 0]
    ca_b = ca[tc:0:-1, 1]

    def gmap(x):
        return jnp.transpose(x, (1, 3, 0, 2)).reshape(ng, p, tc * cpg)
    gmat = jnp.concatenate([gmap(ca_f.real), gmap(-ca_f.imag), gmap(ca_b.real), gmap(-ca_b.imag)], axis=1)
    dtc = pw[tc]
    rows = []
    for r in range(2):
        rows.append(jnp.concatenate([dtc[r].real, dtc[r].real], axis=-1))
        rows.append(jnp.concatenate([-dtc[r].imag, dtc[r].imag], axis=-1))
    dec = jnp.stack(rows + [jnp.zeros_like(rows[0])] * 4, axis=1)
    return w1.astype(BF16), gmat.astype(BF16), dec


def _ssm_kernel(u_ref, w1_ref, g_ref, dec_ref, y_ref, s_scr, x_scr, *, nb, nchunks):
    n_out = y_ref.shape[-1]
    p2 = 2 * SSM_STATE
    z = _dot(u_ref[0], w1_ref[0])
    s_scr[...] = z[:, n_out:]
    dec = dec_ref[0]
    for r in range(2):
        dr = dec[2 * r:2 * r + 1]
        di = dec[2 * r + 1:2 * r + 2]
        lo, hi = r * p2, (r + 1) * p2
        x = jnp.zeros((nb, p2), F32)
        order = range(nchunks) if r == 0 else range(nchunks - 1, -1, -1)
        for c in order:
            rows = slice(c * nb, (c + 1) * nb)
            x_scr[rows, lo:hi] = x
            x = dr * x + di * pltpu.roll(x, SSM_STATE, 1) + s_scr[rows, lo:hi]
    y_ref[0] = z[:, :n_out] + _dot(x_scr[...].astype(BF16), g_ref[0])


def _ssm_conv(u, tables, bsz, seq):
    w1, gmat, dec = tables
    tc, cpg, ng = SSM_TC, SSM_CPG, SSM_GROUPS
    nchunks = seq // tc
    r = nchunks * bsz
    n_out = tc * cpg
    ug = u.astype(BF16).reshape(bsz, nchunks, tc, ng, cpg)
    ug = jnp.transpose(ug, (3, 1, 0, 2, 4)).reshape(ng, r, n_out)
    yg = pl.pallas_call(
        functools.partial(_ssm_kernel, nb=bsz, nchunks=nchunks),
        grid=(ng,),
        in_specs=[pl.BlockSpec((1, r, n_out), lambda g: (g, 0, 0)),
                  pl.BlockSpec((1,) + w1.shape[1:], lambda g: (g, 0, 0)),
                  pl.BlockSpec((1,) + gmat.shape[1:], lambda g: (g, 0, 0)),
                  pl.BlockSpec((1,) + dec.shape[1:], lambda g: (g, 0, 0))],
        out_specs=pl.BlockSpec((1, r, n_out), lambda g: (g, 0, 0)),
        out_shape=jax.ShapeDtypeStruct((ng, r, n_out), F32),
        scratch_shapes=[pltpu.VMEM((r, 4 * SSM_STATE), F32), pltpu.VMEM((r, 4 * SSM_STATE), F32)],
        compiler_params=_cparams(("parallel",)),
    )(ug, w1, gmat, dec)
    y = yg.reshape(ng, nchunks, bsz, tc, cpg)
    return jnp.transpose(y, (2, 1, 3, 0, 4)).reshape(bsz * seq, ng * cpg)


def _retstate_kernel(lg_ref, kf_ref, vf_ref, kb_ref, vb_ref, sf_ref, sb_ref, accf, accb):
    c = pl.program_id(1)

    @pl.when(c == 0)
    def _():
        accf[...] = jnp.zeros_like(accf)
        accb[...] = jnp.zeros_like(accb)

    sf_ref[0, 0] = accf[...]
    sb_ref[0, 0] = accb[...]
    ch = RET_CHUNK
    pos = lax.broadcasted_iota(jnp.int32, (ch, 1), 0).astype(F32)
    one = jnp.ones((1, RET_DK), F32)
    for h in range(RET_HEADS):
        cols = slice(h * RET_DK, (h + 1) * RET_DK)
        lgf = lg_ref[0, h]
        lgb = lg_ref[1, h]
        kf = (kf_ref[:, cols] * RET_DK ** -0.5) * jnp.exp(lgf * (ch - 1.0 - pos))
        kb = (kb_ref[:, cols] * RET_DK ** -0.5) * jnp.exp(lgb * pos)
        kvf = _dot_tn(kf.astype(BF16), vf_ref[:, cols].astype(BF16))
        kvb = _dot_tn(kb.astype(BF16), vb_ref[:, cols].astype(BF16))
        accf[cols, :] = jnp.exp(one * (lgf * ch)) * accf[cols, :] + kvf
        accb[cols, :] = jnp.exp(one * (lgb * ch)) * accb[cols, :] + kvb


def _ret_states(proj, lg, bsz, seq):
    nc = seq // RET_CHUNK
    w = RET_HEADS * RET_DK

    def blk(col, rev):
        if rev:
            return pl.BlockSpec((RET_CHUNK, w), lambda b, c: (b * nc + nc - 1 - c, col))
        return pl.BlockSpec((RET_CHUNK, w), lambda b, c: (b * nc + c, col))
    st = jax.ShapeDtypeStruct((bsz, nc, w, RET_DK), F32)
    return pl.pallas_call(
        _retstate_kernel, grid=(bsz, nc),
        in_specs=[pl.BlockSpec(memory_space=pltpu.SMEM),
                  blk(COL_RK // w, False), blk(COL_RV // w, False), blk(COL_RK // w, True), blk(COL_RV // w, True)],
        out_specs=[pl.BlockSpec((1, 1, w, RET_DK), lambda b, c: (b, c, 0, 0)),
                   pl.BlockSpec((1, 1, w, RET_DK), lambda b, c: (b, nc - 1 - c, 0, 0))],
        out_shape=[st, st],
        scratch_shapes=[pltpu.VMEM((w, RET_DK), F32), pltpu.VMEM((w, RET_DK), F32)],
        compiler_params=_cparams(("parallel", "arbitrary")),
    )(lg, proj, proj, proj, proj)


def _gelu_tanh(x):
    return 0.5 * x * (1.0 + jnp.tanh(math.sqrt(2.0 / math.pi) * (x + 0.044715 * (x * x * x))))


def _mixer_kernel(lg_ref, sink_ref, proj_ref, kvp_ref, kvn_ref, sf_ref, sb_ref, yssm_ref, h_ref,
                  wout_ref, wglu_ref, d_ref, bglu_ref, lng_ref, lnb_ref, rw_ref, rwt_ref,
                  h1_ref, afft_ref, aff_ref, mix_scr, *, seq):
    c = pl.program_id(1)
    ch = RET_CHUNK
    hp = lax.Precision.HIGHEST

    pos = lax.broadcasted_iota(jnp.int32, (ch, 1), 0).astype(F32)
    dist = lax.broadcasted_iota(jnp.int32, (ch, ch), 0) - lax.broadcasted_iota(jnp.int32, (ch, ch), 1)
    adist = jnp.abs(dist).astype(F32)
    for h in range(RET_HEADS):
        lgf = lg_ref[0, h]
        lgb = lg_ref[1, h]
        q = proj_ref[:, COL_RQ + h * RET_DK:COL_RQ + (h + 1) * RET_DK]
        k = proj_ref[:, COL_RK + h * RET_DK:COL_RK + (h + 1) * RET_DK] * RET_DK ** -0.5
        v = proj_ref[:, COL_RV + h * RET_DK:COL_RV + (h + 1) * RET_DK].astype(BF16)
        g = proj_ref[:, COL_RG + h * RET_DK:COL_RG + (h + 1) * RET_DK]
        dmat = jnp.exp(jnp.where(dist >= 0, lgf, lgb) * adist)
        scores = _dot_nt(q.astype(BF16), k.astype(BF16)) * dmat
        q_f = q * jnp.exp(lgf * (pos + 1.0))
        q_b = q * jnp.exp(lgb * (ch - pos))
        rows = slice(h * RET_DK, (h + 1) * RET_DK)
        o = (_dot(scores.astype(BF16), v)
             + _dot(q_f.astype(BF16), sf_ref[0, 0, rows, :].astype(BF16))
             + _dot(q_b.astype(BF16), sb_ref[0, 0, rows, :].astype(BF16)))
        mu = jnp.mean(o, axis=-1, keepdims=True)
        oc = o - mu
        var = jnp.mean(oc * oc, axis=-1, keepdims=True)
        o = oc * lax.rsqrt(var + LN_EPS)
        mix_scr[:, h * RET_DK:(h + 1) * RET_DK] = (g * jax.nn.sigmoid(g)) * o

    w_ssm = yssm_ref.shape[-1]
    y = yssm_ref[...] + d_ref[...] * proj_ref[:, COL_SU:COL_SU + w_ssm]
    y = _gelu_tanh(y)
    gate = jax.nn.sigmoid(_dot(y.astype(BF16), wglu_ref[...]) + bglu_ref[...])
    mix_scr[:, w_ssm:2 * w_ssm] = y * gate

    blk = ATT_BLOCK
    hd = ATT_HEAD_DIM
    kvw = ATT_KV_HEADS * hd
    s_idx = lax.broadcasted_iota(jnp.int32, (blk, 3 * blk), 1)
    t_idx = lax.broadcasted_iota(jnp.int32, (blk, 3 * blk), 0)
    rel = s_idx - blk - t_idx
    arel = jnp.abs(rel)
    spos = c * blk - blk + s_idx
    valid = (arel <= blk) & (spos >= 0) & (spos < seq)
    arel_f = arel.astype(F32)
    att_base = 2 * w_ssm
    for kvh in range(ATT_KV_HEADS):
        kc = slice(kvh * hd, (kvh + 1) * hd)
        vc = slice(kvw + kvh * hd, kvw + (kvh + 1) * hd)
        kb = jnp.concatenate([kvp_ref[:, kc], proj_ref[:, COL_AK + kvh * hd:COL_AK + (kvh + 1) * hd],
                              kvn_ref[:, kc]], axis=0).astype(BF16)
        vb = jnp.concatenate([kvp_ref[:, vc], proj_ref[:, COL_AV + kvh * hd:COL_AV + (kvh + 1) * hd],
                              kvn_ref[:, vc]], axis=0).astype(BF16)
        for gq in range(ATT_GQ):
            hh = kvh * ATT_GQ + gq
            q = proj_ref[:, COL_AQ + hh * hd:COL_AQ + (hh + 1) * hd] * hd ** -0.5
            sc = _dot_nt(q.astype(BF16), kb) - (2.0 ** -(hh + 1)) * arel_f
            sc = jnp.where(valid, sc, NEG_INF)
            sink = sink_ref[hh]
            m = jnp.maximum(jnp.max(sc, axis=-1, keepdims=True), sink)
            p = jnp.exp(sc - m)
            denom = jnp.sum(p, axis=-1, keepdims=True) + jnp.exp(sink - m)
            probs = (p / denom).astype(BF16)
            mix_scr[:, att_base + hh * hd:att_base + (hh + 1) * hd] = _dot(probs, vb)

    mix = _dot(mix_scr[...].astype(BF16), wout_ref[...])
    h1 = _layer_norm(DEEPNORM_ALPHA * h_ref[...] + mix, lng_ref[...], lnb_ref[...])
    h1_ref[...] = h1

    lt = lax.dot_general(rwt_ref[...], h1, (((1,), (1,)), ((), ())), precision=hp, preferred_element_type=F32)
    lt = lt - jnp.max(lt, axis=0, keepdims=True)
    et = jnp.exp(lt)
    afft_ref[0, 0] = et / jnp.sum(et, axis=0, keepdims=True)
    ln_ = jnp.dot(h1, rw_ref[...], precision=hp, preferred_element_type=F32)
    ln_ = ln_ - jnp.max(ln_, axis=-1, keepdims=True)
    en = jnp.exp(ln_)
    aff_ref[...] = en / jnp.sum(en, axis=-1, keepdims=True)


def _mixer(proj, sf, sb, yssm, h2d, lg, sink, wout_bf, wglu_bf, d, bglu, lng, lnb, rw, bsz, seq):
    nc = seq // ATT_BLOCK
    t, dm = h2d.shape
    ne = rw.shape[1]
    kvw = 2 * ATT_KV_HEADS * ATT_HEAD_DIM
    w_ssm = yssm.shape[1]
    sw = RET_HEADS * RET_DK
    kvcol = COL_AK // kvw

    def full(shape):
        return pl.BlockSpec(shape, lambda b, c: (0,) * len(shape))
    smem = pl.BlockSpec(memory_space=pltpu.SMEM)
    in_specs = [
        smem, smem,
        pl.BlockSpec((ATT_BLOCK, D_IN), lambda b, c: (b * nc + c, 0)),
        pl.BlockSpec((ATT_BLOCK, kvw), lambda b, c: (b * nc + jnp.maximum(c - 1, 0), kvcol)),
        pl.BlockSpec((ATT_BLOCK, kvw), lambda b, c: (b * nc + jnp.minimum(c + 1, nc - 1), kvcol)),
        pl.BlockSpec((1, 1, sw, RET_DK), lambda b, c: (b, c, 0, 0)),
        pl.BlockSpec((1, 1, sw, RET_DK), lambda b, c: (b, c, 0, 0)),
        pl.BlockSpec((ATT_BLOCK, w_ssm), lambda b, c: (b * nc + c, 0)),
        pl.BlockSpec((ATT_BLOCK, dm), lambda b, c: (b * nc + c, 0)),
        full(wout_bf.shape), full(wglu_bf.shape), full((1, w_ssm)), full((1, w_ssm)),
        full((1, dm)), full((1, dm)), full((dm, ne)), full((ne, dm)),
    ]
    out_specs = [
        pl.BlockSpec((ATT_BLOCK, dm), lambda b, c: (b * nc + c, 0)),
        pl.BlockSpec((1, 1, ne, ATT_BLOCK), lambda b, c: (b, c, 0, 0)),
        pl.BlockSpec((ATT_BLOCK, ne), lambda b, c: (b * nc + c, 0)),
    ]
    out_shape = [jax.ShapeDtypeStruct((t, dm), F32),
                 jax.ShapeDtypeStruct((bsz, nc, ne, ATT_BLOCK), F32),
                 jax.ShapeDtypeStruct((t, ne), F32)]
    return pl.pallas_call(
        functools.partial(_mixer_kernel, seq=seq), grid=(bsz, nc),
        in_specs=in_specs, out_specs=out_specs, out_shape=out_shape,
        scratch_shapes=[pltpu.VMEM((ATT_BLOCK, dm), F32)],
        compiler_params=_cparams(("parallel", "parallel")),
    )(lg, sink, proj, proj, proj, sf, sb, yssm, h2d, wout_bf, wglu_bf, d.reshape(1, -1), bglu.reshape(1, -1),
      lng.reshape(1, -1), lnb.reshape(1, -1), rw, rw.T)


def _route_kernel(a_ref, slot_ref, start_ref, *, nc, ne, cap):
    blk = ATT_BLOCK
    bits = pltpu.bitcast(a_ref[0], jnp.int32)

    def chunk(x, c):
        return x[c * ne:(c + 1) * ne]

    def count(pred):
        x = pred.astype(jnp.int32)
        tot = chunk(x, 0)
        for c in range(1, nc):
            tot = tot + chunk(x, c)
        return jnp.sum(tot, axis=1, keepdims=True)

    def tile_e(v):
        return jnp.concatenate([v] * nc, axis=0)

    tau = jnp.zeros((ne, 1), jnp.int32)
    for bit in range(30, -1, -1):
        cand = tau | (1 << bit)
        tau = jnp.where(count(bits >= tile_e(cand)) >= cap, cand, tau)
    tau_t = tile_e(tau)
    gt = bits > tau_t
    eq = bits == tau_t
    need = cap - count(gt)

    tri = (lax.broadcasted_iota(jnp.int32, (blk, blk), 0) <= lax.broadcasted_iota(jnp.int32, (blk, blk), 1)).astype(BF16)

    def ranks(mask):
        mf = mask.astype(F32)
        incl = _dot(mf.astype(BF16), tri)
        run = jnp.zeros((ne, 1), F32)
        offs = []
        for c in range(nc):
            offs.append(run)
            run = run + chunk(incl, c)[:, blk - 1:blk]
        return incl - mf, jnp.concatenate(offs, axis=0)

    eq_local, eq_off = ranks(eq)
    mask = gt | (eq & ((eq_local + eq_off) < tile_e(need).astype(F32)))
    local, off = ranks(mask)
    off_i = off.astype(jnp.int32)
    start = (off_i // SLOT_ALIGN) * SLOT_ALIGN
    slot = (off_i - start) + local.astype(jnp.int32)
    slot_ref[0] = jnp.where(mask, slot, -1)
    start_ref[0] = jnp.broadcast_to(start, (nc * ne, blk))


def _route(afft, cap):
    bsz, nc, ne, blk = afft.shape
    a2 = afft.reshape(bsz, nc * ne, blk)
    spec = pl.BlockSpec((1, nc * ne, blk), lambda b: (b, 0, 0))
    slot, start = pl.pallas_call(
        functools.partial(_route_kernel, nc=nc, ne=ne, cap=cap), grid=(bsz,),
        in_specs=[spec], out_specs=[spec, spec],
        out_shape=[jax.ShapeDtypeStruct(a2.shape, jnp.int32)] * 2,
        compiler_params=_cparams(("parallel",)),
    )(a2)
    return slot.reshape(bsz, nc, ne, blk), start[:, :, 0].reshape(bsz * nc * ne)


def _dispatch_kernel(ws_ref, h_ref, slot_ref, xs_ref, *, nc, ne, eg):
    b = pl.program_id(0)
    g = pl.program_id(1)
    c = pl.program_id(2)

    @pl.when(c == 0)
    def _():
        xs_ref[...] = jnp.zeros_like(xs_ref)

    hb = h_ref[...].astype(BF16)
    riota = lax.broadcasted_iota(jnp.int32, (WIN, ATT_BLOCK), 0)
    onehots = [(riota == slot_ref[0, 0, 0, j:j + 1, :]).astype(BF16) for j in range(eg)]
    res = _dot(jnp.concatenate(onehots, axis=0), hb)
    for j in range(eg):
        w = pl.multiple_of(ws_ref[(b * nc + c) * ne + g * eg + j], SLOT_ALIGN)
        cur = xs_ref[0, j, pl.ds(w, WIN), :].astype(F32)
        xs_ref[0, j, pl.ds(w, WIN), :] = (cur + res[j * WIN:(j + 1) * WIN]).astype(BF16)


def _dispatch(h1, slot, wstart, cap, eg=8):
    bsz, nc, ne, blk = slot.shape
    t, dm = h1.shape
    cp = cap + WIN
    return pl.pallas_call(
        functools.partial(_dispatch_kernel, nc=nc, ne=ne, eg=eg),
        grid_spec=pltpu.PrefetchScalarGridSpec(
            num_scalar_prefetch=1, grid=(bsz, ne // eg, nc),
            in_specs=[pl.BlockSpec((blk, dm), lambda b, g, c, ws: (b * nc + c, 0)),
                      pl.BlockSpec((1, 1, 1, eg, blk), lambda b, g, c, ws: (b, c, g, 0, 0))],
            out_specs=pl.BlockSpec((1, eg, cp, dm), lambda b, g, c, ws: (b, g, 0, 0))),
        out_shape=jax.ShapeDtypeStruct((bsz, ne, cp, dm), BF16),
        compiler_params=_cparams(("parallel", "parallel", "arbitrary")),
    )(wstart, h1, slot.reshape(bsz, nc, ne // eg, eg, blk))


def _ffn_kernel(xs_ref, wg_ref, wu_ref, wd_ref, out_ref, acc_ref, *, cap, nf):
    f = pl.program_id(1)
    bsz = xs_ref.shape[0]
    wg = wg_ref[0].astype(BF16)
    wu = wu_ref[0].astype(BF16)
    wd = wd_ref[0].astype(BF16)
    for b in range(bsz):
        x = xs_ref[b, 0]
        hg = _dot(x, wg)
        hu = _dot(x, wu)
        hdn = ((hg * jax.nn.sigmoid(hg)) * hu).astype(BF16)
        contrib = _dot(hdn, wd)
        rows = slice(b * cap, (b + 1) * cap)

        @pl.when(f == 0)
        def _():
            acc_ref[rows, :] = contrib

        @pl.when(f > 0)
        def _():
            acc_ref[rows, :] = acc_ref[rows, :] + contrib

    @pl.when(f == nf - 1)
    def _():
        cp = out_ref.shape[2]
        for b in range(bsz):
            out_ref[b, 0, 0:cap, :] = acc_ref[b * cap:(b + 1) * cap, :].astype(BF16)
            out_ref[b, 0, cap:cp, :] = jnp.zeros((cp - cap, out_ref.shape[3]), BF16)


def _ffn(xs, wg, wu, wd, cap, tf=512):
    bsz, ne, cp, dm = xs.shape
    ff = wg.shape[2]
    nf = ff // tf
    return pl.pallas_call(
        functools.partial(_ffn_kernel, cap=cap, nf=nf), grid=(ne, nf),
        in_specs=[pl.BlockSpec((bsz, 1, cap, dm), lambda e, f: (0, e, 0, 0)),
                  pl.BlockSpec((1, dm, tf), lambda e, f: (e, 0, f)),
                  pl.BlockSpec((1, dm, tf), lambda e, f: (e, 0, f)),
                  pl.BlockSpec((1, tf, dm), lambda e, f: (e, f, 0))],
        out_specs=pl.BlockSpec((bsz, 1, cp, dm), lambda e, f: (0, e, 0, 0)),
        out_shape=jax.ShapeDtypeStruct((bsz, ne, cp, dm), BF16),
        scratch_shapes=[pltpu.VMEM((bsz * cap, dm), F32)],
        compiler_params=_cparams(("parallel", "arbitrary")),
    )(xs, wg, wu, wd)


def _combine_kernel(ws_ref, out_ref, slot_ref, gate_ref, h1_ref, lng_ref, lnb_ref, h2_ref, acc_ref, *, nc, ne, eg, ng):
    b = pl.program_id(0)
    g = pl.program_id(1)
    c = pl.program_id(2)
    blk = ATT_BLOCK
    rows = pl.ds(pl.multiple_of(c * blk, blk), blk)
    riota = lax.broadcasted_iota(jnp.int32, (WIN, blk), 0)
    tot = jnp.zeros((blk, h2_ref.shape[1]), F32)
    for j in range(eg):
        w = pl.multiple_of(ws_ref[(b * nc + c) * ne + g * eg + j], SLOT_ALIGN)
        onehot = (riota == slot_ref[0, 0, 0, j:j + 1, :]).astype(BF16)
        y = _dot_tn(onehot, out_ref[0, j, pl.ds(w, WIN), :])
        tot = tot + gate_ref[0, :, j:j + 1] * y

    @pl.when(g == 0)
    def _():
        acc_ref[rows, :] = tot

    @pl.when(g > 0)
    def _():
        acc_ref[rows, :] = acc_ref[rows, :] + tot

    @pl.when(g == ng - 1)
    def _():
        h2_ref[...] = _layer_norm(DEEPNORM_ALPHA * h1_ref[...] + acc_ref[rows, :], lng_ref[...], lnb_ref[...])


def _combine(out, slot, wstart, aff, h1, lng, lnb, seq, eg=4):
    bsz, ne, cp, dm = out.shape
    nc = seq // ATT_BLOCK
    ng = ne // eg
    blk = ATT_BLOCK
    t = h1.shape[0]
    gates = jnp.transpose(aff.reshape(t, ng, eg), (1, 0, 2))
    vec = pl.BlockSpec((1, dm), lambda b, g, c, ws: (0, 0))
    return pl.pallas_call(
        functools.partial(_combine_kernel, nc=nc, ne=ne, eg=eg, ng=ng),
        grid_spec=pltpu.PrefetchScalarGridSpec(
            num_scalar_prefetch=1, grid=(bsz, ng, nc),
            in_specs=[pl.BlockSpec((1, eg, cp, dm), lambda b, g, c, ws: (b, g, 0, 0)),
                      pl.BlockSpec((1, 1, 1, eg, blk), lambda b, g, c, ws: (b, c, g, 0, 0)),
                      pl.BlockSpec((1, blk, eg), lambda b, g, c, ws: (g, b * nc + c, 0)),
                      pl.BlockSpec((blk, dm), lambda b, g, c, ws: (b * nc + c, 0)),
                      vec, vec],
            out_specs=pl.BlockSpec((blk, dm), lambda b, g, c, ws: (b * nc + jnp.where(g == ng - 1, c, 0), 0)),
            scratch_shapes=[pltpu.VMEM((seq, dm), F32)]),
        out_shape=jax.ShapeDtypeStruct((t, dm), F32),
        compiler_params=_cparams(("parallel", "arbitrary", "arbitrary")),
    )(wstart, out, slot.reshape(bsz, nc, ng, eg, blk), gates, h1, lng.reshape(1, -1), lnb.reshape(1, -1))


def kernel(x, ln_in_g, ln_in_b, w_in, ret_theta, ssm_lambda_re, ssm_lambda_im, ssm_log_step, ssm_b_re, ssm_b_im,
           ssm_c_re, ssm_c_im, ssm_d, ssm_w_glu, ssm_b_glu, attn_sink, w_out, ln1_g, ln1_b, router_w,
           exp_w_gate, exp_w_up, exp_w_down, ln2_g, ln2_b):
    bsz, seq, dm = x.shape
    depth = w_in.shape[0]
    ne = router_w.shape[-1]
    cap = EC_FACTOR * seq // ne
    t = bsz * seq
    h = x.reshape(t, dm)
    for l in range(depth):
        w_in_bf = w_in[l].astype(BF16)
        if l == 0:
            h, proj = _inproj(h, w_in_bf, ln=(ln_in_g, ln_in_b))
        else:
            proj = _inproj(h, w_in_bf)
        lg = jax.nn.log_sigmoid(ret_theta[l].astype(F32))
        tables = _ssm_tables(ssm_lambda_re[l], ssm_lambda_im[l], ssm_log_step[l], ssm_b_re[l], ssm_b_im[l],
                             ssm_c_re[l], ssm_c_im[l])
        yssm = _ssm_conv(proj[:, COL_SU:COL_SU + SSM_GROUPS * SSM_CPG], tables, bsz, seq)
        sf, sb = _ret_states(proj, lg, bsz, seq)
        h1, afft, aff = _mixer(proj, sf, sb, yssm, h, lg, attn_sink[l].astype(F32), w_out[l].astype(BF16),
                               ssm_w_glu[l].astype(BF16), ssm_d[l], ssm_b_glu[l], ln1_g[l], ln1_b[l],
                               router_w[l], bsz, seq)
        slot, wstart = _route(afft, cap)
        xs = _dispatch(h1, slot, wstart, cap)
        out = _ffn(xs, exp_w_gate[l], exp_w_up[l], exp_w_down[l], cap)
        h = _combine(out, slot, wstart, aff, h1, ln2_g[l], ln2_b[l], seq)
    return h.reshape(bsz, seq, dm)
```

```python
import functools
import math

import jax
import jax.numpy as jnp
from jax import lax
from jax.experimental import pallas as pl
from jax.experimental.pallas import tpu as pltpu

F32 = jnp.float32
BF16 = jnp.bfloat16

RET_HEADS = 4
RET_DK = 64
RET_CHUNK = 128
SSM_CPG = 16
SSM_GROUPS = 16
SSM_STATE = 64
SSM_WIDTH = SSM_CPG * SSM_GROUPS
ATT_HEADS = 8
ATT_KV_HEADS = 2
ATT_GQ = ATT_HEADS // ATT_KV_HEADS
ATT_HEAD_DIM = 64
ATT_BLOCK = 128
EC_FACTOR = 2
DEPTH = 2
DEEPNORM_ALPHA = (2.0 * DEPTH) ** 0.25
LN_EPS = 1e-5
NEG_INF = -1e30

COL_RQ, COL_RK, COL_RV, COL_RG, COL_SU, COL_AQ, COL_AK, COL_AV = 0, 256, 512, 768, 1024, 1280, 1792, 1920
D_IN = 2048

CHUNK = 128
SUBLANES = 8
SLOT_ALIGN = 16
WIN = CHUNK + SLOT_ALIGN
VMEM_LIMIT = 56 * 1024 * 1024
ROUTE_REFINE_STEPS = 10


def _cparams(sem):
    return pltpu.CompilerParams(dimension_semantics=sem, vmem_limit_bytes=VMEM_LIMIT)


def _layer_norm(x, g, b):
    mu = jnp.mean(x, axis=-1, keepdims=True)
    xc = x - mu
    var = jnp.mean(xc * xc, axis=-1, keepdims=True)
    return xc * lax.rsqrt(var + LN_EPS) * g + b


def _dot(a, b):
    return jnp.dot(a, b, preferred_element_type=F32)


def _dot_hp(a, b):
    return jnp.dot(a, b, precision=lax.Precision.HIGHEST, preferred_element_type=F32)


def _dot_nt(a, b):
    return lax.dot_general(a, b, (((1,), (1,)), ((), ())), preferred_element_type=F32)


def _dot_tn(a, b):
    return lax.dot_general(a, b, (((0,), (0,)), ((), ())), preferred_element_type=F32)


def _cmul(ar, ai, br, bi):
    return ar * br - ai * bi, ar * bi + ai * br


def _cpow(br, bi, expo, nbits, shape):
    br = jnp.broadcast_to(br, shape)
    bi = jnp.broadcast_to(bi, shape)
    rr = jnp.ones(shape, F32)
    ri = jnp.zeros(shape, F32)
    for j in range(nbits):
        bit = jnp.broadcast_to(((expo >> j) & 1) == 1, shape)
        nr, ni = _cmul(rr, ri, br, bi)
        rr = jnp.where(bit, nr, rr)
        ri = jnp.where(bit, ni, ri)
        if j + 1 < nbits:
            br, bi = _cmul(br, bi, br, bi)
    return rr, ri


def _inproj_kernel(*refs, apply_ln):
    if apply_ln:
        x_ref, g_ref, b_ref, w_ref, wsu_ref, h_ref, proj_ref, ut_ref = refs
    else:
        x_ref, w_ref, wsu_ref, proj_ref, ut_ref = refs
    nb, tl, d = x_ref.shape
    h = x_ref[...].reshape(nb * tl, d)
    if apply_ln:
        h = _layer_norm(h, g_ref[...], b_ref[...])
        h_ref[...] = h.reshape(nb, tl, d)
    hb = h.astype(BF16)
    proj_ref[...] = _dot(hb, w_ref[0]).reshape(nb, tl, -1)
    ut = _dot_nt(wsu_ref[0], hb)
    k = tl // CHUNK
    for j in range(k):
        for b in range(nb):
            ut_ref[:, j * nb + b, :] = ut[:, (b * k + j) * CHUNK:(b * k + j + 1) * CHUNK]


def _inproj(x3, w_bf, wsu_bf, l, ln=None):
    bsz, seq, d = x3.shape
    n = w_bf.shape[2]
    k = SUBLANES // bsz
    tl = k * CHUNK
    nc = seq // CHUNK
    xspec = pl.BlockSpec((bsz, tl, d), lambda i: (0, i, 0))
    wspec = pl.BlockSpec((1, d, n), lambda i: (l, 0, 0))
    wsuspec = pl.BlockSpec((1, SSM_WIDTH, d), lambda i: (l, 0, 0))
    pspec = pl.BlockSpec((bsz, tl, n), lambda i: (0, i, 0))
    utspec = pl.BlockSpec((SSM_WIDTH, k * bsz, CHUNK), lambda i: (0, i, 0))
    pshape = jax.ShapeDtypeStruct((bsz, seq, n), F32)
    utshape = jax.ShapeDtypeStruct((SSM_WIDTH, nc * bsz, CHUNK), F32)
    if ln is None:
        return pl.pallas_call(
            functools.partial(_inproj_kernel, apply_ln=False), grid=(seq // tl,),
            in_specs=[xspec, wspec, wsuspec], out_specs=[pspec, utspec], out_shape=[pshape, utshape],
            compiler_params=_cparams(("parallel",)),
        )(x3, w_bf, wsu_bf)
    g, b = ln
    vec = pl.BlockSpec((1, d), lambda i: (0, 0))
    return pl.pallas_call(
        functools.partial(_inproj_kernel, apply_ln=True), grid=(seq // tl,),
        in_specs=[xspec, vec, vec, wspec, wsuspec], out_specs=[xspec, pspec, utspec],
        out_shape=[jax.ShapeDtypeStruct(x3.shape, F32), pshape, utshape],
        compiler_params=_cparams(("parallel",)),
    )(x3, g.reshape(1, d), b.reshape(1, d), w_bf, wsu_bf)


def _ssm_params(lam_re, lam_im, log_step, b_re, b_im, c_re, c_im):
    lr, li = lam_re.astype(F32), lam_im.astype(F32)
    step = jnp.exp(log_step.astype(F32))[..., None]
    er = jnp.exp(lr * step)
    lbr, lbi = er * jnp.cos(li * step), er * jnp.sin(li * step)
    den = lr * lr + li * li
    fr = ((lbr - 1.0) * lr + lbi * li) / den
    fi = (lbi * lr - (lbr - 1.0) * li) / den
    br = jnp.swapaxes(b_re.astype(F32), -1, -2)[:, None]
    bi = jnp.swapaxes(b_im.astype(F32), -1, -2)[:, None]
    bbr = fr[..., None, :] * br - fi[..., None, :] * bi
    bbi = fr[..., None, :] * bi + fi[..., None, :] * br
    lam4 = jnp.stack([lbr[:, 0], lbi[:, 0], lbr[:, 1], lbi[:, 1]], axis=2)
    lrow = jnp.concatenate([lam4, jnp.zeros_like(lam4)], axis=2)
    lcol = jnp.swapaxes(lam4, -1, -2)
    bt = jnp.stack([bbr[:, 0], bbi[:, 0], bbr[:, 1], bbi[:, 1]], axis=2)
    cr, ci = c_re.astype(F32), c_im.astype(F32)
    c4 = jnp.stack([cr[:, 0], ci[:, 0], cr[:, 1], ci[:, 1]], axis=2)
    return lrow, lcol, bt, c4, jnp.swapaxes(c4, -1, -2)


def _ssm_kernel(ut_ref, lrow_ref, lcol_ref, bt_ref, c_ref, ct_ref, y_ref,
                w_scr, g_scr, vf_scr, vb_scr, acc_scr, x_scr, *, nb, nchunks):
    tc, p, cpg = CHUNK, SSM_STATE, SSM_CPG
    n_out = cpg * tc
    lrow = lrow_ref[0, 0]
    lcol = lcol_ref[0, 0]

    s_col = lax.broadcasted_iota(jnp.int32, (tc, 1), 0)
    afr, afi = _cpow(lrow[0:1], lrow[1:2], tc - 1 - s_col, 7, (tc, p))
    abr, abi = _cpow(lrow[2:3], lrow[3:4], s_col, 7, (tc, p))
    m_row = lax.broadcasted_iota(jnp.int32, (1, tc), 1)
    pfr, pfi = _cpow(lcol[:, 0:1], lcol[:, 1:2], m_row, 7, (p, tc))
    pf1r, pf1i = _cmul(pfr, pfi, lcol[:, 0:1], lcol[:, 1:2])
    pbr, pbi = _cpow(lcol[:, 2:3], lcol[:, 3:4], tc - m_row, 8, (p, tc))

    def rep_co(x):
        return jnp.broadcast_to(x[:, None, :], (cpg, cpg, p)).reshape(cpg * cpg, p)

    def rep_ci(x):
        return jnp.broadcast_to(x[None, :, :], (cpg, cpg, p)).reshape(cpg * cpg, p)

    cbfr, cbfi = _cmul(rep_co(c_ref[0, 0, 0]), rep_co(c_ref[0, 0, 1]), rep_ci(bt_ref[0, 0, 0]), rep_ci(bt_ref[0, 0, 1]))
    cbbr, cbbi = _cmul(rep_co(c_ref[0, 0, 2]), rep_co(c_ref[0, 0, 3]), rep_ci(bt_ref[0, 0, 2]), rep_ci(bt_ref[0, 0, 3]))
    lane = lax.broadcasted_iota(jnp.int32, (cpg * cpg, tc), 1)
    kb0 = jnp.sum(cbbr, axis=1, keepdims=True)
    vf_scr[...] = _dot_hp(cbfr, pfr) - _dot_hp(cbfi, pfi) + jnp.where(lane == 0, kb0, 0.0)
    vb_scr[...] = _dot_hp(cbbr, pbr) - _dot_hp(cbbi, pbi)

    for co in range(cpg):
        cols = slice(co * tc, (co + 1) * tc)
        for r, (ar, ai) in enumerate(((pf1r, pf1i), (pbr, pbi))):
            gr, gi = _cmul(ct_ref[0, 0, 2 * r][:, co:co + 1], ct_ref[0, 0, 2 * r + 1][:, co:co + 1], ar, ai)
            g_scr[2 * r * p:(2 * r + 1) * p, cols] = gr.astype(BF16)
            g_scr[(2 * r + 1) * p:(2 * r + 2) * p, cols] = (-gi).astype(BF16)

    t_ge_s = lax.broadcasted_iota(jnp.int32, (tc, tc), 1) >= lax.broadcasted_iota(jnp.int32, (tc, tc), 0)
    acc_scr[...] = jnp.zeros_like(acc_scr)

    def per_channel(ci, carry):
        for co in range(cpg):
            rf = jnp.broadcast_to(vf_scr[pl.ds(co * cpg + ci, 1), :], (tc, tc))
            rb = jnp.broadcast_to(vb_scr[pl.ds(co * cpg + ci, 1), :], (tc, tc))
            mf = pltpu.roll(rf, 0, 1, stride=1, stride_axis=0)
            mb = pltpu.roll(rb, 0, 1, stride=1, stride_axis=0)
            w_scr[ci, :, co * tc:(co + 1) * tc] = jnp.where(t_ge_s, mf, mb).astype(BF16)
        sfr, sfi = _cmul(afr, afi, bt_ref[0, 0, 0, pl.ds(ci, 1), :], bt_ref[0, 0, 1, pl.ds(ci, 1), :])
        sbr, sbi = _cmul(abr, abi, bt_ref[0, 0, 2, pl.ds(ci, 1), :], bt_ref[0, 0, 3, pl.ds(ci, 1), :])
        for q, v in enumerate((sfr, sfi, sbr, sbi)):
            w_scr[ci, :, n_out + q * p:n_out + (q + 1) * p] = v.astype(BF16)
        acc_scr[...] += _dot(ut_ref[ci].astype(BF16), w_scr[ci])
        return carry

    lax.fori_loop(0, cpg, per_channel, 0)

    dec = []
    for r in range(2):
        dr, di = lrow[2 * r:2 * r + 1], lrow[2 * r + 1:2 * r + 2]
        for _ in range(7):
            dr, di = _cmul(dr, di, dr, di)
        dec.append((jnp.concatenate([dr, dr], axis=1), jnp.concatenate([-di, di], axis=1)))
    p2 = 2 * p
    xf = jnp.zeros((nb, p2), F32)
    xb = jnp.zeros((nb, p2), F32)
    for i in range(nchunks):
        rf = slice(i * nb, (i + 1) * nb)
        rb = slice((nchunks - 1 - i) * nb, (nchunks - i) * nb)
        x_scr[rf, 0:p2] = xf
        x_scr[rb, p2:2 * p2] = xb
        xf = dec[0][0] * xf + dec[0][1] * pltpu.roll(xf, p, 1) + acc_scr[rf, n_out:n_out + p2]
        xb = dec[1][0] * xb + dec[1][1] * pltpu.roll(xb, p, 1) + acc_scr[rb, n_out + p2:n_out + 2 * p2]
    y = acc_scr[:, 0:n_out] + _dot(x_scr[...].astype(BF16), g_scr[...])
    for co in range(cpg):
        y_ref[co] = y[:, co * tc:(co + 1) * tc]


def _ssm_conv(ut, params, l, bsz):
    lrow, lcol, bt, c4, ct = params
    _, r, tc = ut.shape
    cpg, p, ng = SSM_CPG, SSM_STATE, SSM_GROUPS
    n_w = cpg * tc + 4 * p

    def pspec(a):
        return pl.BlockSpec((1, 1) + a.shape[2:], lambda g: (l, g) + (0,) * (a.ndim - 2))
    return pl.pallas_call(
        functools.partial(_ssm_kernel, nb=bsz, nchunks=r // bsz), grid=(ng,),
        in_specs=[pl.BlockSpec((cpg, r, tc), lambda g: (g, 0, 0)),
                  pspec(lrow), pspec(lcol), pspec(bt), pspec(c4), pspec(ct)],
        out_specs=pl.BlockSpec((cpg, r, tc), lambda g: (g, 0, 0)),
        out_shape=jax.ShapeDtypeStruct(ut.shape, F32),
        scratch_shapes=[pltpu.VMEM((cpg, tc, n_w), BF16), pltpu.VMEM((4 * p, cpg * tc), BF16),
                        pltpu.VMEM((cpg * cpg, tc), F32), pltpu.VMEM((cpg * cpg, tc), F32),
                        pltpu.VMEM((r, n_w), F32), pltpu.VMEM((r, 4 * p), F32)],
        compiler_params=_cparams(("parallel",)),
    )(ut, lrow, lcol, bt, c4, ct)


def _retstate_kernel(lg_ref, kf_ref, vf_ref, kb_ref, vb_ref, sf_ref, sb_ref, accf, accb, *, l):
    c = pl.program_id(1)

    @pl.when(c == 0)
    def _():
        accf[...] = jnp.zeros_like(accf)
        accb[...] = jnp.zeros_like(accb)

    sf_ref[0, 0] = accf[...]
    sb_ref[0, 0] = accb[...]
    ch = RET_CHUNK
    pos = lax.broadcasted_iota(jnp.int32, (ch, 1), 0).astype(F32)
    one = jnp.ones((1, RET_DK), F32)
    for h in range(RET_HEADS):
        cols = slice(h * RET_DK, (h + 1) * RET_DK)
        lgf = lg_ref[l, 0, h]
        lgb = lg_ref[l, 1, h]
        kf = (kf_ref[:, cols] * RET_DK ** -0.5) * jnp.exp(lgf * (ch - 1.0 - pos))
        kb = (kb_ref[:, cols] * RET_DK ** -0.5) * jnp.exp(lgb * pos)
        kvf = _dot_tn(kf.astype(BF16), vf_ref[:, cols].astype(BF16))
        kvb = _dot_tn(kb.astype(BF16), vb_ref[:, cols].astype(BF16))
        accf[cols, :] = jnp.exp(one * (lgf * ch)) * accf[cols, :] + kvf
        accb[cols, :] = jnp.exp(one * (lgb * ch)) * accb[cols, :] + kvb


def _ret_states(proj, lg, l, bsz, seq):
    nc = seq // RET_CHUNK
    w = RET_HEADS * RET_DK

    def blk(col, rev):
        if rev:
            return pl.BlockSpec((RET_CHUNK, w), lambda b, c: (b * nc + nc - 1 - c, col))
        return pl.BlockSpec((RET_CHUNK, w), lambda b, c: (b * nc + c, col))
    st = jax.ShapeDtypeStruct((bsz, nc, w, RET_DK), F32)
    return pl.pallas_call(
        functools.partial(_retstate_kernel, l=l), grid=(bsz, nc),
        in_specs=[pl.BlockSpec(memory_space=pltpu.SMEM),
                  blk(COL_RK // w, False), blk(COL_RV // w, False), blk(COL_RK // w, True), blk(COL_RV // w, True)],
        out_specs=[pl.BlockSpec((1, 1, w, RET_DK), lambda b, c: (b, c, 0, 0)),
                   pl.BlockSpec((1, 1, w, RET_DK), lambda b, c: (b, nc - 1 - c, 0, 0))],
        out_shape=[st, st],
        scratch_shapes=[pltpu.VMEM((w, RET_DK), F32), pltpu.VMEM((w, RET_DK), F32)],
        compiler_params=_cparams(("parallel", "arbitrary")),
    )(lg, proj, proj, proj, proj)


def _gelu_tanh(x):
    return 0.5 * x * (1.0 + jnp.tanh(math.sqrt(2.0 / math.pi) * (x + 0.044715 * (x * x * x))))


def _mixer_kernel(lg_ref, sink_ref, proj_ref, kvp_ref, kvn_ref, sf_ref, sb_ref, yt_ref, h_ref,
                  wout_ref, wglu_ref, d_ref, bglu_ref, lng_ref, lnb_ref, rw_ref, rwt_ref,
                  h1_ref, afft_ref, aff_ref, mix_scr, *, l, seq, nb):
    c = pl.program_id(0)
    b = pl.program_id(1)
    ch = RET_CHUNK
    hp = lax.Precision.HIGHEST

    pos = lax.broadcasted_iota(jnp.int32, (ch, 1), 0).astype(F32)
    dist = lax.broadcasted_iota(jnp.int32, (ch, ch), 0) - lax.broadcasted_iota(jnp.int32, (ch, ch), 1)
    adist = jnp.abs(dist).astype(F32)
    for h in range(RET_HEADS):
        lgf = lg_ref[l, 0, h]
        lgb = lg_ref[l, 1, h]
        q = proj_ref[:, COL_RQ + h * RET_DK:COL_RQ + (h + 1) * RET_DK]
        k = proj_ref[:, COL_RK + h * RET_DK:COL_RK + (h + 1) * RET_DK] * RET_DK ** -0.5
        v = proj_ref[:, COL_RV + h * RET_DK:COL_RV + (h + 1) * RET_DK].astype(BF16)
        g = proj_ref[:, COL_RG + h * RET_DK:COL_RG + (h + 1) * RET_DK]
        dmat = jnp.exp(jnp.where(dist >= 0, lgf, lgb) * adist)
        scores = _dot_nt(q.astype(BF16), k.astype(BF16)) * dmat
        q_f = q * jnp.exp(lgf * (pos + 1.0))
        q_b = q * jnp.exp(lgb * (ch - pos))
        rows = slice(h * RET_DK, (h + 1) * RET_DK)
        o = (_dot(scores.astype(BF16), v)
             + _dot(q_f.astype(BF16), sf_ref[0, 0, rows, :].astype(BF16))
             + _dot(q_b.astype(BF16), sb_ref[0, 0, rows, :].astype(BF16)))
        mu = jnp.mean(o, axis=-1, keepdims=True)
        oc = o - mu
        var = jnp.mean(oc * oc, axis=-1, keepdims=True)
        o = oc * lax.rsqrt(var + LN_EPS)
        mix_scr[:, h * RET_DK:(h + 1) * RET_DK] = (g * jax.nn.sigmoid(g)) * o

    w_ssm = SSM_WIDTH
    j = (c * nb + b) % SUBLANES
    yraw = yt_ref[:, pl.ds(j, 1), :].reshape(w_ssm, ch).T
    y = yraw + d_ref[0] * proj_ref[:, COL_SU:COL_SU + w_ssm]
    y = _gelu_tanh(y)
    gate = jax.nn.sigmoid(_dot(y.astype(BF16), wglu_ref[0]) + bglu_ref[0])
    mix_scr[:, w_ssm:2 * w_ssm] = y * gate

    blk = ATT_BLOCK
    hd = ATT_HEAD_DIM
    kvw = ATT_KV_HEADS * hd
    s_idx = lax.broadcasted_iota(jnp.int32, (blk, 3 * blk), 1)
    t_idx = lax.broadcasted_iota(jnp.int32, (blk, 3 * blk), 0)
    rel = s_idx - blk - t_idx
    arel = jnp.abs(rel)
    spos = c * blk - blk + s_idx
    valid = (arel <= blk) & (spos >= 0) & (spos < seq)
    arel_f = arel.astype(F32)
    att_base = 2 * w_ssm
    for kvh in range(ATT_KV_HEADS):
        kc = slice(kvh * hd, (kvh + 1) * hd)
        vc = slice(kvw + kvh * hd, kvw + (kvh + 1) * hd)
        kb = jnp.concatenate([kvp_ref[:, kc], proj_ref[:, COL_AK + kvh * hd:COL_AK + (kvh + 1) * hd],
                              kvn_ref[:, kc]], axis=0).astype(BF16)
        vb = jnp.concatenate([kvp_ref[:, vc], proj_ref[:, COL_AV + kvh * hd:COL_AV + (kvh + 1) * hd],
                              kvn_ref[:, vc]], axis=0).astype(BF16)
        for gq in range(ATT_GQ):
            hh = kvh * ATT_GQ + gq
            q = proj_ref[:, COL_AQ + hh * hd:COL_AQ + (hh + 1) * hd] * hd ** -0.5
            sc = _dot_nt(q.astype(BF16), kb) - (2.0 ** -(hh + 1)) * arel_f
            sc = jnp.where(valid, sc, NEG_INF)
            sink = sink_ref[l, hh]
            m = jnp.maximum(jnp.max(sc, axis=-1, keepdims=True), sink)
            p = jnp.exp(sc - m)
            denom = jnp.sum(p, axis=-1, keepdims=True) + jnp.exp(sink - m)
            probs = (p / denom).astype(BF16)
            mix_scr[:, att_base + hh * hd:att_base + (hh + 1) * hd] = _dot(probs, vb)

    mix = _dot(mix_scr[...].astype(BF16), wout_ref[0])
    h1 = _layer_norm(DEEPNORM_ALPHA * h_ref[...] + mix, lng_ref[0], lnb_ref[0])
    h1_ref[...] = h1

    lt = lax.dot_general(rwt_ref[0], h1, (((1,), (1,)), ((), ())), precision=hp, preferred_element_type=F32)
    lt = lt - jnp.max(lt, axis=0, keepdims=True)
    et = jnp.exp(lt)
    afft_ref[0, 0] = et / jnp.sum(et, axis=0, keepdims=True)
    ln_ = jnp.dot(h1, rw_ref[0], precision=hp, preferred_element_type=F32)
    ln_ = ln_ - jnp.max(ln_, axis=-1, keepdims=True)
    en = jnp.exp(ln_)
    aff_ref[...] = en / jnp.sum(en, axis=-1, keepdims=True)


def _mixer(proj, sf, sb, yt, h2d, lg, sink, wout_bf, wglu_bf, d3, bglu3, lng3, lnb3, rw, rwt, l, bsz, seq):
    nc = seq // ATT_BLOCK
    t, dm = h2d.shape
    ne = rw.shape[-1]
    kvw = 2 * ATT_KV_HEADS * ATT_HEAD_DIM
    sw = RET_HEADS * RET_DK
    kvcol = COL_AK // kvw

    def layer(a):
        return pl.BlockSpec((1,) + a.shape[1:], lambda c, b: (l,) + (0,) * (a.ndim - 1))
    smem = pl.BlockSpec(memory_space=pltpu.SMEM)
    in_specs = [
        smem, smem,
        pl.BlockSpec((ATT_BLOCK, D_IN), lambda c, b: (b * nc + c, 0)),
        pl.BlockSpec((ATT_BLOCK, kvw), lambda c, b: (b * nc + jnp.maximum(c - 1, 0), kvcol)),
        pl.BlockSpec((ATT_BLOCK, kvw), lambda c, b: (b * nc + jnp.minimum(c + 1, nc - 1), kvcol)),
        pl.BlockSpec((1, 1, sw, RET_DK), lambda c, b: (b, c, 0, 0)),
        pl.BlockSpec((1, 1, sw, RET_DK), lambda c, b: (b, c, 0, 0)),
        pl.BlockSpec((SSM_WIDTH, SUBLANES, CHUNK), lambda c, b: (0, (c * bsz + b) // SUBLANES, 0)),
        pl.BlockSpec((ATT_BLOCK, dm), lambda c, b: (b * nc + c, 0)),
        layer(wout_bf), layer(wglu_bf), layer(d3), layer(bglu3), layer(lng3), layer(lnb3), layer(rw), layer(rwt),
    ]
    out_specs = [
        pl.BlockSpec((ATT_BLOCK, dm), lambda c, b: (b * nc + c, 0)),
        pl.BlockSpec((1, 1, ne, ATT_BLOCK), lambda c, b: (b, c, 0, 0)),
        pl.BlockSpec((ATT_BLOCK, ne), lambda c, b: (b * nc + c, 0)),
    ]
    out_shape = [jax.ShapeDtypeStruct((t, dm), F32),
                 jax.ShapeDtypeStruct((bsz, nc, ne, ATT_BLOCK), F32),
                 jax.ShapeDtypeStruct((t, ne), F32)]
    return pl.pallas_call(
        functools.partial(_mixer_kernel, l=l, seq=seq, nb=bsz), grid=(nc, bsz),
        in_specs=in_specs, out_specs=out_specs, out_shape=out_shape,
        scratch_shapes=[pltpu.VMEM((ATT_BLOCK, dm), F32)],
        compiler_params=_cparams(("parallel", "parallel")),
    )(lg, sink, proj, proj, proj, sf, sb, yt, h2d, wout_bf, wglu_bf, d3, bglu3, lng3, lnb3, rw, rwt)


def _route_kernel(a_ref, slot_ref, start_ref, *, nc, ne, cap):
    blk = ATT_BLOCK
    a = a_ref[0]

    def chunk(x, c):
        return x[c * ne:(c + 1) * ne]

    def count(pred):
        x = pred.astype(jnp.int32)
        tot = chunk(x, 0)
        for c in range(1, nc):
            tot = tot + chunk(x, c)
        return jnp.sum(tot, axis=1, keepdims=True)

    def tile_e(v):
        return jnp.concatenate([v] * nc, axis=0)

    tau = jnp.zeros((ne, 1), jnp.int32)
    for bit in range(30, -1, -1):
        cand = tau | (1 << bit)
        tau = jnp.where(count(a >= tile_e(pltpu.bitcast(cand, F32))) >= cap, cand, tau)
    lo = pltpu.bitcast(tau, F32)
    hi = pltpu.bitcast(tau + 1, F32)
    for _ in range(ROUTE_REFINE_STEPS):
        mid = lo + (hi - lo) * 0.5
        ok = count(a >= tile_e(mid)) >= cap
        lo = jnp.where(ok, mid, lo)
        hi = jnp.where(ok, hi, mid)
    gt = a >= tile_e(hi)
    eq = (a >= tile_e(lo)) & jnp.logical_not(gt)
    need = cap - count(gt)

    tri = (lax.broadcasted_iota(jnp.int32, (blk, blk), 0) <= lax.broadcasted_iota(jnp.int32, (blk, blk), 1)).astype(BF16)

    def ranks(mask):
        mf = mask.astype(F32)
        incl = _dot(mf.astype(BF16), tri)
        run = jnp.zeros((ne, 1), F32)
        offs = []
        for c in range(nc):
            offs.append(run)
            run = run + chunk(incl, c)[:, blk - 1:blk]
        return incl - mf, jnp.concatenate(offs, axis=0)

    eq_local, eq_off = ranks(eq)
    mask = gt | (eq & ((eq_local + eq_off) < tile_e(need).astype(F32)))
    local, off = ranks(mask)
    off_i = off.astype(jnp.int32)
    start = (off_i // SLOT_ALIGN) * SLOT_ALIGN
    slot = (off_i - start) + local.astype(jnp.int32)
    slot_ref[0] = jnp.where(mask, slot, -1)
    start_ref[0] = jnp.broadcast_to(start, (nc * ne, blk))


def _route(afft, cap):
    bsz, nc, ne, blk = afft.shape
    a2 = afft.reshape(bsz, nc * ne, blk)
    spec = pl.BlockSpec((1, nc * ne, blk), lambda b: (b, 0, 0))
    slot, start = pl.pallas_call(
        functools.partial(_route_kernel, nc=nc, ne=ne, cap=cap), grid=(bsz,),
        in_specs=[spec], out_specs=[spec, spec],
        out_shape=[jax.ShapeDtypeStruct(a2.shape, jnp.int32)] * 2,
        compiler_params=_cparams(("parallel",)),
    )(a2)
    return slot.reshape(bsz, nc, ne, blk), start[:, :, 0].reshape(bsz * nc * ne)


def _dispatch_kernel(ws_ref, h_ref, slot_ref, xs_ref, *, nc, ne, eg):
    b = pl.program_id(0)
    g = pl.program_id(1)
    c = pl.program_id(2)

    @pl.when(c == 0)
    def _():
        xs_ref[...] = jnp.zeros_like(xs_ref)

    hb = h_ref[...].astype(BF16)
    riota = lax.broadcasted_iota(jnp.int32, (WIN, ATT_BLOCK), 0)
    onehots = [(riota == slot_ref[0, 0, 0, j:j + 1, :]).astype(BF16) for j in range(eg)]
    res = _dot(jnp.concatenate(onehots, axis=0), hb)
    for j in range(eg):
        w = pl.multiple_of(ws_ref[(b * nc + c) * ne + g * eg + j], SLOT_ALIGN)
        cur = xs_ref[0, j, pl.ds(w, WIN), :].astype(F32)
        xs_ref[0, j, pl.ds(w, WIN), :] = (cur + res[j * WIN:(j + 1) * WIN]).astype(BF16)


def _dispatch(h1, slot, wstart, cap, eg=8):
    bsz, nc, ne, blk = slot.shape
    t, dm = h1.shape
    cp = cap + WIN
    return pl.pallas_call(
        functools.partial(_dispatch_kernel, nc=nc, ne=ne, eg=eg),
        grid_spec=pltpu.PrefetchScalarGridSpec(
            num_scalar_prefetch=1, grid=(bsz, ne // eg, nc),
            in_specs=[pl.BlockSpec((blk, dm), lambda b, g, c, ws: (b * nc + c, 0)),
                      pl.BlockSpec((1, 1, 1, eg, blk), lambda b, g, c, ws: (b, c, g, 0, 0))],
            out_specs=pl.BlockSpec((1, eg, cp, dm), lambda b, g, c, ws: (b, g, 0, 0))),
        out_shape=jax.ShapeDtypeStruct((bsz, ne, cp, dm), BF16),
        compiler_params=_cparams(("parallel", "parallel", "arbitrary")),
    )(wstart, h1, slot.reshape(bsz, nc, ne // eg, eg, blk))


def _ffn_kernel(xs_ref, wg_ref, wu_ref, wd_ref, out_ref, acc_ref, *, cap, nf):
    f = pl.program_id(1)
    bsz = xs_ref.shape[0]
    wg = wg_ref[0, 0].astype(BF16)
    wu = wu_ref[0, 0].astype(BF16)
    wd = wd_ref[0, 0].astype(BF16)
    for b in range(bsz):
        x = xs_ref[b, 0]
        hg = _dot(x, wg)
        hu = _dot(x, wu)
        hdn = ((hg * jax.nn.sigmoid(hg)) * hu).astype(BF16)
        contrib = _dot(hdn, wd)
        rows = slice(b * cap, (b + 1) * cap)

        @pl.when(f == 0)
        def _():
            acc_ref[rows, :] = contrib

        @pl.when(f > 0)
        def _():
            acc_ref[rows, :] = acc_ref[rows, :] + contrib

    @pl.when(f == nf - 1)
    def _():
        cp = out_ref.shape[2]
        for b in range(bsz):
            out_ref[b, 0, 0:cap, :] = acc_ref[b * cap:(b + 1) * cap, :].astype(BF16)
            out_ref[b, 0, cap:cp, :] = jnp.zeros((cp - cap, out_ref.shape[3]), BF16)


def _ffn(xs, wg, wu, wd, l, cap, tf=512):
    bsz, ne, cp, dm = xs.shape
    ff = wg.shape[-1]
    nf = ff // tf
    return pl.pallas_call(
        functools.partial(_ffn_kernel, cap=cap, nf=nf), grid=(ne, nf),
        in_specs=[pl.BlockSpec((bsz, 1, cap, dm), lambda e, f: (0, e, 0, 0)),
                  pl.BlockSpec((1, 1, dm, tf), lambda e, f: (l, e, 0, f)),
                  pl.BlockSpec((1, 1, dm, tf), lambda e, f: (l, e, 0, f)),
                  pl.BlockSpec((1, 1, tf, dm), lambda e, f: (l, e, f, 0))],
        out_specs=pl.BlockSpec((bsz, 1, cp, dm), lambda e, f: (0, e, 0, 0)),
        out_shape=jax.ShapeDtypeStruct((bsz, ne, cp, dm), BF16),
        scratch_shapes=[pltpu.VMEM((bsz * cap, dm), F32)],
        compiler_params=_cparams(("parallel", "arbitrary")),
    )(xs, wg, wu, wd)


def _combine_kernel(ws_ref, out_ref, slot_ref, gate_ref, h1_ref, lng_ref, lnb_ref, h2_ref, acc_ref, *, nc, ne, eg, ng):
    b = pl.program_id(0)
    g = pl.program_id(1)
    c = pl.program_id(2)
    blk = ATT_BLOCK
    rows = pl.ds(pl.multiple_of(c * blk, blk), blk)
    riota = lax.broadcasted_iota(jnp.int32, (WIN, blk), 0)
    tot = jnp.zeros((blk, h2_ref.shape[1]), F32)
    for j in range(eg):
        w = pl.multiple_of(ws_ref[(b * nc + c) * ne + g * eg + j], SLOT_ALIGN)
        onehot = (riota == slot_ref[0, 0, 0, j:j + 1, :]).astype(BF16)
        y = _dot_tn(onehot, out_ref[0, j, pl.ds(w, WIN), :])
        tot = tot + gate_ref[0, :, j:j + 1] * y

    @pl.when(g == 0)
    def _():
        acc_ref[rows, :] = tot

    @pl.when(g > 0)
    def _():
        acc_ref[rows, :] = acc_ref[rows, :] + tot

    @pl.when(g == ng - 1)
    def _():
        h2_ref[...] = _layer_norm(DEEPNORM_ALPHA * h1_ref[...] + acc_ref[rows, :], lng_ref[0], lnb_ref[0])


def _combine(out, slot, wstart, aff, h1, lng3, lnb3, l, seq, eg=4):
    bsz, ne, cp, dm = out.shape
    nc = seq // ATT_BLOCK
    ng = ne // eg
    blk = ATT_BLOCK
    t = h1.shape[0]
    gates = jnp.transpose(aff.reshape(t, ng, eg), (1, 0, 2))
    vec = pl.BlockSpec((1, 1, dm), lambda b, g, c, ws: (l, 0, 0))
    return pl.pallas_call(
        functools.partial(_combine_kernel, nc=nc, ne=ne, eg=eg, ng=ng),
        grid_spec=pltpu.PrefetchScalarGridSpec(
            num_scalar_prefetch=1, grid=(bsz, ng, nc),
            in_specs=[pl.BlockSpec((1, eg, cp, dm), lambda b, g, c, ws: (b, g, 0, 0)),
                      pl.BlockSpec((1, 1, 1, eg, blk), lambda b, g, c, ws: (b, c, g, 0, 0)),
                      pl.BlockSpec((1, blk, eg), lambda b, g, c, ws: (g, b * nc + c, 0)),
                      pl.BlockSpec((blk, dm), lambda b, g, c, ws: (b * nc + c, 0)),
                      vec, vec],
            out_specs=pl.BlockSpec((blk, dm), lambda b, g, c, ws: (b * nc + jnp.where(g == ng - 1, c, 0), 0)),
            scratch_shapes=[pltpu.VMEM((seq, dm), F32)]),
        out_shape=jax.ShapeDtypeStruct((t, dm), F32),
        compiler_params=_cparams(("parallel", "arbitrary", "arbitrary")),
    )(wstart, out, slot.reshape(bsz, nc, ng, eg, blk), gates, h1, lng3, lnb3)


def kernel(x, ln_in_g, ln_in_b, w_in, ret_theta, ssm_lambda_re, ssm_lambda_im, ssm_log_step, ssm_b_re, ssm_b_im,
           ssm_c_re, ssm_c_im, ssm_d, ssm_w_glu, ssm_b_glu, attn_sink, w_out, ln1_g, ln1_b, router_w,
           exp_w_gate, exp_w_up, exp_w_down, ln2_g, ln2_b):
    bsz, seq, dm = x.shape
    depth = w_in.shape[0]
    ne = router_w.shape[-1]
    cap = EC_FACTOR * seq // ne
    t = bsz * seq
    w_in_bf = w_in.astype(BF16)
    wsu_bf = jnp.swapaxes(w_in[:, :, COL_SU:COL_SU + SSM_WIDTH], 1, 2).astype(BF16)
    w_out_bf = w_out.astype(BF16)
    w_glu_bf = ssm_w_glu.astype(BF16)
    lg = jax.nn.log_sigmoid(ret_theta.astype(F32))
    sink = attn_sink.astype(F32)
    ssm_par = _ssm_params(ssm_lambda_re, ssm_lambda_im, ssm_log_step, ssm_b_re, ssm_b_im, ssm_c_re, ssm_c_im)
    vec3 = lambda a: a.astype(F32).reshape(depth, 1, -1)
    d3, bglu3, ln1g3, ln1b3, ln2g3, ln2b3 = map(vec3, (ssm_d, ssm_b_glu, ln1_g, ln1_b, ln2_g, ln2_b))
    rw = router_w.astype(F32)
    rwt = jnp.swapaxes(rw, 1, 2)

    h = x
    for l in range(depth):
        if l == 0:
            h, proj, ut = _inproj(h, w_in_bf, wsu_bf, l, ln=(ln_in_g, ln_in_b))
        else:
            proj, ut = _inproj(h, w_in_bf, wsu_bf, l)
        h2d = h.reshape(t, dm)
        proj = proj.reshape(t, D_IN)
        yt = _ssm_conv(ut, ssm_par, l, bsz)
        sf, sb = _ret_states(proj, lg, l, bsz, seq)
        h1, afft, aff = _mixer(proj, sf, sb, yt, h2d, lg, sink, w_out_bf, w_glu_bf, d3, bglu3, ln1g3, ln1b3,
                               rw, rwt, l, bsz, seq)
        slot, wstart = _route(afft, cap)
        xs = _dispatch(h1, slot, wstart, cap)
        out = _ffn(xs, exp_w_gate, exp_w_up, exp_w_down, l, cap)
        h = _combine(out, slot, wstart, aff, h1, ln2g3, ln2b3, l, seq).reshape(bsz, seq, dm)
    return h
```

```python
import functools
import math

import jax
import jax.numpy as jnp
from jax import lax
from jax.experimental import pallas as pl
from jax.experimental.pallas import tpu as pltpu

F32 = jnp.float32
BF16 = jnp.bfloat16

RET_HEADS = 4
RET_DK = 64
RET_CHUNK = 128
SSM_CPG = 16
SSM_GROUPS = 16
SSM_STATE = 64
SSM_WIDTH = SSM_CPG * SSM_GROUPS
ATT_HEADS = 8
ATT_KV_HEADS = 2
ATT_GQ = ATT_HEADS // ATT_KV_HEADS
ATT_HEAD_DIM = 64
ATT_BLOCK = 128
EC_FACTOR = 2
DEPTH = 2
DEEPNORM_ALPHA = (2.0 * DEPTH) ** 0.25
LN_EPS = 1e-5
NEG_INF = -1e30

COL_RQ, COL_RK, COL_RV, COL_RG, COL_SU, COL_AQ, COL_AK, COL_AV = 0, 256, 512, 768, 1024, 1280, 1792, 1920
D_IN = 2048

CHUNK = 128
SUBLANES = 8
SLOT_ALIGN = 16
WIN = CHUNK + SLOT_ALIGN
GATE_LANES = 128
VMEM_LIMIT = 56 * 1024 * 1024
ROUTE_REFINE_STEPS = 10


def _cparams(sem):
    return pltpu.CompilerParams(dimension_semantics=sem, vmem_limit_bytes=VMEM_LIMIT)


def _layer_norm(x, g, b):
    mu = jnp.mean(x, axis=-1, keepdims=True)
    xc = x - mu
    var = jnp.mean(xc * xc, axis=-1, keepdims=True)
    return xc * lax.rsqrt(var + LN_EPS) * g + b


def _dot(a, b):
    return jnp.dot(a, b, preferred_element_type=F32)


def _dot_hp(a, b):
    return jnp.dot(a, b, precision=lax.Precision.HIGHEST, preferred_element_type=F32)


def _dot_nt(a, b):
    return lax.dot_general(a, b, (((1,), (1,)), ((), ())), preferred_element_type=F32)


def _dot_tn(a, b):
    return lax.dot_general(a, b, (((0,), (0,)), ((), ())), preferred_element_type=F32)


def _cmul(ar, ai, br, bi):
    return ar * br - ai * bi, ar * bi + ai * br


def _cpow(br, bi, expo, nbits, shape):
    br = jnp.broadcast_to(br, shape)
    bi = jnp.broadcast_to(bi, shape)
    rr = jnp.ones(shape, F32)
    ri = jnp.zeros(shape, F32)
    for j in range(nbits):
        bit = jnp.broadcast_to(((expo >> j) & 1) == 1, shape)
        nr, ni = _cmul(rr, ri, br, bi)
        rr = jnp.where(bit, nr, rr)
        ri = jnp.where(bit, ni, ri)
        if j + 1 < nbits:
            br, bi = _cmul(br, bi, br, bi)
    return rr, ri


def _inproj_kernel(*refs, apply_ln):
    if apply_ln:
        x_ref, g_ref, b_ref, w_ref, wsu_ref, h_ref, proj_ref, ut_ref = refs
    else:
        x_ref, w_ref, wsu_ref, proj_ref, ut_ref = refs
    nb, tl, d = x_ref.shape
    h = x_ref[...].reshape(nb * tl, d)
    if apply_ln:
        h = _layer_norm(h, g_ref[...], b_ref[...])
        h_ref[...] = h.reshape(nb, tl, d)
    hb = h.astype(BF16)
    proj_ref[...] = _dot(hb, w_ref[0]).reshape(nb, tl, -1)
    ut = _dot_nt(wsu_ref[0], hb)
    k = tl // CHUNK
    for j in range(k):
        for b in range(nb):
            ut_ref[:, j * nb + b, :] = ut[:, (b * k + j) * CHUNK:(b * k + j + 1) * CHUNK]


def _inproj(x3, w_bf, wsu_bf, l, ln=None):
    bsz, seq, d = x3.shape
    n = w_bf.shape[2]
    k = SUBLANES // bsz
    tl = k * CHUNK
    nc = seq // CHUNK
    xspec = pl.BlockSpec((bsz, tl, d), lambda i: (0, i, 0))
    wspec = pl.BlockSpec((1, d, n), lambda i: (l, 0, 0))
    wsuspec = pl.BlockSpec((1, SSM_WIDTH, d), lambda i: (l, 0, 0))
    pspec = pl.BlockSpec((bsz, tl, n), lambda i: (0, i, 0))
    utspec = pl.BlockSpec((SSM_WIDTH, k * bsz, CHUNK), lambda i: (0, i, 0))
    pshape = jax.ShapeDtypeStruct((bsz, seq, n), F32)
    utshape = jax.ShapeDtypeStruct((SSM_WIDTH, nc * bsz, CHUNK), F32)
    if ln is None:
        return pl.pallas_call(
            functools.partial(_inproj_kernel, apply_ln=False), grid=(seq // tl,),
            in_specs=[xspec, wspec, wsuspec], out_specs=[pspec, utspec], out_shape=[pshape, utshape],
            compiler_params=_cparams(("parallel",)),
        )(x3, w_bf, wsu_bf)
    g, b = ln
    vec = pl.BlockSpec((1, d), lambda i: (0, 0))
    return pl.pallas_call(
        functools.partial(_inproj_kernel, apply_ln=True), grid=(seq // tl,),
        in_specs=[xspec, vec, vec, wspec, wsuspec], out_specs=[xspec, pspec, utspec],
        out_shape=[jax.ShapeDtypeStruct(x3.shape, F32), pshape, utshape],
        compiler_params=_cparams(("parallel",)),
    )(x3, g.reshape(1, d), b.reshape(1, d), w_bf, wsu_bf)


def _ssm_params(lam_re, lam_im, log_step, b_re, b_im, c_re, c_im):
    lr, li = lam_re.astype(F32), lam_im.astype(F32)
    step = jnp.exp(log_step.astype(F32))[..., None]
    er = jnp.exp(lr * step)
    lbr, lbi = er * jnp.cos(li * step), er * jnp.sin(li * step)
    den = lr * lr + li * li
    fr = ((lbr - 1.0) * lr + lbi * li) / den
    fi = (lbi * lr - (lbr - 1.0) * li) / den
    br = jnp.swapaxes(b_re.astype(F32), -1, -2)[:, None]
    bi = jnp.swapaxes(b_im.astype(F32), -1, -2)[:, None]
    bbr = fr[..., None, :] * br - fi[..., None, :] * bi
    bbi = fr[..., None, :] * bi + fi[..., None, :] * br
    lam4 = jnp.stack([lbr[:, 0], lbi[:, 0], lbr[:, 1], lbi[:, 1]], axis=2)
    lrow = jnp.concatenate([lam4, jnp.zeros_like(lam4)], axis=2)
    lcol = jnp.swapaxes(lam4, -1, -2)
    bt = jnp.stack([bbr[:, 0], bbi[:, 0], bbr[:, 1], bbi[:, 1]], axis=2)
    cr, ci = c_re.astype(F32), c_im.astype(F32)
    c4 = jnp.stack([cr[:, 0], ci[:, 0], cr[:, 1], ci[:, 1]], axis=2)
    return lrow, lcol, bt, c4, jnp.swapaxes(c4, -1, -2)


def _ssm_kernel(ut_ref, lrow_ref, lcol_ref, bt_ref, c_ref, ct_ref, y_ref,
                w_scr, g_scr, vf_scr, vb_scr, acc_scr, x_scr, *, nb, nchunks):
    tc, p, cpg = CHUNK, SSM_STATE, SSM_CPG
    n_out = cpg * tc
    lrow = lrow_ref[0, 0]
    lcol = lcol_ref[0, 0]

    s_col = lax.broadcasted_iota(jnp.int32, (tc, 1), 0)
    afr, afi = _cpow(lrow[0:1], lrow[1:2], tc - 1 - s_col, 7, (tc, p))
    abr, abi = _cpow(lrow[2:3], lrow[3:4], s_col, 7, (tc, p))
    m_row = lax.broadcasted_iota(jnp.int32, (1, tc), 1)
    pfr, pfi = _cpow(lcol[:, 0:1], lcol[:, 1:2], m_row, 7, (p, tc))
    pf1r, pf1i = _cmul(pfr, pfi, lcol[:, 0:1], lcol[:, 1:2])
    pbr, pbi = _cpow(lcol[:, 2:3], lcol[:, 3:4], tc - m_row, 8, (p, tc))

    def rep_co(x):
        return jnp.broadcast_to(x[:, None, :], (cpg, cpg, p)).reshape(cpg * cpg, p)

    def rep_ci(x):
        return jnp.broadcast_to(x[None, :, :], (cpg, cpg, p)).reshape(cpg * cpg, p)

    cbfr, cbfi = _cmul(rep_co(c_ref[0, 0, 0]), rep_co(c_ref[0, 0, 1]), rep_ci(bt_ref[0, 0, 0]), rep_ci(bt_ref[0, 0, 1]))
    cbbr, cbbi = _cmul(rep_co(c_ref[0, 0, 2]), rep_co(c_ref[0, 0, 3]), rep_ci(bt_ref[0, 0, 2]), rep_ci(bt_ref[0, 0, 3]))
    lane = lax.broadcasted_iota(jnp.int32, (cpg * cpg, tc), 1)
    kb0 = jnp.sum(cbbr, axis=1, keepdims=True)
    vf_scr[...] = _dot_hp(cbfr, pfr) - _dot_hp(cbfi, pfi) + jnp.where(lane == 0, kb0, 0.0)
    vb_scr[...] = _dot_hp(cbbr, pbr) - _dot_hp(cbbi, pbi)

    for co in range(cpg):
        cols = slice(co * tc, (co + 1) * tc)
        for r, (ar, ai) in enumerate(((pf1r, pf1i), (pbr, pbi))):
            gr, gi = _cmul(ct_ref[0, 0, 2 * r][:, co:co + 1], ct_ref[0, 0, 2 * r + 1][:, co:co + 1], ar, ai)
            g_scr[2 * r * p:(2 * r + 1) * p, cols] = gr.astype(BF16)
            g_scr[(2 * r + 1) * p:(2 * r + 2) * p, cols] = (-gi).astype(BF16)

    t_ge_s = lax.broadcasted_iota(jnp.int32, (tc, tc), 1) >= lax.broadcasted_iota(jnp.int32, (tc, tc), 0)
    acc_scr[...] = jnp.zeros_like(acc_scr)

    def per_channel(ci, carry):
        for co in range(cpg):
            rf = jnp.broadcast_to(vf_scr[pl.ds(co * cpg + ci, 1), :], (tc, tc))
            rb = jnp.broadcast_to(vb_scr[pl.ds(co * cpg + ci, 1), :], (tc, tc))
            mf = pltpu.roll(rf, 0, 1, stride=1, stride_axis=0)
            mb = pltpu.roll(rb, 0, 1, stride=1, stride_axis=0)
            w_scr[ci, :, co * tc:(co + 1) * tc] = jnp.where(t_ge_s, mf, mb).astype(BF16)
        sfr, sfi = _cmul(afr, afi, bt_ref[0, 0, 0, pl.ds(ci, 1), :], bt_ref[0, 0, 1, pl.ds(ci, 1), :])
        sbr, sbi = _cmul(abr, abi, bt_ref[0, 0, 2, pl.ds(ci, 1), :], bt_ref[0, 0, 3, pl.ds(ci, 1), :])
        for q, v in enumerate((sfr, sfi, sbr, sbi)):
            w_scr[ci, :, n_out + q * p:n_out + (q + 1) * p] = v.astype(BF16)
        acc_scr[...] += _dot(ut_ref[ci].astype(BF16), w_scr[ci])
        return carry

    lax.fori_loop(0, cpg, per_channel, 0)

    dec = []
    for r in range(2):
        dr, di = lrow[2 * r:2 * r + 1], lrow[2 * r + 1:2 * r + 2]
        for _ in range(7):
            dr, di = _cmul(dr, di, dr, di)
        dec.append((jnp.concatenate([dr, dr], axis=1), jnp.concatenate([-di, di], axis=1)))
    p2 = 2 * p
    xf = jnp.zeros((nb, p2), F32)
    xb = jnp.zeros((nb, p2), F32)
    for i in range(nchunks):
        rf = slice(i * nb, (i + 1) * nb)
        rb = slice((nchunks - 1 - i) * nb, (nchunks - i) * nb)
        x_scr[rf, 0:p2] = xf
        x_scr[rb, p2:2 * p2] = xb
        xf = dec[0][0] * xf + dec[0][1] * pltpu.roll(xf, p, 1) + acc_scr[rf, n_out:n_out + p2]
        xb = dec[1][0] * xb + dec[1][1] * pltpu.roll(xb, p, 1) + acc_scr[rb, n_out + p2:n_out + 2 * p2]
    y = acc_scr[:, 0:n_out] + _dot(x_scr[...].astype(BF16), g_scr[...])
    for co in range(cpg):
        y_ref[co] = y[:, co * tc:(co + 1) * tc]


def _ssm_conv(ut, params, l, bsz):
    lrow, lcol, bt, c4, ct = params
    _, r, tc = ut.shape
    cpg, p, ng = SSM_CPG, SSM_STATE, SSM_GROUPS
    n_w = cpg * tc + 4 * p

    def pspec(a):
        return pl.BlockSpec((1, 1) + a.shape[2:], lambda g: (l, g) + (0,) * (a.ndim - 2))
    return pl.pallas_call(
        functools.partial(_ssm_kernel, nb=bsz, nchunks=r // bsz), grid=(ng,),
        in_specs=[pl.BlockSpec((cpg, r, tc), lambda g: (g, 0, 0)),
                  pspec(lrow), pspec(lcol), pspec(bt), pspec(c4), pspec(ct)],
        out_specs=pl.BlockSpec((cpg, r, tc), lambda g: (g, 0, 0)),
        out_shape=jax.ShapeDtypeStruct(ut.shape, F32),
        scratch_shapes=[pltpu.VMEM((cpg, tc, n_w), BF16), pltpu.VMEM((4 * p, cpg * tc), BF16),
                        pltpu.VMEM((cpg * cpg, tc), F32), pltpu.VMEM((cpg * cpg, tc), F32),
                        pltpu.VMEM((r, n_w), F32), pltpu.VMEM((r, 4 * p), F32)],
        compiler_params=_cparams(("parallel",)),
    )(ut, lrow, lcol, bt, c4, ct)


def _retstate_kernel(lg_ref, kf_ref, vf_ref, kb_ref, vb_ref, sf_ref, sb_ref, accf, accb, *, l):
    c = pl.program_id(1)

    @pl.when(c == 0)
    def _():
        accf[...] = jnp.zeros_like(accf)
        accb[...] = jnp.zeros_like(accb)

    sf_ref[0, 0] = accf[...]
    sb_ref[0, 0] = accb[...]
    ch = RET_CHUNK
    pos = lax.broadcasted_iota(jnp.int32, (ch, 1), 0).astype(F32)
    one = jnp.ones((1, RET_DK), F32)
    for h in range(RET_HEADS):
        cols = slice(h * RET_DK, (h + 1) * RET_DK)
        lgf = lg_ref[l, 0, h]
        lgb = lg_ref[l, 1, h]
        kf = (kf_ref[:, cols] * RET_DK ** -0.5) * jnp.exp(lgf * (ch - 1.0 - pos))
        kb = (kb_ref[:, cols] * RET_DK ** -0.5) * jnp.exp(lgb * pos)
        kvf = _dot_tn(kf.astype(BF16), vf_ref[:, cols].astype(BF16))
        kvb = _dot_tn(kb.astype(BF16), vb_ref[:, cols].astype(BF16))
        accf[cols, :] = jnp.exp(one * (lgf * ch)) * accf[cols, :] + kvf
        accb[cols, :] = jnp.exp(one * (lgb * ch)) * accb[cols, :] + kvb


def _ret_states(proj, lg, l, bsz, seq):
    nc = seq // RET_CHUNK
    w = RET_HEADS * RET_DK

    def blk(col, rev):
        if rev:
            return pl.BlockSpec((RET_CHUNK, w), lambda b, c: (b * nc + nc - 1 - c, col))
        return pl.BlockSpec((RET_CHUNK, w), lambda b, c: (b * nc + c, col))
    st = jax.ShapeDtypeStruct((bsz, nc, w, RET_DK), F32)
    return pl.pallas_call(
        functools.partial(_retstate_kernel, l=l), grid=(bsz, nc),
        in_specs=[pl.BlockSpec(memory_space=pltpu.SMEM),
                  blk(COL_RK // w, False), blk(COL_RV // w, False), blk(COL_RK // w, True), blk(COL_RV // w, True)],
        out_specs=[pl.BlockSpec((1, 1, w, RET_DK), lambda b, c: (b, c, 0, 0)),
                   pl.BlockSpec((1, 1, w, RET_DK), lambda b, c: (b, nc - 1 - c, 0, 0))],
        out_shape=[st, st],
        scratch_shapes=[pltpu.VMEM((w, RET_DK), F32), pltpu.VMEM((w, RET_DK), F32)],
        compiler_params=_cparams(("parallel", "arbitrary")),
    )(lg, proj, proj, proj, proj)


def _gelu_tanh(x):
    return 0.5 * x * (1.0 + jnp.tanh(math.sqrt(2.0 / math.pi) * (x + 0.044715 * (x * x * x))))


def _mixer_kernel(lg_ref, sink_ref, proj_ref, kvp_ref, kvn_ref, sf_ref, sb_ref, yt_ref, h_ref,
                  wout_ref, wglu_ref, d_ref, bglu_ref, lng_ref, lnb_ref, rwt_ref,
                  h1_ref, afft_ref, mix_scr, *, l, seq, nb):
    c = pl.program_id(0)
    b = pl.program_id(1)
    ch = RET_CHUNK
    hp = lax.Precision.HIGHEST

    pos = lax.broadcasted_iota(jnp.int32, (ch, 1), 0).astype(F32)
    dist = lax.broadcasted_iota(jnp.int32, (ch, ch), 0) - lax.broadcasted_iota(jnp.int32, (ch, ch), 1)
    adist = jnp.abs(dist).astype(F32)
    for h in range(RET_HEADS):
        lgf = lg_ref[l, 0, h]
        lgb = lg_ref[l, 1, h]
        q = proj_ref[:, COL_RQ + h * RET_DK:COL_RQ + (h + 1) * RET_DK]
        k = proj_ref[:, COL_RK + h * RET_DK:COL_RK + (h + 1) * RET_DK] * RET_DK ** -0.5
        v = proj_ref[:, COL_RV + h * RET_DK:COL_RV + (h + 1) * RET_DK].astype(BF16)
        g = proj_ref[:, COL_RG + h * RET_DK:COL_RG + (h + 1) * RET_DK]
        dmat = jnp.exp(jnp.where(dist >= 0, lgf, lgb) * adist)
        scores = _dot_nt(q.astype(BF16), k.astype(BF16)) * dmat
        q_f = q * jnp.exp(lgf * (pos + 1.0))
        q_b = q * jnp.exp(lgb * (ch - pos))
        rows = slice(h * RET_DK, (h + 1) * RET_DK)
        o = (_dot(scores.astype(BF16), v)
             + _dot(q_f.astype(BF16), sf_ref[0, 0, rows, :].astype(BF16))
             + _dot(q_b.astype(BF16), sb_ref[0, 0, rows, :].astype(BF16)))
        mu = jnp.mean(o, axis=-1, keepdims=True)
        oc = o - mu
        var = jnp.mean(oc * oc, axis=-1, keepdims=True)
        o = oc * lax.rsqrt(var + LN_EPS)
        mix_scr[:, h * RET_DK:(h + 1) * RET_DK] = (g * jax.nn.sigmoid(g)) * o

    w_ssm = SSM_WIDTH
    j = (c * nb + b) % SUBLANES
    yraw = yt_ref[:, pl.ds(j, 1), :].reshape(w_ssm, ch).T
    y = yraw + d_ref[0] * proj_ref[:, COL_SU:COL_SU + w_ssm]
    y = _gelu_tanh(y)
    gate = jax.nn.sigmoid(_dot(y.astype(BF16), wglu_ref[0]) + bglu_ref[0])
    mix_scr[:, w_ssm:2 * w_ssm] = y * gate

    blk = ATT_BLOCK
    hd = ATT_HEAD_DIM
    kvw = ATT_KV_HEADS * hd
    s_idx = lax.broadcasted_iota(jnp.int32, (blk, 3 * blk), 1)
    t_idx = lax.broadcasted_iota(jnp.int32, (blk, 3 * blk), 0)
    rel = s_idx - blk - t_idx
    arel = jnp.abs(rel)
    spos = c * blk - blk + s_idx
    valid = (arel <= blk) & (spos >= 0) & (spos < seq)
    arel_f = arel.astype(F32)
    att_base = 2 * w_ssm
    for kvh in range(ATT_KV_HEADS):
        kc = slice(kvh * hd, (kvh + 1) * hd)
        vc = slice(kvw + kvh * hd, kvw + (kvh + 1) * hd)
        kb = jnp.concatenate([kvp_ref[:, kc], proj_ref[:, COL_AK + kvh * hd:COL_AK + (kvh + 1) * hd],
                              kvn_ref[:, kc]], axis=0).astype(BF16)
        vb = jnp.concatenate([kvp_ref[:, vc], proj_ref[:, COL_AV + kvh * hd:COL_AV + (kvh + 1) * hd],
                              kvn_ref[:, vc]], axis=0).astype(BF16)
        for gq in range(ATT_GQ):
            hh = kvh * ATT_GQ + gq
            q = proj_ref[:, COL_AQ + hh * hd:COL_AQ + (hh + 1) * hd] * hd ** -0.5
            sc = _dot_nt(q.astype(BF16), kb) - (2.0 ** -(hh + 1)) * arel_f
            sc = jnp.where(valid, sc, NEG_INF)
            sink = sink_ref[l, hh]
            m = jnp.maximum(jnp.max(sc, axis=-1, keepdims=True), sink)
            p = jnp.exp(sc - m)
            denom = jnp.sum(p, axis=-1, keepdims=True) + jnp.exp(sink - m)
            probs = (p / denom).astype(BF16)
            mix_scr[:, att_base + hh * hd:att_base + (hh + 1) * hd] = _dot(probs, vb)

    mix = _dot(mix_scr[...].astype(BF16), wout_ref[0])
    h1 = _layer_norm(DEEPNORM_ALPHA * h_ref[...] + mix, lng_ref[0], lnb_ref[0])
    h1_ref[...] = h1

    lt = lax.dot_general(rwt_ref[0], h1, (((1,), (1,)), ((), ())), precision=hp, preferred_element_type=F32)
    lt = lt - jnp.max(lt, axis=0, keepdims=True)
    et = jnp.exp(lt)
    afft_ref[0, 0] = et / jnp.sum(et, axis=0, keepdims=True)


def _mixer(proj, sf, sb, yt, h2d, lg, sink, wout_bf, wglu_bf, d3, bglu3, lng3, lnb3, rwt, l, bsz, seq):
    nc = seq // ATT_BLOCK
    t, dm = h2d.shape
    ne = rwt.shape[1]
    kvw = 2 * ATT_KV_HEADS * ATT_HEAD_DIM
    sw = RET_HEADS * RET_DK
    kvcol = COL_AK // kvw

    def layer(a):
        return pl.BlockSpec((1,) + a.shape[1:], lambda c, b: (l,) + (0,) * (a.ndim - 1))
    smem = pl.BlockSpec(memory_space=pltpu.SMEM)
    in_specs = [
        smem, smem,
        pl.BlockSpec((ATT_BLOCK, D_IN), lambda c, b: (b * nc + c, 0)),
        pl.BlockSpec((ATT_BLOCK, kvw), lambda c, b: (b * nc + jnp.maximum(c - 1, 0), kvcol)),
        pl.BlockSpec((ATT_BLOCK, kvw), lambda c, b: (b * nc + jnp.minimum(c + 1, nc - 1), kvcol)),
        pl.BlockSpec((1, 1, sw, RET_DK), lambda c, b: (b, c, 0, 0)),
        pl.BlockSpec((1, 1, sw, RET_DK), lambda c, b: (b, c, 0, 0)),
        pl.BlockSpec((SSM_WIDTH, SUBLANES, CHUNK), lambda c, b: (0, (c * bsz + b) // SUBLANES, 0)),
        pl.BlockSpec((ATT_BLOCK, dm), lambda c, b: (b * nc + c, 0)),
        layer(wout_bf), layer(wglu_bf), layer(d3), layer(bglu3), layer(lng3), layer(lnb3), layer(rwt),
    ]
    out_specs = [
        pl.BlockSpec((ATT_BLOCK, dm), lambda c, b: (b * nc + c, 0)),
        pl.BlockSpec((1, 1, ne, ATT_BLOCK), lambda c, b: (b, c, 0, 0)),
    ]
    out_shape = [jax.ShapeDtypeStruct((t, dm), F32),
                 jax.ShapeDtypeStruct((bsz, nc, ne, ATT_BLOCK), F32)]
    return pl.pallas_call(
        functools.partial(_mixer_kernel, l=l, seq=seq, nb=bsz), grid=(nc, bsz),
        in_specs=in_specs, out_specs=out_specs, out_shape=out_shape,
        scratch_shapes=[pltpu.VMEM((ATT_BLOCK, dm), F32)],
        compiler_params=_cparams(("parallel", "parallel")),
    )(lg, sink, proj, proj, proj, sf, sb, yt, h2d, wout_bf, wglu_bf, d3, bglu3, lng3, lnb3, rwt)


def _route_kernel(a_ref, slot_ref, start_ref, *, nc, ne, cap):
    blk = ATT_BLOCK
    a = a_ref[0]

    def chunk(x, c):
        return x[c * ne:(c + 1) * ne]

    def count(pred):
        x = pred.astype(jnp.int32)
        tot = chunk(x, 0)
        for c in range(1, nc):
            tot = tot + chunk(x, c)
        return jnp.sum(tot, axis=1, keepdims=True)

    def tile_e(v):
        return jnp.concatenate([v] * nc, axis=0)

    tau = jnp.zeros((ne, 1), jnp.int32)
    for bit in range(30, -1, -1):
        cand = tau | (1 << bit)
        tau = jnp.where(count(a >= tile_e(pltpu.bitcast(cand, F32))) >= cap, cand, tau)
    lo = pltpu.bitcast(tau, F32)
    hi = pltpu.bitcast(tau + 1, F32)
    for _ in range(ROUTE_REFINE_STEPS):
        mid = lo + (hi - lo) * 0.5
        ok = count(a >= tile_e(mid)) >= cap
        lo = jnp.where(ok, mid, lo)
        hi = jnp.where(ok, hi, mid)
    gt = a >= tile_e(hi)
    eq = (a >= tile_e(lo)) & jnp.logical_not(gt)
    need = cap - count(gt)

    tri = (lax.broadcasted_iota(jnp.int32, (blk, blk), 0) <= lax.broadcasted_iota(jnp.int32, (blk, blk), 1)).astype(BF16)

    def ranks(mask):
        mf = mask.astype(F32)
        incl = _dot(mf.astype(BF16), tri)
        run = jnp.zeros((ne, 1), F32)
        offs = []
        for c in range(nc):
            offs.append(run)
            run = run + chunk(incl, c)[:, blk - 1:blk]
        return incl - mf, jnp.concatenate(offs, axis=0)

    eq_local, eq_off = ranks(eq)
    mask = gt | (eq & ((eq_local + eq_off) < tile_e(need).astype(F32)))
    local, off = ranks(mask)
    off_i = off.astype(jnp.int32)
    start = (off_i // SLOT_ALIGN) * SLOT_ALIGN
    slot = (off_i - start) + local.astype(jnp.int32)
    slot_ref[0] = jnp.where(mask, slot, -1)
    start_ref[0] = jnp.broadcast_to(start, (nc * ne, blk))


def _route(afft, cap):
    bsz, nc, ne, blk = afft.shape
    a2 = afft.reshape(bsz, nc * ne, blk)
    spec = pl.BlockSpec((1, nc * ne, blk), lambda b: (b, 0, 0))
    slot, start = pl.pallas_call(
        functools.partial(_route_kernel, nc=nc, ne=ne, cap=cap), grid=(bsz,),
        in_specs=[spec], out_specs=[spec, spec],
        out_shape=[jax.ShapeDtypeStruct(a2.shape, jnp.int32)] * 2,
        compiler_params=_cparams(("parallel",)),
    )(a2)
    return slot.reshape(bsz, nc, ne, blk), start[:, :, 0].reshape(bsz * nc * ne)


def _dispatch_kernel(ws_ref, h_ref, afft_ref, slot_ref, xs_ref, *, nc, ne, eg):
    b = pl.program_id(0)
    g = pl.program_id(1)
    c = pl.program_id(2)
    dm = h_ref.shape[1]

    @pl.when(c == 0)
    def _():
        xs_ref[...] = jnp.zeros_like(xs_ref)

    hb = h_ref[...].astype(BF16)
    a = afft_ref[0, 0]
    a_hi = a.astype(BF16)
    r1 = a - a_hi.astype(F32)
    a_mid = r1.astype(BF16)
    a_lo = (r1 - a_mid.astype(F32)).astype(BF16)
    a3 = jnp.concatenate([a_hi, a_mid, a_lo, jnp.zeros((GATE_LANES - 3 * ne, ATT_BLOCK), BF16)], axis=0)
    riota = lax.broadcasted_iota(jnp.int32, (WIN, ATT_BLOCK), 0)
    onehots = jnp.concatenate([(riota == slot_ref[0, 0, 0, j:j + 1, :]).astype(BF16) for j in range(eg)], axis=0)
    res = _dot(onehots, hb)
    resg = _dot_nt(onehots, a3)
    for j in range(eg):
        w = pl.multiple_of(ws_ref[(b * nc + c) * ne + g * eg + j], SLOT_ALIGN)
        head = pl.ds(w, SLOT_ALIGN)
        tail = pl.ds(w + SLOT_ALIGN, ATT_BLOCK)
        for cols, r in ((slice(0, dm), res), (slice(dm, dm + GATE_LANES), resg)):
            rj = r[j * WIN:(j + 1) * WIN]
            xs_ref[0, j, head, cols] = (xs_ref[0, j, head, cols].astype(F32) + rj[0:SLOT_ALIGN]).astype(BF16)
            xs_ref[0, j, tail, cols] = rj[SLOT_ALIGN:].astype(BF16)


def _dispatch(h1, afft, slot, wstart, cap, eg=8):
    bsz, nc, ne, blk = slot.shape
    t, dm = h1.shape
    cp = cap + WIN
    return pl.pallas_call(
        functools.partial(_dispatch_kernel, nc=nc, ne=ne, eg=eg),
        grid_spec=pltpu.PrefetchScalarGridSpec(
            num_scalar_prefetch=1, grid=(bsz, ne // eg, nc),
            in_specs=[pl.BlockSpec((blk, dm), lambda b, g, c, ws: (b * nc + c, 0)),
                      pl.BlockSpec((1, 1, ne, blk), lambda b, g, c, ws: (b, c, 0, 0)),
                      pl.BlockSpec((1, 1, 1, eg, blk), lambda b, g, c, ws: (b, c, g, 0, 0))],
            out_specs=pl.BlockSpec((1, eg, cp, dm + GATE_LANES), lambda b, g, c, ws: (b, g, 0, 0))),
        out_shape=jax.ShapeDtypeStruct((bsz, ne, cp, dm + GATE_LANES), BF16),
        compiler_params=_cparams(("parallel", "parallel", "arbitrary")),
    )(wstart, h1, afft, slot.reshape(bsz, nc, ne // eg, eg, blk))


def _ffn_kernel(xs_ref, wg_ref, wu_ref, wd_ref, out_ref, acc_ref, *, cap, nf, ne):
    e = pl.program_id(0)
    f = pl.program_id(1)
    bsz = xs_ref.shape[0]
    dm = wd_ref.shape[3]
    wg = wg_ref[0, 0].astype(BF16)
    wu = wu_ref[0, 0].astype(BF16)
    wd = wd_ref[0, 0].astype(BF16)
    for b in range(bsz):
        x = xs_ref[b, 0, :, 0:dm]
        hg = _dot(x, wg)
        hu = _dot(x, wu)
        hdn = ((hg * jax.nn.sigmoid(hg)) * hu).astype(BF16)
        contrib = _dot(hdn, wd)
        rows = slice(b * cap, (b + 1) * cap)

        @pl.when(f == 0)
        def _():
            acc_ref[rows, :] = contrib

        @pl.when(f > 0)
        def _():
            acc_ref[rows, :] = acc_ref[rows, :] + contrib

    @pl.when(f == nf - 1)
    def _():
        cp = out_ref.shape[2]
        lane = lax.broadcasted_iota(jnp.int32, (cap, GATE_LANES), 1)
        mine = ((lane % ne) == e) & (lane < 3 * ne)
        for b in range(bsz):
            pieces = xs_ref[b, 0, :, dm:dm + GATE_LANES].astype(F32)
            gate = jnp.sum(jnp.where(mine, pieces, 0.0), axis=1, keepdims=True)
            out_ref[b, 0, 0:cap, :] = (acc_ref[b * cap:(b + 1) * cap, :] * gate).astype(BF16)
            out_ref[b, 0, cap:cp, :] = jnp.zeros((cp - cap, dm), BF16)


def _ffn(xs, wg, wu, wd, l, cap, tf=512):
    bsz, ne, cp, dx = xs.shape
    dm = wg.shape[2]
    ff = wg.shape[-1]
    nf = ff // tf
    return pl.pallas_call(
        functools.partial(_ffn_kernel, cap=cap, nf=nf, ne=ne), grid=(ne, nf),
        in_specs=[pl.BlockSpec((bsz, 1, cap, dx), lambda e, f: (0, e, 0, 0)),
                  pl.BlockSpec((1, 1, dm, tf), lambda e, f: (l, e, 0, f)),
                  pl.BlockSpec((1, 1, dm, tf), lambda e, f: (l, e, 0, f)),
                  pl.BlockSpec((1, 1, tf, dm), lambda e, f: (l, e, f, 0))],
        out_specs=pl.BlockSpec((bsz, 1, cp, dm), lambda e, f: (0, e, 0, 0)),
        out_shape=jax.ShapeDtypeStruct((bsz, ne, cp, dm), BF16),
        scratch_shapes=[pltpu.VMEM((bsz * cap, dm), F32)],
        compiler_params=_cparams(("parallel", "arbitrary")),
    )(xs, wg, wu, wd)


def _combine_kernel(ws_ref, out_ref, slott_ref, h1_ref, lng_ref, lnb_ref, h2_ref, *, nc, ne):
    b = pl.program_id(0)
    c = pl.program_id(1)
    blk = ATT_BLOCK
    liota = lax.broadcasted_iota(jnp.int32, (blk, WIN), 1)
    slot_t = slott_ref[0, 0]
    tot = None
    for e in range(ne):
        w = pl.multiple_of(ws_ref[(b * nc + c) * ne + e], SLOT_ALIGN)
        onehot = (liota == slot_t[:, e:e + 1]).astype(BF16)
        y = _dot(onehot, out_ref[0, e, pl.ds(w, WIN), :])
        tot = y if tot is None else tot + y
    h2_ref[...] = _layer_norm(DEEPNORM_ALPHA * h1_ref[...] + tot, lng_ref[0], lnb_ref[0])


def _combine(out, slot, wstart, h1, lng3, lnb3, l, seq):
    bsz, ne, cp, dm = out.shape
    nc = seq // ATT_BLOCK
    blk = ATT_BLOCK
    t = h1.shape[0]
    slot_t = jnp.swapaxes(slot, 2, 3)
    vec = pl.BlockSpec((1, 1, dm), lambda b, c, ws: (l, 0, 0))
    return pl.pallas_call(
        functools.partial(_combine_kernel, nc=nc, ne=ne),
        grid_spec=pltpu.PrefetchScalarGridSpec(
            num_scalar_prefetch=1, grid=(bsz, nc),
            in_specs=[pl.BlockSpec((1, ne, cp, dm), lambda b, c, ws: (b, 0, 0, 0), pipeline_mode=pl.Buffered(1)),
                      pl.BlockSpec((1, 1, blk, ne), lambda b, c, ws: (b, c, 0, 0)),
                      pl.BlockSpec((blk, dm), lambda b, c, ws: (b * nc + c, 0)),
                      vec, vec],
            out_specs=pl.BlockSpec((blk, dm), lambda b, c, ws: (b * nc + c, 0))),
        out_shape=jax.ShapeDtypeStruct((t, dm), F32),
        compiler_params=_cparams(("parallel", "arbitrary")),
    )(wstart, out, slot_t, h1, lng3, lnb3)


def kernel(x, ln_in_g, ln_in_b, w_in, ret_theta, ssm_lambda_re, ssm_lambda_im, ssm_log_step, ssm_b_re, ssm_b_im,
           ssm_c_re, ssm_c_im, ssm_d, ssm_w_glu, ssm_b_glu, attn_sink, w_out, ln1_g, ln1_b, router_w,
           exp_w_gate, exp_w_up, exp_w_down, ln2_g, ln2_b):
    bsz, seq, dm = x.shape
    depth = w_in.shape[0]
    ne = router_w.shape[-1]
    cap = EC_FACTOR * seq // ne
    t = bsz * seq
    w_in_bf = w_in.astype(BF16)
    wsu_bf = jnp.swapaxes(w_in[:, :, COL_SU:COL_SU + SSM_WIDTH], 1, 2).astype(BF16)
    w_out_bf = w_out.astype(BF16)
    w_glu_bf = ssm_w_glu.astype(BF16)
    lg = jax.nn.log_sigmoid(ret_theta.astype(F32))
    sink = attn_sink.astype(F32)
    ssm_par = _ssm_params(ssm_lambda_re, ssm_lambda_im, ssm_log_step, ssm_b_re, ssm_b_im, ssm_c_re, ssm_c_im)
    vec3 = lambda a: a.astype(F32).reshape(depth, 1, -1)
    d3, bglu3, ln1g3, ln1b3, ln2g3, ln2b3 = map(vec3, (ssm_d, ssm_b_glu, ln1_g, ln1_b, ln2_g, ln2_b))
    rwt = jnp.swapaxes(router_w.astype(F32), 1, 2)

    h = x
    for l in range(depth):
        if l == 0:
            h, proj, ut = _inproj(h, w_in_bf, wsu_bf, l, ln=(ln_in_g, ln_in_b))
        else:
            proj, ut = _inproj(h, w_in_bf, wsu_bf, l)
        h2d = h.reshape(t, dm)
        proj = proj.reshape(t, D_IN)
        yt = _ssm_conv(ut, ssm_par, l, bsz)
        sf, sb = _ret_states(proj, lg, l, bsz, seq)
        h1, afft = _mixer(proj, sf, sb, yt, h2d, lg, sink, w_out_bf, w_glu_bf, d3, bglu3, ln1g3, ln1b3,
                          rwt, l, bsz, seq)
        slot, wstart = _route(afft, cap)
        xs = _dispatch(h1, afft, slot, wstart, cap)
        out = _ffn(xs, exp_w_gate, exp_w_up, exp_w_down, l, cap)
        h = _combine(out, slot, wstart, h1, ln2g3, ln2b3, l, seq).reshape(bsz, seq, dm)
    return h
```

```python
import functools
import math

import jax
import jax.numpy as jnp
from jax import lax
from jax.experimental import pallas as pl
from jax.experimental.pallas import tpu as pltpu

F32 = jnp.float32
BF16 = jnp.bfloat16

RET_HEADS = 4
RET_DK = 64
RET_CHUNK = 128
RET_WIDTH = RET_HEADS * RET_DK
SSM_CPG = 16
SSM_GROUPS = 16
SSM_STATE = 64
SSM_WIDTH = SSM_CPG * SSM_GROUPS
ATT_HEADS = 8
ATT_KV_HEADS = 2
ATT_GQ = ATT_HEADS // ATT_KV_HEADS
ATT_HEAD_DIM = 64
ATT_BLOCK = 128
ATT_WINDOW = 128
EC_FACTOR = 2
DEPTH = 2
DEEPNORM_ALPHA = (2.0 * DEPTH) ** 0.25
LN_EPS = 1e-5
NEG_INF = -1e30

COL_RQ, COL_RK, COL_RV, COL_RG, COL_SU, COL_AQ, COL_AK, COL_AV = 0, 256, 512, 768, 1024, 1280, 1792, 1920
D_IN = 2048

CHUNK = 128
SUBLANES = 8
SLOT_ALIGN = 16
WIN = CHUNK + SLOT_ALIGN
GATE_LANES = 128
RET_STEP_CHUNKS = 8
VMEM_LIMIT = 56 * 1024 * 1024
ROUTE_REFINE_STEPS = 10


def _cparams(sem):
    return pltpu.CompilerParams(dimension_semantics=sem, vmem_limit_bytes=VMEM_LIMIT)


def _layer_norm(x, g, b):
    mu = jnp.mean(x, axis=-1, keepdims=True)
    xc = x - mu
    var = jnp.mean(xc * xc, axis=-1, keepdims=True)
    return xc * lax.rsqrt(var + LN_EPS) * g + b


def _dot(a, b):
    return jnp.dot(a, b, preferred_element_type=F32)


def _dot_hp(a, b):
    return jnp.dot(a, b, precision=lax.Precision.HIGHEST, preferred_element_type=F32)


def _dot_nt(a, b):
    return lax.dot_general(a, b, (((1,), (1,)), ((), ())), preferred_element_type=F32)


def _cmul(ar, ai, br, bi):
    return ar * br - ai * bi, ar * bi + ai * br


def _cpow(br, bi, expo, nbits):
    shape = (br.shape[0], expo.shape[1])
    rr = jnp.ones(shape, F32)
    ri = jnp.zeros(shape, F32)
    for j in range(nbits):
        bit = jnp.broadcast_to(((expo >> j) & 1) == 1, shape)
        nr, ni = _cmul(rr, ri, br, bi)
        rr = jnp.where(bit, nr, rr)
        ri = jnp.where(bit, ni, ri)
        if j + 1 < nbits:
            br, bi = _cmul(br, bi, br, bi)
    return rr, ri


def _inproj_kernel(*refs, apply_ln):
    if apply_ln:
        x_ref, g_ref, b_ref, w_ref, wt_ref, h_ref, proj_ref, ut_ref, kt_ref = refs
    else:
        x_ref, w_ref, wt_ref, proj_ref, ut_ref, kt_ref = refs
    nb, tl, d = x_ref.shape
    h = x_ref[...].reshape(nb * tl, d)
    if apply_ln:
        h = _layer_norm(h, g_ref[...], b_ref[...])
        h_ref[...] = h.reshape(nb, tl, d)
    hb = h.astype(BF16)
    proj_ref[...] = _dot(hb, w_ref[0]).reshape(nb, tl, -1)
    tt = _dot_nt(wt_ref[0], hb)
    k = tl // CHUNK
    for j in range(k):
        for b in range(nb):
            ut_ref[:, j * nb + b, :] = tt[0:SSM_WIDTH, (b * k + j) * CHUNK:(b * k + j + 1) * CHUNK]
    for b in range(nb):
        kt_ref[b] = tt[SSM_WIDTH:SSM_WIDTH + RET_WIDTH, b * tl:(b + 1) * tl]


def _inproj(x3, w_bf, wt_bf, l, ln=None):
    bsz, seq, d = x3.shape
    n = w_bf.shape[2]
    k = SUBLANES // bsz
    tl = k * CHUNK
    nc = seq // CHUNK
    xspec = pl.BlockSpec((bsz, tl, d), lambda i: (0, i, 0))
    wspec = pl.BlockSpec((1, d, n), lambda i: (l, 0, 0))
    wtspec = pl.BlockSpec((1,) + wt_bf.shape[1:], lambda i: (l, 0, 0))
    out_specs = [pl.BlockSpec((bsz, tl, n), lambda i: (0, i, 0)),
                 pl.BlockSpec((SSM_WIDTH, k * bsz, CHUNK), lambda i: (0, i, 0)),
                 pl.BlockSpec((bsz, RET_WIDTH, tl), lambda i: (0, 0, i))]
    out_shape = [jax.ShapeDtypeStruct((bsz, seq, n), F32),
                 jax.ShapeDtypeStruct((SSM_WIDTH, nc * bsz, CHUNK), F32),
                 jax.ShapeDtypeStruct((bsz, RET_WIDTH, seq), F32)]
    if ln is None:
        return pl.pallas_call(
            functools.partial(_inproj_kernel, apply_ln=False), grid=(seq // tl,),
            in_specs=[xspec, wspec, wtspec], out_specs=out_specs, out_shape=out_shape,
            compiler_params=_cparams(("parallel",)),
        )(x3, w_bf, wt_bf)
    g, b = ln
    vec = pl.BlockSpec((1, d), lambda i: (0, 0))
    return pl.pallas_call(
        functools.partial(_inproj_kernel, apply_ln=True), grid=(seq // tl,),
        in_specs=[xspec, vec, vec, wspec, wtspec], out_specs=[xspec] + out_specs,
        out_shape=[jax.ShapeDtypeStruct(x3.shape, F32)] + out_shape,
        compiler_params=_cparams(("parallel",)),
    )(x3, g.reshape(1, d), b.reshape(1, d), w_bf, wt_bf)


def _ssm_params(lam_re, lam_im, log_step, b_re, b_im, c_re, c_im):
    lr, li = lam_re.astype(F32), lam_im.astype(F32)
    step = jnp.exp(log_step.astype(F32))[..., None]
    er = jnp.exp(lr * step)
    lbr, lbi = er * jnp.cos(li * step), er * jnp.sin(li * step)
    den = lr * lr + li * li
    fr = ((lbr - 1.0) * lr + lbi * li) / den
    fi = (lbi * lr - (lbr - 1.0) * li) / den
    br = jnp.swapaxes(b_re.astype(F32), -1, -2)[:, None]
    bi = jnp.swapaxes(b_im.astype(F32), -1, -2)[:, None]
    bbr = fr[..., None, :] * br - fi[..., None, :] * bi
    bbi = fr[..., None, :] * bi + fi[..., None, :] * br
    lcol = jnp.stack([lbr[:, 0], lbi[:, 0], lbr[:, 1], lbi[:, 1]], axis=-1)
    lrow = jnp.swapaxes(lcol, -1, -2)
    lrow = jnp.concatenate([lrow, jnp.zeros_like(lrow)], axis=2)
    bt = jnp.stack([bbr[:, 0], bbi[:, 0], bbr[:, 1], bbi[:, 1]], axis=2)
    cr, ci = c_re.astype(F32), c_im.astype(F32)
    c4 = jnp.stack([cr[:, 0], ci[:, 0], cr[:, 1], ci[:, 1]], axis=2)
    return lcol, lrow, bt, jnp.swapaxes(bt, -1, -2), c4, jnp.swapaxes(c4, -1, -2)


def _ssm_kernel(ut_ref, lcol_ref, lrow_ref, bt_ref, bcol_ref, c_ref, ct_ref, y_ref,
                w_scr, wst_scr, g_scr, vf_scr, vb_scr, acc_scr, s_scr, x_scr, *, nb, nchunks):
    tc, p, cpg = CHUNK, SSM_STATE, SSM_CPG
    lcol = lcol_ref[0, 0]
    lf = (lcol[:, 0:1], lcol[:, 1:2])
    lb = (lcol[:, 2:3], lcol[:, 3:4])

    m_row = lax.broadcasted_iota(jnp.int32, (1, tc), 1)
    pfr, pfi = _cpow(*lf, m_row, 7)
    pf1r, pf1i = _cmul(pfr, pfi, *lf)
    prr, pri = _cpow(*lf, tc - 1 - m_row, 7)
    pbr, pbi = _cpow(*lb, tc - m_row, 8)
    pqr, pqi = _cpow(*lb, m_row, 7)

    def rep_co(x):
        return jnp.broadcast_to(x[:, None, :], (cpg, cpg, p)).reshape(cpg * cpg, p)

    def rep_ci(x):
        return jnp.broadcast_to(x[None, :, :], (cpg, cpg, p)).reshape(cpg * cpg, p)

    cbfr, cbfi = _cmul(rep_co(c_ref[0, 0, 0]), rep_co(c_ref[0, 0, 1]), rep_ci(bt_ref[0, 0, 0]), rep_ci(bt_ref[0, 0, 1]))
    cbbr, cbbi = _cmul(rep_co(c_ref[0, 0, 2]), rep_co(c_ref[0, 0, 3]), rep_ci(bt_ref[0, 0, 2]), rep_ci(bt_ref[0, 0, 3]))
    lane = lax.broadcasted_iota(jnp.int32, (cpg * cpg, tc), 1)
    kb0 = jnp.sum(cbbr, axis=1, keepdims=True)
    vf_scr[...] = _dot_hp(cbfr, pfr) - _dot_hp(cbfi, pfi) + jnp.where(lane == 0, kb0, 0.0)
    vb_scr[...] = _dot_hp(cbbr, pbr) - _dot_hp(cbbi, pbi)

    for co in range(cpg):
        cols = slice(co * tc, (co + 1) * tc)
        for r, (ar, ai) in enumerate(((pf1r, pf1i), (pbr, pbi))):
            gr, gi = _cmul(ct_ref[0, 0, 2 * r][:, co:co + 1], ct_ref[0, 0, 2 * r + 1][:, co:co + 1], ar, ai)
            g_scr[2 * r * p:(2 * r + 1) * p, cols] = gr.astype(BF16)
            g_scr[(2 * r + 1) * p:(2 * r + 2) * p, cols] = (-gi).astype(BF16)
        for r, (ar, ai) in enumerate(((prr, pri), (pqr, pqi))):
            sr, si = _cmul(bcol_ref[0, 0, 2 * r][:, co:co + 1], bcol_ref[0, 0, 2 * r + 1][:, co:co + 1], ar, ai)
            wst_scr[co, 2 * r * p:(2 * r + 1) * p, :] = sr.astype(BF16)
            wst_scr[co, (2 * r + 1) * p:(2 * r + 2) * p, :] = si.astype(BF16)

    s_idx = lax.broadcasted_iota(jnp.int32, (tc, tc), 0)
    j_idx = lax.broadcasted_iota(jnp.int32, (tc, tc), 1)
    fwd_part = j_idx < tc - s_idx
    acc_scr[...] = jnp.zeros_like(acc_scr)
    s_scr[...] = jnp.zeros_like(s_scr)

    def per_channel(ci, carry):
        for co in range(cpg):
            rf = jnp.broadcast_to(vf_scr[pl.ds(co * cpg + ci, 1), :], (tc, tc))
            rb = jnp.broadcast_to(vb_scr[pl.ds(co * cpg + ci, 1), :], (tc, tc))
            m = pltpu.roll(jnp.where(fwd_part, rf, rb), 0, 1, stride=1, stride_axis=0)
            w_scr[ci, :, co * tc:(co + 1) * tc] = m.astype(BF16)
        u = ut_ref[ci].astype(BF16)
        acc_scr[...] += _dot(u, w_scr[ci])
        s_scr[...] += _dot_nt(u, wst_scr[ci])
        return carry

    lax.fori_loop(0, cpg, per_channel, 0)

    lrow = lrow_ref[0, 0]
    dec = []
    for r in range(2):
        dr, di = lrow[2 * r:2 * r + 1], lrow[2 * r + 1:2 * r + 2]
        for _ in range(7):
            dr, di = _cmul(dr, di, dr, di)
        dec.append((jnp.concatenate([dr, dr], axis=1), jnp.concatenate([-di, di], axis=1)))
    p2 = 2 * p
    xf = jnp.zeros((nb, p2), F32)
    xb = jnp.zeros((nb, p2), F32)
    for i in range(nchunks):
        rf = slice(i * nb, (i + 1) * nb)
        rb = slice((nchunks - 1 - i) * nb, (nchunks - i) * nb)
        x_scr[rf, 0:p2] = xf
        x_scr[rb, p2:2 * p2] = xb
        xf = dec[0][0] * xf + dec[0][1] * pltpu.roll(xf, p, 1) + s_scr[rf, 0:p2]
        xb = dec[1][0] * xb + dec[1][1] * pltpu.roll(xb, p, 1) + s_scr[rb, p2:2 * p2]
    y = acc_scr[...] + _dot(x_scr[...].astype(BF16), g_scr[...])
    for co in range(cpg):
        y_ref[co] = y[:, co * tc:(co + 1) * tc]


def _ssm_conv(ut, params, l, bsz):
    _, r, tc = ut.shape
    cpg, p, ng = SSM_CPG, SSM_STATE, SSM_GROUPS

    def pspec(a):
        return pl.BlockSpec((1, 1) + a.shape[2:], lambda g: (l, g) + (0,) * (a.ndim - 2))
    return pl.pallas_call(
        functools.partial(_ssm_kernel, nb=bsz, nchunks=r // bsz), grid=(ng,),
        in_specs=[pl.BlockSpec((cpg, r, tc), lambda g: (g, 0, 0))] + [pspec(a) for a in params],
        out_specs=pl.BlockSpec((cpg, r, tc), lambda g: (g, 0, 0)),
        out_shape=jax.ShapeDtypeStruct(ut.shape, F32),
        scratch_shapes=[pltpu.VMEM((cpg, tc, cpg * tc), BF16), pltpu.VMEM((cpg, 4 * p, tc), BF16),
                        pltpu.VMEM((4 * p, cpg * tc), BF16),
                        pltpu.VMEM((cpg * cpg, tc), F32), pltpu.VMEM((cpg * cpg, tc), F32),
                        pltpu.VMEM((r, cpg * tc), F32), pltpu.VMEM((r, 4 * p), F32), pltpu.VMEM((r, 4 * p), F32)],
        compiler_params=_cparams(("parallel",)),
    )(ut, *params)


def _retstate_kernel(lg_ref, ktf_ref, vf_ref, ktb_ref, vb_ref, sf_ref, sb_ref, accf, accb, *, l, cs):
    i = pl.program_id(1)

    @pl.when(i == 0)
    def _():
        accf[...] = jnp.zeros_like(accf)
        accb[...] = jnp.zeros_like(accb)

    ch = RET_CHUNK
    pos = lax.broadcasted_iota(jnp.int32, (1, ch), 1).astype(F32)
    one = jnp.ones((1, RET_DK), F32)
    wts = []
    for h in range(RET_HEADS):
        lgf = lg_ref[l, 0, h]
        lgb = lg_ref[l, 1, h]
        wts.append((jnp.exp(lgf * (ch - 1.0 - pos)) * RET_DK ** -0.5, jnp.exp(lgb * pos) * RET_DK ** -0.5,
                    jnp.exp(one * (lgf * ch)), jnp.exp(one * (lgb * ch))))
    for j in range(cs):
        jb = cs - 1 - j
        sf_ref[0, j] = accf[...]
        sb_ref[0, jb] = accb[...]
        for h in range(RET_HEADS):
            rows = slice(h * RET_DK, (h + 1) * RET_DK)
            wf, wb, decf, decb = wts[h]
            kf = (ktf_ref[0, rows, j * ch:(j + 1) * ch] * wf).astype(BF16)
            kb = (ktb_ref[0, rows, jb * ch:(jb + 1) * ch] * wb).astype(BF16)
            accf[rows, :] = decf * accf[rows, :] + _dot(kf, vf_ref[j * ch:(j + 1) * ch, rows].astype(BF16))
            accb[rows, :] = decb * accb[rows, :] + _dot(kb, vb_ref[jb * ch:(jb + 1) * ch, rows].astype(BF16))


def _ret_states(proj, kt, lg, l, bsz, seq):
    nc = seq // RET_CHUNK
    cs = min(RET_STEP_CHUNKS, nc)
    ns = nc // cs
    w = RET_WIDTH
    tl = cs * RET_CHUNK
    vcol = COL_RV // w
    st = jax.ShapeDtypeStruct((bsz, nc, w, RET_DK), F32)
    return pl.pallas_call(
        functools.partial(_retstate_kernel, l=l, cs=cs), grid=(bsz, ns),
        in_specs=[pl.BlockSpec(memory_space=pltpu.SMEM),
                  pl.BlockSpec((1, w, tl), lambda b, i: (b, 0, i)),
                  pl.BlockSpec((tl, w), lambda b, i: (b * ns + i, vcol)),
                  pl.BlockSpec((1, w, tl), lambda b, i: (b, 0, ns - 1 - i)),
                  pl.BlockSpec((tl, w), lambda b, i: (b * ns + ns - 1 - i, vcol))],
        out_specs=[pl.BlockSpec((1, cs, w, RET_DK), lambda b, i: (b, i, 0, 0)),
                   pl.BlockSpec((1, cs, w, RET_DK), lambda b, i: (b, ns - 1 - i, 0, 0))],
        out_shape=[st, st],
        scratch_shapes=[pltpu.VMEM((w, RET_DK), F32), pltpu.VMEM((w, RET_DK), F32)],
        compiler_params=_cparams(("parallel", "arbitrary")),
    )(lg, kt, proj, kt, proj)


def _gelu_tanh(x):
    return 0.5 * x * (1.0 + jnp.tanh(math.sqrt(2.0 / math.pi) * (x + 0.044715 * (x * x * x))))


def _attention_bias(seq):
    blk = ATT_BLOCK
    s_idx = jnp.arange(3 * blk)[None, :]
    t_idx = jnp.arange(blk)[:, None]
    arel = jnp.abs(s_idx - blk - t_idx)
    band = arel <= ATT_WINDOW
    slopes = jnp.exp2(-8.0 * jnp.arange(1, ATT_HEADS + 1, dtype=F32) / ATT_HEADS)
    alibi = -slopes[:, None, None] * arel.astype(F32)[None]
    variants = []
    for prev_ok, next_ok in ((False, True), (True, True), (True, False)):
        ok = band & (prev_ok | (s_idx >= blk)) & (next_ok | (s_idx < 2 * blk))
        variants.append(jnp.where(ok[None], alibi, NEG_INF).reshape(ATT_KV_HEADS, ATT_GQ * blk, 3 * blk))
    return jnp.stack(variants)


def _mixer_kernel(lg_ref, sink_ref, proj_ref, kvp_ref, kvn_ref, sf_ref, sb_ref, yt_ref, h_ref, bias_ref,
                  wout_ref, wglu_ref, d_ref, bglu_ref, lng_ref, lnb_ref, rwt_ref,
                  h1_ref, afft_ref, mix_scr, sret_scr, satt_scr, pret_scr, patt_scr, *, l, nb):
    c = pl.program_id(0)
    b = pl.program_id(1)
    ch = RET_CHUNK
    dk = RET_DK
    blk = ATT_BLOCK
    hd = ATT_HEAD_DIM
    kvw = ATT_KV_HEADS * hd
    hp = lax.Precision.HIGHEST

    for h in range(RET_HEADS):
        q = proj_ref[:, COL_RQ + h * dk:COL_RQ + (h + 1) * dk]
        k = proj_ref[:, COL_RK + h * dk:COL_RK + (h + 1) * dk] * dk ** -0.5
        sret_scr[h] = _dot_nt(q.astype(BF16), k.astype(BF16))
    vbs = []
    for kvh in range(ATT_KV_HEADS):
        kc = slice(kvh * hd, (kvh + 1) * hd)
        vc = slice(kvw + kvh * hd, kvw + (kvh + 1) * hd)
        kb = jnp.concatenate([kvp_ref[:, kc], proj_ref[:, COL_AK + kvh * hd:COL_AK + (kvh + 1) * hd],
                              kvn_ref[:, kc]], axis=0).astype(BF16)
        vbs.append(jnp.concatenate([kvp_ref[:, vc], proj_ref[:, COL_AV + kvh * hd:COL_AV + (kvh + 1) * hd],
                                    kvn_ref[:, vc]], axis=0).astype(BF16))
        q4 = jnp.concatenate([proj_ref[:, COL_AQ + (kvh * ATT_GQ + gq) * hd:COL_AQ + (kvh * ATT_GQ + gq + 1) * hd]
                              for gq in range(ATT_GQ)], axis=0) * hd ** -0.5
        satt_scr[kvh] = _dot_nt(q4.astype(BF16), kb)

    pos = lax.broadcasted_iota(jnp.int32, (ch, 1), 0).astype(F32)
    dist = lax.broadcasted_iota(jnp.int32, (ch, ch), 0) - lax.broadcasted_iota(jnp.int32, (ch, ch), 1)
    adist = jnp.abs(dist).astype(F32)
    for h in range(RET_HEADS):
        lgf = lg_ref[l, 0, h]
        lgb = lg_ref[l, 1, h]
        q = proj_ref[:, COL_RQ + h * dk:COL_RQ + (h + 1) * dk]
        dmat = jnp.exp(jnp.where(dist >= 0, lgf, lgb) * adist)
        pret_scr[h, :, 0:ch] = (sret_scr[h] * dmat).astype(BF16)
        pret_scr[h, :, ch:ch + dk] = (q * jnp.exp(lgf * (pos + 1.0))).astype(BF16)
        pret_scr[h, :, ch + dk:ch + 2 * dk] = (q * jnp.exp(lgb * (ch - pos))).astype(BF16)
    for kvh in range(ATT_KV_HEADS):
        for gq in range(ATT_GQ):
            rows = slice(gq * blk, (gq + 1) * blk)
            sink = sink_ref[l, kvh * ATT_GQ + gq]
            sc = satt_scr[kvh, rows, :] + bias_ref[0, kvh, rows, :]
            m = jnp.maximum(jnp.max(sc, axis=-1, keepdims=True), sink)
            p = jnp.exp(sc - m)
            denom = jnp.sum(p, axis=-1, keepdims=True) + jnp.exp(sink - m)
            patt_scr[kvh, rows, :] = (p * (1.0 / denom)).astype(BF16)

    for h in range(RET_HEADS):
        rows = slice(h * dk, (h + 1) * dk)
        v = proj_ref[:, COL_RV + h * dk:COL_RV + (h + 1) * dk]
        g = proj_ref[:, COL_RG + h * dk:COL_RG + (h + 1) * dk]
        rhs = jnp.concatenate([v, sf_ref[0, 0, rows, :], sb_ref[0, 0, rows, :]], axis=0).astype(BF16)
        o = _dot(pret_scr[h], rhs)
        mu = jnp.mean(o, axis=-1, keepdims=True)
        oc = o - mu
        var = jnp.mean(oc * oc, axis=-1, keepdims=True)
        mix_scr[:, h * dk:(h + 1) * dk] = (g * jax.nn.sigmoid(g)) * (oc * lax.rsqrt(var + LN_EPS))
    att_base = RET_WIDTH + SSM_WIDTH
    for kvh in range(ATT_KV_HEADS):
        o = _dot(patt_scr[kvh], vbs[kvh])
        for gq in range(ATT_GQ):
            hh = kvh * ATT_GQ + gq
            mix_scr[:, att_base + hh * hd:att_base + (hh + 1) * hd] = o[gq * blk:(gq + 1) * blk]

    w_ssm = SSM_WIDTH
    j = (c * nb + b) % SUBLANES
    yraw = yt_ref[:, pl.ds(j, 1), :].reshape(w_ssm, ch).T
    y = yraw + d_ref[0] * proj_ref[:, COL_SU:COL_SU + w_ssm]
    y = _gelu_tanh(y)
    gate = jax.nn.sigmoid(_dot(y.astype(BF16), wglu_ref[0]) + bglu_ref[0])
    mix_scr[:, RET_WIDTH:RET_WIDTH + w_ssm] = y * gate

    mix = _dot(mix_scr[...].astype(BF16), wout_ref[0])
    h1 = _layer_norm(DEEPNORM_ALPHA * h_ref[...] + mix, lng_ref[0], lnb_ref[0])
    h1_ref[...] = h1

    lt = lax.dot_general(rwt_ref[0], h1, (((1,), (1,)), ((), ())), precision=hp, preferred_element_type=F32)
    lt = lt - jnp.max(lt, axis=0, keepdims=True)
    et = jnp.exp(lt)
    afft_ref[0, 0] = et / jnp.sum(et, axis=0, keepdims=True)


def _mixer(proj, sf, sb, yt, h2d, bias, lg, sink, wout_bf, wglu_bf, d3, bglu3, lng3, lnb3, rwt, l, bsz, seq):
    nc = seq // ATT_BLOCK
    assert nc >= 2
    t, dm = h2d.shape
    ne = rwt.shape[1]
    kvw = 2 * ATT_KV_HEADS * ATT_HEAD_DIM
    sw = RET_WIDTH
    kvcol = COL_AK // kvw
    blk = ATT_BLOCK

    def layer(a):
        return pl.BlockSpec((1,) + a.shape[1:], lambda c, b: (l,) + (0,) * (a.ndim - 1))
    smem = pl.BlockSpec(memory_space=pltpu.SMEM)
    in_specs = [
        smem, smem,
        pl.BlockSpec((blk, D_IN), lambda c, b: (b * nc + c, 0)),
        pl.BlockSpec((blk, kvw), lambda c, b: (b * nc + jnp.maximum(c - 1, 0), kvcol)),
        pl.BlockSpec((blk, kvw), lambda c, b: (b * nc + jnp.minimum(c + 1, nc - 1), kvcol)),
        pl.BlockSpec((1, 1, sw, RET_DK), lambda c, b: (b, c, 0, 0)),
        pl.BlockSpec((1, 1, sw, RET_DK), lambda c, b: (b, c, 0, 0)),
        pl.BlockSpec((SSM_WIDTH, SUBLANES, CHUNK), lambda c, b: (0, (c * bsz + b) // SUBLANES, 0)),
        pl.BlockSpec((blk, dm), lambda c, b: (b * nc + c, 0)),
        pl.BlockSpec((1,) + bias.shape[1:], lambda c, b: (jnp.where(c == 0, 0, jnp.where(c == nc - 1, 2, 1)), 0, 0, 0)),
        layer(wout_bf), layer(wglu_bf), layer(d3), layer(bglu3), layer(lng3), layer(lnb3), layer(rwt),
    ]
    out_specs = [
        pl.BlockSpec((blk, dm), lambda c, b: (b * nc + c, 0)),
        pl.BlockSpec((1, 1, ne, blk), lambda c, b: (b, c, 0, 0)),
    ]
    out_shape = [jax.ShapeDtypeStruct((t, dm), F32),
                 jax.ShapeDtypeStruct((bsz, nc, ne, blk), F32)]
    scratch = [pltpu.VMEM((blk, dm), F32),
               pltpu.VMEM((RET_HEADS, blk, blk), F32),
               pltpu.VMEM((ATT_KV_HEADS, ATT_GQ * blk, 3 * blk), F32),
               pltpu.VMEM((RET_HEADS, blk, blk + 2 * RET_DK), BF16),
               pltpu.VMEM((ATT_KV_HEADS, ATT_GQ * blk, 3 * blk), BF16)]
    return pl.pallas_call(
        functools.partial(_mixer_kernel, l=l, nb=bsz), grid=(nc, bsz),
        in_specs=in_specs, out_specs=out_specs, out_shape=out_shape, scratch_shapes=scratch,
        compiler_params=_cparams(("parallel", "parallel")),
    )(lg, sink, proj, proj, proj, sf, sb, yt, h2d, bias, wout_bf, wglu_bf, d3, bglu3, lng3, lnb3, rwt)


def _route_kernel(a_ref, slot_ref, start_ref, *, nc, ne, cap):
    blk = ATT_BLOCK
    a = a_ref[0]

    def chunk(x, c):
        return x[c * ne:(c + 1) * ne]

    def count(pred):
        x = pred.astype(jnp.int32)
        tot = chunk(x, 0)
        for c in range(1, nc):
            tot = tot + chunk(x, c)
        return jnp.sum(tot, axis=1, keepdims=True)

    def tile_e(v):
        return jnp.concatenate([v] * nc, axis=0)

    tau = jnp.zeros((ne, 1), jnp.int32)
    for bit in range(30, -1, -1):
        cand = tau | (1 << bit)
        tau = jnp.where(count(a >= tile_e(pltpu.bitcast(cand, F32))) >= cap, cand, tau)
    lo = pltpu.bitcast(tau, F32)
    hi = pltpu.bitcast(tau + 1, F32)
    for _ in range(ROUTE_REFINE_STEPS):
        mid = lo + (hi - lo) * 0.5
        ok = count(a >= tile_e(mid)) >= cap
        lo = jnp.where(ok, mid, lo)
        hi = jnp.where(ok, hi, mid)
    gt = a >= tile_e(hi)
    eq = (a >= tile_e(lo)) & jnp.logical_not(gt)
    need = cap - count(gt)

    tri = (lax.broadcasted_iota(jnp.int32, (blk, blk), 0) <= lax.broadcasted_iota(jnp.int32, (blk, blk), 1)).astype(BF16)

    def ranks(mask):
        mf = mask.astype(F32)
        incl = _dot(mf.astype(BF16), tri)
        run = jnp.zeros((ne, 1), F32)
        offs = []
        for c in range(nc):
            offs.append(run)
            run = run + chunk(incl, c)[:, blk - 1:blk]
        return incl - mf, jnp.concatenate(offs, axis=0)

    eq_local, eq_off = ranks(eq)
    mask = gt | (eq & ((eq_local + eq_off) < tile_e(need).astype(F32)))
    local, off = ranks(mask)
    off_i = off.astype(jnp.int32)
    start = (off_i // SLOT_ALIGN) * SLOT_ALIGN
    slot = (off_i - start) + local.astype(jnp.int32)
    slot_ref[0] = jnp.where(mask, slot, -1)
    start_ref[0] = jnp.broadcast_to(start, (nc * ne, blk))


def _route(afft, cap):
    bsz, nc, ne, blk = afft.shape
    a2 = afft.reshape(bsz, nc * ne, blk)
    spec = pl.BlockSpec((1, nc * ne, blk), lambda b: (b, 0, 0))
    slot, start = pl.pallas_call(
        functools.partial(_route_kernel, nc=nc, ne=ne, cap=cap), grid=(bsz,),
        in_specs=[spec], out_specs=[spec, spec],
        out_shape=[jax.ShapeDtypeStruct(a2.shape, jnp.int32)] * 2,
        compiler_params=_cparams(("parallel",)),
    )(a2)
    return slot.reshape(bsz, nc, ne, blk), start[:, :, 0].reshape(bsz * nc * ne)


def _dispatch_kernel(ws_ref, h_ref, afft_ref, slot_ref, xs_ref, *, nc, ne, eg):
    b = pl.program_id(0)
    g = pl.program_id(1)
    c = pl.program_id(2)
    dm = h_ref.shape[1]

    @pl.when(c == 0)
    def _():
        xs_ref[...] = jnp.zeros_like(xs_ref)

    hb = h_ref[...].astype(BF16)
    a = afft_ref[0, 0]
    a_hi = a.astype(BF16)
    r1 = a - a_hi.astype(F32)
    a_mid = r1.astype(BF16)
    a_lo = (r1 - a_mid.astype(F32)).astype(BF16)
    a3 = jnp.concatenate([a_hi, a_mid, a_lo, jnp.zeros((GATE_LANES - 3 * ne, ATT_BLOCK), BF16)], axis=0)
    riota = lax.broadcasted_iota(jnp.int32, (WIN, ATT_BLOCK), 0)
    onehots = jnp.concatenate([(riota == slot_ref[0, 0, 0, j:j + 1, :]).astype(BF16) for j in range(eg)], axis=0)
    res = _dot(onehots, hb)
    resg = _dot_nt(onehots, a3)
    for j in range(eg):
        w = pl.multiple_of(ws_ref[(b * nc + c) * ne + g * eg + j], SLOT_ALIGN)
        head = pl.ds(w, SLOT_ALIGN)
        tail = pl.ds(w + SLOT_ALIGN, ATT_BLOCK)
        for cols, r in ((slice(0, dm), res), (slice(dm, dm + GATE_LANES), resg)):
            rj = r[j * WIN:(j + 1) * WIN]
            xs_ref[0, j, head, cols] = (xs_ref[0, j, head, cols].astype(F32) + rj[0:SLOT_ALIGN]).astype(BF16)
            xs_ref[0, j, tail, cols] = rj[SLOT_ALIGN:].astype(BF16)


def _dispatch(h1, afft, slot, wstart, cap, eg=8):
    bsz, nc, ne, blk = slot.shape
    t, dm = h1.shape
    cp = cap + WIN
    return pl.pallas_call(
        functools.partial(_dispatch_kernel, nc=nc, ne=ne, eg=eg),
        grid_spec=pltpu.PrefetchScalarGridSpec(
            num_scalar_prefetch=1, grid=(bsz, ne // eg, nc),
            in_specs=[pl.BlockSpec((blk, dm), lambda b, g, c, ws: (b * nc + c, 0)),
                      pl.BlockSpec((1, 1, ne, blk), lambda b, g, c, ws: (b, c, 0, 0)),
                      pl.BlockSpec((1, 1, 1, eg, blk), lambda b, g, c, ws: (b, c, g, 0, 0))],
            out_specs=pl.BlockSpec((1, eg, cp, dm + GATE_LANES), lambda b, g, c, ws: (b, g, 0, 0))),
        out_shape=jax.ShapeDtypeStruct((bsz, ne, cp, dm + GATE_LANES), BF16),
        compiler_params=_cparams(("parallel", "parallel", "arbitrary")),
    )(wstart, h1, afft, slot.reshape(bsz, nc, ne // eg, eg, blk))


def _ffn_kernel(xs_ref, wg_ref, wu_ref, wd_ref, out_ref, acc_ref, *, cap, nf, ne):
    e = pl.program_id(0)
    f = pl.program_id(1)
    bsz = xs_ref.shape[0]
    dm = wd_ref.shape[3]
    wg = wg_ref[0, 0].astype(BF16)
    wu = wu_ref[0, 0].astype(BF16)
    wd = wd_ref[0, 0].astype(BF16)
    for b in range(bsz):
        x = xs_ref[b, 0, :, 0:dm]
        hg = _dot(x, wg)
        hu = _dot(x, wu)
        hdn = ((hg * jax.nn.sigmoid(hg)) * hu).astype(BF16)
        contrib = _dot(hdn, wd)
        rows = slice(b * cap, (b + 1) * cap)

        @pl.when(f == 0)
        def _():
            acc_ref[rows, :] = contrib

        @pl.when(f > 0)
        def _():
            acc_ref[rows, :] = acc_ref[rows, :] + contrib

    @pl.when(f == nf - 1)
    def _():
        cp = out_ref.shape[2]
        lane = lax.broadcasted_iota(jnp.int32, (cap, GATE_LANES), 1)
        mine = ((lane % ne) == e) & (lane < 3 * ne)
        for b in range(bsz):
            pieces = xs_ref[b, 0, :, dm:dm + GATE_LANES].astype(F32)
            gate = jnp.sum(jnp.where(mine, pieces, 0.0), axis=1, keepdims=True)
            out_ref[b, 0, 0:cap, :] = (acc_ref[b * cap:(b + 1) * cap, :] * gate).astype(BF16)
            out_ref[b, 0, cap:cp, :] = jnp.zeros((cp - cap, dm), BF16)


def _ffn(xs, wg, wu, wd, l, cap, tf=512):
    bsz, ne, cp, dx = xs.shape
    dm = wg.shape[2]
    ff = wg.shape[-1]
    nf = ff // tf
    return pl.pallas_call(
        functools.partial(_ffn_kernel, cap=cap, nf=nf, ne=ne), grid=(ne, nf),
        in_specs=[pl.BlockSpec((bsz, 1, cap, dx), lambda e, f: (0, e, 0, 0)),
                  pl.BlockSpec((1, 1, dm, tf), lambda e, f: (l, e, 0, f)),
                  pl.BlockSpec((1, 1, dm, tf), lambda e, f: (l, e, 0, f)),
                  pl.BlockSpec((1, 1, tf, dm), lambda e, f: (l, e, f, 0))],
        out_specs=pl.BlockSpec((bsz, 1, cp, dm), lambda e, f: (0, e, 0, 0)),
        out_shape=jax.ShapeDtypeStruct((bsz, ne, cp, dm), BF16),
        scratch_shapes=[pltpu.VMEM((bsz * cap, dm), F32)],
        compiler_params=_cparams(("parallel", "arbitrary")),
    )(xs, wg, wu, wd)


def _combine_kernel(ws_ref, out_ref, slott_ref, h1_ref, lng_ref, lnb_ref, h2_ref, *, nc, ne):
    b = pl.program_id(0)
    c = pl.program_id(1)
    blk = ATT_BLOCK
    liota = lax.broadcasted_iota(jnp.int32, (blk, WIN), 1)
    slot_t = slott_ref[0, 0]
    tot = None
    for e in range(ne):
        w = pl.multiple_of(ws_ref[(b * nc + c) * ne + e], SLOT_ALIGN)
        onehot = (liota == slot_t[:, e:e + 1]).astype(BF16)
        y = _dot(onehot, out_ref[0, e, pl.ds(w, WIN), :])
        tot = y if tot is None else tot + y
    h2_ref[...] = _layer_norm(DEEPNORM_ALPHA * h1_ref[...] + tot, lng_ref[0], lnb_ref[0])


def _combine(out, slot, wstart, h1, lng3, lnb3, l, seq):
    bsz, ne, cp, dm = out.shape
    nc = seq // ATT_BLOCK
    blk = ATT_BLOCK
    t = h1.shape[0]
    slot_t = jnp.swapaxes(slot, 2, 3)
    vec = pl.BlockSpec((1, 1, dm), lambda b, c, ws: (l, 0, 0))
    return pl.pallas_call(
        functools.partial(_combine_kernel, nc=nc, ne=ne),
        grid_spec=pltpu.PrefetchScalarGridSpec(
            num_scalar_prefetch=1, grid=(bsz, nc),
            in_specs=[pl.BlockSpec((1, ne, cp, dm), lambda b, c, ws: (b, 0, 0, 0), pipeline_mode=pl.Buffered(1)),
                      pl.BlockSpec((1, 1, blk, ne), lambda b, c, ws: (b, c, 0, 0)),
                      pl.BlockSpec((blk, dm), lambda b, c, ws: (b * nc + c, 0)),
                      vec, vec],
            out_specs=pl.BlockSpec((blk, dm), lambda b, c, ws: (b * nc + c, 0))),
        out_shape=jax.ShapeDtypeStruct((t, dm), F32),
        compiler_params=_cparams(("parallel", "arbitrary")),
    )(wstart, out, slot_t, h1, lng3, lnb3)


def kernel(x, ln_in_g, ln_in_b, w_in, ret_theta, ssm_lambda_re, ssm_lambda_im, ssm_log_step, ssm_b_re, ssm_b_im,
           ssm_c_re, ssm_c_im, ssm_d, ssm_w_glu, ssm_b_glu, attn_sink, w_out, ln1_g, ln1_b, router_w,
           exp_w_gate, exp_w_up, exp_w_down, ln2_g, ln2_b):
    bsz, seq, dm = x.shape
    depth = w_in.shape[0]
    ne = router_w.shape[-1]
    cap = EC_FACTOR * seq // ne
    t = bsz * seq
    w_in_bf = w_in.astype(BF16)
    wt_bf = jnp.swapaxes(jnp.concatenate([w_in[:, :, COL_SU:COL_SU + SSM_WIDTH], w_in[:, :, COL_RK:COL_RK + RET_WIDTH]],
                                         axis=2), 1, 2).astype(BF16)
    w_out_bf = w_out.astype(BF16)
    w_glu_bf = ssm_w_glu.astype(BF16)
    lg = jax.nn.log_sigmoid(ret_theta.astype(F32))
    sink = attn_sink.astype(F32)
    ssm_par = _ssm_params(ssm_lambda_re, ssm_lambda_im, ssm_log_step, ssm_b_re, ssm_b_im, ssm_c_re, ssm_c_im)
    vec3 = lambda a: a.astype(F32).reshape(depth, 1, -1)
    d3, bglu3, ln1g3, ln1b3, ln2g3, ln2b3 = map(vec3, (ssm_d, ssm_b_glu, ln1_g, ln1_b, ln2_g, ln2_b))
    rwt = jnp.swapaxes(router_w.astype(F32), 1, 2)
    bias = _attention_bias(seq)

    h = x
    for l in range(depth):
        if l == 0:
            h, proj, ut, kt = _inproj(h, w_in_bf, wt_bf, l, ln=(ln_in_g, ln_in_b))
        else:
            proj, ut, kt = _inproj(h, w_in_bf, wt_bf, l)
        h2d = h.reshape(t, dm)
        proj = proj.reshape(t, D_IN)
        yt = _ssm_conv(ut, ssm_par, l, bsz)
        sf, sb = _ret_states(proj, kt, lg, l, bsz, seq)
        h1, afft = _mixer(proj, sf, sb, yt, h2d, bias, lg, sink, w_out_bf, w_glu_bf, d3, bglu3, ln1g3, ln1b3,
                          rwt, l, bsz, seq)
        slot, wstart = _route(afft, cap)
        xs = _dispatch(h1, afft, slot, wstart, cap)
        out = _ffn(xs, exp_w_gate, exp_w_up, exp_w_down, l, cap)
        h = _combine(out, slot, wstart, h1, ln2g3, ln2b3, l, seq).reshape(bsz, seq, dm)
    return h
```

```python
import functools
import math

import jax
import jax.numpy as jnp
from jax import lax
from jax.experimental import pallas as pl
from jax.experimental.pallas import tpu as pltpu

F32 = jnp.float32
BF16 = jnp.bfloat16

RET_HEADS = 4
RET_DK = 64
RET_CHUNK = 128
RET_WIDTH = RET_HEADS * RET_DK
SSM_CPG = 16
SSM_GROUPS = 16
SSM_STATE = 64
SSM_WIDTH = SSM_CPG * SSM_GROUPS
ATT_HEADS = 8
ATT_KV_HEADS = 2
ATT_GQ = ATT_HEADS // ATT_KV_HEADS
ATT_HEAD_DIM = 64
ATT_BLOCK = 128
ATT_WINDOW = 128
EC_FACTOR = 2
DEPTH = 2
DEEPNORM_ALPHA = (2.0 * DEPTH) ** 0.25
LN_EPS = 1e-5
NEG_INF = -1e30

COL_RQ, COL_RK, COL_RV, COL_RG, COL_SU, COL_AQ, COL_AK, COL_AV = 0, 256, 512, 768, 1024, 1280, 1792, 1920
D_IN = 2048

CHUNK = 128
SUBLANES = 8
SLOT_ALIGN = 16
WIN = CHUNK + SLOT_ALIGN
SMALL_WIN = 48
GATE_LANES = 128
RET_STEP_CHUNKS = 8
VMEM_LIMIT = 56 * 1024 * 1024
ROUTE_REFINE_STEPS = 10


def _cparams(sem):
    return pltpu.CompilerParams(dimension_semantics=sem, vmem_limit_bytes=VMEM_LIMIT)


def _layer_norm(x, g, b):
    mu = jnp.mean(x, axis=-1, keepdims=True)
    xc = x - mu
    var = jnp.mean(xc * xc, axis=-1, keepdims=True)
    return xc * lax.rsqrt(var + LN_EPS) * g + b


def _dot(a, b):
    return jnp.dot(a, b, preferred_element_type=F32)


def _dot_hp(a, b):
    return jnp.dot(a, b, precision=lax.Precision.HIGHEST, preferred_element_type=F32)


def _dot_nt(a, b):
    return lax.dot_general(a, b, (((1,), (1,)), ((), ())), preferred_element_type=F32)


def _cmul(ar, ai, br, bi):
    return ar * br - ai * bi, ar * bi + ai * br


def _cpow(br, bi, expo, nbits):
    shape = (br.shape[0], expo.shape[1])
    rr = jnp.ones(shape, F32)
    ri = jnp.zeros(shape, F32)
    for j in range(nbits):
        bit = jnp.broadcast_to(((expo >> j) & 1) == 1, shape)
        nr, ni = _cmul(rr, ri, br, bi)
        rr = jnp.where(bit, nr, rr)
        ri = jnp.where(bit, ni, ri)
        if j + 1 < nbits:
            br, bi = _cmul(br, bi, br, bi)
    return rr, ri


def _inproj_kernel(*refs, apply_ln):
    if apply_ln:
        x_ref, g_ref, b_ref, w_ref, wt_ref, h_ref, proj_ref, ut_ref, kt_ref = refs
    else:
        x_ref, w_ref, wt_ref, proj_ref, ut_ref, kt_ref = refs
    nb, tl, d = x_ref.shape
    h = x_ref[...].reshape(nb * tl, d)
    if apply_ln:
        h = _layer_norm(h, g_ref[...], b_ref[...])
        h_ref[...] = h.reshape(nb, tl, d)
    hb = h.astype(BF16)
    proj_ref[...] = _dot(hb, w_ref[0]).reshape(nb, tl, -1)
    tt = _dot_nt(wt_ref[0], hb)
    k = tl // CHUNK
    for j in range(k):
        for b in range(nb):
            ut_ref[:, j * nb + b, :] = tt[0:SSM_WIDTH, (b * k + j) * CHUNK:(b * k + j + 1) * CHUNK]
    for b in range(nb):
        kt_ref[b] = tt[SSM_WIDTH:SSM_WIDTH + RET_WIDTH, b * tl:(b + 1) * tl]


def _inproj(x3, w_bf, wt_bf, l, ln=None):
    bsz, seq, d = x3.shape
    n = w_bf.shape[2]
    k = SUBLANES // bsz
    tl = k * CHUNK
    nc = seq // CHUNK
    xspec = pl.BlockSpec((bsz, tl, d), lambda i: (0, i, 0))
    wspec = pl.BlockSpec((1, d, n), lambda i: (l, 0, 0))
    wtspec = pl.BlockSpec((1,) + wt_bf.shape[1:], lambda i: (l, 0, 0))
    out_specs = [pl.BlockSpec((bsz, tl, n), lambda i: (0, i, 0)),
                 pl.BlockSpec((SSM_WIDTH, k * bsz, CHUNK), lambda i: (0, i, 0)),
                 pl.BlockSpec((bsz, RET_WIDTH, tl), lambda i: (0, 0, i))]
    out_shape = [jax.ShapeDtypeStruct((bsz, seq, n), F32),
                 jax.ShapeDtypeStruct((SSM_WIDTH, nc * bsz, CHUNK), F32),
                 jax.ShapeDtypeStruct((bsz, RET_WIDTH, seq), F32)]
    if ln is None:
        return pl.pallas_call(
            functools.partial(_inproj_kernel, apply_ln=False), grid=(seq // tl,),
            in_specs=[xspec, wspec, wtspec], out_specs=out_specs, out_shape=out_shape,
            compiler_params=_cparams(("parallel",)),
        )(x3, w_bf, wt_bf)
    g, b = ln
    vec = pl.BlockSpec((1, d), lambda i: (0, 0))
    return pl.pallas_call(
        functools.partial(_inproj_kernel, apply_ln=True), grid=(seq // tl,),
        in_specs=[xspec, vec, vec, wspec, wtspec], out_specs=[xspec] + out_specs,
        out_shape=[jax.ShapeDtypeStruct(x3.shape, F32)] + out_shape,
        compiler_params=_cparams(("parallel",)),
    )(x3, g.reshape(1, d), b.reshape(1, d), w_bf, wt_bf)


def _ssm_params(lam_re, lam_im, log_step, b_re, b_im, c_re, c_im):
    lr, li = lam_re.astype(F32), lam_im.astype(F32)
    step = jnp.exp(log_step.astype(F32))[..., None]
    er = jnp.exp(lr * step)
    lbr, lbi = er * jnp.cos(li * step), er * jnp.sin(li * step)
    den = lr * lr + li * li
    fr = ((lbr - 1.0) * lr + lbi * li) / den
    fi = (lbi * lr - (lbr - 1.0) * li) / den
    br = jnp.swapaxes(b_re.astype(F32), -1, -2)[:, None]
    bi = jnp.swapaxes(b_im.astype(F32), -1, -2)[:, None]
    bbr = fr[..., None, :] * br - fi[..., None, :] * bi
    bbi = fr[..., None, :] * bi + fi[..., None, :] * br
    lcol = jnp.stack([lbr[:, 0], lbi[:, 0], lbr[:, 1], lbi[:, 1]], axis=-1)
    lrow = jnp.swapaxes(lcol, -1, -2)
    lrow = jnp.concatenate([lrow, jnp.zeros_like(lrow)], axis=2)
    bt = jnp.stack([bbr[:, 0], bbi[:, 0], bbr[:, 1], bbi[:, 1]], axis=2)
    cr, ci = c_re.astype(F32), c_im.astype(F32)
    c4 = jnp.stack([cr[:, 0], ci[:, 0], cr[:, 1], ci[:, 1]], axis=2)
    return lcol, lrow, bt, jnp.swapaxes(bt, -1, -2), c4, jnp.swapaxes(c4, -1, -2)


def _ssm_kernel(ut_ref, lcol_ref, lrow_ref, bt_ref, bcol_ref, c_ref, ct_ref, y_ref,
                w_scr, wst_scr, g_scr, vf_scr, vb_scr, acc_scr, s_scr, x_scr, *, nb, nchunks):
    tc, p, cpg = CHUNK, SSM_STATE, SSM_CPG
    lcol = lcol_ref[0, 0]
    lf = (lcol[:, 0:1], lcol[:, 1:2])
    lb = (lcol[:, 2:3], lcol[:, 3:4])

    m_row = lax.broadcasted_iota(jnp.int32, (1, tc), 1)
    pfr, pfi = _cpow(*lf, m_row, 7)
    pf1r, pf1i = _cmul(pfr, pfi, *lf)
    prr, pri = _cpow(*lf, tc - 1 - m_row, 7)
    pbr, pbi = _cpow(*lb, tc - m_row, 8)
    pqr, pqi = _cpow(*lb, m_row, 7)

    def rep_co(x):
        return jnp.broadcast_to(x[:, None, :], (cpg, cpg, p)).reshape(cpg * cpg, p)

    def rep_ci(x):
        return jnp.broadcast_to(x[None, :, :], (cpg, cpg, p)).reshape(cpg * cpg, p)

    cbfr, cbfi = _cmul(rep_co(c_ref[0, 0, 0]), rep_co(c_ref[0, 0, 1]), rep_ci(bt_ref[0, 0, 0]), rep_ci(bt_ref[0, 0, 1]))
    cbbr, cbbi = _cmul(rep_co(c_ref[0, 0, 2]), rep_co(c_ref[0, 0, 3]), rep_ci(bt_ref[0, 0, 2]), rep_ci(bt_ref[0, 0, 3]))
    lane = lax.broadcasted_iota(jnp.int32, (cpg * cpg, tc), 1)
    kb0 = jnp.sum(cbbr, axis=1, keepdims=True)
    vf_scr[...] = _dot_hp(cbfr, pfr) - _dot_hp(cbfi, pfi) + jnp.where(lane == 0, kb0, 0.0)
    vb_scr[...] = _dot_hp(cbbr, pbr) - _dot_hp(cbbi, pbi)

    for co in range(cpg):
        cols = slice(co * tc, (co + 1) * tc)
        for r, (ar, ai) in enumerate(((pf1r, pf1i), (pbr, pbi))):
            gr, gi = _cmul(ct_ref[0, 0, 2 * r][:, co:co + 1], ct_ref[0, 0, 2 * r + 1][:, co:co + 1], ar, ai)
            g_scr[2 * r * p:(2 * r + 1) * p, cols] = gr.astype(BF16)
            g_scr[(2 * r + 1) * p:(2 * r + 2) * p, cols] = (-gi).astype(BF16)
        for r, (ar, ai) in enumerate(((prr, pri), (pqr, pqi))):
            sr, si = _cmul(bcol_ref[0, 0, 2 * r][:, co:co + 1], bcol_ref[0, 0, 2 * r + 1][:, co:co + 1], ar, ai)
            wst_scr[co, 2 * r * p:(2 * r + 1) * p, :] = sr.astype(BF16)
            wst_scr[co, (2 * r + 1) * p:(2 * r + 2) * p, :] = si.astype(BF16)

    s_idx = lax.broadcasted_iota(jnp.int32, (tc, tc), 0)
    j_idx = lax.broadcasted_iota(jnp.int32, (tc, tc), 1)
    fwd_part = j_idx < tc - s_idx
    acc_scr[...] = jnp.zeros_like(acc_scr)
    s_scr[...] = jnp.zeros_like(s_scr)

    def per_channel(ci, carry):
        for co in range(cpg):
            rf = jnp.broadcast_to(vf_scr[pl.ds(co * cpg + ci, 1), :], (tc, tc))
            rb = jnp.broadcast_to(vb_scr[pl.ds(co * cpg + ci, 1), :], (tc, tc))
            m = pltpu.roll(jnp.where(fwd_part, rf, rb), 0, 1, stride=1, stride_axis=0)
            w_scr[ci, :, co * tc:(co + 1) * tc] = m.astype(BF16)
        u = ut_ref[ci].astype(BF16)
        acc_scr[...] += _dot(u, w_scr[ci])
        s_scr[...] += _dot_nt(u, wst_scr[ci])
        return carry

    lax.fori_loop(0, cpg, per_channel, 0)

    lrow = lrow_ref[0, 0]
    dec = []
    for r in range(2):
        dr, di = lrow[2 * r:2 * r + 1], lrow[2 * r + 1:2 * r + 2]
        for _ in range(7):
            dr, di = _cmul(dr, di, dr, di)
        dec.append((jnp.concatenate([dr, dr], axis=1), jnp.concatenate([-di, di], axis=1)))
    p2 = 2 * p
    xf = jnp.zeros((nb, p2), F32)
    xb = jnp.zeros((nb, p2), F32)
    for i in range(nchunks):
        rf = slice(i * nb, (i + 1) * nb)
        rb = slice((nchunks - 1 - i) * nb, (nchunks - i) * nb)
        x_scr[rf, 0:p2] = xf
        x_scr[rb, p2:2 * p2] = xb
        xf = dec[0][0] * xf + dec[0][1] * pltpu.roll(xf, p, 1) + s_scr[rf, 0:p2]
        xb = dec[1][0] * xb + dec[1][1] * pltpu.roll(xb, p, 1) + s_scr[rb, p2:2 * p2]
    y = acc_scr[...] + _dot(x_scr[...].astype(BF16), g_scr[...])
    for co in range(cpg):
        y_ref[co] = y[:, co * tc:(co + 1) * tc]


def _ssm_conv(ut, params, l, bsz):
    _, r, tc = ut.shape
    cpg, p, ng = SSM_CPG, SSM_STATE, SSM_GROUPS

    def pspec(a):
        return pl.BlockSpec((1, 1) + a.shape[2:], lambda g: (l, g) + (0,) * (a.ndim - 2))
    return pl.pallas_call(
        functools.partial(_ssm_kernel, nb=bsz, nchunks=r // bsz), grid=(ng,),
        in_specs=[pl.BlockSpec((cpg, r, tc), lambda g: (g, 0, 0))] + [pspec(a) for a in params],
        out_specs=pl.BlockSpec((cpg, r, tc), lambda g: (g, 0, 0)),
        out_shape=jax.ShapeDtypeStruct(ut.shape, F32),
        scratch_shapes=[pltpu.VMEM((cpg, tc, cpg * tc), BF16), pltpu.VMEM((cpg, 4 * p, tc), BF16),
                        pltpu.VMEM((4 * p, cpg * tc), BF16),
                        pltpu.VMEM((cpg * cpg, tc), F32), pltpu.VMEM((cpg * cpg, tc), F32),
                        pltpu.VMEM((r, cpg * tc), F32), pltpu.VMEM((r, 4 * p), F32), pltpu.VMEM((r, 4 * p), F32)],
        compiler_params=_cparams(("parallel",)),
    )(ut, *params)


def _retstate_kernel(lg_ref, ktf_ref, vf_ref, ktb_ref, vb_ref, sf_ref, sb_ref, accf, accb, *, l, cs):
    i = pl.program_id(1)

    @pl.when(i == 0)
    def _():
        accf[...] = jnp.zeros_like(accf)
        accb[...] = jnp.zeros_like(accb)

    ch = RET_CHUNK
    pos = lax.broadcasted_iota(jnp.int32, (1, ch), 1).astype(F32)
    one = jnp.ones((1, RET_DK), F32)
    wts = []
    for h in range(RET_HEADS):
        lgf = lg_ref[l, 0, h]
        lgb = lg_ref[l, 1, h]
        wts.append((jnp.exp(lgf * (ch - 1.0 - pos)) * RET_DK ** -0.5, jnp.exp(lgb * pos) * RET_DK ** -0.5,
                    jnp.exp(one * (lgf * ch)), jnp.exp(one * (lgb * ch))))
    for j in range(cs):
        jb = cs - 1 - j
        sf_ref[0, j] = accf[...]
        sb_ref[0, jb] = accb[...]
        for h in range(RET_HEADS):
            rows = slice(h * RET_DK, (h + 1) * RET_DK)
            wf, wb, decf, decb = wts[h]
            kf = (ktf_ref[0, rows, j * ch:(j + 1) * ch] * wf).astype(BF16)
            kb = (ktb_ref[0, rows, jb * ch:(jb + 1) * ch] * wb).astype(BF16)
            accf[rows, :] = decf * accf[rows, :] + _dot(kf, vf_ref[j * ch:(j + 1) * ch, rows].astype(BF16))
            accb[rows, :] = decb * accb[rows, :] + _dot(kb, vb_ref[jb * ch:(jb + 1) * ch, rows].astype(BF16))


def _ret_states(proj, kt, lg, l, bsz, seq):
    nc = seq // RET_CHUNK
    cs = min(RET_STEP_CHUNKS, nc)
    ns = nc // cs
    w = RET_WIDTH
    tl = cs * RET_CHUNK
    vcol = COL_RV // w
    st = jax.ShapeDtypeStruct((bsz, nc, w, RET_DK), F32)
    return pl.pallas_call(
        functools.partial(_retstate_kernel, l=l, cs=cs), grid=(bsz, ns),
        in_specs=[pl.BlockSpec(memory_space=pltpu.SMEM),
                  pl.BlockSpec((1, w, tl), lambda b, i: (b, 0, i)),
                  pl.BlockSpec((tl, w), lambda b, i: (b * ns + i, vcol)),
                  pl.BlockSpec((1, w, tl), lambda b, i: (b, 0, ns - 1 - i)),
                  pl.BlockSpec((tl, w), lambda b, i: (b * ns + ns - 1 - i, vcol))],
        out_specs=[pl.BlockSpec((1, cs, w, RET_DK), lambda b, i: (b, i, 0, 0)),
                   pl.BlockSpec((1, cs, w, RET_DK), lambda b, i: (b, ns - 1 - i, 0, 0))],
        out_shape=[st, st],
        scratch_shapes=[pltpu.VMEM((w, RET_DK), F32), pltpu.VMEM((w, RET_DK), F32)],
        compiler_params=_cparams(("parallel", "arbitrary")),
    )(lg, kt, proj, kt, proj)


def _gelu_tanh(x):
    return 0.5 * x * (1.0 + jnp.tanh(math.sqrt(2.0 / math.pi) * (x + 0.044715 * (x * x * x))))


def _attention_bias(seq):
    blk = ATT_BLOCK
    s_idx = jnp.arange(3 * blk)[None, :]
    t_idx = jnp.arange(blk)[:, None]
    arel = jnp.abs(s_idx - blk - t_idx)
    band = arel <= ATT_WINDOW
    slopes = jnp.exp2(-8.0 * jnp.arange(1, ATT_HEADS + 1, dtype=F32) / ATT_HEADS)
    alibi = -slopes[:, None, None] * arel.astype(F32)[None]
    variants = []
    for prev_ok, next_ok in ((False, True), (True, True), (True, False)):
        ok = band & (prev_ok | (s_idx >= blk)) & (next_ok | (s_idx < 2 * blk))
        variants.append(jnp.where(ok[None], alibi, NEG_INF).reshape(ATT_KV_HEADS, ATT_GQ * blk, 3 * blk))
    return jnp.stack(variants)


def _mixer_kernel(lg_ref, sink_ref, proj_ref, kvp_ref, kvn_ref, sf_ref, sb_ref, yt_ref, h_ref, bias_ref,
                  wout_ref, wglu_ref, d_ref, bglu_ref, lng_ref, lnb_ref, rwt_ref,
                  h1_ref, afft_ref, mix_scr, sret_scr, satt_scr, pret_scr, patt_scr, *, l, nb):
    c = pl.program_id(0)
    b = pl.program_id(1)
    ch = RET_CHUNK
    dk = RET_DK
    blk = ATT_BLOCK
    hd = ATT_HEAD_DIM
    kvw = ATT_KV_HEADS * hd
    hp = lax.Precision.HIGHEST

    for h in range(RET_HEADS):
        q = proj_ref[:, COL_RQ + h * dk:COL_RQ + (h + 1) * dk]
        k = proj_ref[:, COL_RK + h * dk:COL_RK + (h + 1) * dk] * dk ** -0.5
        sret_scr[h] = _dot_nt(q.astype(BF16), k.astype(BF16))
    vbs = []
    for kvh in range(ATT_KV_HEADS):
        kc = slice(kvh * hd, (kvh + 1) * hd)
        vc = slice(kvw + kvh * hd, kvw + (kvh + 1) * hd)
        kb = jnp.concatenate([kvp_ref[:, kc], proj_ref[:, COL_AK + kvh * hd:COL_AK + (kvh + 1) * hd],
                              kvn_ref[:, kc]], axis=0).astype(BF16)
        vbs.append(jnp.concatenate([kvp_ref[:, vc], proj_ref[:, COL_AV + kvh * hd:COL_AV + (kvh + 1) * hd],
                                    kvn_ref[:, vc]], axis=0).astype(BF16))
        q4 = jnp.concatenate([proj_ref[:, COL_AQ + (kvh * ATT_GQ + gq) * hd:COL_AQ + (kvh * ATT_GQ + gq + 1) * hd]
                              for gq in range(ATT_GQ)], axis=0) * hd ** -0.5
        satt_scr[kvh] = _dot_nt(q4.astype(BF16), kb)

    pos = lax.broadcasted_iota(jnp.int32, (ch, 1), 0).astype(F32)
    dist = lax.broadcasted_iota(jnp.int32, (ch, ch), 0) - lax.broadcasted_iota(jnp.int32, (ch, ch), 1)
    adist = jnp.abs(dist).astype(F32)
    for h in range(RET_HEADS):
        lgf = lg_ref[l, 0, h]
        lgb = lg_ref[l, 1, h]
        q = proj_ref[:, COL_RQ + h * dk:COL_RQ + (h + 1) * dk]
        dmat = jnp.exp(jnp.where(dist >= 0, lgf, lgb) * adist)
        pret_scr[h, :, 0:ch] = (sret_scr[h] * dmat).astype(BF16)
        pret_scr[h, :, ch:ch + dk] = (q * jnp.exp(lgf * (pos + 1.0))).astype(BF16)
        pret_scr[h, :, ch + dk:ch + 2 * dk] = (q * jnp.exp(lgb * (ch - pos))).astype(BF16)
    for kvh in range(ATT_KV_HEADS):
        for gq in range(ATT_GQ):
            rows = slice(gq * blk, (gq + 1) * blk)
            sink = sink_ref[l, kvh * ATT_GQ + gq]
            sc = satt_scr[kvh, rows, :] + bias_ref[0, kvh, rows, :]
            m = jnp.maximum(jnp.max(sc, axis=-1, keepdims=True), sink)
            p = jnp.exp(sc - m)
            denom = jnp.sum(p, axis=-1, keepdims=True) + jnp.exp(sink - m)
            patt_scr[kvh, rows, :] = (p * (1.0 / denom)).astype(BF16)

    for h in range(RET_HEADS):
        rows = slice(h * dk, (h + 1) * dk)
        v = proj_ref[:, COL_RV + h * dk:COL_RV + (h + 1) * dk]
        g = proj_ref[:, COL_RG + h * dk:COL_RG + (h + 1) * dk]
        rhs = jnp.concatenate([v, sf_ref[0, 0, rows, :], sb_ref[0, 0, rows, :]], axis=0).astype(BF16)
        o = _dot(pret_scr[h], rhs)
        mu = jnp.mean(o, axis=-1, keepdims=True)
        oc = o - mu
        var = jnp.mean(oc * oc, axis=-1, keepdims=True)
        mix_scr[:, h * dk:(h + 1) * dk] = (g * jax.nn.sigmoid(g)) * (oc * lax.rsqrt(var + LN_EPS))
    att_base = RET_WIDTH + SSM_WIDTH
    for kvh in range(ATT_KV_HEADS):
        o = _dot(patt_scr[kvh], vbs[kvh])
        for gq in range(ATT_GQ):
            hh = kvh * ATT_GQ + gq
            mix_scr[:, att_base + hh * hd:att_base + (hh + 1) * hd] = o[gq * blk:(gq + 1) * blk]

    w_ssm = SSM_WIDTH
    j = (c * nb + b) % SUBLANES
    yraw = yt_ref[:, pl.ds(j, 1), :].reshape(w_ssm, ch).T
    y = yraw + d_ref[0] * proj_ref[:, COL_SU:COL_SU + w_ssm]
    y = _gelu_tanh(y)
    gate = jax.nn.sigmoid(_dot(y.astype(BF16), wglu_ref[0]) + bglu_ref[0])
    mix_scr[:, RET_WIDTH:RET_WIDTH + w_ssm] = y * gate

    mix = _dot(mix_scr[...].astype(BF16), wout_ref[0])
    h1 = _layer_norm(DEEPNORM_ALPHA * h_ref[...] + mix, lng_ref[0], lnb_ref[0])
    h1_ref[...] = h1

    lt = lax.dot_general(rwt_ref[0], h1, (((1,), (1,)), ((), ())), precision=hp, preferred_element_type=F32)
    lt = lt - jnp.max(lt, axis=0, keepdims=True)
    et = jnp.exp(lt)
    afft_ref[0, 0] = et / jnp.sum(et, axis=0, keepdims=True)


def _mixer(proj, sf, sb, yt, h2d, bias, lg, sink, wout_bf, wglu_bf, d3, bglu3, lng3, lnb3, rwt, l, bsz, seq):
    nc = seq // ATT_BLOCK
    assert nc >= 2
    t, dm = h2d.shape
    ne = rwt.shape[1]
    kvw = 2 * ATT_KV_HEADS * ATT_HEAD_DIM
    sw = RET_WIDTH
    kvcol = COL_AK // kvw
    blk = ATT_BLOCK

    def layer(a):
        return pl.BlockSpec((1,) + a.shape[1:], lambda c, b: (l,) + (0,) * (a.ndim - 1))
    smem = pl.BlockSpec(memory_space=pltpu.SMEM)
    in_specs = [
        smem, smem,
        pl.BlockSpec((blk, D_IN), lambda c, b: (b * nc + c, 0)),
        pl.BlockSpec((blk, kvw), lambda c, b: (b * nc + jnp.maximum(c - 1, 0), kvcol)),
        pl.BlockSpec((blk, kvw), lambda c, b: (b * nc + jnp.minimum(c + 1, nc - 1), kvcol)),
        pl.BlockSpec((1, 1, sw, RET_DK), lambda c, b: (b, c, 0, 0)),
        pl.BlockSpec((1, 1, sw, RET_DK), lambda c, b: (b, c, 0, 0)),
        pl.BlockSpec((SSM_WIDTH, SUBLANES, CHUNK), lambda c, b: (0, (c * bsz + b) // SUBLANES, 0)),
        pl.BlockSpec((blk, dm), lambda c, b: (b * nc + c, 0)),
        pl.BlockSpec((1,) + bias.shape[1:], lambda c, b: (jnp.where(c == 0, 0, jnp.where(c == nc - 1, 2, 1)), 0, 0, 0)),
        layer(wout_bf), layer(wglu_bf), layer(d3), layer(bglu3), layer(lng3), layer(lnb3), layer(rwt),
    ]
    out_specs = [
        pl.BlockSpec((blk, dm), lambda c, b: (b * nc + c, 0)),
        pl.BlockSpec((1, 1, ne, blk), lambda c, b: (b, c, 0, 0)),
    ]
    out_shape = [jax.ShapeDtypeStruct((t, dm), F32),
                 jax.ShapeDtypeStruct((bsz, nc, ne, blk), F32)]
    scratch = [pltpu.VMEM((blk, dm), F32),
               pltpu.VMEM((RET_HEADS, blk, blk), F32),
               pltpu.VMEM((ATT_KV_HEADS, ATT_GQ * blk, 3 * blk), F32),
               pltpu.VMEM((RET_HEADS, blk, blk + 2 * RET_DK), BF16),
               pltpu.VMEM((ATT_KV_HEADS, ATT_GQ * blk, 3 * blk), BF16)]
    return pl.pallas_call(
        functools.partial(_mixer_kernel, l=l, nb=bsz), grid=(nc, bsz),
        in_specs=in_specs, out_specs=out_specs, out_shape=out_shape, scratch_shapes=scratch,
        compiler_params=_cparams(("parallel", "parallel")),
    )(lg, sink, proj, proj, proj, sf, sb, yt, h2d, bias, wout_bf, wglu_bf, d3, bglu3, lng3, lnb3, rwt)


def _route_kernel(a_ref, slot_ref, start_ref, extent_ref, *, nc, ne, cap):
    blk = ATT_BLOCK
    a = a_ref[0]

    def chunk(x, c):
        return x[c * ne:(c + 1) * ne]

    def count(pred):
        x = pred.astype(jnp.int32)
        tot = chunk(x, 0)
        for c in range(1, nc):
            tot = tot + chunk(x, c)
        return jnp.sum(tot, axis=1, keepdims=True)

    def tile_e(v):
        return jnp.concatenate([v] * nc, axis=0)

    tau = jnp.zeros((ne, 1), jnp.int32)
    for bit in range(30, -1, -1):
        cand = tau | (1 << bit)
        tau = jnp.where(count(a >= tile_e(pltpu.bitcast(cand, F32))) >= cap, cand, tau)
    lo = pltpu.bitcast(tau, F32)
    hi = pltpu.bitcast(tau + 1, F32)
    for _ in range(ROUTE_REFINE_STEPS):
        mid = lo + (hi - lo) * 0.5
        ok = count(a >= tile_e(mid)) >= cap
        lo = jnp.where(ok, mid, lo)
        hi = jnp.where(ok, hi, mid)
    gt = a >= tile_e(hi)
    eq = (a >= tile_e(lo)) & jnp.logical_not(gt)
    need = cap - count(gt)

    tri = (lax.broadcasted_iota(jnp.int32, (blk, blk), 0) <= lax.broadcasted_iota(jnp.int32, (blk, blk), 1)).astype(BF16)

    def ranks(mask):
        mf = mask.astype(F32)
        incl = _dot(mf.astype(BF16), tri)
        run = jnp.zeros((ne, 1), F32)
        offs = []
        tots = []
        for c in range(nc):
            offs.append(run)
            tots.append(chunk(incl, c)[:, blk - 1:blk])
            run = run + tots[-1]
        return incl - mf, jnp.concatenate(offs, axis=0), jnp.concatenate(tots, axis=0)

    eq_local, eq_off, _ = ranks(eq)
    mask = gt | (eq & ((eq_local + eq_off) < tile_e(need).astype(F32)))
    local, off, tot = ranks(mask)
    off_i = off.astype(jnp.int32)
    start = (off_i // SLOT_ALIGN) * SLOT_ALIGN
    slot = (off_i - start) + local.astype(jnp.int32)
    slot_ref[0] = jnp.where(mask, slot, -1)
    start_ref[0] = jnp.broadcast_to(start, (nc * ne, blk))
    extent_ref[0] = jnp.broadcast_to((off_i - start) + tot.astype(jnp.int32), (nc * ne, blk))


def _route(afft, cap):
    bsz, nc, ne, blk = afft.shape
    a2 = afft.reshape(bsz, nc * ne, blk)
    spec = pl.BlockSpec((1, nc * ne, blk), lambda b: (b, 0, 0))
    slot, start, extent = pl.pallas_call(
        functools.partial(_route_kernel, nc=nc, ne=ne, cap=cap), grid=(bsz,),
        in_specs=[spec], out_specs=[spec, spec, spec],
        out_shape=[jax.ShapeDtypeStruct(a2.shape, jnp.int32)] * 3,
        compiler_params=_cparams(("parallel",)),
    )(a2)
    flat = lambda v: v[:, :, 0].reshape(bsz * nc * ne)
    return slot.reshape(bsz, nc, ne, blk), flat(start), flat(extent)


def _max_extent(we_ref, base, n):
    mx = we_ref[base]
    for j in range(1, n):
        mx = jnp.maximum(mx, we_ref[base + j])
    return mx


def _dispatch_kernel(ws_ref, we_ref, h_ref, afft_ref, slot_ref, xs_ref, *, nc, ne, eg):
    b = pl.program_id(0)
    g = pl.program_id(1)
    c = pl.program_id(2)
    dm = h_ref.shape[1]
    base = (b * nc + c) * ne + g * eg

    @pl.when(c == 0)
    def _():
        xs_ref[...] = jnp.zeros_like(xs_ref)

    hb = h_ref[...].astype(BF16)
    a = afft_ref[0, 0]
    a_hi = a.astype(BF16)
    r1 = a - a_hi.astype(F32)
    a_mid = r1.astype(BF16)
    a_lo = (r1 - a_mid.astype(F32)).astype(BF16)
    a3 = jnp.concatenate([a_hi, a_mid, a_lo, jnp.zeros((GATE_LANES - 3 * ne, ATT_BLOCK), BF16)], axis=0)

    def scatter_rows(win):
        riota = lax.broadcasted_iota(jnp.int32, (win, ATT_BLOCK), 0)
        onehots = jnp.concatenate([(riota == slot_ref[0, 0, 0, j:j + 1, :]).astype(BF16) for j in range(eg)], axis=0)
        res = _dot(onehots, hb)
        resg = _dot_nt(onehots, a3)
        for j in range(eg):
            w = pl.multiple_of(ws_ref[base + j], SLOT_ALIGN)
            head = pl.ds(w, SLOT_ALIGN)
            tail = pl.ds(w + SLOT_ALIGN, win - SLOT_ALIGN)
            for cols, r in ((slice(0, dm), res), (slice(dm, dm + GATE_LANES), resg)):
                rj = r[j * win:(j + 1) * win]
                xs_ref[0, j, head, cols] = (xs_ref[0, j, head, cols].astype(F32) + rj[0:SLOT_ALIGN]).astype(BF16)
                xs_ref[0, j, tail, cols] = rj[SLOT_ALIGN:].astype(BF16)

    small = _max_extent(we_ref, base, eg) <= SMALL_WIN
    pl.when(small)(lambda: scatter_rows(SMALL_WIN))
    pl.when(jnp.logical_not(small))(lambda: scatter_rows(WIN))


def _dispatch(h1, afft, slot, wstart, wextent, cap, eg=8):
    bsz, nc, ne, blk = slot.shape
    t, dm = h1.shape
    cp = cap + WIN
    return pl.pallas_call(
        functools.partial(_dispatch_kernel, nc=nc, ne=ne, eg=eg),
        grid_spec=pltpu.PrefetchScalarGridSpec(
            num_scalar_prefetch=2, grid=(bsz, ne // eg, nc),
            in_specs=[pl.BlockSpec((blk, dm), lambda b, g, c, ws, we: (b * nc + c, 0)),
                      pl.BlockSpec((1, 1, ne, blk), lambda b, g, c, ws, we: (b, c, 0, 0)),
                      pl.BlockSpec((1, 1, 1, eg, blk), lambda b, g, c, ws, we: (b, c, g, 0, 0))],
            out_specs=pl.BlockSpec((1, eg, cp, dm + GATE_LANES), lambda b, g, c, ws, we: (b, g, 0, 0))),
        out_shape=jax.ShapeDtypeStruct((bsz, ne, cp, dm + GATE_LANES), BF16),
        compiler_params=_cparams(("parallel", "parallel", "arbitrary")),
    )(wstart, wextent, h1, afft, slot.reshape(bsz, nc, ne // eg, eg, blk))


def _ffn_kernel(xs_ref, wg_ref, wu_ref, wd_ref, out_ref, acc_ref, *, cap, nf, ne):
    e = pl.program_id(0)
    f = pl.program_id(1)
    bsz = xs_ref.shape[0]
    dm = wd_ref.shape[3]
    wg = wg_ref[0, 0].astype(BF16)
    wu = wu_ref[0, 0].astype(BF16)
    wd = wd_ref[0, 0].astype(BF16)
    for b in range(bsz):
        x = xs_ref[b, 0, :, 0:dm]
        hg = _dot(x, wg)
        hu = _dot(x, wu)
        hdn = ((hg * jax.nn.sigmoid(hg)) * hu).astype(BF16)
        contrib = _dot(hdn, wd)
        rows = slice(b * cap, (b + 1) * cap)

        @pl.when(f == 0)
        def _():
            acc_ref[rows, :] = contrib

        @pl.when(f > 0)
        def _():
            acc_ref[rows, :] = acc_ref[rows, :] + contrib

    @pl.when(f == nf - 1)
    def _():
        cp = out_ref.shape[2]
        lane = lax.broadcasted_iota(jnp.int32, (cap, GATE_LANES), 1)
        mine = ((lane % ne) == e) & (lane < 3 * ne)
        for b in range(bsz):
            pieces = xs_ref[b, 0, :, dm:dm + GATE_LANES].astype(F32)
            gate = jnp.sum(jnp.where(mine, pieces, 0.0), axis=1, keepdims=True)
            out_ref[b, 0, 0:cap, :] = (acc_ref[b * cap:(b + 1) * cap, :] * gate).astype(BF16)
            out_ref[b, 0, cap:cp, :] = jnp.zeros((cp - cap, dm), BF16)


def _ffn(xs, wg, wu, wd, l, cap, tf=512):
    bsz, ne, cp, dx = xs.shape
    dm = wg.shape[2]
    ff = wg.shape[-1]
    nf = ff // tf
    return pl.pallas_call(
        functools.partial(_ffn_kernel, cap=cap, nf=nf, ne=ne), grid=(ne, nf),
        in_specs=[pl.BlockSpec((bsz, 1, cap, dx), lambda e, f: (0, e, 0, 0)),
                  pl.BlockSpec((1, 1, dm, tf), lambda e, f: (l, e, 0, f)),
                  pl.BlockSpec((1, 1, dm, tf), lambda e, f: (l, e, 0, f)),
                  pl.BlockSpec((1, 1, tf, dm), lambda e, f: (l, e, f, 0))],
        out_specs=pl.BlockSpec((bsz, 1, cp, dm), lambda e, f: (0, e, 0, 0)),
        out_shape=jax.ShapeDtypeStruct((bsz, ne, cp, dm), BF16),
        scratch_shapes=[pltpu.VMEM((bsz * cap, dm), F32)],
        compiler_params=_cparams(("parallel", "arbitrary")),
    )(xs, wg, wu, wd)


def _combine_kernel(ws_ref, we_ref, out_ref, slott_ref, h1_ref, lng_ref, lnb_ref, h2_ref, *, nc, ne):
    b = pl.program_id(0)
    c = pl.program_id(1)
    blk = ATT_BLOCK
    base = (b * nc + c) * ne
    slot_t = slott_ref[0, 0]

    def gather_rows(win):
        liota = lax.broadcasted_iota(jnp.int32, (blk, win), 1)
        tot = None
        for e in range(ne):
            w = pl.multiple_of(ws_ref[base + e], SLOT_ALIGN)
            onehot = (liota == slot_t[:, e:e + 1]).astype(BF16)
            y = _dot(onehot, out_ref[0, e, pl.ds(w, win), :])
            tot = y if tot is None else tot + y
        h2_ref[...] = _layer_norm(DEEPNORM_ALPHA * h1_ref[...] + tot, lng_ref[0], lnb_ref[0])

    small = _max_extent(we_ref, base, ne) <= SMALL_WIN
    pl.when(small)(lambda: gather_rows(SMALL_WIN))
    pl.when(jnp.logical_not(small))(lambda: gather_rows(WIN))


def _combine(out, slot, wstart, wextent, h1, lng3, lnb3, l, seq):
    bsz, ne, cp, dm = out.shape
    nc = seq // ATT_BLOCK
    blk = ATT_BLOCK
    t = h1.shape[0]
    slot_t = jnp.swapaxes(slot, 2, 3)
    vec = pl.BlockSpec((1, 1, dm), lambda b, c, ws, we: (l, 0, 0))
    return pl.pallas_call(
        functools.partial(_combine_kernel, nc=nc, ne=ne),
        grid_spec=pltpu.PrefetchScalarGridSpec(
            num_scalar_prefetch=2, grid=(bsz, nc),
            in_specs=[pl.BlockSpec((1, ne, cp, dm), lambda b, c, ws, we: (b, 0, 0, 0), pipeline_mode=pl.Buffered(1)),
                      pl.BlockSpec((1, 1, blk, ne), lambda b, c, ws, we: (b, c, 0, 0)),
                      pl.BlockSpec((blk, dm), lambda b, c, ws, we: (b * nc + c, 0)),
                      vec, vec],
            out_specs=pl.BlockSpec((blk, dm), lambda b, c, ws, we: (b * nc + c, 0))),
        out_shape=jax.ShapeDtypeStruct((t, dm), F32),
        compiler_params=_cparams(("parallel", "arbitrary")),
    )(wstart, wextent, out, slot_t, h1, lng3, lnb3)


def kernel(x, ln_in_g, ln_in_b, w_in, ret_theta, ssm_lambda_re, ssm_lambda_im, ssm_log_step, ssm_b_re, ssm_b_im,
           ssm_c_re, ssm_c_im, ssm_d, ssm_w_glu, ssm_b_glu, attn_sink, w_out, ln1_g, ln1_b, router_w,
           exp_w_gate, exp_w_up, exp_w_down, ln2_g, ln2_b):
    bsz, seq, dm = x.shape
    depth = w_in.shape[0]
    ne = router_w.shape[-1]
    cap = EC_FACTOR * seq // ne
    t = bsz * seq
    w_in_bf = w_in.astype(BF16)
    wt_bf = jnp.swapaxes(jnp.concatenate([w_in[:, :, COL_SU:COL_SU + SSM_WIDTH], w_in[:, :, COL_RK:COL_RK + RET_WIDTH]],
                                         axis=2), 1, 2).astype(BF16)
    w_out_bf = w_out.astype(BF16)
    w_glu_bf = ssm_w_glu.astype(BF16)
    lg = jax.nn.log_sigmoid(ret_theta.astype(F32))
    sink = attn_sink.astype(F32)
    ssm_par = _ssm_params(ssm_lambda_re, ssm_lambda_im, ssm_log_step, ssm_b_re, ssm_b_im, ssm_c_re, ssm_c_im)
    vec3 = lambda a: a.astype(F32).reshape(depth, 1, -1)
    d3, bglu3, ln1g3, ln1b3, ln2g3, ln2b3 = map(vec3, (ssm_d, ssm_b_glu, ln1_g, ln1_b, ln2_g, ln2_b))
    rwt = jnp.swapaxes(router_w.astype(F32), 1, 2)
    bias = _attention_bias(seq)

    h = x
    for l in range(depth):
        if l == 0:
            h, proj, ut, kt = _inproj(h, w_in_bf, wt_bf, l, ln=(ln_in_g, ln_in_b))
        else:
            proj, ut, kt = _inproj(h, w_in_bf, wt_bf, l)
        h2d = h.reshape(t, dm)
        proj = proj.reshape(t, D_IN)
        yt = _ssm_conv(ut, ssm_par, l, bsz)
        sf, sb = _ret_states(proj, kt, lg, l, bsz, seq)
        h1, afft = _mixer(proj, sf, sb, yt, h2d, bias, lg, sink, w_out_bf, w_glu_bf, d3, bglu3, ln1g3, ln1b3,
                          rwt, l, bsz, seq)
        slot, wstart, wextent = _route(afft, cap)
        xs = _dispatch(h1, afft, slot, wstart, wextent, cap)
        out = _ffn(xs, exp_w_gate, exp_w_up, exp_w_down, l, cap)
        h = _combine(out, slot, wstart, wextent, h1, ln2g3, ln2b3, l, seq).reshape(bsz, seq, dm)
    return h
```

```python
import functools
import math

import jax
import jax.numpy as jnp
from jax import lax
from jax.experimental import pallas as pl
from jax.experimental.pallas import tpu as pltpu

F32 = jnp.float32
BF16 = jnp.bfloat16

RET_HEADS = 4
RET_DK = 64
RET_CHUNK = 128
RET_WIDTH = RET_HEADS * RET_DK
SSM_CPG = 16
SSM_GROUPS = 16
SSM_STATE = 64
SSM_WIDTH = SSM_CPG * SSM_GROUPS
ATT_HEADS = 8
ATT_KV_HEADS = 2
ATT_GQ = ATT_HEADS // ATT_KV_HEADS
ATT_HEAD_DIM = 64
ATT_BLOCK = 128
ATT_WINDOW = 128
EC_FACTOR = 2
DEPTH = 2
DEEPNORM_ALPHA = (2.0 * DEPTH) ** 0.25
LN_EPS = 1e-5
NEG_INF = -1e30

COL_RQ, COL_RK, COL_RV, COL_RG, COL_SU, COL_AQ, COL_AK, COL_AV = 0, 256, 512, 768, 1024, 1280, 1792, 1920
D_IN = 2048

CHUNK = 128
SUBLANES = 8
SLOT_ALIGN = 16
WIN = CHUNK + SLOT_ALIGN
LANES = 128
SMALL_WIN = LANES // 2
DISPATCH_STEP_CHUNKS = 4
COMBINE_STEP_CHUNKS = 2
GATE_LANES = 128
RET_STEP_CHUNKS = 8
VMEM_LIMIT = 56 * 1024 * 1024
ROUTE_REFINE_STEPS = 10


def _cparams(sem):
    return pltpu.CompilerParams(dimension_semantics=sem, vmem_limit_bytes=VMEM_LIMIT)


def _layer_norm(x, g, b):
    mu = jnp.mean(x, axis=-1, keepdims=True)
    xc = x - mu
    var = jnp.mean(xc * xc, axis=-1, keepdims=True)
    return xc * lax.rsqrt(var + LN_EPS) * g + b


def _dot(a, b):
    return jnp.dot(a, b, preferred_element_type=F32)


def _dot_hp(a, b):
    return jnp.dot(a, b, precision=lax.Precision.HIGHEST, preferred_element_type=F32)


def _dot_nt(a, b):
    return lax.dot_general(a, b, (((1,), (1,)), ((), ())), preferred_element_type=F32)


def _cmul(ar, ai, br, bi):
    return ar * br - ai * bi, ar * bi + ai * br


def _cpow(br, bi, expo, nbits):
    shape = (br.shape[0], expo.shape[1])
    rr = jnp.ones(shape, F32)
    ri = jnp.zeros(shape, F32)
    for j in range(nbits):
        bit = jnp.broadcast_to(((expo >> j) & 1) == 1, shape)
        nr, ni = _cmul(rr, ri, br, bi)
        rr = jnp.where(bit, nr, rr)
        ri = jnp.where(bit, ni, ri)
        if j + 1 < nbits:
            br, bi = _cmul(br, bi, br, bi)
    return rr, ri


def _inproj_kernel(*refs, apply_ln):
    if apply_ln:
        x_ref, g_ref, b_ref, w_ref, wt_ref, h_ref, proj_ref, ut_ref, kt_ref = refs
    else:
        x_ref, w_ref, wt_ref, proj_ref, ut_ref, kt_ref = refs
    nb, tl, d = x_ref.shape
    h = x_ref[...].reshape(nb * tl, d)
    if apply_ln:
        h = _layer_norm(h, g_ref[...], b_ref[...])
        h_ref[...] = h.reshape(nb, tl, d)
    hb = h.astype(BF16)
    proj_ref[...] = _dot(hb, w_ref[0]).reshape(nb, tl, -1)
    tt = _dot_nt(wt_ref[0], hb)
    k = tl // CHUNK
    for j in range(k):
        for b in range(nb):
            ut_ref[:, j * nb + b, :] = tt[0:SSM_WIDTH, (b * k + j) * CHUNK:(b * k + j + 1) * CHUNK]
    for b in range(nb):
        kt_ref[b] = tt[SSM_WIDTH:SSM_WIDTH + RET_WIDTH, b * tl:(b + 1) * tl]


def _inproj(x3, w_bf, wt_bf, l, ln=None):
    bsz, seq, d = x3.shape
    n = w_bf.shape[2]
    k = SUBLANES // bsz
    tl = k * CHUNK
    nc = seq // CHUNK
    xspec = pl.BlockSpec((bsz, tl, d), lambda i: (0, i, 0))
    wspec = pl.BlockSpec((1, d, n), lambda i: (l, 0, 0))
    wtspec = pl.BlockSpec((1,) + wt_bf.shape[1:], lambda i: (l, 0, 0))
    out_specs = [pl.BlockSpec((bsz, tl, n), lambda i: (0, i, 0)),
                 pl.BlockSpec((SSM_WIDTH, k * bsz, CHUNK), lambda i: (0, i, 0)),
                 pl.BlockSpec((bsz, RET_WIDTH, tl), lambda i: (0, 0, i))]
    out_shape = [jax.ShapeDtypeStruct((bsz, seq, n), F32),
                 jax.ShapeDtypeStruct((SSM_WIDTH, nc * bsz, CHUNK), F32),
                 jax.ShapeDtypeStruct((bsz, RET_WIDTH, seq), F32)]
    if ln is None:
        return pl.pallas_call(
            functools.partial(_inproj_kernel, apply_ln=False), grid=(seq // tl,),
            in_specs=[xspec, wspec, wtspec], out_specs=out_specs, out_shape=out_shape,
            compiler_params=_cparams(("parallel",)),
        )(x3, w_bf, wt_bf)
    g, b = ln
    vec = pl.BlockSpec((1, d), lambda i: (0, 0))
    return pl.pallas_call(
        functools.partial(_inproj_kernel, apply_ln=True), grid=(seq // tl,),
        in_specs=[xspec, vec, vec, wspec, wtspec], out_specs=[xspec] + out_specs,
        out_shape=[jax.ShapeDtypeStruct(x3.shape, F32)] + out_shape,
        compiler_params=_cparams(("parallel",)),
    )(x3, g.reshape(1, d), b.reshape(1, d), w_bf, wt_bf)


def _ssm_params(lam_re, lam_im, log_step, b_re, b_im, c_re, c_im):
    lr, li = lam_re.astype(F32), lam_im.astype(F32)
    step = jnp.exp(log_step.astype(F32))[..., None]
    er = jnp.exp(lr * step)
    lbr, lbi = er * jnp.cos(li * step), er * jnp.sin(li * step)
    den = lr * lr + li * li
    fr = ((lbr - 1.0) * lr + lbi * li) / den
    fi = (lbi * lr - (lbr - 1.0) * li) / den
    br = jnp.swapaxes(b_re.astype(F32), -1, -2)[:, None]
    bi = jnp.swapaxes(b_im.astype(F32), -1, -2)[:, None]
    bbr = fr[..., None, :] * br - fi[..., None, :] * bi
    bbi = fr[..., None, :] * bi + fi[..., None, :] * br
    lcol = jnp.stack([lbr[:, 0], lbi[:, 0], lbr[:, 1], lbi[:, 1]], axis=-1)
    lrow = jnp.swapaxes(lcol, -1, -2)
    lrow = jnp.concatenate([lrow, jnp.zeros_like(lrow)], axis=2)
    bt = jnp.stack([bbr[:, 0], bbi[:, 0], bbr[:, 1], bbi[:, 1]], axis=2)
    cr, ci = c_re.astype(F32), c_im.astype(F32)
    c4 = jnp.stack([cr[:, 0], ci[:, 0], cr[:, 1], ci[:, 1]], axis=2)
    return lcol, lrow, bt, jnp.swapaxes(bt, -1, -2), c4, jnp.swapaxes(c4, -1, -2)


def _ssm_kernel(ut_ref, lcol_ref, lrow_ref, bt_ref, bcol_ref, c_ref, ct_ref, y_ref,
                w_scr, wst_scr, g_scr, vf_scr, vb_scr, acc_scr, s_scr, sp_scr, x_scr, *, nb, nchunks):
    tc, p, cpg = CHUNK, SSM_STATE, SSM_CPG
    lcol = lcol_ref[0, 0]
    lf = (lcol[:, 0:1], lcol[:, 1:2])
    lb = (lcol[:, 2:3], lcol[:, 3:4])

    m_row = lax.broadcasted_iota(jnp.int32, (1, tc), 1)
    pfr, pfi = _cpow(*lf, m_row, 7)
    pf1r, pf1i = _cmul(pfr, pfi, *lf)
    prr, pri = _cpow(*lf, tc - 1 - m_row, 7)
    pbr, pbi = _cpow(*lb, tc - m_row, 8)
    pqr, pqi = _cpow(*lb, m_row, 7)

    def rep_co(x):
        return jnp.broadcast_to(x[:, None, :], (cpg, cpg, p)).reshape(cpg * cpg, p)

    def rep_ci(x):
        return jnp.broadcast_to(x[None, :, :], (cpg, cpg, p)).reshape(cpg * cpg, p)

    cbfr, cbfi = _cmul(rep_co(c_ref[0, 0, 0]), rep_co(c_ref[0, 0, 1]), rep_ci(bt_ref[0, 0, 0]), rep_ci(bt_ref[0, 0, 1]))
    cbbr, cbbi = _cmul(rep_co(c_ref[0, 0, 2]), rep_co(c_ref[0, 0, 3]), rep_ci(bt_ref[0, 0, 2]), rep_ci(bt_ref[0, 0, 3]))
    lane = lax.broadcasted_iota(jnp.int32, (cpg * cpg, tc), 1)
    kb0 = jnp.sum(cbbr, axis=1, keepdims=True)
    vf_scr[...] = _dot_hp(cbfr, pfr) - _dot_hp(cbfi, pfi) + jnp.where(lane == 0, kb0, 0.0)
    vb_scr[...] = _dot_hp(cbbr, pbr) - _dot_hp(cbbi, pbi)

    for co in range(cpg):
        cols = slice(co * tc, (co + 1) * tc)
        for r, (ar, ai) in enumerate(((pf1r, pf1i), (pbr, pbi))):
            gr, gi = _cmul(ct_ref[0, 0, 2 * r][:, co:co + 1], ct_ref[0, 0, 2 * r + 1][:, co:co + 1], ar, ai)
            g_scr[2 * r * p:(2 * r + 1) * p, cols] = gr.astype(BF16)
            g_scr[(2 * r + 1) * p:(2 * r + 2) * p, cols] = (-gi).astype(BF16)
        for r, (ar, ai) in enumerate(((prr, pri), (pqr, pqi))):
            sr, si = _cmul(bcol_ref[0, 0, 2 * r][:, co:co + 1], bcol_ref[0, 0, 2 * r + 1][:, co:co + 1], ar, ai)
            wst_scr[co, 2 * r * p:(2 * r + 1) * p, :] = sr.astype(BF16)
            wst_scr[co, (2 * r + 1) * p:(2 * r + 2) * p, :] = si.astype(BF16)

    s_idx = lax.broadcasted_iota(jnp.int32, (tc, tc), 0)
    j_idx = lax.broadcasted_iota(jnp.int32, (tc, tc), 1)
    fwd_part = j_idx < tc - s_idx
    acc_scr[...] = jnp.zeros_like(acc_scr)
    s_scr[...] = jnp.zeros_like(s_scr)

    def per_channel(ci, carry):
        for co in range(cpg):
            rf = jnp.broadcast_to(vf_scr[pl.ds(co * cpg + ci, 1), :], (tc, tc))
            rb = jnp.broadcast_to(vb_scr[pl.ds(co * cpg + ci, 1), :], (tc, tc))
            m = pltpu.roll(jnp.where(fwd_part, rf, rb), 0, 1, stride=1, stride_axis=0)
            w_scr[ci, :, co * tc:(co + 1) * tc] = m.astype(BF16)
        u = ut_ref[ci].astype(BF16)
        acc_scr[...] += _dot(u, w_scr[ci])
        s_scr[...] += _dot_nt(u, wst_scr[ci])
        return carry

    lax.fori_loop(0, cpg, per_channel, 0)

    lrow = lrow_ref[0, 0]
    dec = []
    for r in range(2):
        dr, di = lrow[2 * r:2 * r + 1], lrow[2 * r + 1:2 * r + 2]
        for _ in range(7):
            dr, di = _cmul(dr, di, dr, di)
        dec.append((dr, di))
    for q in range(4):
        sp_scr[q] = s_scr[:, q * p:(q + 1) * p]
    xfr = xfi = xbr = xbi = jnp.zeros((nb, p), F32)
    for i in range(nchunks):
        rf = slice(i * nb, (i + 1) * nb)
        rb = slice((nchunks - 1 - i) * nb, (nchunks - i) * nb)
        x_scr[0, rf, :] = xfr
        x_scr[1, rf, :] = xfi
        x_scr[2, rb, :] = xbr
        x_scr[3, rb, :] = xbi
        xfr, xfi = _cmul(dec[0][0], dec[0][1], xfr, xfi)
        xbr, xbi = _cmul(dec[1][0], dec[1][1], xbr, xbi)
        xfr, xfi = xfr + sp_scr[0, rf, :], xfi + sp_scr[1, rf, :]
        xbr, xbi = xbr + sp_scr[2, rb, :], xbi + sp_scr[3, rb, :]
    xin = jnp.concatenate([x_scr[q] for q in range(4)], axis=1).astype(BF16)
    y = acc_scr[...] + _dot(xin, g_scr[...])
    for co in range(cpg):
        y_ref[co] = y[:, co * tc:(co + 1) * tc]


def _ssm_conv(ut, params, l, bsz):
    _, r, tc = ut.shape
    cpg, p, ng = SSM_CPG, SSM_STATE, SSM_GROUPS

    def pspec(a):
        return pl.BlockSpec((1, 1) + a.shape[2:], lambda g: (l, g) + (0,) * (a.ndim - 2))
    return pl.pallas_call(
        functools.partial(_ssm_kernel, nb=bsz, nchunks=r // bsz), grid=(ng,),
        in_specs=[pl.BlockSpec((cpg, r, tc), lambda g: (g, 0, 0))] + [pspec(a) for a in params],
        out_specs=pl.BlockSpec((cpg, r, tc), lambda g: (g, 0, 0)),
        out_shape=jax.ShapeDtypeStruct(ut.shape, F32),
        scratch_shapes=[pltpu.VMEM((cpg, tc, cpg * tc), BF16), pltpu.VMEM((cpg, 4 * p, tc), BF16),
                        pltpu.VMEM((4 * p, cpg * tc), BF16),
                        pltpu.VMEM((cpg * cpg, tc), F32), pltpu.VMEM((cpg * cpg, tc), F32),
                        pltpu.VMEM((r, cpg * tc), F32), pltpu.VMEM((r, 4 * p), F32),
                        pltpu.VMEM((4, r, p), F32), pltpu.VMEM((4, r, p), F32)],
        compiler_params=_cparams(("parallel",)),
    )(ut, *params)


def _retstate_kernel(lg_ref, ktf_ref, vf_ref, ktb_ref, vb_ref, sf_ref, sb_ref, accf, accb, *, l, cs):
    i = pl.program_id(1)

    @pl.when(i == 0)
    def _():
        accf[...] = jnp.zeros_like(accf)
        accb[...] = jnp.zeros_like(accb)

    ch = RET_CHUNK
    pos = lax.broadcasted_iota(jnp.int32, (1, ch), 1).astype(F32)
    one = jnp.ones((1, RET_DK), F32)
    wts = []
    for h in range(RET_HEADS):
        lgf = lg_ref[l, 0, h]
        lgb = lg_ref[l, 1, h]
        wts.append((jnp.exp(lgf * (ch - 1.0 - pos)) * RET_DK ** -0.5, jnp.exp(lgb * pos) * RET_DK ** -0.5,
                    jnp.exp(one * (lgf * ch)), jnp.exp(one * (lgb * ch))))
    for j in range(cs):
        jb = cs - 1 - j
        sf_ref[0, j] = accf[...]
        sb_ref[0, jb] = accb[...]
        for h in range(RET_HEADS):
            rows = slice(h * RET_DK, (h + 1) * RET_DK)
            wf, wb, decf, decb = wts[h]
            kf = (ktf_ref[0, rows, j * ch:(j + 1) * ch] * wf).astype(BF16)
            kb = (ktb_ref[0, rows, jb * ch:(jb + 1) * ch] * wb).astype(BF16)
            accf[rows, :] = decf * accf[rows, :] + _dot(kf, vf_ref[j * ch:(j + 1) * ch, rows].astype(BF16))
            accb[rows, :] = decb * accb[rows, :] + _dot(kb, vb_ref[jb * ch:(jb + 1) * ch, rows].astype(BF16))


def _ret_states(proj, kt, lg, l, bsz, seq):
    nc = seq // RET_CHUNK
    cs = min(RET_STEP_CHUNKS, nc)
    ns = nc // cs
    w = RET_WIDTH
    tl = cs * RET_CHUNK
    vcol = COL_RV // w
    st = jax.ShapeDtypeStruct((bsz, nc, w, RET_DK), F32)
    return pl.pallas_call(
        functools.partial(_retstate_kernel, l=l, cs=cs), grid=(bsz, ns),
        in_specs=[pl.BlockSpec(memory_space=pltpu.SMEM),
                  pl.BlockSpec((1, w, tl), lambda b, i: (b, 0, i)),
                  pl.BlockSpec((tl, w), lambda b, i: (b * ns + i, vcol)),
                  pl.BlockSpec((1, w, tl), lambda b, i: (b, 0, ns - 1 - i)),
                  pl.BlockSpec((tl, w), lambda b, i: (b * ns + ns - 1 - i, vcol))],
        out_specs=[pl.BlockSpec((1, cs, w, RET_DK), lambda b, i: (b, i, 0, 0)),
                   pl.BlockSpec((1, cs, w, RET_DK), lambda b, i: (b, ns - 1 - i, 0, 0))],
        out_shape=[st, st],
        scratch_shapes=[pltpu.VMEM((w, RET_DK), F32), pltpu.VMEM((w, RET_DK), F32)],
        compiler_params=_cparams(("parallel", "arbitrary")),
    )(lg, kt, proj, kt, proj)


def _gelu_tanh(x):
    return 0.5 * x * (1.0 + jnp.tanh(math.sqrt(2.0 / math.pi) * (x + 0.044715 * (x * x * x))))


def _attention_bias(seq):
    blk = ATT_BLOCK
    s_idx = jnp.arange(3 * blk)[None, :]
    t_idx = jnp.arange(blk)[:, None]
    arel = jnp.abs(s_idx - blk - t_idx)
    band = arel <= ATT_WINDOW
    slopes = jnp.exp2(-8.0 * jnp.arange(1, ATT_HEADS + 1, dtype=F32) / ATT_HEADS)
    alibi = -slopes[:, None, None] * arel.astype(F32)[None]
    variants = []
    for prev_ok, next_ok in ((False, True), (True, True), (True, False)):
        ok = band & (prev_ok | (s_idx >= blk)) & (next_ok | (s_idx < 2 * blk))
        variants.append(jnp.where(ok[None], alibi, NEG_INF).reshape(ATT_KV_HEADS, ATT_GQ * blk, 3 * blk))
    return jnp.stack(variants)


def _mixer_kernel(lg_ref, sink_ref, proj_ref, kvp_ref, kvn_ref, sf_ref, sb_ref, yt_ref, h_ref, bias_ref,
                  wout_ref, wglu_ref, d_ref, bglu_ref, lng_ref, lnb_ref, rwt_ref,
                  h1_ref, afft_ref, mix_scr, sret_scr, satt_scr, pret_scr, patt_scr, *, l, nb):
    c = pl.program_id(0)
    b = pl.program_id(1)
    ch = RET_CHUNK
    dk = RET_DK
    blk = ATT_BLOCK
    hd = ATT_HEAD_DIM
    kvw = ATT_KV_HEADS * hd
    hp = lax.Precision.HIGHEST

    for h in range(RET_HEADS):
        q = proj_ref[:, COL_RQ + h * dk:COL_RQ + (h + 1) * dk]
        k = proj_ref[:, COL_RK + h * dk:COL_RK + (h + 1) * dk] * dk ** -0.5
        sret_scr[h] = _dot_nt(q.astype(BF16), k.astype(BF16))
    vbs = []
    for kvh in range(ATT_KV_HEADS):
        kc = slice(kvh * hd, (kvh + 1) * hd)
        vc = slice(kvw + kvh * hd, kvw + (kvh + 1) * hd)
        kb = jnp.concatenate([kvp_ref[:, kc], proj_ref[:, COL_AK + kvh * hd:COL_AK + (kvh + 1) * hd],
                              kvn_ref[:, kc]], axis=0).astype(BF16)
        vbs.append(jnp.concatenate([kvp_ref[:, vc], proj_ref[:, COL_AV + kvh * hd:COL_AV + (kvh + 1) * hd],
                                    kvn_ref[:, vc]], axis=0).astype(BF16))
        q4 = jnp.concatenate([proj_ref[:, COL_AQ + (kvh * ATT_GQ + gq) * hd:COL_AQ + (kvh * ATT_GQ + gq + 1) * hd]
                              for gq in range(ATT_GQ)], axis=0) * hd ** -0.5
        satt_scr[kvh] = _dot_nt(q4.astype(BF16), kb)

    pos = lax.broadcasted_iota(jnp.int32, (ch, 1), 0).astype(F32)
    dist = lax.broadcasted_iota(jnp.int32, (ch, ch), 0) - lax.broadcasted_iota(jnp.int32, (ch, ch), 1)
    adist = jnp.abs(dist).astype(F32)
    for h in range(RET_HEADS):
        lgf = lg_ref[l, 0, h]
        lgb = lg_ref[l, 1, h]
        q = proj_ref[:, COL_RQ + h * dk:COL_RQ + (h + 1) * dk]
        dmat = jnp.exp(jnp.where(dist >= 0, lgf, lgb) * adist)
        pret_scr[h, :, 0:ch] = (sret_scr[h] * dmat).astype(BF16)
        pret_scr[h, :, ch:ch + dk] = (q * jnp.exp(lgf * (pos + 1.0))).astype(BF16)
        pret_scr[h, :, ch + dk:ch + 2 * dk] = (q * jnp.exp(lgb * (ch - pos))).astype(BF16)
    for kvh in range(ATT_KV_HEADS):
        for gq in range(ATT_GQ):
            rows = slice(gq * blk, (gq + 1) * blk)
            sink = sink_ref[l, kvh * ATT_GQ + gq]
            sc = satt_scr[kvh, rows, :] + bias_ref[0, kvh, rows, :]
            m = jnp.maximum(jnp.max(sc, axis=-1, keepdims=True), sink)
            p = jnp.exp(sc - m)
            denom = jnp.sum(p, axis=-1, keepdims=True) + jnp.exp(sink - m)
            patt_scr[kvh, rows, :] = (p * (1.0 / denom)).astype(BF16)

    for h in range(RET_HEADS):
        rows = slice(h * dk, (h + 1) * dk)
        v = proj_ref[:, COL_RV + h * dk:COL_RV + (h + 1) * dk]
        g = proj_ref[:, COL_RG + h * dk:COL_RG + (h + 1) * dk]
        rhs = jnp.concatenate([v, sf_ref[0, 0, rows, :], sb_ref[0, 0, rows, :]], axis=0).astype(BF16)
        o = _dot(pret_scr[h], rhs)
        mu = jnp.mean(o, axis=-1, keepdims=True)
        oc = o - mu
        var = jnp.mean(oc * oc, axis=-1, keepdims=True)
        mix_scr[:, h * dk:(h + 1) * dk] = (g * jax.nn.sigmoid(g)) * (oc * lax.rsqrt(var + LN_EPS))
    att_base = RET_WIDTH + SSM_WIDTH
    for kvh in range(ATT_KV_HEADS):
        o = _dot(patt_scr[kvh], vbs[kvh])
        for gq in range(ATT_GQ):
            hh = kvh * ATT_GQ + gq
            mix_scr[:, att_base + hh * hd:att_base + (hh + 1) * hd] = o[gq * blk:(gq + 1) * blk]

    w_ssm = SSM_WIDTH
    j = (c * nb + b) % SUBLANES
    yraw = yt_ref[:, pl.ds(j, 1), :].reshape(w_ssm, ch).T
    y = yraw + d_ref[0] * proj_ref[:, COL_SU:COL_SU + w_ssm]
    y = _gelu_tanh(y)
    gate = jax.nn.sigmoid(_dot(y.astype(BF16), wglu_ref[0]) + bglu_ref[0])
    mix_scr[:, RET_WIDTH:RET_WIDTH + w_ssm] = y * gate

    mix = _dot(mix_scr[...].astype(BF16), wout_ref[0])
    h1 = _layer_norm(DEEPNORM_ALPHA * h_ref[...] + mix, lng_ref[0], lnb_ref[0])
    h1_ref[...] = h1

    lt = lax.dot_general(rwt_ref[0], h1, (((1,), (1,)), ((), ())), precision=hp, preferred_element_type=F32)
    lt = lt - jnp.max(lt, axis=0, keepdims=True)
    et = jnp.exp(lt)
    afft_ref[0, 0] = et / jnp.sum(et, axis=0, keepdims=True)


def _mixer(proj, sf, sb, yt, h2d, bias, lg, sink, wout_bf, wglu_bf, d3, bglu3, lng3, lnb3, rwt, l, bsz, seq):
    nc = seq // ATT_BLOCK
    assert nc >= 2
    t, dm = h2d.shape
    ne = rwt.shape[1]
    kvw = 2 * ATT_KV_HEADS * ATT_HEAD_DIM
    sw = RET_WIDTH
    kvcol = COL_AK // kvw
    blk = ATT_BLOCK

    def layer(a):
        return pl.BlockSpec((1,) + a.shape[1:], lambda c, b: (l,) + (0,) * (a.ndim - 1))
    smem = pl.BlockSpec(memory_space=pltpu.SMEM)
    in_specs = [
        smem, smem,
        pl.BlockSpec((blk, D_IN), lambda c, b: (b * nc + c, 0)),
        pl.BlockSpec((blk, kvw), lambda c, b: (b * nc + jnp.maximum(c - 1, 0), kvcol)),
        pl.BlockSpec((blk, kvw), lambda c, b: (b * nc + jnp.minimum(c + 1, nc - 1), kvcol)),
        pl.BlockSpec((1, 1, sw, RET_DK), lambda c, b: (b, c, 0, 0)),
        pl.BlockSpec((1, 1, sw, RET_DK), lambda c, b: (b, c, 0, 0)),
        pl.BlockSpec((SSM_WIDTH, SUBLANES, CHUNK), lambda c, b: (0, (c * bsz + b) // SUBLANES, 0)),
        pl.BlockSpec((blk, dm), lambda c, b: (b * nc + c, 0)),
        pl.BlockSpec((1,) + bias.shape[1:], lambda c, b: (jnp.where(c == 0, 0, jnp.where(c == nc - 1, 2, 1)), 0, 0, 0)),
        layer(wout_bf), layer(wglu_bf), layer(d3), layer(bglu3), layer(lng3), layer(lnb3), layer(rwt),
    ]
    out_specs = [
        pl.BlockSpec((blk, dm), lambda c, b: (b * nc + c, 0)),
        pl.BlockSpec((1, 1, ne, blk), lambda c, b: (b, c, 0, 0)),
    ]
    out_shape = [jax.ShapeDtypeStruct((t, dm), F32),
                 jax.ShapeDtypeStruct((bsz, nc, ne, blk), F32)]
    scratch = [pltpu.VMEM((blk, dm), F32),
               pltpu.VMEM((RET_HEADS, blk, blk), F32),
               pltpu.VMEM((ATT_KV_HEADS, ATT_GQ * blk, 3 * blk), F32),
               pltpu.VMEM((RET_HEADS, blk, blk + 2 * RET_DK), BF16),
               pltpu.VMEM((ATT_KV_HEADS, ATT_GQ * blk, 3 * blk), BF16)]
    return pl.pallas_call(
        functools.partial(_mixer_kernel, l=l, nb=bsz), grid=(nc, bsz),
        in_specs=in_specs, out_specs=out_specs, out_shape=out_shape, scratch_shapes=scratch,
        compiler_params=_cparams(("parallel", "parallel")),
    )(lg, sink, proj, proj, proj, sf, sb, yt, h2d, bias, wout_bf, wglu_bf, d3, bglu3, lng3, lnb3, rwt)


def _route_kernel(a_ref, slot_ref, start_ref, extent_ref, *, nc, ne, cap):
    blk = ATT_BLOCK
    a = a_ref[0]

    def chunk(x, c):
        return x[c * ne:(c + 1) * ne]

    def count(pred):
        x = pred.astype(jnp.int32)
        tot = chunk(x, 0)
        for c in range(1, nc):
            tot = tot + chunk(x, c)
        return jnp.sum(tot, axis=1, keepdims=True)

    def tile_e(v):
        return jnp.concatenate([v] * nc, axis=0)

    tau = jnp.zeros((ne, 1), jnp.int32)
    for bit in range(30, -1, -1):
        cand = tau | (1 << bit)
        tau = jnp.where(count(a >= tile_e(pltpu.bitcast(cand, F32))) >= cap, cand, tau)
    lo = pltpu.bitcast(tau, F32)
    hi = pltpu.bitcast(tau + 1, F32)
    for _ in range(ROUTE_REFINE_STEPS):
        mid = lo + (hi - lo) * 0.5
        ok = count(a >= tile_e(mid)) >= cap
        lo = jnp.where(ok, mid, lo)
        hi = jnp.where(ok, hi, mid)
    gt = a >= tile_e(hi)
    eq = (a >= tile_e(lo)) & jnp.logical_not(gt)
    need = cap - count(gt)

    tri = (lax.broadcasted_iota(jnp.int32, (blk, blk), 0) <= lax.broadcasted_iota(jnp.int32, (blk, blk), 1)).astype(BF16)

    def ranks(mask):
        mf = mask.astype(F32)
        incl = _dot(mf.astype(BF16), tri)
        run = jnp.zeros((ne, 1), F32)
        offs = []
        tots = []
        for c in range(nc):
            offs.append(run)
            tots.append(chunk(incl, c)[:, blk - 1:blk])
            run = run + tots[-1]
        return incl - mf, jnp.concatenate(offs, axis=0), jnp.concatenate(tots, axis=0)

    eq_local, eq_off, _ = ranks(eq)
    mask = gt | (eq & ((eq_local + eq_off) < tile_e(need).astype(F32)))
    local, off, tot = ranks(mask)
    off_i = off.astype(jnp.int32)
    start = (off_i // SLOT_ALIGN) * SLOT_ALIGN
    slot = (off_i - start) + local.astype(jnp.int32)
    slot_ref[0] = jnp.where(mask, slot, -1)
    start_ref[0] = jnp.broadcast_to(start, (nc * ne, blk))
    extent_ref[0] = jnp.broadcast_to((off_i - start) + tot.astype(jnp.int32), (nc * ne, blk))


def _route(afft, cap):
    bsz, nc, ne, blk = afft.shape
    a2 = afft.reshape(bsz, nc * ne, blk)
    spec = pl.BlockSpec((1, nc * ne, blk), lambda b: (b, 0, 0))
    slot, start, extent = pl.pallas_call(
        functools.partial(_route_kernel, nc=nc, ne=ne, cap=cap), grid=(bsz,),
        in_specs=[spec], out_specs=[spec, spec, spec],
        out_shape=[jax.ShapeDtypeStruct(a2.shape, jnp.int32)] * 3,
        compiler_params=_cparams(("parallel",)),
    )(a2)
    flat = lambda v: v[:, :, 0].reshape(bsz * nc * ne)
    return slot.reshape(bsz, nc, ne, blk), flat(start), flat(extent)


def _max_extent(we_ref, base, n):
    mx = we_ref[base]
    for j in range(1, n):
        mx = jnp.maximum(mx, we_ref[base + j])
    return mx


def _dispatch_kernel(ws_ref, we_ref, h_ref, afft_ref, slot_ref, xs_ref, *, nc, ne, eg, cd):
    b = pl.program_id(0)
    g = pl.program_id(1)
    i = pl.program_id(2)
    dm = h_ref.shape[1]
    blk = ATT_BLOCK

    @pl.when(i == 0)
    def _():
        xs_ref[...] = jnp.zeros_like(xs_ref)

    def scatter_rows(win, k, base):
        hb = h_ref[k * blk:(k + 1) * blk, :].astype(BF16)
        a = afft_ref[0, k]
        a_hi = a.astype(BF16)
        r1 = a - a_hi.astype(F32)
        a_mid = r1.astype(BF16)
        a_lo = (r1 - a_mid.astype(F32)).astype(BF16)
        a3 = jnp.concatenate([a_hi, a_mid, a_lo, jnp.zeros((GATE_LANES - 3 * ne, blk), BF16)], axis=0)
        riota = lax.broadcasted_iota(jnp.int32, (win, blk), 0)
        onehots = jnp.concatenate([(riota == slot_ref[0, k, 0, j:j + 1, :]).astype(BF16) for j in range(eg)], axis=0)
        res = _dot(onehots, hb)
        resg = _dot_nt(onehots, a3)
        for j in range(eg):
            w = pl.multiple_of(ws_ref[base + j], SLOT_ALIGN)
            head = pl.ds(w, SLOT_ALIGN)
            tail = pl.ds(w + SLOT_ALIGN, win - SLOT_ALIGN)
            for cols, r in ((slice(0, dm), res), (slice(dm, dm + GATE_LANES), resg)):
                rj = r[j * win:(j + 1) * win]
                xs_ref[0, j, head, cols] = (xs_ref[0, j, head, cols].astype(F32) + rj[0:SLOT_ALIGN]).astype(BF16)
                xs_ref[0, j, tail, cols] = rj[SLOT_ALIGN:].astype(BF16)

    for k in range(cd):
        base = (b * nc + i * cd + k) * ne + g * eg
        small = _max_extent(we_ref, base, eg) <= SMALL_WIN
        pl.when(small)(functools.partial(scatter_rows, SMALL_WIN, k, base))
        pl.when(jnp.logical_not(small))(functools.partial(scatter_rows, WIN, k, base))


def _dispatch(h1, afft, slot, wstart, wextent, cap, eg=8):
    bsz, nc, ne, blk = slot.shape
    t, dm = h1.shape
    cp = cap + WIN
    cd = math.gcd(DISPATCH_STEP_CHUNKS, nc)
    ns = nc // cd
    return pl.pallas_call(
        functools.partial(_dispatch_kernel, nc=nc, ne=ne, eg=eg, cd=cd),
        grid_spec=pltpu.PrefetchScalarGridSpec(
            num_scalar_prefetch=2, grid=(bsz, ne // eg, ns),
            in_specs=[pl.BlockSpec((cd * blk, dm), lambda b, g, i, ws, we: (b * ns + i, 0)),
                      pl.BlockSpec((1, cd, ne, blk), lambda b, g, i, ws, we: (b, i, 0, 0)),
                      pl.BlockSpec((1, cd, 1, eg, blk), lambda b, g, i, ws, we: (b, i, g, 0, 0))],
            out_specs=pl.BlockSpec((1, eg, cp, dm + GATE_LANES), lambda b, g, i, ws, we: (b, g, 0, 0))),
        out_shape=jax.ShapeDtypeStruct((bsz, ne, cp, dm + GATE_LANES), BF16),
        compiler_params=_cparams(("parallel", "parallel", "arbitrary")),
    )(wstart, wextent, h1, afft, slot.reshape(bsz, nc, ne // eg, eg, blk))


def _ffn_kernel(xs_ref, wg_ref, wu_ref, wd_ref, out_ref, acc_ref, *, cap, nf, ne):
    e = pl.program_id(0)
    f = pl.program_id(1)
    bsz = xs_ref.shape[0]
    dm = wd_ref.shape[3]
    wg = wg_ref[0, 0].astype(BF16)
    wu = wu_ref[0, 0].astype(BF16)
    wd = wd_ref[0, 0].astype(BF16)
    for b in range(bsz):
        x = xs_ref[b, 0, :, 0:dm]
        hg = _dot(x, wg)
        hu = _dot(x, wu)
        hdn = ((hg * jax.nn.sigmoid(hg)) * hu).astype(BF16)
        contrib = _dot(hdn, wd)
        rows = slice(b * cap, (b + 1) * cap)

        @pl.when(f == 0)
        def _():
            acc_ref[rows, :] = contrib

        @pl.when(f > 0)
        def _():
            acc_ref[rows, :] = acc_ref[rows, :] + contrib

    @pl.when(f == nf - 1)
    def _():
        cp = out_ref.shape[2]
        lane = lax.broadcasted_iota(jnp.int32, (cap, GATE_LANES), 1)
        mine = ((lane % ne) == e) & (lane < 3 * ne)
        for b in range(bsz):
            pieces = xs_ref[b, 0, :, dm:dm + GATE_LANES].astype(F32)
            gate = jnp.sum(jnp.where(mine, pieces, 0.0), axis=1, keepdims=True)
            out_ref[b, 0, 0:cap, :] = (acc_ref[b * cap:(b + 1) * cap, :] * gate).astype(BF16)
            out_ref[b, 0, cap:cp, :] = jnp.zeros((cp - cap, dm), BF16)


def _ffn(xs, wg, wu, wd, l, cap, tf=512):
    bsz, ne, cp, dx = xs.shape
    dm = wg.shape[2]
    ff = wg.shape[-1]
    nf = ff // tf
    return pl.pallas_call(
        functools.partial(_ffn_kernel, cap=cap, nf=nf, ne=ne), grid=(ne, nf),
        in_specs=[pl.BlockSpec((bsz, 1, cap, dx), lambda e, f: (0, e, 0, 0)),
                  pl.BlockSpec((1, 1, dm, tf), lambda e, f: (l, e, 0, f)),
                  pl.BlockSpec((1, 1, dm, tf), lambda e, f: (l, e, 0, f)),
                  pl.BlockSpec((1, 1, tf, dm), lambda e, f: (l, e, f, 0))],
        out_specs=pl.BlockSpec((bsz, 1, cp, dm), lambda e, f: (0, e, 0, 0)),
        out_shape=jax.ShapeDtypeStruct((bsz, ne, cp, dm), BF16),
        scratch_shapes=[pltpu.VMEM((bsz * cap, dm), F32)],
        compiler_params=_cparams(("parallel", "arbitrary")),
    )(xs, wg, wu, wd)


def _combine_kernel(ws_ref, we_ref, out_ref, slott_ref, h1_ref, lng_ref, lnb_ref, h2_ref, rhs_scr, *, nc, ne, cc):
    b = pl.program_id(0)
    i = pl.program_id(1)
    blk = ATT_BLOCK

    def finish(k, tot):
        rows = slice(k * blk, (k + 1) * blk)
        h2_ref[rows, :] = _layer_norm(DEEPNORM_ALPHA * h1_ref[rows, :] + tot, lng_ref[0], lnb_ref[0])

    def gather_packed(k, base):
        kk = ne * SMALL_WIN
        slot_t = slott_ref[0, k].astype(F32).astype(BF16)
        expand = (lax.broadcasted_iota(jnp.int32, (ne, kk), 1) // SMALL_WIN
                  == lax.broadcasted_iota(jnp.int32, (ne, kk), 0)).astype(BF16)
        spread = _dot(slot_t, expand)
        row = (lax.broadcasted_iota(jnp.int32, (blk, kk), 1) % SMALL_WIN).astype(F32)
        onehot = (spread == row).astype(BF16)
        for e in range(ne):
            w = pl.multiple_of(ws_ref[base + e], SLOT_ALIGN)
            rhs_scr[e * SMALL_WIN:(e + 1) * SMALL_WIN, :] = out_ref[0, e, pl.ds(w, SMALL_WIN), :]
        finish(k, _dot(onehot, rhs_scr[...]))

    def gather_per_expert(k, base):
        slot_t = slott_ref[0, k]
        liota = lax.broadcasted_iota(jnp.int32, (blk, WIN), 1)
        tot = None
        for e in range(ne):
            w = pl.multiple_of(ws_ref[base + e], SLOT_ALIGN)
            onehot = (liota == slot_t[:, e:e + 1]).astype(BF16)
            y = _dot(onehot, out_ref[0, e, pl.ds(w, WIN), :])
            tot = y if tot is None else tot + y
        finish(k, tot)

    for k in range(cc):
        base = (b * nc + i * cc + k) * ne
        small = _max_extent(we_ref, base, ne) <= SMALL_WIN
        pl.when(small)(functools.partial(gather_packed, k, base))
        pl.when(jnp.logical_not(small))(functools.partial(gather_per_expert, k, base))


def _combine(out, slot, wstart, wextent, h1, lng3, lnb3, l, seq):
    bsz, ne, cp, dm = out.shape
    nc = seq // ATT_BLOCK
    blk = ATT_BLOCK
    t = h1.shape[0]
    slot_t = jnp.swapaxes(slot, 2, 3)
    cc = math.gcd(COMBINE_STEP_CHUNKS, nc)
    ns = nc // cc
    vec = pl.BlockSpec((1, 1, dm), lambda b, i, ws, we: (l, 0, 0))
    return pl.pallas_call(
        functools.partial(_combine_kernel, nc=nc, ne=ne, cc=cc),
        grid_spec=pltpu.PrefetchScalarGridSpec(
            num_scalar_prefetch=2, grid=(bsz, ns),
            in_specs=[pl.BlockSpec((1, ne, cp, dm), lambda b, i, ws, we: (b, 0, 0, 0), pipeline_mode=pl.Buffered(1)),
                      pl.BlockSpec((1, cc, blk, ne), lambda b, i, ws, we: (b, i, 0, 0)),
                      pl.BlockSpec((cc * blk, dm), lambda b, i, ws, we: (b * ns + i, 0)),
                      vec, vec],
            out_specs=pl.BlockSpec((cc * blk, dm), lambda b, i, ws, we: (b * ns + i, 0)),
            scratch_shapes=[pltpu.VMEM((ne * SMALL_WIN, dm), BF16)]),
        out_shape=jax.ShapeDtypeStruct((t, dm), F32),
        compiler_params=_cparams(("parallel", "arbitrary")),
    )(wstart, wextent, out, slot_t, h1, lng3, lnb3)


def kernel(x, ln_in_g, ln_in_b, w_in, ret_theta, ssm_lambda_re, ssm_lambda_im, ssm_log_step, ssm_b_re, ssm_b_im,
           ssm_c_re, ssm_c_im, ssm_d, ssm_w_glu, ssm_b_glu, attn_sink, w_out, ln1_g, ln1_b, router_w,
           exp_w_gate, exp_w_up, exp_w_down, ln2_g, ln2_b):
    bsz, seq, dm = x.shape
    depth = w_in.shape[0]
    ne = router_w.shape[-1]
    cap = EC_FACTOR * seq // ne
    t = bsz * seq
    w_in_bf = w_in.astype(BF16)
    wt_bf = jnp.swapaxes(jnp.concatenate([w_in[:, :, COL_SU:COL_SU + SSM_WIDTH], w_in[:, :, COL_RK:COL_RK + RET_WIDTH]],
                                         axis=2), 1, 2).astype(BF16)
    w_out_bf = w_out.astype(BF16)
    w_glu_bf = ssm_w_glu.astype(BF16)
    lg = jax.nn.log_sigmoid(ret_theta.astype(F32))
    sink = attn_sink.astype(F32)
    ssm_par = _ssm_params(ssm_lambda_re, ssm_lambda_im, ssm_log_step, ssm_b_re, ssm_b_im, ssm_c_re, ssm_c_im)
    vec3 = lambda a: a.astype(F32).reshape(depth, 1, -1)
    d3, bglu3, ln1g3, ln1b3, ln2g3, ln2b3 = map(vec3, (ssm_d, ssm_b_glu, ln1_g, ln1_b, ln2_g, ln2_b))
    rwt = jnp.swapaxes(router_w.astype(F32), 1, 2)
    bias = _attention_bias(seq)

    h = x
    for l in range(depth):
        if l == 0:
            h, proj, ut, kt = _inproj(h, w_in_bf, wt_bf, l, ln=(ln_in_g, ln_in_b))
        else:
            proj, ut, kt = _inproj(h, w_in_bf, wt_bf, l)
        h2d = h.reshape(t, dm)
        proj = proj.reshape(t, D_IN)
        yt = _ssm_conv(ut, ssm_par, l, bsz)
        sf, sb = _ret_states(proj, kt, lg, l, bsz, seq)
        h1, afft = _mixer(proj, sf, sb, yt, h2d, bias, lg, sink, w_out_bf, w_glu_bf, d3, bglu3, ln1g3, ln1b3,
                          rwt, l, bsz, seq)
        slot, wstart, wextent = _route(afft, cap)
        xs = _dispatch(h1, afft, slot, wstart, wextent, cap)
        out = _ffn(xs, exp_w_gate, exp_w_up, exp_w_down, l, cap)
        h = _combine(out, slot, wstart, wextent, h1, ln2g3, ln2b3, l, seq).reshape(bsz, seq, dm)
    return h
```

```python
import functools
import math

import jax
import jax.numpy as jnp
from jax import lax
from jax.experimental import pallas as pl
from jax.experimental.pallas import tpu as pltpu

F32 = jnp.float32
BF16 = jnp.bfloat16

RET_HEADS = 4
RET_DK = 64
RET_CHUNK = 128
RET_WIDTH = RET_HEADS * RET_DK
SSM_CPG = 16
SSM_GROUPS = 16
SSM_STATE = 64
SSM_WIDTH = SSM_CPG * SSM_GROUPS
ATT_HEADS = 8
ATT_KV_HEADS = 2
ATT_GQ = ATT_HEADS // ATT_KV_HEADS
ATT_HEAD_DIM = 64
ATT_BLOCK = 128
ATT_WINDOW = 128
EC_FACTOR = 2
DEPTH = 2
DEEPNORM_ALPHA = (2.0 * DEPTH) ** 0.25
LN_EPS = 1e-5
NEG_INF = -1e30

COL_RQ, COL_RK, COL_RV, COL_RG, COL_SU, COL_AQ, COL_AK, COL_AV = 0, 256, 512, 768, 1024, 1280, 1792, 1920
D_IN = 2048

CHUNK = 128
SUBLANES = 8
SLOT_ALIGN = 16
WIN = CHUNK + SLOT_ALIGN
LANES = 128
SMALL_WIN = LANES // 2
DISPATCH_STEP_CHUNKS = 4
COMBINE_STEP_CHUNKS = 2
MIXER_STEP_CHUNKS = 2
GATE_LANES = 128
RET_STEP_CHUNKS = 8
VMEM_LIMIT = 56 * 1024 * 1024
ROUTE_REFINE_STEPS = 10


def _cparams(sem):
    return pltpu.CompilerParams(dimension_semantics=sem, vmem_limit_bytes=VMEM_LIMIT)


def _layer_norm(x, g, b):
    mu = jnp.mean(x, axis=-1, keepdims=True)
    xc = x - mu
    var = jnp.mean(xc * xc, axis=-1, keepdims=True)
    return xc * lax.rsqrt(var + LN_EPS) * g + b


def _dot(a, b):
    return jnp.dot(a, b, preferred_element_type=F32)


def _dot_hp(a, b):
    return jnp.dot(a, b, precision=lax.Precision.HIGHEST, preferred_element_type=F32)


def _dot_nt(a, b):
    return lax.dot_general(a, b, (((1,), (1,)), ((), ())), preferred_element_type=F32)


def _cmul(ar, ai, br, bi):
    return ar * br - ai * bi, ar * bi + ai * br


def _cpow(br, bi, expo, nbits):
    shape = (br.shape[0], expo.shape[1])
    rr = jnp.ones(shape, F32)
    ri = jnp.zeros(shape, F32)
    for j in range(nbits):
        bit = jnp.broadcast_to(((expo >> j) & 1) == 1, shape)
        nr, ni = _cmul(rr, ri, br, bi)
        rr = jnp.where(bit, nr, rr)
        ri = jnp.where(bit, ni, ri)
        if j + 1 < nbits:
            br, bi = _cmul(br, bi, br, bi)
    return rr, ri


def _inproj_kernel(*refs, apply_ln):
    if apply_ln:
        x_ref, g_ref, b_ref, w_ref, wt_ref, h_ref, proj_ref, ut_ref, kt_ref = refs
    else:
        x_ref, w_ref, wt_ref, proj_ref, ut_ref, kt_ref = refs
    nb, tl, d = x_ref.shape
    h = x_ref[...].reshape(nb * tl, d)
    if apply_ln:
        h = _layer_norm(h, g_ref[...], b_ref[...])
        h_ref[...] = h.reshape(nb, tl, d)
    hb = h.astype(BF16)
    proj_ref[...] = _dot(hb, w_ref[0]).reshape(nb, tl, -1)
    tt = _dot_nt(wt_ref[0], hb)
    k = tl // CHUNK
    for j in range(k):
        for b in range(nb):
            ut_ref[:, j * nb + b, :] = tt[0:SSM_WIDTH, (b * k + j) * CHUNK:(b * k + j + 1) * CHUNK]
    for b in range(nb):
        kt_ref[b] = tt[SSM_WIDTH:SSM_WIDTH + RET_WIDTH, b * tl:(b + 1) * tl]


def _inproj(x3, w_bf, wt_bf, l, ln=None):
    bsz, seq, d = x3.shape
    n = w_bf.shape[2]
    k = SUBLANES // bsz
    tl = k * CHUNK
    nc = seq // CHUNK
    xspec = pl.BlockSpec((bsz, tl, d), lambda i: (0, i, 0))
    wspec = pl.BlockSpec((1, d, n), lambda i: (l, 0, 0))
    wtspec = pl.BlockSpec((1,) + wt_bf.shape[1:], lambda i: (l, 0, 0))
    out_specs = [pl.BlockSpec((bsz, tl, n), lambda i: (0, i, 0)),
                 pl.BlockSpec((SSM_WIDTH, k * bsz, CHUNK), lambda i: (0, i, 0)),
                 pl.BlockSpec((bsz, RET_WIDTH, tl), lambda i: (0, 0, i))]
    out_shape = [jax.ShapeDtypeStruct((bsz, seq, n), F32),
                 jax.ShapeDtypeStruct((SSM_WIDTH, nc * bsz, CHUNK), F32),
                 jax.ShapeDtypeStruct((bsz, RET_WIDTH, seq), F32)]
    if ln is None:
        return pl.pallas_call(
            functools.partial(_inproj_kernel, apply_ln=False), grid=(seq // tl,),
            in_specs=[xspec, wspec, wtspec], out_specs=out_specs, out_shape=out_shape,
            compiler_params=_cparams(("parallel",)),
        )(x3, w_bf, wt_bf)
    g, b = ln
    vec = pl.BlockSpec((1, d), lambda i: (0, 0))
    return pl.pallas_call(
        functools.partial(_inproj_kernel, apply_ln=True), grid=(seq // tl,),
        in_specs=[xspec, vec, vec, wspec, wtspec], out_specs=[xspec] + out_specs,
        out_shape=[jax.ShapeDtypeStruct(x3.shape, F32)] + out_shape,
        compiler_params=_cparams(("parallel",)),
    )(x3, g.reshape(1, d), b.reshape(1, d), w_bf, wt_bf)


def _ssm_params(lam_re, lam_im, log_step, b_re, b_im, c_re, c_im):
    lr, li = lam_re.astype(F32), lam_im.astype(F32)
    step = jnp.exp(log_step.astype(F32))[..., None]
    er = jnp.exp(lr * step)
    lbr, lbi = er * jnp.cos(li * step), er * jnp.sin(li * step)
    den = lr * lr + li * li
    fr = ((lbr - 1.0) * lr + lbi * li) / den
    fi = (lbi * lr - (lbr - 1.0) * li) / den
    br = jnp.swapaxes(b_re.astype(F32), -1, -2)[:, None]
    bi = jnp.swapaxes(b_im.astype(F32), -1, -2)[:, None]
    bbr = fr[..., None, :] * br - fi[..., None, :] * bi
    bbi = fr[..., None, :] * bi + fi[..., None, :] * br
    lcol = jnp.stack([lbr[:, 0], lbi[:, 0], lbr[:, 1], lbi[:, 1]], axis=-1)
    lrow = jnp.swapaxes(lcol, -1, -2)
    lrow = jnp.concatenate([lrow, jnp.zeros_like(lrow)], axis=2)
    bt = jnp.stack([bbr[:, 0], bbi[:, 0], bbr[:, 1], bbi[:, 1]], axis=2)
    cr, ci = c_re.astype(F32), c_im.astype(F32)
    c4 = jnp.stack([cr[:, 0], ci[:, 0], cr[:, 1], ci[:, 1]], axis=2)
    return lcol, lrow, bt, jnp.swapaxes(bt, -1, -2), c4, jnp.swapaxes(c4, -1, -2)


def _ssm_kernel(ut_ref, lcol_ref, lrow_ref, bt_ref, bcol_ref, c_ref, ct_ref, y_ref,
                w_scr, wst_scr, g_scr, vf_scr, vb_scr, acc_scr, s_scr, sp_scr, x_scr, *, nb, nchunks):
    tc, p, cpg = CHUNK, SSM_STATE, SSM_CPG
    lcol = lcol_ref[0, 0]
    lf = (lcol[:, 0:1], lcol[:, 1:2])
    lb = (lcol[:, 2:3], lcol[:, 3:4])

    m_row = lax.broadcasted_iota(jnp.int32, (1, tc), 1)
    pfr, pfi = _cpow(*lf, m_row, 7)
    pf1r, pf1i = _cmul(pfr, pfi, *lf)
    prr, pri = _cpow(*lf, tc - 1 - m_row, 7)
    pbr, pbi = _cpow(*lb, tc - m_row, 8)
    pqr, pqi = _cpow(*lb, m_row, 7)

    def rep_co(x):
        return jnp.broadcast_to(x[:, None, :], (cpg, cpg, p)).reshape(cpg * cpg, p)

    def rep_ci(x):
        return jnp.broadcast_to(x[None, :, :], (cpg, cpg, p)).reshape(cpg * cpg, p)

    cbfr, cbfi = _cmul(rep_co(c_ref[0, 0, 0]), rep_co(c_ref[0, 0, 1]), rep_ci(bt_ref[0, 0, 0]), rep_ci(bt_ref[0, 0, 1]))
    cbbr, cbbi = _cmul(rep_co(c_ref[0, 0, 2]), rep_co(c_ref[0, 0, 3]), rep_ci(bt_ref[0, 0, 2]), rep_ci(bt_ref[0, 0, 3]))
    lane = lax.broadcasted_iota(jnp.int32, (cpg * cpg, tc), 1)
    kb0 = jnp.sum(cbbr, axis=1, keepdims=True)
    vf_scr[...] = _dot_hp(cbfr, pfr) - _dot_hp(cbfi, pfi) + jnp.where(lane == 0, kb0, 0.0)
    vb_scr[...] = _dot_hp(cbbr, pbr) - _dot_hp(cbbi, pbi)

    for co in range(cpg):
        cols = slice(co * tc, (co + 1) * tc)
        for r, (ar, ai) in enumerate(((pf1r, pf1i), (pbr, pbi))):
            gr, gi = _cmul(ct_ref[0, 0, 2 * r][:, co:co + 1], ct_ref[0, 0, 2 * r + 1][:, co:co + 1], ar, ai)
            g_scr[2 * r * p:(2 * r + 1) * p, cols] = gr.astype(BF16)
            g_scr[(2 * r + 1) * p:(2 * r + 2) * p, cols] = (-gi).astype(BF16)
        for r, (ar, ai) in enumerate(((prr, pri), (pqr, pqi))):
            sr, si = _cmul(bcol_ref[0, 0, 2 * r][:, co:co + 1], bcol_ref[0, 0, 2 * r + 1][:, co:co + 1], ar, ai)
            wst_scr[co, 2 * r * p:(2 * r + 1) * p, :] = sr.astype(BF16)
            wst_scr[co, (2 * r + 1) * p:(2 * r + 2) * p, :] = si.astype(BF16)

    s_idx = lax.broadcasted_iota(jnp.int32, (tc, tc), 0)
    j_idx = lax.broadcasted_iota(jnp.int32, (tc, tc), 1)
    fwd_part = j_idx < tc - s_idx
    acc_scr[...] = jnp.zeros_like(acc_scr)
    s_scr[...] = jnp.zeros_like(s_scr)

    def per_channel(ci, carry):
        for co in range(cpg):
            rf = jnp.broadcast_to(vf_scr[pl.ds(co * cpg + ci, 1), :], (tc, tc))
            rb = jnp.broadcast_to(vb_scr[pl.ds(co * cpg + ci, 1), :], (tc, tc))
            m = pltpu.roll(jnp.where(fwd_part, rf, rb), 0, 1, stride=1, stride_axis=0)
            w_scr[ci, :, co * tc:(co + 1) * tc] = m.astype(BF16)
        u = ut_ref[ci].astype(BF16)
        acc_scr[...] += _dot(u, w_scr[ci])
        s_scr[...] += _dot_nt(u, wst_scr[ci])
        return carry

    lax.fori_loop(0, cpg, per_channel, 0)

    lrow = lrow_ref[0, 0]
    dec = []
    for r in range(2):
        dr, di = lrow[2 * r:2 * r + 1], lrow[2 * r + 1:2 * r + 2]
        for _ in range(7):
            dr, di = _cmul(dr, di, dr, di)
        dec.append((dr, di))
    for q in range(4):
        sp_scr[q] = s_scr[:, q * p:(q + 1) * p]
    xfr = xfi = xbr = xbi = jnp.zeros((nb, p), F32)
    for i in range(nchunks):
        rf = slice(i * nb, (i + 1) * nb)
        rb = slice((nchunks - 1 - i) * nb, (nchunks - i) * nb)
        x_scr[0, rf, :] = xfr
        x_scr[1, rf, :] = xfi
        x_scr[2, rb, :] = xbr
        x_scr[3, rb, :] = xbi
        xfr, xfi = _cmul(dec[0][0], dec[0][1], xfr, xfi)
        xbr, xbi = _cmul(dec[1][0], dec[1][1], xbr, xbi)
        xfr, xfi = xfr + sp_scr[0, rf, :], xfi + sp_scr[1, rf, :]
        xbr, xbi = xbr + sp_scr[2, rb, :], xbi + sp_scr[3, rb, :]
    xin = jnp.concatenate([x_scr[q] for q in range(4)], axis=1).astype(BF16)
    y = acc_scr[...] + _dot(xin, g_scr[...])
    for co in range(cpg):
        y_ref[co] = y[:, co * tc:(co + 1) * tc]


def _ssm_conv(ut, params, l, bsz):
    _, r, tc = ut.shape
    cpg, p, ng = SSM_CPG, SSM_STATE, SSM_GROUPS

    def pspec(a):
        return pl.BlockSpec((1, 1) + a.shape[2:], lambda g: (l, g) + (0,) * (a.ndim - 2))
    return pl.pallas_call(
        functools.partial(_ssm_kernel, nb=bsz, nchunks=r // bsz), grid=(ng,),
        in_specs=[pl.BlockSpec((cpg, r, tc), lambda g: (g, 0, 0))] + [pspec(a) for a in params],
        out_specs=pl.BlockSpec((cpg, r, tc), lambda g: (g, 0, 0)),
        out_shape=jax.ShapeDtypeStruct(ut.shape, F32),
        scratch_shapes=[pltpu.VMEM((cpg, tc, cpg * tc), BF16), pltpu.VMEM((cpg, 4 * p, tc), BF16),
                        pltpu.VMEM((4 * p, cpg * tc), BF16),
                        pltpu.VMEM((cpg * cpg, tc), F32), pltpu.VMEM((cpg * cpg, tc), F32),
                        pltpu.VMEM((r, cpg * tc), F32), pltpu.VMEM((r, 4 * p), F32),
                        pltpu.VMEM((4, r, p), F32), pltpu.VMEM((4, r, p), F32)],
        compiler_params=_cparams(("parallel",)),
    )(ut, *params)


def _retstate_kernel(lg_ref, ktf_ref, vf_ref, ktb_ref, vb_ref, sf_ref, sb_ref, accf, accb, *, l, cs):
    i = pl.program_id(1)

    @pl.when(i == 0)
    def _():
        accf[...] = jnp.zeros_like(accf)
        accb[...] = jnp.zeros_like(accb)

    ch = RET_CHUNK
    pos = lax.broadcasted_iota(jnp.int32, (1, ch), 1).astype(F32)
    one = jnp.ones((1, RET_DK), F32)
    wts = []
    for h in range(RET_HEADS):
        lgf = lg_ref[l, 0, h]
        lgb = lg_ref[l, 1, h]
        wts.append((jnp.exp(lgf * (ch - 1.0 - pos)) * RET_DK ** -0.5, jnp.exp(lgb * pos) * RET_DK ** -0.5,
                    jnp.exp(one * (lgf * ch)), jnp.exp(one * (lgb * ch))))
    for j in range(cs):
        jb = cs - 1 - j
        sf_ref[0, j] = accf[...]
        sb_ref[0, jb] = accb[...]
        for h in range(RET_HEADS):
            rows = slice(h * RET_DK, (h + 1) * RET_DK)
            wf, wb, decf, decb = wts[h]
            kf = (ktf_ref[0, rows, j * ch:(j + 1) * ch] * wf).astype(BF16)
            kb = (ktb_ref[0, rows, jb * ch:(jb + 1) * ch] * wb).astype(BF16)
            accf[rows, :] = decf * accf[rows, :] + _dot(kf, vf_ref[j * ch:(j + 1) * ch, rows].astype(BF16))
            accb[rows, :] = decb * accb[rows, :] + _dot(kb, vb_ref[jb * ch:(jb + 1) * ch, rows].astype(BF16))


def _ret_states(proj, kt, lg, l, bsz, seq):
    nc = seq // RET_CHUNK
    cs = min(RET_STEP_CHUNKS, nc)
    ns = nc // cs
    w = RET_WIDTH
    tl = cs * RET_CHUNK
    vcol = COL_RV // w
    st = jax.ShapeDtypeStruct((bsz, nc, w, RET_DK), F32)
    return pl.pallas_call(
        functools.partial(_retstate_kernel, l=l, cs=cs), grid=(bsz, ns),
        in_specs=[pl.BlockSpec(memory_space=pltpu.SMEM),
                  pl.BlockSpec((1, w, tl), lambda b, i: (b, 0, i)),
                  pl.BlockSpec((tl, w), lambda b, i: (b * ns + i, vcol)),
                  pl.BlockSpec((1, w, tl), lambda b, i: (b, 0, ns - 1 - i)),
                  pl.BlockSpec((tl, w), lambda b, i: (b * ns + ns - 1 - i, vcol))],
        out_specs=[pl.BlockSpec((1, cs, w, RET_DK), lambda b, i: (b, i, 0, 0)),
                   pl.BlockSpec((1, cs, w, RET_DK), lambda b, i: (b, ns - 1 - i, 0, 0))],
        out_shape=[st, st],
        scratch_shapes=[pltpu.VMEM((w, RET_DK), F32), pltpu.VMEM((w, RET_DK), F32)],
        compiler_params=_cparams(("parallel", "arbitrary")),
    )(lg, kt, proj, kt, proj)


def _gelu_tanh(x):
    return 0.5 * x * (1.0 + jnp.tanh(math.sqrt(2.0 / math.pi) * (x + 0.044715 * (x * x * x))))


def _attention_bias(seq):
    blk = ATT_BLOCK
    s_idx = jnp.arange(3 * blk)[None, :]
    t_idx = jnp.arange(blk)[:, None]
    arel = jnp.abs(s_idx - blk - t_idx)
    band = arel <= ATT_WINDOW
    slopes = jnp.exp2(-8.0 * jnp.arange(1, ATT_HEADS + 1, dtype=F32) / ATT_HEADS)
    alibi = -slopes[:, None, None] * arel.astype(F32)[None]
    variants = []
    for prev_ok, next_ok in ((False, True), (True, True), (True, False)):
        ok = band & (prev_ok | (s_idx >= blk)) & (next_ok | (s_idx < 2 * blk))
        variants.append(jnp.where(ok[None], alibi, NEG_INF).reshape(ATT_KV_HEADS, ATT_GQ * blk, 3 * blk))
    return jnp.stack(variants)


def _mixer_kernel(lg_ref, sink_ref, proj_ref, kvp_ref, kvn_ref, sf_ref, sb_ref, yt_ref, h_ref, bias0_ref, bias1_ref,
                  wout_ref, wglu_ref, d_ref, bglu_ref, lng_ref, lnb_ref, rwt_ref,
                  h1_ref, afft_ref, mix_scr, sret_scr, satt_scr, pret_scr, patt_scr, *, l, nb):
    b = pl.program_id(1)
    cm = MIXER_STEP_CHUNKS
    bias_refs = (bias0_ref, bias1_ref)
    ch = RET_CHUNK
    dk = RET_DK
    blk = ATT_BLOCK
    hd = ATT_HEAD_DIM
    kvw = ATT_KV_HEADS * hd

    def kv_block(u, cols):
        if u < 0:
            return kvp_ref[:, cols]
        if u >= cm:
            return kvn_ref[:, cols]
        return proj_ref[u * ch:(u + 1) * ch, COL_AK + cols.start:COL_AK + cols.stop]

    vbs = {}
    for u in range(cm):
        r = slice(u * ch, (u + 1) * ch)
        for h in range(RET_HEADS):
            q = proj_ref[r, COL_RQ + h * dk:COL_RQ + (h + 1) * dk]
            k = proj_ref[r, COL_RK + h * dk:COL_RK + (h + 1) * dk] * dk ** -0.5
            sret_scr[u * RET_HEADS + h] = _dot_nt(q.astype(BF16), k.astype(BF16))
        for kvh in range(ATT_KV_HEADS):
            kc = slice(kvh * hd, (kvh + 1) * hd)
            vc = slice(kvw + kvh * hd, kvw + (kvh + 1) * hd)
            kb = jnp.concatenate([kv_block(u + d, kc) for d in (-1, 0, 1)], axis=0).astype(BF16)
            vbs[u, kvh] = jnp.concatenate([kv_block(u + d, vc) for d in (-1, 0, 1)], axis=0).astype(BF16)
            q4 = jnp.concatenate([proj_ref[r, COL_AQ + (kvh * ATT_GQ + gq) * hd:COL_AQ + (kvh * ATT_GQ + gq + 1) * hd]
                                  for gq in range(ATT_GQ)], axis=0) * hd ** -0.5
            satt_scr[u * ATT_KV_HEADS + kvh] = _dot_nt(q4.astype(BF16), kb)

    pos = lax.broadcasted_iota(jnp.int32, (ch, 1), 0).astype(F32)
    dist = lax.broadcasted_iota(jnp.int32, (ch, ch), 0) - lax.broadcasted_iota(jnp.int32, (ch, ch), 1)
    adist = jnp.abs(dist).astype(F32)
    for h in range(RET_HEADS):
        lgf = lg_ref[l, 0, h]
        lgb = lg_ref[l, 1, h]
        dmat = jnp.exp(jnp.where(dist >= 0, lgf, lgb) * adist)
        wqf = jnp.exp(lgf * (pos + 1.0))
        wqb = jnp.exp(lgb * (ch - pos))
        for u in range(cm):
            q = proj_ref[u * ch:(u + 1) * ch, COL_RQ + h * dk:COL_RQ + (h + 1) * dk]
            uh = u * RET_HEADS + h
            pret_scr[uh, :, 0:ch] = (sret_scr[uh] * dmat).astype(BF16)
            pret_scr[uh, :, ch:ch + dk] = (q * wqf).astype(BF16)
            pret_scr[uh, :, ch + dk:ch + 2 * dk] = (q * wqb).astype(BF16)
    for u in range(cm):
        for kvh in range(ATT_KV_HEADS):
            uk = u * ATT_KV_HEADS + kvh
            for gq in range(ATT_GQ):
                rows = slice(gq * blk, (gq + 1) * blk)
                sink = sink_ref[l, kvh * ATT_GQ + gq]
                sc = satt_scr[uk, rows, :] + bias_refs[u][0, kvh, rows, :]
                m = jnp.maximum(jnp.max(sc, axis=-1, keepdims=True), sink)
                p = jnp.exp(sc - m)
                denom = jnp.sum(p, axis=-1, keepdims=True) + jnp.exp(sink - m)
                patt_scr[uk, rows, :] = (p * (1.0 / denom)).astype(BF16)

    att_base = RET_WIDTH + SSM_WIDTH
    w_ssm = SSM_WIDTH
    for u in range(cm):
        r = slice(u * ch, (u + 1) * ch)
        for h in range(RET_HEADS):
            rows = slice(h * dk, (h + 1) * dk)
            v = proj_ref[r, COL_RV + h * dk:COL_RV + (h + 1) * dk]
            g = proj_ref[r, COL_RG + h * dk:COL_RG + (h + 1) * dk]
            rhs = jnp.concatenate([v, sf_ref[0, u, rows, :], sb_ref[0, u, rows, :]], axis=0).astype(BF16)
            o = _dot(pret_scr[u * RET_HEADS + h], rhs)
            mu = jnp.mean(o, axis=-1, keepdims=True)
            oc = o - mu
            var = jnp.mean(oc * oc, axis=-1, keepdims=True)
            mix_scr[r, h * dk:(h + 1) * dk] = (g * jax.nn.sigmoid(g)) * (oc * lax.rsqrt(var + LN_EPS))
        for kvh in range(ATT_KV_HEADS):
            o = _dot(patt_scr[u * ATT_KV_HEADS + kvh], vbs[u, kvh])
            for gq in range(ATT_GQ):
                hh = kvh * ATT_GQ + gq
                mix_scr[r, att_base + hh * hd:att_base + (hh + 1) * hd] = o[gq * blk:(gq + 1) * blk]

        yraw = yt_ref[:, pl.ds(u * nb + b, 1), :].reshape(w_ssm, ch).T
        y = yraw + d_ref[0] * proj_ref[r, COL_SU:COL_SU + w_ssm]
        y = _gelu_tanh(y)
        gate = jax.nn.sigmoid(_dot(y.astype(BF16), wglu_ref[0]) + bglu_ref[0])
        mix_scr[r, RET_WIDTH:RET_WIDTH + w_ssm] = y * gate

    mix = _dot(mix_scr[...].astype(BF16), wout_ref[0])
    h1 = _layer_norm(DEEPNORM_ALPHA * h_ref[...] + mix, lng_ref[0], lnb_ref[0])
    h1_ref[...] = h1

    ne = rwt_ref.shape[1] // 2
    h_hi = h1.astype(BF16)
    h_lo = (h1 - h_hi.astype(F32)).astype(BF16)
    both = _dot_nt(rwt_ref[0], h_hi)
    lt = both[0:ne] + both[ne:2 * ne] + _dot_nt(rwt_ref[0, 0:ne, :], h_lo)
    lt = lt - jnp.max(lt, axis=0, keepdims=True)
    et = jnp.exp(lt)
    aff = et / jnp.sum(et, axis=0, keepdims=True)
    for u in range(cm):
        afft_ref[0, u] = aff[:, u * ch:(u + 1) * ch]


def _mixer(proj, sf, sb, yt, h2d, bias, lg, sink, wout_bf, wglu_bf, d3, bglu3, lng3, lnb3, rwt, l, bsz, seq):
    nc = seq // ATT_BLOCK
    cm = MIXER_STEP_CHUNKS
    ns = nc // cm
    assert nc % cm == 0 and cm * bsz == SUBLANES
    t, dm = h2d.shape
    ne = rwt.shape[1] // 2
    kvw = 2 * ATT_KV_HEADS * ATT_HEAD_DIM
    sw = RET_WIDTH
    kvcol = COL_AK // kvw
    blk = ATT_BLOCK

    def layer(a):
        return pl.BlockSpec((1,) + a.shape[1:], lambda i, b: (l,) + (0,) * (a.ndim - 1))
    smem = pl.BlockSpec(memory_space=pltpu.SMEM)
    in_specs = [
        smem, smem,
        pl.BlockSpec((cm * blk, D_IN), lambda i, b: (b * ns + i, 0)),
        pl.BlockSpec((blk, kvw), lambda i, b: (b * nc + jnp.maximum(cm * i - 1, 0), kvcol)),
        pl.BlockSpec((blk, kvw), lambda i, b: (b * nc + jnp.minimum(cm * i + cm, nc - 1), kvcol)),
        pl.BlockSpec((1, cm, sw, RET_DK), lambda i, b: (b, i, 0, 0)),
        pl.BlockSpec((1, cm, sw, RET_DK), lambda i, b: (b, i, 0, 0)),
        pl.BlockSpec((SSM_WIDTH, SUBLANES, CHUNK), lambda i, b: (0, i, 0)),
        pl.BlockSpec((cm * blk, dm), lambda i, b: (b * ns + i, 0)),
        pl.BlockSpec((1,) + bias.shape[1:], lambda i, b: (jnp.where(i == 0, 0, 1), 0, 0, 0)),
        pl.BlockSpec((1,) + bias.shape[1:], lambda i, b: (jnp.where(i == ns - 1, 2, 1), 0, 0, 0)),
        layer(wout_bf), layer(wglu_bf), layer(d3), layer(bglu3), layer(lng3), layer(lnb3), layer(rwt),
    ]
    out_specs = [
        pl.BlockSpec((cm * blk, dm), lambda i, b: (b * ns + i, 0)),
        pl.BlockSpec((1, cm, ne, blk), lambda i, b: (b, i, 0, 0)),
    ]
    out_shape = [jax.ShapeDtypeStruct((t, dm), F32),
                 jax.ShapeDtypeStruct((bsz, nc, ne, blk), F32)]
    scratch = [pltpu.VMEM((cm * blk, dm), F32),
               pltpu.VMEM((cm * RET_HEADS, blk, blk), F32),
               pltpu.VMEM((cm * ATT_KV_HEADS, ATT_GQ * blk, 3 * blk), F32),
               pltpu.VMEM((cm * RET_HEADS, blk, blk + 2 * RET_DK), BF16),
               pltpu.VMEM((cm * ATT_KV_HEADS, ATT_GQ * blk, 3 * blk), BF16)]
    return pl.pallas_call(
        functools.partial(_mixer_kernel, l=l, nb=bsz), grid=(ns, bsz),
        in_specs=in_specs, out_specs=out_specs, out_shape=out_shape, scratch_shapes=scratch,
        compiler_params=_cparams(("parallel", "parallel")),
    )(lg, sink, proj, proj, proj, sf, sb, yt, h2d, bias, bias, wout_bf, wglu_bf, d3, bglu3, lng3, lnb3, rwt)


def _route_kernel(a_ref, slot_ref, start_ref, extent_ref, *, nc, ne, cap):
    blk = ATT_BLOCK
    a = a_ref[0]

    def chunk(x, c):
        return x[c * ne:(c + 1) * ne]

    def count(pred):
        x = pred.astype(jnp.int32)
        tot = chunk(x, 0)
        for c in range(1, nc):
            tot = tot + chunk(x, c)
        return jnp.sum(tot, axis=1, keepdims=True)

    def tile_e(v):
        return jnp.concatenate([v] * nc, axis=0)

    tau = jnp.zeros((ne, 1), jnp.int32)
    for bit in range(30, -1, -1):
        cand = tau | (1 << bit)
        tau = jnp.where(count(a >= tile_e(pltpu.bitcast(cand, F32))) >= cap, cand, tau)
    lo = pltpu.bitcast(tau, F32)
    hi = pltpu.bitcast(tau + 1, F32)
    for _ in range(ROUTE_REFINE_STEPS):
        mid = lo + (hi - lo) * 0.5
        ok = count(a >= tile_e(mid)) >= cap
        lo = jnp.where(ok, mid, lo)
        hi = jnp.where(ok, hi, mid)
    gt = a >= tile_e(hi)
    eq = (a >= tile_e(lo)) & jnp.logical_not(gt)
    need = cap - count(gt)

    tri = (lax.broadcasted_iota(jnp.int32, (blk, blk), 0) <= lax.broadcasted_iota(jnp.int32, (blk, blk), 1)).astype(BF16)

    def ranks(mask):
        mf = mask.astype(F32)
        incl = _dot(mf.astype(BF16), tri)
        run = jnp.zeros((ne, 1), F32)
        offs = []
        tots = []
        for c in range(nc):
            offs.append(run)
            tots.append(chunk(incl, c)[:, blk - 1:blk])
            run = run + tots[-1]
        return incl - mf, jnp.concatenate(offs, axis=0), jnp.concatenate(tots, axis=0)

    eq_local, eq_off, _ = ranks(eq)
    mask = gt | (eq & ((eq_local + eq_off) < tile_e(need).astype(F32)))
    local, off, tot = ranks(mask)
    off_i = off.astype(jnp.int32)
    start = (off_i // SLOT_ALIGN) * SLOT_ALIGN
    slot = (off_i - start) + local.astype(jnp.int32)
    slot_ref[0] = jnp.where(mask, slot, -1)
    start_ref[0] = jnp.broadcast_to(start, (nc * ne, blk))
    extent_ref[0] = jnp.broadcast_to((off_i - start) + tot.astype(jnp.int32), (nc * ne, blk))


def _route(afft, cap):
    bsz, nc, ne, blk = afft.shape
    a2 = afft.reshape(bsz, nc * ne, blk)
    spec = pl.BlockSpec((1, nc * ne, blk), lambda b: (b, 0, 0))
    slot, start, extent = pl.pallas_call(
        functools.partial(_route_kernel, nc=nc, ne=ne, cap=cap), grid=(bsz,),
        in_specs=[spec], out_specs=[spec, spec, spec],
        out_shape=[jax.ShapeDtypeStruct(a2.shape, jnp.int32)] * 3,
        compiler_params=_cparams(("parallel",)),
    )(a2)
    flat = lambda v: v[:, :, 0].reshape(bsz * nc * ne)
    return slot.reshape(bsz, nc, ne, blk), flat(start), flat(extent)


def _max_extent(we_ref, base, n):
    mx = we_ref[base]
    for j in range(1, n):
        mx = jnp.maximum(mx, we_ref[base + j])
    return mx


def _dispatch_kernel(ws_ref, we_ref, h_ref, afft_ref, slot_ref, xs_ref, *, nc, ne, eg, cd):
    b = pl.program_id(0)
    g = pl.program_id(1)
    i = pl.program_id(2)
    dm = h_ref.shape[1]
    blk = ATT_BLOCK

    @pl.when(i == 0)
    def _():
        xs_ref[...] = jnp.zeros_like(xs_ref)

    def scatter_rows(win, k, base):
        hb = h_ref[k * blk:(k + 1) * blk, :].astype(BF16)
        a = afft_ref[0, k]
        a_hi = a.astype(BF16)
        r1 = a - a_hi.astype(F32)
        a_mid = r1.astype(BF16)
        a_lo = (r1 - a_mid.astype(F32)).astype(BF16)
        a3 = jnp.concatenate([a_hi, a_mid, a_lo, jnp.zeros((GATE_LANES - 3 * ne, blk), BF16)], axis=0)
        riota = lax.broadcasted_iota(jnp.int32, (win, blk), 0)
        onehots = jnp.concatenate([(riota == slot_ref[0, k, 0, j:j + 1, :]).astype(BF16) for j in range(eg)], axis=0)
        res = _dot(onehots, hb)
        resg = _dot_nt(onehots, a3)
        for j in range(eg):
            w = pl.multiple_of(ws_ref[base + j], SLOT_ALIGN)
            head = pl.ds(w, SLOT_ALIGN)
            tail = pl.ds(w + SLOT_ALIGN, win - SLOT_ALIGN)
            for cols, r in ((slice(0, dm), res), (slice(dm, dm + GATE_LANES), resg)):
                rj = r[j * win:(j + 1) * win]
                xs_ref[0, j, head, cols] = (xs_ref[0, j, head, cols].astype(F32) + rj[0:SLOT_ALIGN]).astype(BF16)
                xs_ref[0, j, tail, cols] = rj[SLOT_ALIGN:].astype(BF16)

    for k in range(cd):
        base = (b * nc + i * cd + k) * ne + g * eg
        small = _max_extent(we_ref, base, eg) <= SMALL_WIN
        pl.when(small)(functools.partial(scatter_rows, SMALL_WIN, k, base))
        pl.when(jnp.logical_not(small))(functools.partial(scatter_rows, WIN, k, base))


def _dispatch(h1, afft, slot, wstart, wextent, cap, eg=8):
    bsz, nc, ne, blk = slot.shape
    t, dm = h1.shape
    cp = cap + WIN
    cd = math.gcd(DISPATCH_STEP_CHUNKS, nc)
    ns = nc // cd
    return pl.pallas_call(
        functools.partial(_dispatch_kernel, nc=nc, ne=ne, eg=eg, cd=cd),
        grid_spec=pltpu.PrefetchScalarGridSpec(
            num_scalar_prefetch=2, grid=(bsz, ne // eg, ns),
            in_specs=[pl.BlockSpec((cd * blk, dm), lambda b, g, i, ws, we: (b * ns + i, 0)),
                      pl.BlockSpec((1, cd, ne, blk), lambda b, g, i, ws, we: (b, i, 0, 0)),
                      pl.BlockSpec((1, cd, 1, eg, blk), lambda b, g, i, ws, we: (b, i, g, 0, 0))],
            out_specs=pl.BlockSpec((1, eg, cp, dm + GATE_LANES), lambda b, g, i, ws, we: (b, g, 0, 0))),
        out_shape=jax.ShapeDtypeStruct((bsz, ne, cp, dm + GATE_LANES), BF16),
        compiler_params=_cparams(("parallel", "parallel", "arbitrary")),
    )(wstart, wextent, h1, afft, slot.reshape(bsz, nc, ne // eg, eg, blk))


def _ffn_kernel(xs_ref, wg_ref, wu_ref, wd_ref, out_ref, acc_ref, *, cap, nf, ne):
    e = pl.program_id(0)
    f = pl.program_id(1)
    bsz = xs_ref.shape[0]
    dm = wd_ref.shape[3]
    wg = wg_ref[0, 0].astype(BF16)
    wu = wu_ref[0, 0].astype(BF16)
    wd = wd_ref[0, 0].astype(BF16)
    for b in range(bsz):
        x = xs_ref[b, 0, :, 0:dm]
        hg = _dot(x, wg)
        hu = _dot(x, wu)
        hdn = ((hg * jax.nn.sigmoid(hg)) * hu).astype(BF16)
        contrib = _dot(hdn, wd)
        rows = slice(b * cap, (b + 1) * cap)

        @pl.when(f == 0)
        def _():
            acc_ref[rows, :] = contrib

        @pl.when(f > 0)
        def _():
            acc_ref[rows, :] = acc_ref[rows, :] + contrib

    @pl.when(f == nf - 1)
    def _():
        cp = out_ref.shape[2]
        lane = lax.broadcasted_iota(jnp.int32, (cap, GATE_LANES), 1)
        mine = ((lane % ne) == e) & (lane < 3 * ne)
        for b in range(bsz):
            pieces = xs_ref[b, 0, :, dm:dm + GATE_LANES].astype(F32)
            gate = jnp.sum(jnp.where(mine, pieces, 0.0), axis=1, keepdims=True)
            out_ref[b, 0, 0:cap, :] = (acc_ref[b * cap:(b + 1) * cap, :] * gate).astype(BF16)
            out_ref[b, 0, cap:cp, :] = jnp.zeros((cp - cap, dm), BF16)


def _ffn(xs, wg, wu, wd, l, cap, tf=512):
    bsz, ne, cp, dx = xs.shape
    dm = wg.shape[2]
    ff = wg.shape[-1]
    nf = ff // tf
    return pl.pallas_call(
        functools.partial(_ffn_kernel, cap=cap, nf=nf, ne=ne), grid=(ne, nf),
        in_specs=[pl.BlockSpec((bsz, 1, cap, dx), lambda e, f: (0, e, 0, 0)),
                  pl.BlockSpec((1, 1, dm, tf), lambda e, f: (l, e, 0, f)),
                  pl.BlockSpec((1, 1, dm, tf), lambda e, f: (l, e, 0, f)),
                  pl.BlockSpec((1, 1, tf, dm), lambda e, f: (l, e, f, 0))],
        out_specs=pl.BlockSpec((bsz, 1, cp, dm), lambda e, f: (0, e, 0, 0)),
        out_shape=jax.ShapeDtypeStruct((bsz, ne, cp, dm), BF16),
        scratch_shapes=[pltpu.VMEM((bsz * cap, dm), F32)],
        compiler_params=_cparams(("parallel", "arbitrary")),
    )(xs, wg, wu, wd)


def _combine_kernel(ws_ref, we_ref, out_ref, slott_ref, h1_ref, lng_ref, lnb_ref, h2_ref, rhs_scr, *, nc, ne, cc):
    b = pl.program_id(0)
    i = pl.program_id(1)
    blk = ATT_BLOCK

    def finish(k, tot):
        rows = slice(k * blk, (k + 1) * blk)
        h2_ref[rows, :] = _layer_norm(DEEPNORM_ALPHA * h1_ref[rows, :] + tot, lng_ref[0], lnb_ref[0])

    def gather_packed(k, base):
        kk = ne * SMALL_WIN
        slot_t = slott_ref[0, k].astype(F32).astype(BF16)
        expand = (lax.broadcasted_iota(jnp.int32, (ne, kk), 1) // SMALL_WIN
                  == lax.broadcasted_iota(jnp.int32, (ne, kk), 0)).astype(BF16)
        spread = _dot(slot_t, expand)
        row = (lax.broadcasted_iota(jnp.int32, (blk, kk), 1) % SMALL_WIN).astype(F32)
        onehot = (spread == row).astype(BF16)
        for e in range(ne):
            w = pl.multiple_of(ws_ref[base + e], SLOT_ALIGN)
            rhs_scr[e * SMALL_WIN:(e + 1) * SMALL_WIN, :] = out_ref[0, e, pl.ds(w, SMALL_WIN), :]
        finish(k, _dot(onehot, rhs_scr[...]))

    def gather_per_expert(k, base):
        slot_t = slott_ref[0, k]
        liota = lax.broadcasted_iota(jnp.int32, (blk, WIN), 1)
        tot = None
        for e in range(ne):
            w = pl.multiple_of(ws_ref[base + e], SLOT_ALIGN)
            onehot = (liota == slot_t[:, e:e + 1]).astype(BF16)
            y = _dot(onehot, out_ref[0, e, pl.ds(w, WIN), :])
            tot = y if tot is None else tot + y
        finish(k, tot)

    for k in range(cc):
        base = (b * nc + i * cc + k) * ne
        small = _max_extent(we_ref, base, ne) <= SMALL_WIN
        pl.when(small)(functools.partial(gather_packed, k, base))
        pl.when(jnp.logical_not(small))(functools.partial(gather_per_expert, k, base))


def _combine(out, slot, wstart, wextent, h1, lng3, lnb3, l, seq):
    bsz, ne, cp, dm = out.shape
    nc = seq // ATT_BLOCK
    blk = ATT_BLOCK
    t = h1.shape[0]
    slot_t = jnp.swapaxes(slot, 2, 3)
    cc = math.gcd(COMBINE_STEP_CHUNKS, nc)
    ns = nc // cc
    vec = pl.BlockSpec((1, 1, dm), lambda b, i, ws, we: (l, 0, 0))
    return pl.pallas_call(
        functools.partial(_combine_kernel, nc=nc, ne=ne, cc=cc),
        grid_spec=pltpu.PrefetchScalarGridSpec(
            num_scalar_prefetch=2, grid=(bsz, ns),
            in_specs=[pl.BlockSpec((1, ne, cp, dm), lambda b, i, ws, we: (b, 0, 0, 0), pipeline_mode=pl.Buffered(1)),
                      pl.BlockSpec((1, cc, blk, ne), lambda b, i, ws, we: (b, i, 0, 0)),
                      pl.BlockSpec((cc * blk, dm), lambda b, i, ws, we: (b * ns + i, 0)),
                      vec, vec],
            out_specs=pl.BlockSpec((cc * blk, dm), lambda b, i, ws, we: (b * ns + i, 0)),
            scratch_shapes=[pltpu.VMEM((ne * SMALL_WIN, dm), BF16)]),
        out_shape=jax.ShapeDtypeStruct((t, dm), F32),
        compiler_params=_cparams(("parallel", "arbitrary")),
    )(wstart, wextent, out, slot_t, h1, lng3, lnb3)


def kernel(x, ln_in_g, ln_in_b, w_in, ret_theta, ssm_lambda_re, ssm_lambda_im, ssm_log_step, ssm_b_re, ssm_b_im,
           ssm_c_re, ssm_c_im, ssm_d, ssm_w_glu, ssm_b_glu, attn_sink, w_out, ln1_g, ln1_b, router_w,
           exp_w_gate, exp_w_up, exp_w_down, ln2_g, ln2_b):
    bsz, seq, dm = x.shape
    depth = w_in.shape[0]
    ne = router_w.shape[-1]
    cap = EC_FACTOR * seq // ne
    t = bsz * seq
    w_in_bf = w_in.astype(BF16)
    wt_bf = jnp.swapaxes(jnp.concatenate([w_in[:, :, COL_SU:COL_SU + SSM_WIDTH], w_in[:, :, COL_RK:COL_RK + RET_WIDTH]],
                                         axis=2), 1, 2).astype(BF16)
    w_out_bf = w_out.astype(BF16)
    w_glu_bf = ssm_w_glu.astype(BF16)
    lg = jax.nn.log_sigmoid(ret_theta.astype(F32))
    sink = attn_sink.astype(F32)
    ssm_par = _ssm_params(ssm_lambda_re, ssm_lambda_im, ssm_log_step, ssm_b_re, ssm_b_im, ssm_c_re, ssm_c_im)
    vec3 = lambda a: a.astype(F32).reshape(depth, 1, -1)
    d3, bglu3, ln1g3, ln1b3, ln2g3, ln2b3 = map(vec3, (ssm_d, ssm_b_glu, ln1_g, ln1_b, ln2_g, ln2_b))
    rwt = jnp.swapaxes(router_w.astype(F32), 1, 2)
    rwt_hi = rwt.astype(BF16)
    rwt = jnp.concatenate([rwt_hi, (rwt - rwt_hi.astype(F32)).astype(BF16)], axis=1)
    bias = _attention_bias(seq)

    h = x
    for l in range(depth):
        if l == 0:
            h, proj, ut, kt = _inproj(h, w_in_bf, wt_bf, l, ln=(ln_in_g, ln_in_b))
        else:
            proj, ut, kt = _inproj(h, w_in_bf, wt_bf, l)
        h2d = h.reshape(t, dm)
        proj = proj.reshape(t, D_IN)
        yt = _ssm_conv(ut, ssm_par, l, bsz)
        sf, sb = _ret_states(proj, kt, lg, l, bsz, seq)
        h1, afft = _mixer(proj, sf, sb, yt, h2d, bias, lg, sink, w_out_bf, w_glu_bf, d3, bglu3, ln1g3, ln1b3,
                          rwt, l, bsz, seq)
        slot, wstart, wextent = _route(afft, cap)
        xs = _dispatch(h1, afft, slot, wstart, wextent, cap)
        out = _ffn(xs, exp_w_gate, exp_w_up, exp_w_down, l, cap)
        h = _combine(out, slot, wstart, wextent, h1, ln2g3, ln2b3, l, seq).reshape(bsz, seq, dm)
    return h
```

```python
import functools
import math

import jax
import jax.numpy as jnp
from jax import lax
from jax.experimental import pallas as pl
from jax.experimental.pallas import tpu as pltpu

F32 = jnp.float32
BF16 = jnp.bfloat16

RET_HEADS = 4
RET_DK = 64
RET_CHUNK = 128
RET_WIDTH = RET_HEADS * RET_DK
SSM_CPG = 16
SSM_GROUPS = 16
SSM_STATE = 64
SSM_WIDTH = SSM_CPG * SSM_GROUPS
ATT_HEADS = 8
ATT_KV_HEADS = 2
ATT_GQ = ATT_HEADS // ATT_KV_HEADS
ATT_HEAD_DIM = 64
ATT_BLOCK = 128
ATT_WINDOW = 128
EC_FACTOR = 2
DEPTH = 2
DEEPNORM_ALPHA = (2.0 * DEPTH) ** 0.25
LN_EPS = 1e-5
NEG_INF = -1e30

COL_RQ, COL_RK, COL_RV, COL_RG, COL_SU, COL_AQ, COL_AK, COL_AV = 0, 256, 512, 768, 1024, 1280, 1792, 1920
D_IN = 2048

CHUNK = 128
SUBLANES = 8
SLOT_ALIGN = 16
WIN = CHUNK + SLOT_ALIGN
LANES = 128
SMALL_WIN = LANES // 2
DISPATCH_STEP_CHUNKS = 8
COMBINE_STEP_CHUNKS = 4
MIXER_STEP_CHUNKS = 2
GATE_LANES = 128
RET_STEP_CHUNKS = 8
VMEM_LIMIT = 56 * 1024 * 1024
ROUTE_REFINE_STEPS = 10


def _cparams(sem):
    return pltpu.CompilerParams(dimension_semantics=sem, vmem_limit_bytes=VMEM_LIMIT)


def _layer_norm(x, g, b):
    mu = jnp.mean(x, axis=-1, keepdims=True)
    xc = x - mu
    var = jnp.mean(xc * xc, axis=-1, keepdims=True)
    return xc * lax.rsqrt(var + LN_EPS) * g + b


def _dot(a, b):
    return jnp.dot(a, b, preferred_element_type=F32)


def _dot_hp(a, b):
    return jnp.dot(a, b, precision=lax.Precision.HIGHEST, preferred_element_type=F32)


def _dot_nt(a, b):
    return lax.dot_general(a, b, (((1,), (1,)), ((), ())), preferred_element_type=F32)


def _cmul(ar, ai, br, bi):
    return ar * br - ai * bi, ar * bi + ai * br


def _cpow(br, bi, expo, nbits):
    shape = (br.shape[0], expo.shape[1])
    rr = jnp.ones(shape, F32)
    ri = jnp.zeros(shape, F32)
    for j in range(nbits):
        bit = jnp.broadcast_to(((expo >> j) & 1) == 1, shape)
        nr, ni = _cmul(rr, ri, br, bi)
        rr = jnp.where(bit, nr, rr)
        ri = jnp.where(bit, ni, ri)
        if j + 1 < nbits:
            br, bi = _cmul(br, bi, br, bi)
    return rr, ri


def _inproj_kernel(*refs, apply_ln):
    if apply_ln:
        x_ref, g_ref, b_ref, w_ref, wt_ref, h_ref, proj_ref, ut_ref, kt_ref = refs
    else:
        x_ref, w_ref, wt_ref, proj_ref, ut_ref, kt_ref = refs
    nb, tl, d = x_ref.shape
    h = x_ref[...].reshape(nb * tl, d)
    if apply_ln:
        h = _layer_norm(h, g_ref[...], b_ref[...])
        h_ref[...] = h.reshape(nb, tl, d)
    hb = h.astype(BF16)
    proj_ref[...] = _dot(hb, w_ref[0]).reshape(nb, tl, -1)
    tt = _dot_nt(wt_ref[0], hb)
    k = tl // CHUNK
    for j in range(k):
        for b in range(nb):
            ut_ref[:, j * nb + b, :] = tt[0:SSM_WIDTH, (b * k + j) * CHUNK:(b * k + j + 1) * CHUNK]
    for b in range(nb):
        kt_ref[b] = tt[SSM_WIDTH:SSM_WIDTH + RET_WIDTH, b * tl:(b + 1) * tl]


def _inproj(x3, w_bf, wt_bf, l, ln=None):
    bsz, seq, d = x3.shape
    n = w_bf.shape[2]
    k = SUBLANES // bsz
    tl = k * CHUNK
    nc = seq // CHUNK
    xspec = pl.BlockSpec((bsz, tl, d), lambda i: (0, i, 0))
    wspec = pl.BlockSpec((1, d, n), lambda i: (l, 0, 0))
    wtspec = pl.BlockSpec((1,) + wt_bf.shape[1:], lambda i: (l, 0, 0))
    out_specs = [pl.BlockSpec((bsz, tl, n), lambda i: (0, i, 0)),
                 pl.BlockSpec((SSM_WIDTH, k * bsz, CHUNK), lambda i: (0, i, 0)),
                 pl.BlockSpec((bsz, RET_WIDTH, tl), lambda i: (0, 0, i))]
    out_shape = [jax.ShapeDtypeStruct((bsz, seq, n), F32),
                 jax.ShapeDtypeStruct((SSM_WIDTH, nc * bsz, CHUNK), F32),
                 jax.ShapeDtypeStruct((bsz, RET_WIDTH, seq), F32)]
    if ln is None:
        return pl.pallas_call(
            functools.partial(_inproj_kernel, apply_ln=False), grid=(seq // tl,),
            in_specs=[xspec, wspec, wtspec], out_specs=out_specs, out_shape=out_shape,
            compiler_params=_cparams(("parallel",)),
        )(x3, w_bf, wt_bf)
    g, b = ln
    vec = pl.BlockSpec((1, d), lambda i: (0, 0))
    return pl.pallas_call(
        functools.partial(_inproj_kernel, apply_ln=True), grid=(seq // tl,),
        in_specs=[xspec, vec, vec, wspec, wtspec], out_specs=[xspec] + out_specs,
        out_shape=[jax.ShapeDtypeStruct(x3.shape, F32)] + out_shape,
        compiler_params=_cparams(("parallel",)),
    )(x3, g.reshape(1, d), b.reshape(1, d), w_bf, wt_bf)


def _ssm_params(lam_re, lam_im, log_step, b_re, b_im, c_re, c_im):
    lr, li = lam_re.astype(F32), lam_im.astype(F32)
    step = jnp.exp(log_step.astype(F32))[..., None]
    er = jnp.exp(lr * step)
    lbr, lbi = er * jnp.cos(li * step), er * jnp.sin(li * step)
    den = lr * lr + li * li
    fr = ((lbr - 1.0) * lr + lbi * li) / den
    fi = (lbi * lr - (lbr - 1.0) * li) / den
    br = jnp.swapaxes(b_re.astype(F32), -1, -2)[:, None]
    bi = jnp.swapaxes(b_im.astype(F32), -1, -2)[:, None]
    bbr = fr[..., None, :] * br - fi[..., None, :] * bi
    bbi = fr[..., None, :] * bi + fi[..., None, :] * br
    lcol = jnp.stack([lbr[:, 0], lbi[:, 0], lbr[:, 1], lbi[:, 1]], axis=-1)
    lrow = jnp.swapaxes(lcol, -1, -2)
    lrow = jnp.concatenate([lrow, jnp.zeros_like(lrow)], axis=2)
    bt = jnp.stack([bbr[:, 0], bbi[:, 0], bbr[:, 1], bbi[:, 1]], axis=2)
    cr, ci = c_re.astype(F32), c_im.astype(F32)
    c4 = jnp.stack([cr[:, 0], ci[:, 0], cr[:, 1], ci[:, 1]], axis=2)
    return lcol, lrow, bt, jnp.swapaxes(bt, -1, -2), c4, jnp.swapaxes(c4, -1, -2)


def _ssm_kernel(ut_ref, lcol_ref, lrow_ref, bt_ref, bcol_ref, c_ref, ct_ref, y_ref,
                w_scr, wst_scr, g_scr, vf_scr, vb_scr, acc_scr, s_scr, sp_scr, x_scr, *, nb, nchunks):
    tc, p, cpg = CHUNK, SSM_STATE, SSM_CPG
    lcol = lcol_ref[0, 0]
    lf = (lcol[:, 0:1], lcol[:, 1:2])
    lb = (lcol[:, 2:3], lcol[:, 3:4])

    m_row = lax.broadcasted_iota(jnp.int32, (1, tc), 1)
    pfr, pfi = _cpow(*lf, m_row, 7)
    pf1r, pf1i = _cmul(pfr, pfi, *lf)
    prr, pri = _cpow(*lf, tc - 1 - m_row, 7)
    pbr, pbi = _cpow(*lb, tc - m_row, 8)
    pqr, pqi = _cpow(*lb, m_row, 7)

    def rep_co(x):
        return jnp.broadcast_to(x[:, None, :], (cpg, cpg, p)).reshape(cpg * cpg, p)

    def rep_ci(x):
        return jnp.broadcast_to(x[None, :, :], (cpg, cpg, p)).reshape(cpg * cpg, p)

    cbfr, cbfi = _cmul(rep_co(c_ref[0, 0, 0]), rep_co(c_ref[0, 0, 1]), rep_ci(bt_ref[0, 0, 0]), rep_ci(bt_ref[0, 0, 1]))
    cbbr, cbbi = _cmul(rep_co(c_ref[0, 0, 2]), rep_co(c_ref[0, 0, 3]), rep_ci(bt_ref[0, 0, 2]), rep_ci(bt_ref[0, 0, 3]))
    lane = lax.broadcasted_iota(jnp.int32, (cpg * cpg, tc), 1)
    kb0 = jnp.sum(cbbr, axis=1, keepdims=True)
    vf_scr[...] = _dot_hp(cbfr, pfr) - _dot_hp(cbfi, pfi) + jnp.where(lane == 0, kb0, 0.0)
    vb_scr[...] = _dot_hp(cbbr, pbr) - _dot_hp(cbbi, pbi)

    for co in range(cpg):
        cols = slice(co * tc, (co + 1) * tc)
        for r, (ar, ai) in enumerate(((pf1r, pf1i), (pbr, pbi))):
            gr, gi = _cmul(ct_ref[0, 0, 2 * r][:, co:co + 1], ct_ref[0, 0, 2 * r + 1][:, co:co + 1], ar, ai)
            g_scr[2 * r * p:(2 * r + 1) * p, cols] = gr.astype(BF16)
            g_scr[(2 * r + 1) * p:(2 * r + 2) * p, cols] = (-gi).astype(BF16)
        for r, (ar, ai) in enumerate(((prr, pri), (pqr, pqi))):
            sr, si = _cmul(bcol_ref[0, 0, 2 * r][:, co:co + 1], bcol_ref[0, 0, 2 * r + 1][:, co:co + 1], ar, ai)
            wst_scr[co, 2 * r * p:(2 * r + 1) * p, :] = sr.astype(BF16)
            wst_scr[co, (2 * r + 1) * p:(2 * r + 2) * p, :] = si.astype(BF16)

    s_idx = lax.broadcasted_iota(jnp.int32, (tc, tc), 0)
    j_idx = lax.broadcasted_iota(jnp.int32, (tc, tc), 1)
    fwd_part = j_idx < tc - s_idx
    acc_scr[...] = jnp.zeros_like(acc_scr)
    s_scr[...] = jnp.zeros_like(s_scr)

    def per_channel(ci, carry):
        for co in range(cpg):
            rf = jnp.broadcast_to(vf_scr[pl.ds(co * cpg + ci, 1), :], (tc, tc))
            rb = jnp.broadcast_to(vb_scr[pl.ds(co * cpg + ci, 1), :], (tc, tc))
            m = pltpu.roll(jnp.where(fwd_part, rf, rb), 0, 1, stride=1, stride_axis=0)
            w_scr[ci, :, co * tc:(co + 1) * tc] = m.astype(BF16)
        u = ut_ref[ci].astype(BF16)
        acc_scr[...] += _dot(u, w_scr[ci])
        s_scr[...] += _dot_nt(u, wst_scr[ci])
        return carry

    lax.fori_loop(0, cpg, per_channel, 0)

    lrow = lrow_ref[0, 0]
    dec = []
    for r in range(2):
        dr, di = lrow[2 * r:2 * r + 1], lrow[2 * r + 1:2 * r + 2]
        for _ in range(7):
            dr, di = _cmul(dr, di, dr, di)
        dec.append((dr, di))
    for q in range(4):
        sp_scr[q] = s_scr[:, q * p:(q + 1) * p]
    xfr = xfi = xbr = xbi = jnp.zeros((nb, p), F32)
    for i in range(nchunks):
        rf = slice(i * nb, (i + 1) * nb)
        rb = slice((nchunks - 1 - i) * nb, (nchunks - i) * nb)
        x_scr[0, rf, :] = xfr
        x_scr[1, rf, :] = xfi
        x_scr[2, rb, :] = xbr
        x_scr[3, rb, :] = xbi
        xfr, xfi = _cmul(dec[0][0], dec[0][1], xfr, xfi)
        xbr, xbi = _cmul(dec[1][0], dec[1][1], xbr, xbi)
        xfr, xfi = xfr + sp_scr[0, rf, :], xfi + sp_scr[1, rf, :]
        xbr, xbi = xbr + sp_scr[2, rb, :], xbi + sp_scr[3, rb, :]
    xin = jnp.concatenate([x_scr[q] for q in range(4)], axis=1).astype(BF16)
    y = acc_scr[...] + _dot(xin, g_scr[...])
    for co in range(cpg):
        y_ref[co] = y[:, co * tc:(co + 1) * tc]


def _ssm_conv(ut, params, l, bsz):
    _, r, tc = ut.shape
    cpg, p, ng = SSM_CPG, SSM_STATE, SSM_GROUPS

    def pspec(a):
        return pl.BlockSpec((1, 1) + a.shape[2:], lambda g: (l, g) + (0,) * (a.ndim - 2))
    return pl.pallas_call(
        functools.partial(_ssm_kernel, nb=bsz, nchunks=r // bsz), grid=(ng,),
        in_specs=[pl.BlockSpec((cpg, r, tc), lambda g: (g, 0, 0))] + [pspec(a) for a in params],
        out_specs=pl.BlockSpec((cpg, r, tc), lambda g: (g, 0, 0)),
        out_shape=jax.ShapeDtypeStruct(ut.shape, F32),
        scratch_shapes=[pltpu.VMEM((cpg, tc, cpg * tc), BF16), pltpu.VMEM((cpg, 4 * p, tc), BF16),
                        pltpu.VMEM((4 * p, cpg * tc), BF16),
                        pltpu.VMEM((cpg * cpg, tc), F32), pltpu.VMEM((cpg * cpg, tc), F32),
                        pltpu.VMEM((r, cpg * tc), F32), pltpu.VMEM((r, 4 * p), F32),
                        pltpu.VMEM((4, r, p), F32), pltpu.VMEM((4, r, p), F32)],
        compiler_params=_cparams(("parallel",)),
    )(ut, *params)


def _retstate_kernel(lg_ref, ktf_ref, vf_ref, ktb_ref, vb_ref, sf_ref, sb_ref, accf, accb, *, l, cs):
    i = pl.program_id(1)

    @pl.when(i == 0)
    def _():
        accf[...] = jnp.zeros_like(accf)
        accb[...] = jnp.zeros_like(accb)

    ch = RET_CHUNK
    pos = lax.broadcasted_iota(jnp.int32, (1, ch), 1).astype(F32)
    one = jnp.ones((1, RET_DK), F32)
    wts = []
    for h in range(RET_HEADS):
        lgf = lg_ref[l, 0, h]
        lgb = lg_ref[l, 1, h]
        wts.append((jnp.exp(lgf * (ch - 1.0 - pos)) * RET_DK ** -0.5, jnp.exp(lgb * pos) * RET_DK ** -0.5,
                    jnp.exp(one * (lgf * ch)), jnp.exp(one * (lgb * ch))))
    for j in range(cs):
        jb = cs - 1 - j
        sf_ref[0, j] = accf[...]
        sb_ref[0, jb] = accb[...]
        for h in range(RET_HEADS):
            rows = slice(h * RET_DK, (h + 1) * RET_DK)
            wf, wb, decf, decb = wts[h]
            kf = (ktf_ref[0, rows, j * ch:(j + 1) * ch] * wf).astype(BF16)
            kb = (ktb_ref[0, rows, jb * ch:(jb + 1) * ch] * wb).astype(BF16)
            accf[rows, :] = decf * accf[rows, :] + _dot(kf, vf_ref[j * ch:(j + 1) * ch, rows].astype(BF16))
            accb[rows, :] = decb * accb[rows, :] + _dot(kb, vb_ref[jb * ch:(jb + 1) * ch, rows].astype(BF16))


def _ret_states(proj, kt, lg, l, bsz, seq):
    nc = seq // RET_CHUNK
    cs = min(RET_STEP_CHUNKS, nc)
    ns = nc // cs
    w = RET_WIDTH
    tl = cs * RET_CHUNK
    vcol = COL_RV // w
    st = jax.ShapeDtypeStruct((bsz, nc, w, RET_DK), F32)
    return pl.pallas_call(
        functools.partial(_retstate_kernel, l=l, cs=cs), grid=(bsz, ns),
        in_specs=[pl.BlockSpec(memory_space=pltpu.SMEM),
                  pl.BlockSpec((1, w, tl), lambda b, i: (b, 0, i)),
                  pl.BlockSpec((tl, w), lambda b, i: (b * ns + i, vcol)),
                  pl.BlockSpec((1, w, tl), lambda b, i: (b, 0, ns - 1 - i)),
                  pl.BlockSpec((tl, w), lambda b, i: (b * ns + ns - 1 - i, vcol))],
        out_specs=[pl.BlockSpec((1, cs, w, RET_DK), lambda b, i: (b, i, 0, 0)),
                   pl.BlockSpec((1, cs, w, RET_DK), lambda b, i: (b, ns - 1 - i, 0, 0))],
        out_shape=[st, st],
        scratch_shapes=[pltpu.VMEM((w, RET_DK), F32), pltpu.VMEM((w, RET_DK), F32)],
        compiler_params=_cparams(("parallel", "arbitrary")),
    )(lg, kt, proj, kt, proj)


def _gelu_tanh(x):
    return 0.5 * x * (1.0 + jnp.tanh(math.sqrt(2.0 / math.pi) * (x + 0.044715 * (x * x * x))))


def _attention_bias(seq):
    blk = ATT_BLOCK
    s_idx = jnp.arange(3 * blk)[None, :]
    t_idx = jnp.arange(blk)[:, None]
    arel = jnp.abs(s_idx - blk - t_idx)
    band = arel <= ATT_WINDOW
    slopes = jnp.exp2(-8.0 * jnp.arange(1, ATT_HEADS + 1, dtype=F32) / ATT_HEADS)
    alibi = -slopes[:, None, None] * arel.astype(F32)[None]
    variants = []
    for prev_ok, next_ok in ((False, True), (True, True), (True, False)):
        ok = band & (prev_ok | (s_idx >= blk)) & (next_ok | (s_idx < 2 * blk))
        variants.append(jnp.where(ok[None], alibi, NEG_INF).reshape(ATT_KV_HEADS, ATT_GQ * blk, 3 * blk))
    return jnp.stack(variants)


def _mixer_kernel(lg_ref, sink_ref, proj_ref, kvp_ref, kvn_ref, sf_ref, sb_ref, yt_ref, h_ref, bias0_ref, bias1_ref,
                  wout_ref, wglu_ref, d_ref, bglu_ref, lng_ref, lnb_ref, rwt_ref,
                  h1_ref, afft_ref, mix_scr, sret_scr, satt_scr, pret_scr, patt_scr, *, l, nb):
    b = pl.program_id(1)
    cm = MIXER_STEP_CHUNKS
    bias_refs = (bias0_ref, bias1_ref)
    ch = RET_CHUNK
    dk = RET_DK
    blk = ATT_BLOCK
    hd = ATT_HEAD_DIM
    kvw = ATT_KV_HEADS * hd

    def kv_block(u, cols):
        if u < 0:
            return kvp_ref[:, cols]
        if u >= cm:
            return kvn_ref[:, cols]
        return proj_ref[u * ch:(u + 1) * ch, COL_AK + cols.start:COL_AK + cols.stop]

    vbs = {}
    for u in range(cm):
        r = slice(u * ch, (u + 1) * ch)
        for h in range(RET_HEADS):
            q = proj_ref[r, COL_RQ + h * dk:COL_RQ + (h + 1) * dk]
            k = proj_ref[r, COL_RK + h * dk:COL_RK + (h + 1) * dk] * dk ** -0.5
            sret_scr[u * RET_HEADS + h] = _dot_nt(q.astype(BF16), k.astype(BF16))
        for kvh in range(ATT_KV_HEADS):
            kc = slice(kvh * hd, (kvh + 1) * hd)
            vc = slice(kvw + kvh * hd, kvw + (kvh + 1) * hd)
            kb = jnp.concatenate([kv_block(u + d, kc) for d in (-1, 0, 1)], axis=0).astype(BF16)
            vbs[u, kvh] = jnp.concatenate([kv_block(u + d, vc) for d in (-1, 0, 1)], axis=0).astype(BF16)
            q4 = jnp.concatenate([proj_ref[r, COL_AQ + (kvh * ATT_GQ + gq) * hd:COL_AQ + (kvh * ATT_GQ + gq + 1) * hd]
                                  for gq in range(ATT_GQ)], axis=0) * hd ** -0.5
            satt_scr[u * ATT_KV_HEADS + kvh] = _dot_nt(q4.astype(BF16), kb)

    pos = lax.broadcasted_iota(jnp.int32, (ch, 1), 0).astype(F32)
    dist = lax.broadcasted_iota(jnp.int32, (ch, ch), 0) - lax.broadcasted_iota(jnp.int32, (ch, ch), 1)
    adist = jnp.abs(dist).astype(F32)
    for h in range(RET_HEADS):
        lgf = lg_ref[l, 0, h]
        lgb = lg_ref[l, 1, h]
        dmat = jnp.exp(jnp.where(dist >= 0, lgf, lgb) * adist)
        wqf = jnp.exp(lgf * (pos + 1.0))
        wqb = jnp.exp(lgb * (ch - pos))
        for u in range(cm):
            q = proj_ref[u * ch:(u + 1) * ch, COL_RQ + h * dk:COL_RQ + (h + 1) * dk]
            uh = u * RET_HEADS + h
            pret_scr[uh, :, 0:ch] = (sret_scr[uh] * dmat).astype(BF16)
            pret_scr[uh, :, ch:ch + dk] = (q * wqf).astype(BF16)
            pret_scr[uh, :, ch + dk:ch + 2 * dk] = (q * wqb).astype(BF16)
    for u in range(cm):
        for kvh in range(ATT_KV_HEADS):
            uk = u * ATT_KV_HEADS + kvh
            for gq in range(ATT_GQ):
                rows = slice(gq * blk, (gq + 1) * blk)
                sink = sink_ref[l, kvh * ATT_GQ + gq]
                sc = satt_scr[uk, rows, :] + bias_refs[u][0, kvh, rows, :]
                m = jnp.maximum(jnp.max(sc, axis=-1, keepdims=True), sink)
                p = jnp.exp(sc - m)
                denom = jnp.sum(p, axis=-1, keepdims=True) + jnp.exp(sink - m)
                patt_scr[uk, rows, :] = (p * (1.0 / denom)).astype(BF16)

    att_base = RET_WIDTH + SSM_WIDTH
    w_ssm = SSM_WIDTH
    for u in range(cm):
        r = slice(u * ch, (u + 1) * ch)
        for h in range(RET_HEADS):
            rows = slice(h * dk, (h + 1) * dk)
            v = proj_ref[r, COL_RV + h * dk:COL_RV + (h + 1) * dk]
            g = proj_ref[r, COL_RG + h * dk:COL_RG + (h + 1) * dk]
            rhs = jnp.concatenate([v, sf_ref[0, u, rows, :], sb_ref[0, u, rows, :]], axis=0).astype(BF16)
            o = _dot(pret_scr[u * RET_HEADS + h], rhs)
            mu = jnp.mean(o, axis=-1, keepdims=True)
            oc = o - mu
            var = jnp.mean(oc * oc, axis=-1, keepdims=True)
            mix_scr[r, h * dk:(h + 1) * dk] = (g * jax.nn.sigmoid(g)) * (oc * lax.rsqrt(var + LN_EPS))
        for kvh in range(ATT_KV_HEADS):
            o = _dot(patt_scr[u * ATT_KV_HEADS + kvh], vbs[u, kvh])
            for gq in range(ATT_GQ):
                hh = kvh * ATT_GQ + gq
                mix_scr[r, att_base + hh * hd:att_base + (hh + 1) * hd] = o[gq * blk:(gq + 1) * blk]

        yraw = yt_ref[:, pl.ds(u * nb + b, 1), :].reshape(w_ssm, ch).T
        y = yraw + d_ref[0] * proj_ref[r, COL_SU:COL_SU + w_ssm]
        y = _gelu_tanh(y)
        gate = jax.nn.sigmoid(_dot(y.astype(BF16), wglu_ref[0]) + bglu_ref[0])
        mix_scr[r, RET_WIDTH:RET_WIDTH + w_ssm] = y * gate

    mix = _dot(mix_scr[...].astype(BF16), wout_ref[0])
    h1 = _layer_norm(DEEPNORM_ALPHA * h_ref[...] + mix, lng_ref[0], lnb_ref[0])
    h1_ref[...] = h1

    ne = rwt_ref.shape[1] // 2
    h_hi = h1.astype(BF16)
    h_lo = (h1 - h_hi.astype(F32)).astype(BF16)
    both = _dot_nt(rwt_ref[0], h_hi)
    lt = both[0:ne] + both[ne:2 * ne] + _dot_nt(rwt_ref[0, 0:ne, :], h_lo)
    lt = lt - jnp.max(lt, axis=0, keepdims=True)
    et = jnp.exp(lt)
    aff = et / jnp.sum(et, axis=0, keepdims=True)
    for u in range(cm):
        afft_ref[0, u] = aff[:, u * ch:(u + 1) * ch]


def _mixer(proj, sf, sb, yt, h2d, bias, lg, sink, wout_bf, wglu_bf, d3, bglu3, lng3, lnb3, rwt, l, bsz, seq):
    nc = seq // ATT_BLOCK
    cm = MIXER_STEP_CHUNKS
    ns = nc // cm
    assert nc % cm == 0 and cm * bsz == SUBLANES
    t, dm = h2d.shape
    ne = rwt.shape[1] // 2
    kvw = 2 * ATT_KV_HEADS * ATT_HEAD_DIM
    sw = RET_WIDTH
    kvcol = COL_AK // kvw
    blk = ATT_BLOCK

    def layer(a):
        return pl.BlockSpec((1,) + a.shape[1:], lambda i, b: (l,) + (0,) * (a.ndim - 1))
    smem = pl.BlockSpec(memory_space=pltpu.SMEM)
    in_specs = [
        smem, smem,
        pl.BlockSpec((cm * blk, D_IN), lambda i, b: (b * ns + i, 0)),
        pl.BlockSpec((blk, kvw), lambda i, b: (b * nc + jnp.maximum(cm * i - 1, 0), kvcol)),
        pl.BlockSpec((blk, kvw), lambda i, b: (b * nc + jnp.minimum(cm * i + cm, nc - 1), kvcol)),
        pl.BlockSpec((1, cm, sw, RET_DK), lambda i, b: (b, i, 0, 0)),
        pl.BlockSpec((1, cm, sw, RET_DK), lambda i, b: (b, i, 0, 0)),
        pl.BlockSpec((SSM_WIDTH, SUBLANES, CHUNK), lambda i, b: (0, i, 0)),
        pl.BlockSpec((cm * blk, dm), lambda i, b: (b * ns + i, 0)),
        pl.BlockSpec((1,) + bias.shape[1:], lambda i, b: (jnp.where(i == 0, 0, 1), 0, 0, 0)),
        pl.BlockSpec((1,) + bias.shape[1:], lambda i, b: (jnp.where(i == ns - 1, 2, 1), 0, 0, 0)),
        layer(wout_bf), layer(wglu_bf), layer(d3), layer(bglu3), layer(lng3), layer(lnb3), layer(rwt),
    ]
    out_specs = [
        pl.BlockSpec((cm * blk, dm), lambda i, b: (b * ns + i, 0)),
        pl.BlockSpec((1, cm, ne, blk), lambda i, b: (b, i, 0, 0)),
    ]
    out_shape = [jax.ShapeDtypeStruct((t, dm), F32),
                 jax.ShapeDtypeStruct((bsz, nc, ne, blk), F32)]
    scratch = [pltpu.VMEM((cm * blk, dm), F32),
               pltpu.VMEM((cm * RET_HEADS, blk, blk), F32),
               pltpu.VMEM((cm * ATT_KV_HEADS, ATT_GQ * blk, 3 * blk), F32),
               pltpu.VMEM((cm * RET_HEADS, blk, blk + 2 * RET_DK), BF16),
               pltpu.VMEM((cm * ATT_KV_HEADS, ATT_GQ * blk, 3 * blk), BF16)]
    return pl.pallas_call(
        functools.partial(_mixer_kernel, l=l, nb=bsz), grid=(ns, bsz),
        in_specs=in_specs, out_specs=out_specs, out_shape=out_shape, scratch_shapes=scratch,
        compiler_params=_cparams(("parallel", "parallel")),
    )(lg, sink, proj, proj, proj, sf, sb, yt, h2d, bias, bias, wout_bf, wglu_bf, d3, bglu3, lng3, lnb3, rwt)


def _route_kernel(a_ref, slot_ref, start_ref, extent_ref, *, nc, ne, cap):
    blk = ATT_BLOCK
    a = a_ref[0]

    def chunk(x, c):
        return x[c * ne:(c + 1) * ne]

    def count(pred):
        x = pred.astype(jnp.int32)
        tot = chunk(x, 0)
        for c in range(1, nc):
            tot = tot + chunk(x, c)
        return jnp.sum(tot, axis=1, keepdims=True)

    def tile_e(v):
        return jnp.concatenate([v] * nc, axis=0)

    tau = jnp.zeros((ne, 1), jnp.int32)
    for bit in range(30, -1, -1):
        cand = tau | (1 << bit)
        tau = jnp.where(count(a >= tile_e(pltpu.bitcast(cand, F32))) >= cap, cand, tau)
    lo = pltpu.bitcast(tau, F32)
    hi = pltpu.bitcast(tau + 1, F32)
    for _ in range(ROUTE_REFINE_STEPS):
        mid = lo + (hi - lo) * 0.5
        ok = count(a >= tile_e(mid)) >= cap
        lo = jnp.where(ok, mid, lo)
        hi = jnp.where(ok, hi, mid)
    gt = a >= tile_e(hi)
    eq = (a >= tile_e(lo)) & jnp.logical_not(gt)
    need = cap - count(gt)

    tri = (lax.broadcasted_iota(jnp.int32, (blk, blk), 0) <= lax.broadcasted_iota(jnp.int32, (blk, blk), 1)).astype(BF16)

    def ranks(mask):
        mf = mask.astype(F32)
        incl = _dot(mf.astype(BF16), tri)
        run = jnp.zeros((ne, 1), F32)
        offs = []
        tots = []
        for c in range(nc):
            offs.append(run)
            tots.append(chunk(incl, c)[:, blk - 1:blk])
            run = run + tots[-1]
        return incl - mf, jnp.concatenate(offs, axis=0), jnp.concatenate(tots, axis=0)

    eq_local, eq_off, _ = ranks(eq)
    mask = gt | (eq & ((eq_local + eq_off) < tile_e(need).astype(F32)))
    local, off, tot = ranks(mask)
    off_i = off.astype(jnp.int32)
    start = (off_i // SLOT_ALIGN) * SLOT_ALIGN
    slot = (off_i - start) + local.astype(jnp.int32)
    slot_ref[0] = jnp.where(mask, slot, -1)
    start_ref[0] = jnp.broadcast_to(start, (nc * ne, blk))
    extent_ref[0] = jnp.broadcast_to((off_i - start) + tot.astype(jnp.int32), (nc * ne, blk))


def _route(afft, cap):
    bsz, nc, ne, blk = afft.shape
    a2 = afft.reshape(bsz, nc * ne, blk)
    spec = pl.BlockSpec((1, nc * ne, blk), lambda b: (b, 0, 0))
    slot, start, extent = pl.pallas_call(
        functools.partial(_route_kernel, nc=nc, ne=ne, cap=cap), grid=(bsz,),
        in_specs=[spec], out_specs=[spec, spec, spec],
        out_shape=[jax.ShapeDtypeStruct(a2.shape, jnp.int32)] * 3,
        compiler_params=_cparams(("parallel",)),
    )(a2)
    flat = lambda v: v[:, :, 0].reshape(bsz * nc * ne)
    return slot.reshape(bsz, nc, ne, blk), flat(start), flat(extent)


def _max_extent(we_ref, base, n):
    mx = we_ref[base]
    for j in range(1, n):
        mx = jnp.maximum(mx, we_ref[base + j])
    return mx


def _dispatch_kernel(ws_ref, we_ref, h_ref, afft_ref, slot_ref, xs_ref, *, nc, ne, eg, cd):
    b = pl.program_id(0)
    g = pl.program_id(1)
    i = pl.program_id(2)
    dm = h_ref.shape[1]
    blk = ATT_BLOCK

    @pl.when(i == 0)
    def _():
        xs_ref[...] = jnp.zeros_like(xs_ref)

    def scatter_rows(win, k, base):
        hb = h_ref[k * blk:(k + 1) * blk, :].astype(BF16)
        a = afft_ref[0, k]
        a_hi = a.astype(BF16)
        r1 = a - a_hi.astype(F32)
        a_mid = r1.astype(BF16)
        a_lo = (r1 - a_mid.astype(F32)).astype(BF16)
        a3 = jnp.concatenate([a_hi, a_mid, a_lo, jnp.zeros((GATE_LANES - 3 * ne, blk), BF16)], axis=0)
        riota = lax.broadcasted_iota(jnp.int32, (win, blk), 0)
        onehots = jnp.concatenate([(riota == slot_ref[0, k, 0, j:j + 1, :]).astype(BF16) for j in range(eg)], axis=0)
        res = _dot(onehots, hb)
        resg = _dot_nt(onehots, a3)
        for j in range(eg):
            w = pl.multiple_of(ws_ref[base + j], SLOT_ALIGN)
            head = pl.ds(w, SLOT_ALIGN)
            tail = pl.ds(w + SLOT_ALIGN, win - SLOT_ALIGN)
            for cols, r in ((slice(0, dm), res), (slice(dm, dm + GATE_LANES), resg)):
                rj = r[j * win:(j + 1) * win]
                xs_ref[0, j, head, cols] = (xs_ref[0, j, head, cols].astype(F32) + rj[0:SLOT_ALIGN]).astype(BF16)
                xs_ref[0, j, tail, cols] = rj[SLOT_ALIGN:].astype(BF16)

    for k in range(cd):
        base = (b * nc + i * cd + k) * ne + g * eg
        small = _max_extent(we_ref, base, eg) <= SMALL_WIN
        pl.when(small)(functools.partial(scatter_rows, SMALL_WIN, k, base))
        pl.when(jnp.logical_not(small))(functools.partial(scatter_rows, WIN, k, base))


def _dispatch(h1, afft, slot, wstart, wextent, cap, eg=8):
    bsz, nc, ne, blk = slot.shape
    t, dm = h1.shape
    cp = cap + WIN
    cd = math.gcd(DISPATCH_STEP_CHUNKS, nc)
    ns = nc // cd
    return pl.pallas_call(
        functools.partial(_dispatch_kernel, nc=nc, ne=ne, eg=eg, cd=cd),
        grid_spec=pltpu.PrefetchScalarGridSpec(
            num_scalar_prefetch=2, grid=(bsz, ne // eg, ns),
            in_specs=[pl.BlockSpec((cd * blk, dm), lambda b, g, i, ws, we: (b * ns + i, 0)),
                      pl.BlockSpec((1, cd, ne, blk), lambda b, g, i, ws, we: (b, i, 0, 0)),
                      pl.BlockSpec((1, cd, 1, eg, blk), lambda b, g, i, ws, we: (b, i, g, 0, 0))],
            out_specs=pl.BlockSpec((1, eg, cp, dm + GATE_LANES), lambda b, g, i, ws, we: (b, g, 0, 0))),
        out_shape=jax.ShapeDtypeStruct((bsz, ne, cp, dm + GATE_LANES), BF16),
        compiler_params=_cparams(("parallel", "parallel", "arbitrary")),
    )(wstart, wextent, h1, afft, slot.reshape(bsz, nc, ne // eg, eg, blk))


def _ffn_kernel(xs_ref, wg_ref, wu_ref, wd_ref, out_ref, hdn_scr, *, cap, nf, ne):
    e = pl.program_id(0)
    j = pl.program_id(1)
    bsz = xs_ref.shape[0]
    dm = wg_ref.shape[2]
    tf = wg_ref.shape[3]

    @pl.when(j < nf)
    def _():
        wg = wg_ref[0, 0].astype(BF16)
        wu = wu_ref[0, 0].astype(BF16)
        for b in range(bsz):
            x = xs_ref[b, 0, :, 0:dm]
            hg = _dot(x, wg)
            hu = _dot(x, wu)
            hdn_scr[j, b * cap:(b + 1) * cap, :] = ((hg * jax.nn.sigmoid(hg)) * hu).astype(BF16)

    @pl.when(j >= nf)
    def _():
        wd = wd_ref[0, 0].astype(BF16)
        cp = out_ref.shape[2]
        tn = out_ref.shape[3]
        lane = lax.broadcasted_iota(jnp.int32, (cap, GATE_LANES), 1)
        mine = ((lane % ne) == e) & (lane < 3 * ne)
        for b in range(bsz):
            rows = slice(b * cap, (b + 1) * cap)
            y = _dot(hdn_scr[0, rows, :], wd[0:tf])
            for f in range(1, nf):
                y = y + _dot(hdn_scr[f, rows, :], wd[f * tf:(f + 1) * tf])
            pieces = xs_ref[b, 0, :, dm:dm + GATE_LANES].astype(F32)
            gate = jnp.sum(jnp.where(mine, pieces, 0.0), axis=1, keepdims=True)
            out_ref[b, 0, 0:cap, :] = (y * gate).astype(BF16)
            out_ref[b, 0, cap:cp, :] = jnp.zeros((cp - cap, tn), BF16)


def _ffn(xs, wg, wu, wd, l, cap, tf=512):
    bsz, ne, cp, dx = xs.shape
    dm = wg.shape[2]
    ff = wg.shape[-1]
    nf = ff // tf
    tn = dm // nf
    return pl.pallas_call(
        functools.partial(_ffn_kernel, cap=cap, nf=nf, ne=ne), grid=(ne, 2 * nf),
        in_specs=[pl.BlockSpec((bsz, 1, cap, dx), lambda e, j: (0, e, 0, 0)),
                  pl.BlockSpec((1, 1, dm, tf), lambda e, j: (l, e, 0, jnp.minimum(j, nf - 1))),
                  pl.BlockSpec((1, 1, dm, tf), lambda e, j: (l, e, 0, jnp.minimum(j, nf - 1))),
                  pl.BlockSpec((1, 1, ff, tn), lambda e, j: (l, e, 0, jnp.maximum(j - nf, 0)))],
        out_specs=pl.BlockSpec((bsz, 1, cp, tn), lambda e, j: (0, e, 0, jnp.maximum(j - nf, 0))),
        out_shape=jax.ShapeDtypeStruct((bsz, ne, cp, dm), BF16),
        scratch_shapes=[pltpu.VMEM((nf, bsz * cap, tf), BF16)],
        compiler_params=_cparams(("parallel", "arbitrary")),
    )(xs, wg, wu, wd)


def _combine_kernel(ws_ref, we_ref, out_ref, slott_ref, h1_ref, lng_ref, lnb_ref, h2_ref, rhs_scr, *, nc, ne, cc):
    b = pl.program_id(0)
    i = pl.program_id(1)
    blk = ATT_BLOCK

    def finish(k, tot):
        rows = slice(k * blk, (k + 1) * blk)
        h2_ref[rows, :] = _layer_norm(DEEPNORM_ALPHA * h1_ref[rows, :] + tot, lng_ref[0], lnb_ref[0])

    def gather_packed(k, base):
        kk = ne * SMALL_WIN
        slot_t = slott_ref[0, k].astype(F32).astype(BF16)
        expand = (lax.broadcasted_iota(jnp.int32, (ne, kk), 1) // SMALL_WIN
                  == lax.broadcasted_iota(jnp.int32, (ne, kk), 0)).astype(BF16)
        spread = _dot(slot_t, expand)
        row = (lax.broadcasted_iota(jnp.int32, (blk, kk), 1) % SMALL_WIN).astype(F32)
        onehot = (spread == row).astype(BF16)
        for e in range(ne):
            w = pl.multiple_of(ws_ref[base + e], SLOT_ALIGN)
            rhs_scr[e * SMALL_WIN:(e + 1) * SMALL_WIN, :] = out_ref[0, e, pl.ds(w, SMALL_WIN), :]
        finish(k, _dot(onehot, rhs_scr[...]))

    def gather_per_expert(k, base):
        slot_t = slott_ref[0, k]
        liota = lax.broadcasted_iota(jnp.int32, (blk, WIN), 1)
        tot = None
        for e in range(ne):
            w = pl.multiple_of(ws_ref[base + e], SLOT_ALIGN)
            onehot = (liota == slot_t[:, e:e + 1]).astype(BF16)
            y = _dot(onehot, out_ref[0, e, pl.ds(w, WIN), :])
            tot = y if tot is None else tot + y
        finish(k, tot)

    for k in range(cc):
        base = (b * nc + i * cc + k) * ne
        small = _max_extent(we_ref, base, ne) <= SMALL_WIN
        pl.when(small)(functools.partial(gather_packed, k, base))
        pl.when(jnp.logical_not(small))(functools.partial(gather_per_expert, k, base))


def _combine(out, slot, wstart, wextent, h1, lng3, lnb3, l, seq):
    bsz, ne, cp, dm = out.shape
    nc = seq // ATT_BLOCK
    blk = ATT_BLOCK
    t = h1.shape[0]
    slot_t = jnp.swapaxes(slot, 2, 3)
    cc = math.gcd(COMBINE_STEP_CHUNKS, nc)
    ns = nc // cc
    vec = pl.BlockSpec((1, 1, dm), lambda b, i, ws, we: (l, 0, 0))
    return pl.pallas_call(
        functools.partial(_combine_kernel, nc=nc, ne=ne, cc=cc),
        grid_spec=pltpu.PrefetchScalarGridSpec(
            num_scalar_prefetch=2, grid=(bsz, ns),
            in_specs=[pl.BlockSpec((1, ne, cp, dm), lambda b, i, ws, we: (b, 0, 0, 0), pipeline_mode=pl.Buffered(1)),
                      pl.BlockSpec((1, cc, blk, ne), lambda b, i, ws, we: (b, i, 0, 0)),
                      pl.BlockSpec((cc * blk, dm), lambda b, i, ws, we: (b * ns + i, 0)),
                      vec, vec],
            out_specs=pl.BlockSpec((cc * blk, dm), lambda b, i, ws, we: (b * ns + i, 0)),
            scratch_shapes=[pltpu.VMEM((ne * SMALL_WIN, dm), BF16)]),
        out_shape=jax.ShapeDtypeStruct((t, dm), F32),
        compiler_params=_cparams(("parallel", "arbitrary")),
    )(wstart, wextent, out, slot_t, h1, lng3, lnb3)


def kernel(x, ln_in_g, ln_in_b, w_in, ret_theta, ssm_lambda_re, ssm_lambda_im, ssm_log_step, ssm_b_re, ssm_b_im,
           ssm_c_re, ssm_c_im, ssm_d, ssm_w_glu, ssm_b_glu, attn_sink, w_out, ln1_g, ln1_b, router_w,
           exp_w_gate, exp_w_up, exp_w_down, ln2_g, ln2_b):
    bsz, seq, dm = x.shape
    depth = w_in.shape[0]
    ne = router_w.shape[-1]
    cap = EC_FACTOR * seq // ne
    t = bsz * seq
    w_in_bf = w_in.astype(BF16)
    wt_bf = jnp.swapaxes(jnp.concatenate([w_in[:, :, COL_SU:COL_SU + SSM_WIDTH], w_in[:, :, COL_RK:COL_RK + RET_WIDTH]],
                                         axis=2), 1, 2).astype(BF16)
    w_out_bf = w_out.astype(BF16)
    w_glu_bf = ssm_w_glu.astype(BF16)
    lg = jax.nn.log_sigmoid(ret_theta.astype(F32))
    sink = attn_sink.astype(F32)
    ssm_par = _ssm_params(ssm_lambda_re, ssm_lambda_im, ssm_log_step, ssm_b_re, ssm_b_im, ssm_c_re, ssm_c_im)
    vec3 = lambda a: a.astype(F32).reshape(depth, 1, -1)
    d3, bglu3, ln1g3, ln1b3, ln2g3, ln2b3 = map(vec3, (ssm_d, ssm_b_glu, ln1_g, ln1_b, ln2_g, ln2_b))
    rwt = jnp.swapaxes(router_w.astype(F32), 1, 2)
    rwt_hi = rwt.astype(BF16)
    rwt = jnp.concatenate([rwt_hi, (rwt - rwt_hi.astype(F32)).astype(BF16)], axis=1)
    bias = _attention_bias(seq)

    h = x
    for l in range(depth):
        if l == 0:
            h, proj, ut, kt = _inproj(h, w_in_bf, wt_bf, l, ln=(ln_in_g, ln_in_b))
        else:
            proj, ut, kt = _inproj(h, w_in_bf, wt_bf, l)
        h2d = h.reshape(t, dm)
        proj = proj.reshape(t, D_IN)
        yt = _ssm_conv(ut, ssm_par, l, bsz)
        sf, sb = _ret_states(proj, kt, lg, l, bsz, seq)
        h1, afft = _mixer(proj, sf, sb, yt, h2d, bias, lg, sink, w_out_bf, w_glu_bf, d3, bglu3, ln1g3, ln1b3,
                          rwt, l, bsz, seq)
        slot, wstart, wextent = _route(afft, cap)
        xs = _dispatch(h1, afft, slot, wstart, wextent, cap)
        out = _ffn(xs, exp_w_gate, exp_w_up, exp_w_down, l, cap)
        h = _combine(out, slot, wstart, wextent, h1, ln2g3, ln2b3, l, seq).reshape(bsz, seq, dm)
    return h
```

```python
import functools
import math

import jax
import jax.numpy as jnp
from jax import lax
from jax.experimental import pallas as pl
from jax.experimental.pallas import tpu as pltpu

F32 = jnp.float32
BF16 = jnp.bfloat16

RET_HEADS = 4
RET_DK = 64
RET_CHUNK = 128
RET_WIDTH = RET_HEADS * RET_DK
SSM_CPG = 16
SSM_GROUPS = 16
SSM_STATE = 64
SSM_WIDTH = SSM_CPG * SSM_GROUPS
ATT_HEADS = 8
ATT_KV_HEADS = 2
ATT_GQ = ATT_HEADS // ATT_KV_HEADS
ATT_HEAD_DIM = 64
ATT_BLOCK = 128
ATT_WINDOW = 128
EC_FACTOR = 2
DEPTH = 2
DEEPNORM_ALPHA = (2.0 * DEPTH) ** 0.25
LN_EPS = 1e-5
NEG_INF = -1e30

COL_RQ, COL_RK, COL_RV, COL_RG, COL_SU, COL_AQ, COL_AK, COL_AV = 0, 256, 512, 768, 1024, 1280, 1792, 1920
D_IN = 2048

CHUNK = 128
SUBLANES = 8
SLOT_ALIGN = 16
WIN = CHUNK + SLOT_ALIGN
LANES = 128
SMALL_WIN = LANES // 2
DISPATCH_STEP_CHUNKS = 8
COMBINE_STEP_CHUNKS = 4
MIXER_STEP_CHUNKS = 2
GATE_LANES = 128
RET_STEP_CHUNKS = 8
VMEM_LIMIT = 56 * 1024 * 1024
ROUTE_REFINE_STEPS = 10


def _cparams(sem):
    return pltpu.CompilerParams(dimension_semantics=sem, vmem_limit_bytes=VMEM_LIMIT)


def _layer_norm(x, g, b):
    mu = jnp.mean(x, axis=-1, keepdims=True)
    xc = x - mu
    var = jnp.mean(xc * xc, axis=-1, keepdims=True)
    return xc * lax.rsqrt(var + LN_EPS) * g + b


def _dot(a, b):
    return jnp.dot(a, b, preferred_element_type=F32)


def _dot_hp(a, b):
    return jnp.dot(a, b, precision=lax.Precision.HIGHEST, preferred_element_type=F32)


def _dot_nt(a, b):
    return lax.dot_general(a, b, (((1,), (1,)), ((), ())), preferred_element_type=F32)


def _cmul(ar, ai, br, bi):
    return ar * br - ai * bi, ar * bi + ai * br


def _cpow(br, bi, expo, nbits):
    shape = (br.shape[0], expo.shape[1])
    rr = jnp.ones(shape, F32)
    ri = jnp.zeros(shape, F32)
    for j in range(nbits):
        bit = jnp.broadcast_to(((expo >> j) & 1) == 1, shape)
        nr, ni = _cmul(rr, ri, br, bi)
        rr = jnp.where(bit, nr, rr)
        ri = jnp.where(bit, ni, ri)
        if j + 1 < nbits:
            br, bi = _cmul(br, bi, br, bi)
    return rr, ri


def _inproj_kernel(*refs, apply_ln):
    if apply_ln:
        x_ref, g_ref, b_ref, w_ref, wt_ref, h_ref, proj_ref, ut_ref, kt_ref = refs
    else:
        x_ref, w_ref, wt_ref, proj_ref, ut_ref, kt_ref = refs
    nb, tl, d = x_ref.shape
    h = x_ref[...].reshape(nb * tl, d)
    if apply_ln:
        h = _layer_norm(h, g_ref[...], b_ref[...])
        h_ref[...] = h.reshape(nb, tl, d)
    hb = h.astype(BF16)
    proj_ref[...] = _dot(hb, w_ref[0]).reshape(nb, tl, -1)
    tt = _dot_nt(wt_ref[0], hb)
    k = tl // CHUNK
    for j in range(k):
        for b in range(nb):
            ut_ref[:, j * nb + b, :] = tt[0:SSM_WIDTH, (b * k + j) * CHUNK:(b * k + j + 1) * CHUNK]
    for b in range(nb):
        kt_ref[b] = tt[SSM_WIDTH:SSM_WIDTH + RET_WIDTH, b * tl:(b + 1) * tl]


def _inproj(x3, w_bf, wt_bf, l, ln=None):
    bsz, seq, d = x3.shape
    n = w_bf.shape[2]
    k = SUBLANES // bsz
    tl = k * CHUNK
    nc = seq // CHUNK
    xspec = pl.BlockSpec((bsz, tl, d), lambda i: (0, i, 0))
    wspec = pl.BlockSpec((1, d, n), lambda i: (l, 0, 0))
    wtspec = pl.BlockSpec((1,) + wt_bf.shape[1:], lambda i: (l, 0, 0))
    out_specs = [pl.BlockSpec((bsz, tl, n), lambda i: (0, i, 0)),
                 pl.BlockSpec((SSM_WIDTH, k * bsz, CHUNK), lambda i: (0, i, 0)),
                 pl.BlockSpec((bsz, RET_WIDTH, tl), lambda i: (0, 0, i))]
    out_shape = [jax.ShapeDtypeStruct((bsz, seq, n), F32),
                 jax.ShapeDtypeStruct((SSM_WIDTH, nc * bsz, CHUNK), F32),
                 jax.ShapeDtypeStruct((bsz, RET_WIDTH, seq), F32)]
    if ln is None:
        return pl.pallas_call(
            functools.partial(_inproj_kernel, apply_ln=False), grid=(seq // tl,),
            in_specs=[xspec, wspec, wtspec], out_specs=out_specs, out_shape=out_shape,
            compiler_params=_cparams(("parallel",)),
        )(x3, w_bf, wt_bf)
    g, b = ln
    vec = pl.BlockSpec((1, d), lambda i: (0, 0))
    return pl.pallas_call(
        functools.partial(_inproj_kernel, apply_ln=True), grid=(seq // tl,),
        in_specs=[xspec, vec, vec, wspec, wtspec], out_specs=[xspec] + out_specs,
        out_shape=[jax.ShapeDtypeStruct(x3.shape, F32)] + out_shape,
        compiler_params=_cparams(("parallel",)),
    )(x3, g.reshape(1, d), b.reshape(1, d), w_bf, wt_bf)


def _ssm_params(lam_re, lam_im, log_step, b_re, b_im, c_re, c_im):
    lr, li = lam_re.astype(F32), lam_im.astype(F32)
    step = jnp.exp(log_step.astype(F32))[..., None]
    er = jnp.exp(lr * step)
    lbr, lbi = er * jnp.cos(li * step), er * jnp.sin(li * step)
    den = lr * lr + li * li
    fr = ((lbr - 1.0) * lr + lbi * li) / den
    fi = (lbi * lr - (lbr - 1.0) * li) / den
    br = jnp.swapaxes(b_re.astype(F32), -1, -2)[:, None]
    bi = jnp.swapaxes(b_im.astype(F32), -1, -2)[:, None]
    bbr = fr[..., None, :] * br - fi[..., None, :] * bi
    bbi = fr[..., None, :] * bi + fi[..., None, :] * br
    lcol = jnp.stack([lbr[:, 0], lbi[:, 0], lbr[:, 1], lbi[:, 1]], axis=-1)
    lrow = jnp.swapaxes(lcol, -1, -2)
    lrow = jnp.concatenate([lrow, jnp.zeros_like(lrow)], axis=2)
    bt = jnp.stack([bbr[:, 0], bbi[:, 0], bbr[:, 1], bbi[:, 1]], axis=2)
    cr, ci = c_re.astype(F32), c_im.astype(F32)
    c4 = jnp.stack([cr[:, 0], ci[:, 0], cr[:, 1], ci[:, 1]], axis=2)
    return lcol, lrow, bt, jnp.swapaxes(bt, -1, -2), c4, jnp.swapaxes(c4, -1, -2)


def _ssm_kernel(ut_ref, lcol_ref, lrow_ref, bt_ref, bcol_ref, c_ref, ct_ref, y_ref,
                w_scr, wst_scr, g_scr, vf_scr, vb_scr, acc_scr, s_scr, sp_scr, x_scr, *, nb, nchunks):
    tc, p, cpg = CHUNK, SSM_STATE, SSM_CPG
    lcol = lcol_ref[0, 0]
    lf = (lcol[:, 0:1], lcol[:, 1:2])
    lb = (lcol[:, 2:3], lcol[:, 3:4])

    m_row = lax.broadcasted_iota(jnp.int32, (1, tc), 1)
    pfr, pfi = _cpow(*lf, m_row, 7)
    pf1r, pf1i = _cmul(pfr, pfi, *lf)
    prr, pri = _cpow(*lf, tc - 1 - m_row, 7)
    pbr, pbi = _cpow(*lb, tc - m_row, 8)
    pqr, pqi = _cpow(*lb, m_row, 7)

    def rep_co(x):
        return jnp.broadcast_to(x[:, None, :], (cpg, cpg, p)).reshape(cpg * cpg, p)

    def rep_ci(x):
        return jnp.broadcast_to(x[None, :, :], (cpg, cpg, p)).reshape(cpg * cpg, p)

    cbfr, cbfi = _cmul(rep_co(c_ref[0, 0, 0]), rep_co(c_ref[0, 0, 1]), rep_ci(bt_ref[0, 0, 0]), rep_ci(bt_ref[0, 0, 1]))
    cbbr, cbbi = _cmul(rep_co(c_ref[0, 0, 2]), rep_co(c_ref[0, 0, 3]), rep_ci(bt_ref[0, 0, 2]), rep_ci(bt_ref[0, 0, 3]))
    lane = lax.broadcasted_iota(jnp.int32, (cpg * cpg, tc), 1)
    kb0 = jnp.sum(cbbr, axis=1, keepdims=True)
    vf_scr[...] = _dot_hp(cbfr, pfr) - _dot_hp(cbfi, pfi) + jnp.where(lane == 0, kb0, 0.0)
    vb_scr[...] = _dot_hp(cbbr, pbr) - _dot_hp(cbbi, pbi)

    for co in range(cpg):
        cols = slice(co * tc, (co + 1) * tc)
        for r, (ar, ai) in enumerate(((pf1r, pf1i), (pbr, pbi))):
            gr, gi = _cmul(ct_ref[0, 0, 2 * r][:, co:co + 1], ct_ref[0, 0, 2 * r + 1][:, co:co + 1], ar, ai)
            g_scr[2 * r * p:(2 * r + 1) * p, cols] = gr.astype(BF16)
            g_scr[(2 * r + 1) * p:(2 * r + 2) * p, cols] = (-gi).astype(BF16)
        for r, (ar, ai) in enumerate(((prr, pri), (pqr, pqi))):
            sr, si = _cmul(bcol_ref[0, 0, 2 * r][:, co:co + 1], bcol_ref[0, 0, 2 * r + 1][:, co:co + 1], ar, ai)
            wst_scr[co, 2 * r * p:(2 * r + 1) * p, :] = sr.astype(BF16)
            wst_scr[co, (2 * r + 1) * p:(2 * r + 2) * p, :] = si.astype(BF16)

    s_idx = lax.broadcasted_iota(jnp.int32, (tc, tc), 0)
    j_idx = lax.broadcasted_iota(jnp.int32, (tc, tc), 1)
    fwd_part = j_idx < tc - s_idx
    acc_scr[...] = jnp.zeros_like(acc_scr)
    s_scr[...] = jnp.zeros_like(s_scr)

    def per_channel(ci, carry):
        for co in range(cpg):
            rf = jnp.broadcast_to(vf_scr[pl.ds(co * cpg + ci, 1), :], (tc, tc))
            rb = jnp.broadcast_to(vb_scr[pl.ds(co * cpg + ci, 1), :], (tc, tc))
            m = pltpu.roll(jnp.where(fwd_part, rf, rb), 0, 1, stride=1, stride_axis=0)
            w_scr[ci, :, co * tc:(co + 1) * tc] = m.astype(BF16)
        u = ut_ref[ci].astype(BF16)
        acc_scr[...] += _dot(u, w_scr[ci])
        s_scr[...] += _dot_nt(u, wst_scr[ci])
        return carry

    lax.fori_loop(0, cpg, per_channel, 0)

    lrow = lrow_ref[0, 0]
    dec = []
    for r in range(2):
        dr, di = lrow[2 * r:2 * r + 1], lrow[2 * r + 1:2 * r + 2]
        for _ in range(7):
            dr, di = _cmul(dr, di, dr, di)
        dec.append((dr, di))
    for q in range(4):
        sp_scr[q] = s_scr[:, q * p:(q + 1) * p]
    xfr = xfi = xbr = xbi = jnp.zeros((nb, p), F32)
    for i in range(nchunks):
        rf = slice(i * nb, (i + 1) * nb)
        rb = slice((nchunks - 1 - i) * nb, (nchunks - i) * nb)
        x_scr[0, rf, :] = xfr
        x_scr[1, rf, :] = xfi
        x_scr[2, rb, :] = xbr
        x_scr[3, rb, :] = xbi
        xfr, xfi = _cmul(dec[0][0], dec[0][1], xfr, xfi)
        xbr, xbi = _cmul(dec[1][0], dec[1][1], xbr, xbi)
        xfr, xfi = xfr + sp_scr[0, rf, :], xfi + sp_scr[1, rf, :]
        xbr, xbi = xbr + sp_scr[2, rb, :], xbi + sp_scr[3, rb, :]
    xin = jnp.concatenate([x_scr[q] for q in range(4)], axis=1).astype(BF16)
    y = acc_scr[...] + _dot(xin, g_scr[...])
    for co in range(cpg):
        y_ref[co] = y[:, co * tc:(co + 1) * tc]


def _ssm_conv(ut, params, l, bsz):
    _, r, tc = ut.shape
    cpg, p, ng = SSM_CPG, SSM_STATE, SSM_GROUPS

    def pspec(a):
        return pl.BlockSpec((1, 1) + a.shape[2:], lambda g: (l, g) + (0,) * (a.ndim - 2))
    return pl.pallas_call(
        functools.partial(_ssm_kernel, nb=bsz, nchunks=r // bsz), grid=(ng,),
        in_specs=[pl.BlockSpec((cpg, r, tc), lambda g: (g, 0, 0))] + [pspec(a) for a in params],
        out_specs=pl.BlockSpec((cpg, r, tc), lambda g: (g, 0, 0)),
        out_shape=jax.ShapeDtypeStruct(ut.shape, F32),
        scratch_shapes=[pltpu.VMEM((cpg, tc, cpg * tc), BF16), pltpu.VMEM((cpg, 4 * p, tc), BF16),
                        pltpu.VMEM((4 * p, cpg * tc), BF16),
                        pltpu.VMEM((cpg * cpg, tc), F32), pltpu.VMEM((cpg * cpg, tc), F32),
                        pltpu.VMEM((r, cpg * tc), F32), pltpu.VMEM((r, 4 * p), F32),
                        pltpu.VMEM((4, r, p), F32), pltpu.VMEM((4, r, p), F32)],
        compiler_params=_cparams(("parallel",)),
    )(ut, *params)


def _retstate_kernel(lg_ref, ktf_ref, vf_ref, ktb_ref, vb_ref, sf_ref, sb_ref, accf, accb, *, l, cs):
    i = pl.program_id(1)

    @pl.when(i == 0)
    def _():
        accf[...] = jnp.zeros_like(accf)
        accb[...] = jnp.zeros_like(accb)

    ch = RET_CHUNK
    pos = lax.broadcasted_iota(jnp.int32, (1, ch), 1).astype(F32)
    one = jnp.ones((1, RET_DK), F32)
    wts = []
    for h in range(RET_HEADS):
        lgf = lg_ref[l, 0, h]
        lgb = lg_ref[l, 1, h]
        wts.append((jnp.exp(lgf * (ch - 1.0 - pos)) * RET_DK ** -0.5, jnp.exp(lgb * pos) * RET_DK ** -0.5,
                    jnp.exp(one * (lgf * ch)), jnp.exp(one * (lgb * ch))))
    for j in range(cs):
        jb = cs - 1 - j
        sf_ref[0, j] = accf[...]
        sb_ref[0, jb] = accb[...]
        for h in range(RET_HEADS):
            rows = slice(h * RET_DK, (h + 1) * RET_DK)
            wf, wb, decf, decb = wts[h]
            kf = (ktf_ref[0, rows, j * ch:(j + 1) * ch] * wf).astype(BF16)
            kb = (ktb_ref[0, rows, jb * ch:(jb + 1) * ch] * wb).astype(BF16)
            accf[rows, :] = decf * accf[rows, :] + _dot(kf, vf_ref[j * ch:(j + 1) * ch, rows].astype(BF16))
            accb[rows, :] = decb * accb[rows, :] + _dot(kb, vb_ref[jb * ch:(jb + 1) * ch, rows].astype(BF16))


def _ret_states(proj, kt, lg, l, bsz, seq):
    nc = seq // RET_CHUNK
    cs = min(RET_STEP_CHUNKS, nc)
    ns = nc // cs
    w = RET_WIDTH
    tl = cs * RET_CHUNK
    vcol = COL_RV // w
    st = jax.ShapeDtypeStruct((bsz, nc, w, RET_DK), F32)
    return pl.pallas_call(
        functools.partial(_retstate_kernel, l=l, cs=cs), grid=(bsz, ns),
        in_specs=[pl.BlockSpec(memory_space=pltpu.SMEM),
                  pl.BlockSpec((1, w, tl), lambda b, i: (b, 0, i)),
                  pl.BlockSpec((tl, w), lambda b, i: (b * ns + i, vcol)),
                  pl.BlockSpec((1, w, tl), lambda b, i: (b, 0, ns - 1 - i)),
                  pl.BlockSpec((tl, w), lambda b, i: (b * ns + ns - 1 - i, vcol))],
        out_specs=[pl.BlockSpec((1, cs, w, RET_DK), lambda b, i: (b, i, 0, 0)),
                   pl.BlockSpec((1, cs, w, RET_DK), lambda b, i: (b, ns - 1 - i, 0, 0))],
        out_shape=[st, st],
        scratch_shapes=[pltpu.VMEM((w, RET_DK), F32), pltpu.VMEM((w, RET_DK), F32)],
        compiler_params=_cparams(("parallel", "arbitrary")),
    )(lg, kt, proj, kt, proj)


def _gelu_tanh(x):
    return 0.5 * x * (1.0 + jnp.tanh(math.sqrt(2.0 / math.pi) * (x + 0.044715 * (x * x * x))))


def _attention_bias(seq):
    blk = ATT_BLOCK
    s_idx = jnp.arange(3 * blk)[None, :]
    t_idx = jnp.arange(blk)[:, None]
    arel = jnp.abs(s_idx - blk - t_idx)
    band = arel <= ATT_WINDOW
    slopes = jnp.exp2(-8.0 * jnp.arange(1, ATT_HEADS + 1, dtype=F32) / ATT_HEADS)
    alibi = -slopes[:, None, None] * arel.astype(F32)[None]
    variants = []
    for prev_ok, next_ok in ((False, True), (True, True), (True, False)):
        ok = band & (prev_ok | (s_idx >= blk)) & (next_ok | (s_idx < 2 * blk))
        variants.append(jnp.where(ok[None], alibi, NEG_INF).reshape(ATT_KV_HEADS, ATT_GQ * blk, 3 * blk))
    return jnp.stack(variants)


def _mixer_kernel(lg_ref, sink_ref, proj_ref, kvp_ref, kvn_ref, sf_ref, sb_ref, yt_ref, h_ref, bias0_ref, bias1_ref,
                  wout_ref, wglu_ref, d_ref, bglu_ref, lng_ref, lnb_ref, rwt_ref,
                  h1_ref, afft_ref, mix_scr, sret_scr, satt_scr, pret_scr, patt_scr, *, l, nb):
    b = pl.program_id(1)
    cm = MIXER_STEP_CHUNKS
    bias_refs = (bias0_ref, bias1_ref)
    ch = RET_CHUNK
    dk = RET_DK
    blk = ATT_BLOCK
    hd = ATT_HEAD_DIM
    kvw = ATT_KV_HEADS * hd

    def kv_block(u, cols):
        if u < 0:
            return kvp_ref[:, cols]
        if u >= cm:
            return kvn_ref[:, cols]
        return proj_ref[u * ch:(u + 1) * ch, COL_AK + cols.start:COL_AK + cols.stop]

    vbs = {}
    for u in range(cm):
        r = slice(u * ch, (u + 1) * ch)
        for h in range(RET_HEADS):
            q = proj_ref[r, COL_RQ + h * dk:COL_RQ + (h + 1) * dk]
            k = proj_ref[r, COL_RK + h * dk:COL_RK + (h + 1) * dk] * dk ** -0.5
            sret_scr[u * RET_HEADS + h] = _dot_nt(q.astype(BF16), k.astype(BF16))
        for kvh in range(ATT_KV_HEADS):
            kc = slice(kvh * hd, (kvh + 1) * hd)
            vc = slice(kvw + kvh * hd, kvw + (kvh + 1) * hd)
            kb = jnp.concatenate([kv_block(u + d, kc) for d in (-1, 0, 1)], axis=0).astype(BF16)
            vbs[u, kvh] = jnp.concatenate([kv_block(u + d, vc) for d in (-1, 0, 1)], axis=0).astype(BF16)
            q4 = jnp.concatenate([proj_ref[r, COL_AQ + (kvh * ATT_GQ + gq) * hd:COL_AQ + (kvh * ATT_GQ + gq + 1) * hd]
                                  for gq in range(ATT_GQ)], axis=0) * hd ** -0.5
            satt_scr[u * ATT_KV_HEADS + kvh] = _dot_nt(q4.astype(BF16), kb)

    pos = lax.broadcasted_iota(jnp.int32, (ch, 1), 0).astype(F32)
    dist = lax.broadcasted_iota(jnp.int32, (ch, ch), 0) - lax.broadcasted_iota(jnp.int32, (ch, ch), 1)
    adist = jnp.abs(dist).astype(F32)
    for h in range(RET_HEADS):
        lgf = lg_ref[l, 0, h]
        lgb = lg_ref[l, 1, h]
        dmat = jnp.exp(jnp.where(dist >= 0, lgf, lgb) * adist)
        wqf = jnp.exp(lgf * (pos + 1.0))
        wqb = jnp.exp(lgb * (ch - pos))
        for u in range(cm):
            q = proj_ref[u * ch:(u + 1) * ch, COL_RQ + h * dk:COL_RQ + (h + 1) * dk]
            uh = u * RET_HEADS + h
            pret_scr[uh, :, 0:ch] = (sret_scr[uh] * dmat).astype(BF16)
            pret_scr[uh, :, ch:ch + dk] = (q * wqf).astype(BF16)
            pret_scr[uh, :, ch + dk:ch + 2 * dk] = (q * wqb).astype(BF16)
    for u in range(cm):
        for kvh in range(ATT_KV_HEADS):
            uk = u * ATT_KV_HEADS + kvh
            for gq in range(ATT_GQ):
                rows = slice(gq * blk, (gq + 1) * blk)
                sink = sink_ref[l, kvh * ATT_GQ + gq]
                sc = satt_scr[uk, rows, :] + bias_refs[u][0, kvh, rows, :]
                m = jnp.maximum(jnp.max(sc, axis=-1, keepdims=True), sink)
                p = jnp.exp(sc - m)
                denom = jnp.sum(p, axis=-1, keepdims=True) + jnp.exp(sink - m)
                patt_scr[uk, rows, :] = (p * (1.0 / denom)).astype(BF16)

    att_base = RET_WIDTH + SSM_WIDTH
    w_ssm = SSM_WIDTH
    for u in range(cm):
        r = slice(u * ch, (u + 1) * ch)
        for h in range(RET_HEADS):
            rows = slice(h * dk, (h + 1) * dk)
            v = proj_ref[r, COL_RV + h * dk:COL_RV + (h + 1) * dk]
            g = proj_ref[r, COL_RG + h * dk:COL_RG + (h + 1) * dk]
            rhs = jnp.concatenate([v, sf_ref[0, u, rows, :], sb_ref[0, u, rows, :]], axis=0).astype(BF16)
            o = _dot(pret_scr[u * RET_HEADS + h], rhs)
            mu = jnp.mean(o, axis=-1, keepdims=True)
            oc = o - mu
            var = jnp.mean(oc * oc, axis=-1, keepdims=True)
            mix_scr[r, h * dk:(h + 1) * dk] = (g * jax.nn.sigmoid(g)) * (oc * lax.rsqrt(var + LN_EPS))
        for kvh in range(ATT_KV_HEADS):
            o = _dot(patt_scr[u * ATT_KV_HEADS + kvh], vbs[u, kvh])
            for gq in range(ATT_GQ):
                hh = kvh * ATT_GQ + gq
                mix_scr[r, att_base + hh * hd:att_base + (hh + 1) * hd] = o[gq * blk:(gq + 1) * blk]

        yraw = yt_ref[:, pl.ds(u * nb + b, 1), :].reshape(w_ssm, ch).T
        y = yraw + d_ref[0] * proj_ref[r, COL_SU:COL_SU + w_ssm]
        y = _gelu_tanh(y)
        gate = jax.nn.sigmoid(_dot(y.astype(BF16), wglu_ref[0]) + bglu_ref[0])
        mix_scr[r, RET_WIDTH:RET_WIDTH + w_ssm] = y * gate

    mix = _dot(mix_scr[...].astype(BF16), wout_ref[0])
    h1 = _layer_norm(DEEPNORM_ALPHA * h_ref[...] + mix, lng_ref[0], lnb_ref[0])
    h1_ref[...] = h1

    ne = rwt_ref.shape[1] // 2
    h_hi = h1.astype(BF16)
    h_lo = (h1 - h_hi.astype(F32)).astype(BF16)
    both = _dot_nt(rwt_ref[0], h_hi)
    lt = both[0:ne] + both[ne:2 * ne] + _dot_nt(rwt_ref[0, 0:ne, :], h_lo)
    lt = lt - jnp.max(lt, axis=0, keepdims=True)
    et = jnp.exp(lt)
    aff = et / jnp.sum(et, axis=0, keepdims=True)
    for u in range(cm):
        afft_ref[0, u] = aff[:, u * ch:(u + 1) * ch]


def _mixer(proj, sf, sb, yt, h2d, bias, lg, sink, wout_bf, wglu_bf, d3, bglu3, lng3, lnb3, rwt, l, bsz, seq):
    nc = seq // ATT_BLOCK
    cm = MIXER_STEP_CHUNKS
    ns = nc // cm
    assert nc % cm == 0 and cm * bsz == SUBLANES
    t, dm = h2d.shape
    ne = rwt.shape[1] // 2
    kvw = 2 * ATT_KV_HEADS * ATT_HEAD_DIM
    sw = RET_WIDTH
    kvcol = COL_AK // kvw
    blk = ATT_BLOCK

    def layer(a):
        return pl.BlockSpec((1,) + a.shape[1:], lambda i, b: (l,) + (0,) * (a.ndim - 1))
    smem = pl.BlockSpec(memory_space=pltpu.SMEM)
    in_specs = [
        smem, smem,
        pl.BlockSpec((cm * blk, D_IN), lambda i, b: (b * ns + i, 0)),
        pl.BlockSpec((blk, kvw), lambda i, b: (b * nc + jnp.maximum(cm * i - 1, 0), kvcol)),
        pl.BlockSpec((blk, kvw), lambda i, b: (b * nc + jnp.minimum(cm * i + cm, nc - 1), kvcol)),
        pl.BlockSpec((1, cm, sw, RET_DK), lambda i, b: (b, i, 0, 0)),
        pl.BlockSpec((1, cm, sw, RET_DK), lambda i, b: (b, i, 0, 0)),
        pl.BlockSpec((SSM_WIDTH, SUBLANES, CHUNK), lambda i, b: (0, i, 0)),
        pl.BlockSpec((cm * blk, dm), lambda i, b: (b * ns + i, 0)),
        pl.BlockSpec((1,) + bias.shape[1:], lambda i, b: (jnp.where(i == 0, 0, 1), 0, 0, 0)),
        pl.BlockSpec((1,) + bias.shape[1:], lambda i, b: (jnp.where(i == ns - 1, 2, 1), 0, 0, 0)),
        layer(wout_bf), layer(wglu_bf), layer(d3), layer(bglu3), layer(lng3), layer(lnb3), layer(rwt),
    ]
    out_specs = [
        pl.BlockSpec((cm * blk, dm), lambda i, b: (b * ns + i, 0)),
        pl.BlockSpec((1, cm, ne, blk), lambda i, b: (b, i, 0, 0)),
    ]
    out_shape = [jax.ShapeDtypeStruct((t, dm), F32),
                 jax.ShapeDtypeStruct((bsz, nc, ne, blk), F32)]
    scratch = [pltpu.VMEM((cm * blk, dm), F32),
               pltpu.VMEM((cm * RET_HEADS, blk, blk), F32),
               pltpu.VMEM((cm * ATT_KV_HEADS, ATT_GQ * blk, 3 * blk), F32),
               pltpu.VMEM((cm * RET_HEADS, blk, blk + 2 * RET_DK), BF16),
               pltpu.VMEM((cm * ATT_KV_HEADS, ATT_GQ * blk, 3 * blk), BF16)]
    return pl.pallas_call(
        functools.partial(_mixer_kernel, l=l, nb=bsz), grid=(ns, bsz),
        in_specs=in_specs, out_specs=out_specs, out_shape=out_shape, scratch_shapes=scratch,
        compiler_params=_cparams(("parallel", "parallel")),
    )(lg, sink, proj, proj, proj, sf, sb, yt, h2d, bias, bias, wout_bf, wglu_bf, d3, bglu3, lng3, lnb3, rwt)


def _route_kernel(a_ref, slot_ref, start_ref, extent_ref, *, nc, ne, cap):
    blk = ATT_BLOCK
    a = a_ref[0]

    def chunk(x, c):
        return x[c * ne:(c + 1) * ne]

    def count(pred):
        x = pred.astype(jnp.int32)
        tot = chunk(x, 0)
        for c in range(1, nc):
            tot = tot + chunk(x, c)
        return jnp.sum(tot, axis=1, keepdims=True)

    def tile_e(v):
        return jnp.concatenate([v] * nc, axis=0)

    tau = jnp.zeros((ne, 1), jnp.int32)
    for bit in range(30, -1, -1):
        cand = tau | (1 << bit)
        tau = jnp.where(count(a >= tile_e(pltpu.bitcast(cand, F32))) >= cap, cand, tau)
    lo = pltpu.bitcast(tau, F32)
    hi = pltpu.bitcast(tau + 1, F32)
    for _ in range(ROUTE_REFINE_STEPS):
        mid = lo + (hi - lo) * 0.5
        ok = count(a >= tile_e(mid)) >= cap
        lo = jnp.where(ok, mid, lo)
        hi = jnp.where(ok, hi, mid)
    gt = a >= tile_e(hi)
    eq = (a >= tile_e(lo)) & jnp.logical_not(gt)
    need = cap - count(gt)

    tri = (lax.broadcasted_iota(jnp.int32, (blk, blk), 0) <= lax.broadcasted_iota(jnp.int32, (blk, blk), 1)).astype(BF16)

    def ranks(mask):
        mf = mask.astype(F32)
        incl = _dot(mf.astype(BF16), tri)
        run = jnp.zeros((ne, 1), F32)
        offs = []
        tots = []
        for c in range(nc):
            offs.append(run)
            tots.append(chunk(incl, c)[:, blk - 1:blk])
            run = run + tots[-1]
        return incl - mf, jnp.concatenate(offs, axis=0), jnp.concatenate(tots, axis=0)

    eq_local, eq_off, _ = ranks(eq)
    mask = gt | (eq & ((eq_local + eq_off) < tile_e(need).astype(F32)))
    local, off, tot = ranks(mask)
    off_i = off.astype(jnp.int32)
    start = (off_i // SLOT_ALIGN) * SLOT_ALIGN
    slot = (off_i - start) + local.astype(jnp.int32)
    slot_ref[0] = jnp.where(mask, slot, -1)
    start_ref[0] = jnp.broadcast_to(start, (nc * ne, blk))
    extent_ref[0] = jnp.broadcast_to((off_i - start) + tot.astype(jnp.int32), (nc * ne, blk))


def _route(afft, cap):
    bsz, nc, ne, blk = afft.shape
    a2 = afft.reshape(bsz, nc * ne, blk)
    spec = pl.BlockSpec((1, nc * ne, blk), lambda b: (b, 0, 0))
    slot, start, extent = pl.pallas_call(
        functools.partial(_route_kernel, nc=nc, ne=ne, cap=cap), grid=(bsz,),
        in_specs=[spec], out_specs=[spec, spec, spec],
        out_shape=[jax.ShapeDtypeStruct(a2.shape, jnp.int32)] * 3,
        compiler_params=_cparams(("parallel",)),
    )(a2)
    flat = lambda v: v[:, :, 0].reshape(bsz * nc * ne)
    return slot.reshape(bsz, nc, ne, blk), flat(start), flat(extent)


def _max_extent(we_ref, base, n):
    mx = we_ref[base]
    for j in range(1, n):
        mx = jnp.maximum(mx, we_ref[base + j])
    return mx


def _dispatch_kernel(ws_ref, we_ref, h_ref, afft_ref, slot_ref, xs_ref, *, nc, ne, eg, cd):
    b = pl.program_id(0)
    g = pl.program_id(1)
    i = pl.program_id(2)
    dm = h_ref.shape[1]
    blk = ATT_BLOCK

    @pl.when(i == 0)
    def _():
        xs_ref[...] = jnp.zeros_like(xs_ref)

    def scatter_rows(win, k, base):
        hb = h_ref[k * blk:(k + 1) * blk, :].astype(BF16)
        a = afft_ref[0, k]
        a_hi = a.astype(BF16)
        r1 = a - a_hi.astype(F32)
        a_mid = r1.astype(BF16)
        a_lo = (r1 - a_mid.astype(F32)).astype(BF16)
        a3 = jnp.concatenate([a_hi, a_mid, a_lo, jnp.zeros((GATE_LANES - 3 * ne, blk), BF16)], axis=0)
        riota = lax.broadcasted_iota(jnp.int32, (win, blk), 0)
        onehots = jnp.concatenate([(riota == slot_ref[0, k, 0, j:j + 1, :]).astype(BF16) for j in range(eg)], axis=0)
        res = _dot(onehots, hb)
        resg = _dot_nt(onehots, a3)
        for j in range(eg):
            w = pl.multiple_of(ws_ref[base + j], SLOT_ALIGN)
            head = pl.ds(w, SLOT_ALIGN)
            tail = pl.ds(w + SLOT_ALIGN, win - SLOT_ALIGN)
            for cols, r in ((slice(0, dm), res), (slice(dm, dm + GATE_LANES), resg)):
                rj = r[j * win:(j + 1) * win]
                xs_ref[0, j, head, cols] = (xs_ref[0, j, head, cols].astype(F32) + rj[0:SLOT_ALIGN]).astype(BF16)
                xs_ref[0, j, tail, cols] = rj[SLOT_ALIGN:].astype(BF16)

    for k in range(cd):
        base = (b * nc + i * cd + k) * ne + g * eg
        small = _max_extent(we_ref, base, eg) <= SMALL_WIN
        pl.when(small)(functools.partial(scatter_rows, SMALL_WIN, k, base))
        pl.when(jnp.logical_not(small))(functools.partial(scatter_rows, WIN, k, base))


def _dispatch(h1, afft, slot, wstart, wextent, cap, eg=8):
    bsz, nc, ne, blk = slot.shape
    t, dm = h1.shape
    cp = cap + WIN
    cd = math.gcd(DISPATCH_STEP_CHUNKS, nc)
    ns = nc // cd
    return pl.pallas_call(
        functools.partial(_dispatch_kernel, nc=nc, ne=ne, eg=eg, cd=cd),
        grid_spec=pltpu.PrefetchScalarGridSpec(
            num_scalar_prefetch=2, grid=(bsz, ne // eg, ns),
            in_specs=[pl.BlockSpec((cd * blk, dm), lambda b, g, i, ws, we: (b * ns + i, 0)),
                      pl.BlockSpec((1, cd, ne, blk), lambda b, g, i, ws, we: (b, i, 0, 0)),
                      pl.BlockSpec((1, cd, 1, eg, blk), lambda b, g, i, ws, we: (b, i, g, 0, 0))],
            out_specs=pl.BlockSpec((1, eg, cp, dm + GATE_LANES), lambda b, g, i, ws, we: (b, g, 0, 0))),
        out_shape=jax.ShapeDtypeStruct((bsz, ne, cp, dm + GATE_LANES), BF16),
        compiler_params=_cparams(("parallel", "parallel", "arbitrary")),
    )(wstart, wextent, h1, afft, slot.reshape(bsz, nc, ne // eg, eg, blk))


def _ffn_kernel(xs_ref, wg_ref, wu_ref, wd_ref, out_ref, hdn_scr, *, cap, nf, ne):
    e = pl.program_id(0)
    j = pl.program_id(1)
    bsz = xs_ref.shape[0]
    dm = wg_ref.shape[2]
    tf = wg_ref.shape[3]

    @pl.when(j < nf)
    def _():
        wg = wg_ref[0, 0].astype(BF16)
        wu = wu_ref[0, 0].astype(BF16)
        for b in range(bsz):
            x = xs_ref[b, 0, :, 0:dm]
            hg = _dot(x, wg)
            hu = _dot(x, wu)
            hdn_scr[j, b * cap:(b + 1) * cap, :] = ((hg * jax.nn.sigmoid(hg)) * hu).astype(BF16)

    @pl.when(j >= nf)
    def _():
        wd = wd_ref[0, 0].astype(BF16)
        cp = out_ref.shape[2]
        tn = out_ref.shape[3]
        lane = lax.broadcasted_iota(jnp.int32, (cap, GATE_LANES), 1)
        mine = ((lane % ne) == e) & (lane < 3 * ne)
        for b in range(bsz):
            rows = slice(b * cap, (b + 1) * cap)
            y = _dot(hdn_scr[0, rows, :], wd[0:tf])
            for f in range(1, nf):
                y = y + _dot(hdn_scr[f, rows, :], wd[f * tf:(f + 1) * tf])
            pieces = xs_ref[b, 0, :, dm:dm + GATE_LANES].astype(F32)
            gate = jnp.sum(jnp.where(mine, pieces, 0.0), axis=1, keepdims=True)
            out_ref[b, 0, 0:cap, :] = (y * gate).astype(BF16)
            out_ref[b, 0, cap:cp, :] = jnp.zeros((cp - cap, tn), BF16)


def _ffn(xs, wg, wu, wd, l, cap, tf=1024):
    bsz, ne, cp, dx = xs.shape
    dm = wg.shape[2]
    ff = wg.shape[-1]
    nf = ff // tf
    tn = dm // nf
    return pl.pallas_call(
        functools.partial(_ffn_kernel, cap=cap, nf=nf, ne=ne), grid=(ne, 2 * nf),
        in_specs=[pl.BlockSpec((bsz, 1, cap, dx), lambda e, j: (0, e, 0, 0)),
                  pl.BlockSpec((1, 1, dm, tf), lambda e, j: (l, e, 0, jnp.minimum(j, nf - 1))),
                  pl.BlockSpec((1, 1, dm, tf), lambda e, j: (l, e, 0, jnp.minimum(j, nf - 1))),
                  pl.BlockSpec((1, 1, ff, tn), lambda e, j: (l, e, 0, jnp.maximum(j - nf, 0)))],
        out_specs=pl.BlockSpec((bsz, 1, cp, tn), lambda e, j: (0, e, 0, jnp.maximum(j - nf, 0))),
        out_shape=jax.ShapeDtypeStruct((bsz, ne, cp, dm), BF16),
        scratch_shapes=[pltpu.VMEM((nf, bsz * cap, tf), BF16)],
        compiler_params=_cparams(("parallel", "arbitrary")),
    )(xs, wg, wu, wd)


def _combine_kernel(ws_ref, we_ref, out_ref, slott_ref, h1_ref, lng_ref, lnb_ref, h2_ref, rhs_scr, *, nc, ne, cc):
    b = pl.program_id(0)
    i = pl.program_id(1)
    blk = ATT_BLOCK

    def finish(k, tot):
        rows = slice(k * blk, (k + 1) * blk)
        h2_ref[rows, :] = _layer_norm(DEEPNORM_ALPHA * h1_ref[rows, :] + tot, lng_ref[0], lnb_ref[0])

    def gather_packed(k, base):
        kk = ne * SMALL_WIN
        slot_t = slott_ref[0, k].astype(F32).astype(BF16)
        expand = (lax.broadcasted_iota(jnp.int32, (ne, kk), 1) // SMALL_WIN
                  == lax.broadcasted_iota(jnp.int32, (ne, kk), 0)).astype(BF16)
        spread = _dot(slot_t, expand)
        row = (lax.broadcasted_iota(jnp.int32, (blk, kk), 1) % SMALL_WIN).astype(F32)
        onehot = (spread == row).astype(BF16)
        for e in range(ne):
            w = pl.multiple_of(ws_ref[base + e], SLOT_ALIGN)
            rhs_scr[e * SMALL_WIN:(e + 1) * SMALL_WIN, :] = out_ref[0, e, pl.ds(w, SMALL_WIN), :]
        finish(k, _dot(onehot, rhs_scr[...]))

    def gather_per_expert(k, base):
        slot_t = slott_ref[0, k]
        liota = lax.broadcasted_iota(jnp.int32, (blk, WIN), 1)
        tot = None
        for e in range(ne):
            w = pl.multiple_of(ws_ref[base + e], SLOT_ALIGN)
            onehot = (liota == slot_t[:, e:e + 1]).astype(BF16)
            y = _dot(onehot, out_ref[0, e, pl.ds(w, WIN), :])
            tot = y if tot is None else tot + y
        finish(k, tot)

    for k in range(cc):
        base = (b * nc + i * cc + k) * ne
        small = _max_extent(we_ref, base, ne) <= SMALL_WIN
        pl.when(small)(functools.partial(gather_packed, k, base))
        pl.when(jnp.logical_not(small))(functools.partial(gather_per_expert, k, base))


def _combine(out, slot, wstart, wextent, h1, lng3, lnb3, l, seq):
    bsz, ne, cp, dm = out.shape
    nc = seq // ATT_BLOCK
    blk = ATT_BLOCK
    t = h1.shape[0]
    slot_t = jnp.swapaxes(slot, 2, 3)
    cc = math.gcd(COMBINE_STEP_CHUNKS, nc)
    ns = nc // cc
    vec = pl.BlockSpec((1, 1, dm), lambda b, i, ws, we: (l, 0, 0))
    return pl.pallas_call(
        functools.partial(_combine_kernel, nc=nc, ne=ne, cc=cc),
        grid_spec=pltpu.PrefetchScalarGridSpec(
            num_scalar_prefetch=2, grid=(bsz, ns),
            in_specs=[pl.BlockSpec((1, ne, cp, dm), lambda b, i, ws, we: (b, 0, 0, 0), pipeline_mode=pl.Buffered(1)),
                      pl.BlockSpec((1, cc, blk, ne), lambda b, i, ws, we: (b, i, 0, 0)),
                      pl.BlockSpec((cc * blk, dm), lambda b, i, ws, we: (b * ns + i, 0)),
                      vec, vec],
            out_specs=pl.BlockSpec((cc * blk, dm), lambda b, i, ws, we: (b * ns + i, 0)),
            scratch_shapes=[pltpu.VMEM((ne * SMALL_WIN, dm), BF16)]),
        out_shape=jax.ShapeDtypeStruct((t, dm), F32),
        compiler_params=_cparams(("parallel", "arbitrary")),
    )(wstart, wextent, out, slot_t, h1, lng3, lnb3)


def kernel(x, ln_in_g, ln_in_b, w_in, ret_theta, ssm_lambda_re, ssm_lambda_im, ssm_log_step, ssm_b_re, ssm_b_im,
           ssm_c_re, ssm_c_im, ssm_d, ssm_w_glu, ssm_b_glu, attn_sink, w_out, ln1_g, ln1_b, router_w,
           exp_w_gate, exp_w_up, exp_w_down, ln2_g, ln2_b):
    bsz, seq, dm = x.shape
    depth = w_in.shape[0]
    ne = router_w.shape[-1]
    cap = EC_FACTOR * seq // ne
    t = bsz * seq
    w_in_bf = w_in.astype(BF16)
    wt_bf = jnp.swapaxes(jnp.concatenate([w_in[:, :, COL_SU:COL_SU + SSM_WIDTH], w_in[:, :, COL_RK:COL_RK + RET_WIDTH]],
                                         axis=2), 1, 2).astype(BF16)
    w_out_bf = w_out.astype(BF16)
    w_glu_bf = ssm_w_glu.astype(BF16)
    lg = jax.nn.log_sigmoid(ret_theta.astype(F32))
    sink = attn_sink.astype(F32)
    ssm_par = _ssm_params(ssm_lambda_re, ssm_lambda_im, ssm_log_step, ssm_b_re, ssm_b_im, ssm_c_re, ssm_c_im)
    vec3 = lambda a: a.astype(F32).reshape(depth, 1, -1)
    d3, bglu3, ln1g3, ln1b3, ln2g3, ln2b3 = map(vec3, (ssm_d, ssm_b_glu, ln1_g, ln1_b, ln2_g, ln2_b))
    rwt = jnp.swapaxes(router_w.astype(F32), 1, 2)
    rwt_hi = rwt.astype(BF16)
    rwt = jnp.concatenate([rwt_hi, (rwt - rwt_hi.astype(F32)).astype(BF16)], axis=1)
    bias = _attention_bias(seq)

    h = x
    for l in range(depth):
        if l == 0:
            h, proj, ut, kt = _inproj(h, w_in_bf, wt_bf, l, ln=(ln_in_g, ln_in_b))
        else:
            proj, ut, kt = _inproj(h, w_in_bf, wt_bf, l)
        h2d = h.reshape(t, dm)
        proj = proj.reshape(t, D_IN)
        yt = _ssm_conv(ut, ssm_par, l, bsz)
        sf, sb = _ret_states(proj, kt, lg, l, bsz, seq)
        h1, afft = _mixer(proj, sf, sb, yt, h2d, bias, lg, sink, w_out_bf, w_glu_bf, d3, bglu3, ln1g3, ln1b3,
                          rwt, l, bsz, seq)
        slot, wstart, wextent = _route(afft, cap)
        xs = _dispatch(h1, afft, slot, wstart, wextent, cap)
        out = _ffn(xs, exp_w_gate, exp_w_up, exp_w_down, l, cap)
        h = _combine(out, slot, wstart, wextent, h1, ln2g3, ln2b3, l, seq).reshape(bsz, seq, dm)
    return h
```

```python
import functools
import math

import jax
import jax.numpy as jnp
from jax import lax
from jax.experimental import pallas as pl
from jax.experimental.pallas import tpu as pltpu

F32 = jnp.float32
BF16 = jnp.bfloat16

RET_HEADS = 4
RET_DK = 64
RET_CHUNK = 128
RET_WIDTH = RET_HEADS * RET_DK
SSM_CPG = 16
SSM_GROUPS = 16
SSM_STATE = 64
SSM_WIDTH = SSM_CPG * SSM_GROUPS
ATT_HEADS = 8
ATT_KV_HEADS = 2
ATT_GQ = ATT_HEADS // ATT_KV_HEADS
ATT_HEAD_DIM = 64
ATT_BLOCK = 128
ATT_WINDOW = 128
EC_FACTOR = 2
DEPTH = 2
DEEPNORM_ALPHA = (2.0 * DEPTH) ** 0.25
LN_EPS = 1e-5
NEG_INF = -1e30

COL_RQ, COL_RK, COL_RV, COL_RG, COL_SU, COL_AQ, COL_AK, COL_AV = 0, 256, 512, 768, 1024, 1280, 1792, 1920
D_IN = 2048

CHUNK = 128
SUBLANES = 8
SLOT_ALIGN = 16
WIN = CHUNK + SLOT_ALIGN
LANES = 128
SMALL_WIN = LANES // 2
DISPATCH_STEP_CHUNKS = 8
COMBINE_STEP_CHUNKS = 4
MIXER_STEP_CHUNKS = 2
GATE_LANES = 128
RET_STEP_CHUNKS = 8
VMEM_LIMIT = 56 * 1024 * 1024
ROUTE_REFINE_STEPS = 10


def _cparams(sem):
    return pltpu.CompilerParams(dimension_semantics=sem, vmem_limit_bytes=VMEM_LIMIT)


def _layer_norm(x, g, b):
    mu = jnp.mean(x, axis=-1, keepdims=True)
    xc = x - mu
    var = jnp.mean(xc * xc, axis=-1, keepdims=True)
    return xc * lax.rsqrt(var + LN_EPS) * g + b


def _dot(a, b):
    return jnp.dot(a, b, preferred_element_type=F32)


def _dot_hp(a, b):
    return jnp.dot(a, b, precision=lax.Precision.HIGHEST, preferred_element_type=F32)


def _dot_nt(a, b):
    return lax.dot_general(a, b, (((1,), (1,)), ((), ())), preferred_element_type=F32)


def _cmul(ar, ai, br, bi):
    return ar * br - ai * bi, ar * bi + ai * br


def _cpow(br, bi, expo, nbits):
    shape = (br.shape[0], expo.shape[1])
    rr = jnp.ones(shape, F32)
    ri = jnp.zeros(shape, F32)
    for j in range(nbits):
        bit = jnp.broadcast_to(((expo >> j) & 1) == 1, shape)
        nr, ni = _cmul(rr, ri, br, bi)
        rr = jnp.where(bit, nr, rr)
        ri = jnp.where(bit, ni, ri)
        if j + 1 < nbits:
            br, bi = _cmul(br, bi, br, bi)
    return rr, ri


def _inproj_kernel(*refs, apply_ln):
    if apply_ln:
        x_ref, g_ref, b_ref, w_ref, wt_ref, h_ref, proj_ref, ut_ref, kt_ref = refs
    else:
        x_ref, w_ref, wt_ref, proj_ref, ut_ref, kt_ref = refs
    nb, tl, d = x_ref.shape
    h = x_ref[...].reshape(nb * tl, d)
    if apply_ln:
        h = _layer_norm(h, g_ref[...], b_ref[...])
        h_ref[...] = h.reshape(nb, tl, d)
    hb = h.astype(BF16)
    proj_ref[...] = _dot(hb, w_ref[0]).reshape(nb, tl, -1)
    tt = _dot_nt(wt_ref[0], hb)
    k = tl // CHUNK
    for j in range(k):
        for b in range(nb):
            ut_ref[:, j * nb + b, :] = tt[0:SSM_WIDTH, (b * k + j) * CHUNK:(b * k + j + 1) * CHUNK]
    for b in range(nb):
        kt_ref[b] = tt[SSM_WIDTH:SSM_WIDTH + RET_WIDTH, b * tl:(b + 1) * tl]


def _inproj(x3, w_bf, wt_bf, l, ln=None):
    bsz, seq, d = x3.shape
    n = w_bf.shape[2]
    k = SUBLANES // bsz
    tl = k * CHUNK
    nc = seq // CHUNK
    xspec = pl.BlockSpec((bsz, tl, d), lambda i: (0, i, 0))
    wspec = pl.BlockSpec((1, d, n), lambda i: (l, 0, 0))
    wtspec = pl.BlockSpec((1,) + wt_bf.shape[1:], lambda i: (l, 0, 0))
    out_specs = [pl.BlockSpec((bsz, tl, n), lambda i: (0, i, 0)),
                 pl.BlockSpec((SSM_WIDTH, k * bsz, CHUNK), lambda i: (0, i, 0)),
                 pl.BlockSpec((bsz, RET_WIDTH, tl), lambda i: (0, 0, i))]
    out_shape = [jax.ShapeDtypeStruct((bsz, seq, n), F32),
                 jax.ShapeDtypeStruct((SSM_WIDTH, nc * bsz, CHUNK), F32),
                 jax.ShapeDtypeStruct((bsz, RET_WIDTH, seq), F32)]
    if ln is None:
        return pl.pallas_call(
            functools.partial(_inproj_kernel, apply_ln=False), grid=(seq // tl,),
            in_specs=[xspec, wspec, wtspec], out_specs=out_specs, out_shape=out_shape,
            compiler_params=_cparams(("parallel",)),
        )(x3, w_bf, wt_bf)
    g, b = ln
    vec = pl.BlockSpec((1, d), lambda i: (0, 0))
    return pl.pallas_call(
        functools.partial(_inproj_kernel, apply_ln=True), grid=(seq // tl,),
        in_specs=[xspec, vec, vec, wspec, wtspec], out_specs=[xspec] + out_specs,
        out_shape=[jax.ShapeDtypeStruct(x3.shape, F32)] + out_shape,
        compiler_params=_cparams(("parallel",)),
    )(x3, g.reshape(1, d), b.reshape(1, d), w_bf, wt_bf)


def _ssm_params(lam_re, lam_im, log_step, b_re, b_im, c_re, c_im):
    lr, li = lam_re.astype(F32), lam_im.astype(F32)
    step = jnp.exp(log_step.astype(F32))[..., None]
    er = jnp.exp(lr * step)
    lbr, lbi = er * jnp.cos(li * step), er * jnp.sin(li * step)
    den = lr * lr + li * li
    fr = ((lbr - 1.0) * lr + lbi * li) / den
    fi = (lbi * lr - (lbr - 1.0) * li) / den
    br = jnp.swapaxes(b_re.astype(F32), -1, -2)[:, None]
    bi = jnp.swapaxes(b_im.astype(F32), -1, -2)[:, None]
    bbr = fr[..., None, :] * br - fi[..., None, :] * bi
    bbi = fr[..., None, :] * bi + fi[..., None, :] * br
    lcol = jnp.stack([lbr[:, 0], lbi[:, 0], lbr[:, 1], lbi[:, 1]], axis=-1)
    lrow = jnp.swapaxes(lcol, -1, -2)
    lrow = jnp.concatenate([lrow, jnp.zeros_like(lrow)], axis=2)
    bt = jnp.stack([bbr[:, 0], bbi[:, 0], bbr[:, 1], bbi[:, 1]], axis=2)
    cr, ci = c_re.astype(F32), c_im.astype(F32)
    c4 = jnp.stack([cr[:, 0], ci[:, 0], cr[:, 1], ci[:, 1]], axis=2)
    return lcol, lrow, bt, jnp.swapaxes(bt, -1, -2), c4, jnp.swapaxes(c4, -1, -2)


def _ssm_kernel(ut_ref, lcol_ref, lrow_ref, bt_ref, bcol_ref, c_ref, ct_ref, y_ref,
                w_scr, wst_scr, g_scr, vf_scr, vb_scr, acc_scr, s_scr, sp_scr, x_scr, *, nb, nchunks):
    tc, p, cpg = CHUNK, SSM_STATE, SSM_CPG
    lcol = lcol_ref[0, 0]
    lf = (lcol[:, 0:1], lcol[:, 1:2])
    lb = (lcol[:, 2:3], lcol[:, 3:4])

    m_row = lax.broadcasted_iota(jnp.int32, (1, tc), 1)
    pfr, pfi = _cpow(*lf, m_row, 7)
    pf1r, pf1i = _cmul(pfr, pfi, *lf)
    prr, pri = _cpow(*lf, tc - 1 - m_row, 7)
    pbr, pbi = _cpow(*lb, tc - m_row, 8)
    pqr, pqi = _cpow(*lb, m_row, 7)

    def rep_co(x):
        return jnp.broadcast_to(x[:, None, :], (cpg, cpg, p)).reshape(cpg * cpg, p)

    def rep_ci(x):
        return jnp.broadcast_to(x[None, :, :], (cpg, cpg, p)).reshape(cpg * cpg, p)

    cbfr, cbfi = _cmul(rep_co(c_ref[0, 0, 0]), rep_co(c_ref[0, 0, 1]), rep_ci(bt_ref[0, 0, 0]), rep_ci(bt_ref[0, 0, 1]))
    cbbr, cbbi = _cmul(rep_co(c_ref[0, 0, 2]), rep_co(c_ref[0, 0, 3]), rep_ci(bt_ref[0, 0, 2]), rep_ci(bt_ref[0, 0, 3]))
    lane = lax.broadcasted_iota(jnp.int32, (cpg * cpg, tc), 1)
    kb0 = jnp.sum(cbbr, axis=1, keepdims=True)
    vf_scr[...] = _dot_hp(cbfr, pfr) - _dot_hp(cbfi, pfi) + jnp.where(lane == 0, kb0, 0.0)
    vb_scr[...] = _dot_hp(cbbr, pbr) - _dot_hp(cbbi, pbi)

    for co in range(cpg):
        cols = slice(co * tc, (co + 1) * tc)
        for r, (ar, ai) in enumerate(((pf1r, pf1i), (pbr, pbi))):
            gr, gi = _cmul(ct_ref[0, 0, 2 * r][:, co:co + 1], ct_ref[0, 0, 2 * r + 1][:, co:co + 1], ar, ai)
            g_scr[2 * r * p:(2 * r + 1) * p, cols] = gr.astype(BF16)
            g_scr[(2 * r + 1) * p:(2 * r + 2) * p, cols] = (-gi).astype(BF16)
        for r, (ar, ai) in enumerate(((prr, pri), (pqr, pqi))):
            sr, si = _cmul(bcol_ref[0, 0, 2 * r][:, co:co + 1], bcol_ref[0, 0, 2 * r + 1][:, co:co + 1], ar, ai)
            wst_scr[co, 2 * r * p:(2 * r + 1) * p, :] = sr.astype(BF16)
            wst_scr[co, (2 * r + 1) * p:(2 * r + 2) * p, :] = si.astype(BF16)

    s_idx = lax.broadcasted_iota(jnp.int32, (tc, tc), 0)
    j_idx = lax.broadcasted_iota(jnp.int32, (tc, tc), 1)
    fwd_part = j_idx < tc - s_idx
    acc_scr[...] = jnp.zeros_like(acc_scr)
    s_scr[...] = jnp.zeros_like(s_scr)

    def per_channel(ci, carry):
        for co in range(cpg):
            rf = jnp.broadcast_to(vf_scr[pl.ds(co * cpg + ci, 1), :], (tc, tc))
            rb = jnp.broadcast_to(vb_scr[pl.ds(co * cpg + ci, 1), :], (tc, tc))
            m = pltpu.roll(jnp.where(fwd_part, rf, rb), 0, 1, stride=1, stride_axis=0)
            w_scr[ci, :, co * tc:(co + 1) * tc] = m.astype(BF16)
        u = ut_ref[ci].astype(BF16)
        acc_scr[...] += _dot(u, w_scr[ci])
        s_scr[...] += _dot_nt(u, wst_scr[ci])
        return carry

    lax.fori_loop(0, cpg, per_channel, 0)

    lrow = lrow_ref[0, 0]
    dec = []
    for r in range(2):
        dr, di = lrow[2 * r:2 * r + 1], lrow[2 * r + 1:2 * r + 2]
        for _ in range(7):
            dr, di = _cmul(dr, di, dr, di)
        dec.append((dr, di))
    for q in range(4):
        sp_scr[q] = s_scr[:, q * p:(q + 1) * p]
    xfr = xfi = xbr = xbi = jnp.zeros((nb, p), F32)
    for i in range(nchunks):
        rf = slice(i * nb, (i + 1) * nb)
        rb = slice((nchunks - 1 - i) * nb, (nchunks - i) * nb)
        x_scr[0, rf, :] = xfr
        x_scr[1, rf, :] = xfi
        x_scr[2, rb, :] = xbr
        x_scr[3, rb, :] = xbi
        xfr, xfi = _cmul(dec[0][0], dec[0][1], xfr, xfi)
        xbr, xbi = _cmul(dec[1][0], dec[1][1], xbr, xbi)
        xfr, xfi = xfr + sp_scr[0, rf, :], xfi + sp_scr[1, rf, :]
        xbr, xbi = xbr + sp_scr[2, rb, :], xbi + sp_scr[3, rb, :]
    xin = jnp.concatenate([x_scr[q] for q in range(4)], axis=1).astype(BF16)
    y = acc_scr[...] + _dot(xin, g_scr[...])
    for co in range(cpg):
        y_ref[co] = y[:, co * tc:(co + 1) * tc]


def _ssm_conv(ut, params, l, bsz):
    _, r, tc = ut.shape
    cpg, p, ng = SSM_CPG, SSM_STATE, SSM_GROUPS

    def pspec(a):
        return pl.BlockSpec((1, 1) + a.shape[2:], lambda g: (l, g) + (0,) * (a.ndim - 2))
    return pl.pallas_call(
        functools.partial(_ssm_kernel, nb=bsz, nchunks=r // bsz), grid=(ng,),
        in_specs=[pl.BlockSpec((cpg, r, tc), lambda g: (g, 0, 0))] + [pspec(a) for a in params],
        out_specs=pl.BlockSpec((cpg, r, tc), lambda g: (g, 0, 0)),
        out_shape=jax.ShapeDtypeStruct(ut.shape, F32),
        scratch_shapes=[pltpu.VMEM((cpg, tc, cpg * tc), BF16), pltpu.VMEM((cpg, 4 * p, tc), BF16),
                        pltpu.VMEM((4 * p, cpg * tc), BF16),
                        pltpu.VMEM((cpg * cpg, tc), F32), pltpu.VMEM((cpg * cpg, tc), F32),
                        pltpu.VMEM((r, cpg * tc), F32), pltpu.VMEM((r, 4 * p), F32),
                        pltpu.VMEM((4, r, p), F32), pltpu.VMEM((4, r, p), F32)],
        compiler_params=_cparams(("parallel",)),
    )(ut, *params)


def _retstate_kernel(lg_ref, ktf_ref, vf_ref, ktb_ref, vb_ref, sf_ref, sb_ref, accf, accb, *, l, cs):
    i = pl.program_id(1)

    @pl.when(i == 0)
    def _():
        accf[...] = jnp.zeros_like(accf)
        accb[...] = jnp.zeros_like(accb)

    ch = RET_CHUNK
    pos = lax.broadcasted_iota(jnp.int32, (1, ch), 1).astype(F32)
    one = jnp.ones((1, RET_DK), F32)
    wts = []
    for h in range(RET_HEADS):
        lgf = lg_ref[l, 0, h]
        lgb = lg_ref[l, 1, h]
        wts.append((jnp.exp(lgf * (ch - 1.0 - pos)) * RET_DK ** -0.5, jnp.exp(lgb * pos) * RET_DK ** -0.5,
                    jnp.exp(one * (lgf * ch)), jnp.exp(one * (lgb * ch))))
    for j in range(cs):
        jb = cs - 1 - j
        sf_ref[0, j] = accf[...]
        sb_ref[0, jb] = accb[...]
        for h in range(RET_HEADS):
            rows = slice(h * RET_DK, (h + 1) * RET_DK)
            wf, wb, decf, decb = wts[h]
            kf = (ktf_ref[0, rows, j * ch:(j + 1) * ch] * wf).astype(BF16)
            kb = (ktb_ref[0, rows, jb * ch:(jb + 1) * ch] * wb).astype(BF16)
            accf[rows, :] = decf * accf[rows, :] + _dot(kf, vf_ref[j * ch:(j + 1) * ch, rows].astype(BF16))
            accb[rows, :] = decb * accb[rows, :] + _dot(kb, vb_ref[jb * ch:(jb + 1) * ch, rows].astype(BF16))


def _ret_states(proj, kt, lg, l, bsz, seq):
    nc = seq // RET_CHUNK
    cs = min(RET_STEP_CHUNKS, nc)
    ns = nc // cs
    w = RET_WIDTH
    tl = cs * RET_CHUNK
    vcol = COL_RV // w
    st = jax.ShapeDtypeStruct((bsz, nc, w, RET_DK), F32)
    return pl.pallas_call(
        functools.partial(_retstate_kernel, l=l, cs=cs), grid=(bsz, ns),
        in_specs=[pl.BlockSpec(memory_space=pltpu.SMEM),
                  pl.BlockSpec((1, w, tl), lambda b, i: (b, 0, i)),
                  pl.BlockSpec((tl, w), lambda b, i: (b * ns + i, vcol)),
                  pl.BlockSpec((1, w, tl), lambda b, i: (b, 0, ns - 1 - i)),
                  pl.BlockSpec((tl, w), lambda b, i: (b * ns + ns - 1 - i, vcol))],
        out_specs=[pl.BlockSpec((1, cs, w, RET_DK), lambda b, i: (b, i, 0, 0)),
                   pl.BlockSpec((1, cs, w, RET_DK), lambda b, i: (b, ns - 1 - i, 0, 0))],
        out_shape=[st, st],
        scratch_shapes=[pltpu.VMEM((w, RET_DK), F32), pltpu.VMEM((w, RET_DK), F32)],
        compiler_params=_cparams(("parallel", "arbitrary")),
    )(lg, kt, proj, kt, proj)


def _gelu_tanh(x):
    return 0.5 * x * (1.0 + jnp.tanh(math.sqrt(2.0 / math.pi) * (x + 0.044715 * (x * x * x))))


def _attention_bias(seq):
    blk = ATT_BLOCK
    s_idx = jnp.arange(3 * blk)[None, :]
    t_idx = jnp.arange(blk)[:, None]
    arel = jnp.abs(s_idx - blk - t_idx)
    band = arel <= ATT_WINDOW
    slopes = jnp.exp2(-8.0 * jnp.arange(1, ATT_HEADS + 1, dtype=F32) / ATT_HEADS)
    alibi = -slopes[:, None, None] * arel.astype(F32)[None]
    variants = []
    for prev_ok, next_ok in ((False, True), (True, True), (True, False)):
        ok = band & (prev_ok | (s_idx >= blk)) & (next_ok | (s_idx < 2 * blk))
        variants.append(jnp.where(ok[None], alibi, NEG_INF).reshape(ATT_KV_HEADS, ATT_GQ * blk, 3 * blk))
    return jnp.stack(variants)


def _mixer_kernel(lg_ref, sink_ref, proj_ref, kvp_ref, kvn_ref, sf_ref, sb_ref, yt_ref, h_ref, bias0_ref, bias1_ref,
                  wout_ref, wglu_ref, d_ref, bglu_ref, lng_ref, lnb_ref, rwt_ref,
                  h1_ref, afft_ref, mix_scr, sret_scr, satt_scr, pret_scr, patt_scr, *, l, nb):
    b = pl.program_id(1)
    cm = MIXER_STEP_CHUNKS
    bias_refs = (bias0_ref, bias1_ref)
    ch = RET_CHUNK
    dk = RET_DK
    blk = ATT_BLOCK
    hd = ATT_HEAD_DIM
    kvw = ATT_KV_HEADS * hd

    def kv_block(u, cols):
        if u < 0:
            return kvp_ref[:, cols]
        if u >= cm:
            return kvn_ref[:, cols]
        return proj_ref[u * ch:(u + 1) * ch, COL_AK + cols.start:COL_AK + cols.stop]

    vbs = {}
    for u in range(cm):
        r = slice(u * ch, (u + 1) * ch)
        for h in range(RET_HEADS):
            q = proj_ref[r, COL_RQ + h * dk:COL_RQ + (h + 1) * dk]
            k = proj_ref[r, COL_RK + h * dk:COL_RK + (h + 1) * dk] * dk ** -0.5
            sret_scr[u * RET_HEADS + h] = _dot_nt(q.astype(BF16), k.astype(BF16))
        for kvh in range(ATT_KV_HEADS):
            kc = slice(kvh * hd, (kvh + 1) * hd)
            vc = slice(kvw + kvh * hd, kvw + (kvh + 1) * hd)
            kb = jnp.concatenate([kv_block(u + d, kc) for d in (-1, 0, 1)], axis=0).astype(BF16)
            vbs[u, kvh] = jnp.concatenate([kv_block(u + d, vc) for d in (-1, 0, 1)], axis=0).astype(BF16)
            q4 = jnp.concatenate([proj_ref[r, COL_AQ + (kvh * ATT_GQ + gq) * hd:COL_AQ + (kvh * ATT_GQ + gq + 1) * hd]
                                  for gq in range(ATT_GQ)], axis=0) * hd ** -0.5
            satt_scr[u * ATT_KV_HEADS + kvh] = _dot_nt(q4.astype(BF16), kb)

    pos = lax.broadcasted_iota(jnp.int32, (ch, 1), 0).astype(F32)
    dist = lax.broadcasted_iota(jnp.int32, (ch, ch), 0) - lax.broadcasted_iota(jnp.int32, (ch, ch), 1)
    adist = jnp.abs(dist).astype(F32)
    for h in range(RET_HEADS):
        lgf = lg_ref[l, 0, h]
        lgb = lg_ref[l, 1, h]
        dmat = jnp.exp(jnp.where(dist >= 0, lgf, lgb) * adist)
        wqf = jnp.exp(lgf * (pos + 1.0))
        wqb = jnp.exp(lgb * (ch - pos))
        for u in range(cm):
            q = proj_ref[u * ch:(u + 1) * ch, COL_RQ + h * dk:COL_RQ + (h + 1) * dk]
            uh = u * RET_HEADS + h
            pret_scr[uh, :, 0:ch] = (sret_scr[uh] * dmat).astype(BF16)
            pret_scr[uh, :, ch:ch + dk] = (q * wqf).astype(BF16)
            pret_scr[uh, :, ch + dk:ch + 2 * dk] = (q * wqb).astype(BF16)
    for u in range(cm):
        for kvh in range(ATT_KV_HEADS):
            uk = u * ATT_KV_HEADS + kvh
            for gq in range(ATT_GQ):
                rows = slice(gq * blk, (gq + 1) * blk)
                sink = sink_ref[l, kvh * ATT_GQ + gq]
                sc = satt_scr[uk, rows, :] + bias_refs[u][0, kvh, rows, :]
                m = jnp.maximum(jnp.max(sc, axis=-1, keepdims=True), sink)
                p = jnp.exp(sc - m)
                denom = jnp.sum(p, axis=-1, keepdims=True) + jnp.exp(sink - m)
                patt_scr[uk, rows, :] = (p * (1.0 / denom)).astype(BF16)

    att_base = RET_WIDTH + SSM_WIDTH
    w_ssm = SSM_WIDTH
    for u in range(cm):
        r = slice(u * ch, (u + 1) * ch)
        for h in range(RET_HEADS):
            rows = slice(h * dk, (h + 1) * dk)
            v = proj_ref[r, COL_RV + h * dk:COL_RV + (h + 1) * dk]
            g = proj_ref[r, COL_RG + h * dk:COL_RG + (h + 1) * dk]
            rhs = jnp.concatenate([v, sf_ref[0, u, rows, :], sb_ref[0, u, rows, :]], axis=0).astype(BF16)
            o = _dot(pret_scr[u * RET_HEADS + h], rhs)
            mu = jnp.mean(o, axis=-1, keepdims=True)
            oc = o - mu
            var = jnp.mean(oc * oc, axis=-1, keepdims=True)
            mix_scr[r, h * dk:(h + 1) * dk] = (g * jax.nn.sigmoid(g)) * (oc * lax.rsqrt(var + LN_EPS))
        for kvh in range(ATT_KV_HEADS):
            o = _dot(patt_scr[u * ATT_KV_HEADS + kvh], vbs[u, kvh])
            for gq in range(ATT_GQ):
                hh = kvh * ATT_GQ + gq
                mix_scr[r, att_base + hh * hd:att_base + (hh + 1) * hd] = o[gq * blk:(gq + 1) * blk]

        yraw = yt_ref[:, pl.ds(u * nb + b, 1), :].reshape(w_ssm, ch).T
        y = yraw + d_ref[0] * proj_ref[r, COL_SU:COL_SU + w_ssm]
        y = _gelu_tanh(y)
        gate = jax.nn.sigmoid(_dot(y.astype(BF16), wglu_ref[0]) + bglu_ref[0])
        mix_scr[r, RET_WIDTH:RET_WIDTH + w_ssm] = y * gate

    mix = _dot(mix_scr[...].astype(BF16), wout_ref[0])
    h1 = _layer_norm(DEEPNORM_ALPHA * h_ref[...] + mix, lng_ref[0], lnb_ref[0])
    h1_ref[...] = h1

    ne = rwt_ref.shape[1] // 2
    h_hi = h1.astype(BF16)
    h_lo = (h1 - h_hi.astype(F32)).astype(BF16)
    both = _dot_nt(rwt_ref[0], h_hi)
    lt = both[0:ne] + both[ne:2 * ne] + _dot_nt(rwt_ref[0, 0:ne, :], h_lo)
    lt = lt - jnp.max(lt, axis=0, keepdims=True)
    et = jnp.exp(lt)
    aff = et / jnp.sum(et, axis=0, keepdims=True)
    for u in range(cm):
        afft_ref[0, u] = aff[:, u * ch:(u + 1) * ch]


def _mixer(proj, sf, sb, yt, h2d, bias, lg, sink, wout_bf, wglu_bf, d3, bglu3, lng3, lnb3, rwt, l, bsz, seq):
    nc = seq // ATT_BLOCK
    cm = MIXER_STEP_CHUNKS
    ns = nc // cm
    assert nc % cm == 0 and cm * bsz == SUBLANES
    t, dm = h2d.shape
    ne = rwt.shape[1] // 2
    kvw = 2 * ATT_KV_HEADS * ATT_HEAD_DIM
    sw = RET_WIDTH
    kvcol = COL_AK // kvw
    blk = ATT_BLOCK

    def layer(a):
        return pl.BlockSpec((1,) + a.shape[1:], lambda i, b: (l,) + (0,) * (a.ndim - 1))
    smem = pl.BlockSpec(memory_space=pltpu.SMEM)
    in_specs = [
        smem, smem,
        pl.BlockSpec((cm * blk, D_IN), lambda i, b: (b * ns + i, 0)),
        pl.BlockSpec((blk, kvw), lambda i, b: (b * nc + jnp.maximum(cm * i - 1, 0), kvcol)),
        pl.BlockSpec((blk, kvw), lambda i, b: (b * nc + jnp.minimum(cm * i + cm, nc - 1), kvcol)),
        pl.BlockSpec((1, cm, sw, RET_DK), lambda i, b: (b, i, 0, 0)),
        pl.BlockSpec((1, cm, sw, RET_DK), lambda i, b: (b, i, 0, 0)),
        pl.BlockSpec((SSM_WIDTH, SUBLANES, CHUNK), lambda i, b: (0, i, 0)),
        pl.BlockSpec((cm * blk, dm), lambda i, b: (b * ns + i, 0)),
        pl.BlockSpec((1,) + bias.shape[1:], lambda i, b: (jnp.where(i == 0, 0, 1), 0, 0, 0)),
        pl.BlockSpec((1,) + bias.shape[1:], lambda i, b: (jnp.where(i == ns - 1, 2, 1), 0, 0, 0)),
        layer(wout_bf), layer(wglu_bf), layer(d3), layer(bglu3), layer(lng3), layer(lnb3), layer(rwt),
    ]
    out_specs = [
        pl.BlockSpec((cm * blk, dm), lambda i, b: (b * ns + i, 0)),
        pl.BlockSpec((1, cm, ne, blk), lambda i, b: (b, i, 0, 0)),
    ]
    out_shape = [jax.ShapeDtypeStruct((t, dm), F32),
                 jax.ShapeDtypeStruct((bsz, nc, ne, blk), F32)]
    scratch = [pltpu.VMEM((cm * blk, dm), F32),
               pltpu.VMEM((cm * RET_HEADS, blk, blk), F32),
               pltpu.VMEM((cm * ATT_KV_HEADS, ATT_GQ * blk, 3 * blk), F32),
               pltpu.VMEM((cm * RET_HEADS, blk, blk + 2 * RET_DK), BF16),
               pltpu.VMEM((cm * ATT_KV_HEADS, ATT_GQ * blk, 3 * blk), BF16)]
    return pl.pallas_call(
        functools.partial(_mixer_kernel, l=l, nb=bsz), grid=(ns, bsz),
        in_specs=in_specs, out_specs=out_specs, out_shape=out_shape, scratch_shapes=scratch,
        compiler_params=_cparams(("parallel", "parallel")),
    )(lg, sink, proj, proj, proj, sf, sb, yt, h2d, bias, bias, wout_bf, wglu_bf, d3, bglu3, lng3, lnb3, rwt)


def _route_kernel(a_ref, slot_ref, start_ref, extent_ref, *, nc, ne, cap):
    blk = ATT_BLOCK
    a = a_ref[0]

    def chunk(x, c):
        return x[c * ne:(c + 1) * ne]

    def count(pred):
        x = pred.astype(jnp.int32)
        tot = chunk(x, 0)
        for c in range(1, nc):
            tot = tot + chunk(x, c)
        return jnp.sum(tot, axis=1, keepdims=True)

    def tile_e(v):
        return jnp.concatenate([v] * nc, axis=0)

    tau = jnp.zeros((ne, 1), jnp.int32)
    for bit in range(30, -1, -1):
        cand = tau | (1 << bit)
        tau = jnp.where(count(a >= tile_e(pltpu.bitcast(cand, F32))) >= cap, cand, tau)
    lo = pltpu.bitcast(tau, F32)
    hi = pltpu.bitcast(tau + 1, F32)
    for _ in range(ROUTE_REFINE_STEPS):
        mid = lo + (hi - lo) * 0.5
        ok = count(a >= tile_e(mid)) >= cap
        lo = jnp.where(ok, mid, lo)
        hi = jnp.where(ok, hi, mid)
    gt = a >= tile_e(hi)
    eq = (a >= tile_e(lo)) & jnp.logical_not(gt)
    need = cap - count(gt)

    tri = (lax.broadcasted_iota(jnp.int32, (blk, blk), 0) <= lax.broadcasted_iota(jnp.int32, (blk, blk), 1)).astype(BF16)

    def ranks(mask):
        mf = mask.astype(F32)
        incl = _dot(mf.astype(BF16), tri)
        run = jnp.zeros((ne, 1), F32)
        offs = []
        tots = []
        for c in range(nc):
            offs.append(run)
            tots.append(chunk(incl, c)[:, blk - 1:blk])
            run = run + tots[-1]
        return incl - mf, jnp.concatenate(offs, axis=0), jnp.concatenate(tots, axis=0)

    eq_local, eq_off, _ = ranks(eq)
    mask = gt | (eq & ((eq_local + eq_off) < tile_e(need).astype(F32)))
    local, off, tot = ranks(mask)
    off_i = off.astype(jnp.int32)
    start = (off_i // SLOT_ALIGN) * SLOT_ALIGN
    slot = (off_i - start) + local.astype(jnp.int32)
    slot_ref[0] = jnp.where(mask, slot, -1)
    start_ref[0] = jnp.broadcast_to(start, (nc * ne, blk))
    extent_ref[0] = jnp.broadcast_to((off_i - start) + tot.astype(jnp.int32), (nc * ne, blk))


def _route(afft, cap):
    bsz, nc, ne, blk = afft.shape
    a2 = afft.reshape(bsz, nc * ne, blk)
    spec = pl.BlockSpec((1, nc * ne, blk), lambda b: (b, 0, 0))
    slot, start, extent = pl.pallas_call(
        functools.partial(_route_kernel, nc=nc, ne=ne, cap=cap), grid=(bsz,),
        in_specs=[spec], out_specs=[spec, spec, spec],
        out_shape=[jax.ShapeDtypeStruct(a2.shape, jnp.int32)] * 3,
        compiler_params=_cparams(("parallel",)),
    )(a2)
    flat = lambda v: v[:, :, 0].reshape(bsz * nc * ne)
    return slot.reshape(bsz, nc, ne, blk), flat(start), flat(extent)


def _max_extent(we_ref, base, n):
    mx = we_ref[base]
    for j in range(1, n):
        mx = jnp.maximum(mx, we_ref[base + j])
    return mx


def _dispatch_kernel(ws_ref, we_ref, h_ref, afft_ref, slot_ref, xs_ref, *, nc, ne, eg, cd):
    b = pl.program_id(0)
    g = pl.program_id(1)
    i = pl.program_id(2)
    dm = h_ref.shape[1]
    blk = ATT_BLOCK

    @pl.when(i == 0)
    def _():
        xs_ref[...] = jnp.zeros_like(xs_ref)

    def scatter_rows(win, k, base):
        hb = h_ref[k * blk:(k + 1) * blk, :].astype(BF16)
        a = afft_ref[0, k]
        a_hi = a.astype(BF16)
        r1 = a - a_hi.astype(F32)
        a_mid = r1.astype(BF16)
        a_lo = (r1 - a_mid.astype(F32)).astype(BF16)
        a3 = jnp.concatenate([a_hi, a_mid, a_lo, jnp.zeros((GATE_LANES - 3 * ne, blk), BF16)], axis=0)
        riota = lax.broadcasted_iota(jnp.int32, (win, blk), 0)
        onehots = jnp.concatenate([(riota == slot_ref[0, k, 0, j:j + 1, :]).astype(BF16) for j in range(eg)], axis=0)
        res = _dot(onehots, hb)
        resg = _dot_nt(onehots, a3)
        for j in range(eg):
            w = pl.multiple_of(ws_ref[base + j], SLOT_ALIGN)
            head = pl.ds(w, SLOT_ALIGN)
            tail = pl.ds(w + SLOT_ALIGN, win - SLOT_ALIGN)
            for cols, r in ((slice(0, dm), res), (slice(dm, dm + GATE_LANES), resg)):
                rj = r[j * win:(j + 1) * win]
                xs_ref[0, j, head, cols] = (xs_ref[0, j, head, cols].astype(F32) + rj[0:SLOT_ALIGN]).astype(BF16)
                xs_ref[0, j, tail, cols] = rj[SLOT_ALIGN:].astype(BF16)

    for k in range(cd):
        base = (b * nc + i * cd + k) * ne + g * eg
        small = _max_extent(we_ref, base, eg) <= SMALL_WIN
        pl.when(small)(functools.partial(scatter_rows, SMALL_WIN, k, base))
        pl.when(jnp.logical_not(small))(functools.partial(scatter_rows, WIN, k, base))


def _dispatch(h1, afft, slot, wstart, wextent, cap, eg=8):
    bsz, nc, ne, blk = slot.shape
    t, dm = h1.shape
    cp = cap + WIN
    cd = math.gcd(DISPATCH_STEP_CHUNKS, nc)
    ns = nc // cd
    return pl.pallas_call(
        functools.partial(_dispatch_kernel, nc=nc, ne=ne, eg=eg, cd=cd),
        grid_spec=pltpu.PrefetchScalarGridSpec(
            num_scalar_prefetch=2, grid=(bsz, ne // eg, ns),
            in_specs=[pl.BlockSpec((cd * blk, dm), lambda b, g, i, ws, we: (b * ns + i, 0)),
                      pl.BlockSpec((1, cd, ne, blk), lambda b, g, i, ws, we: (b, i, 0, 0)),
                      pl.BlockSpec((1, cd, 1, eg, blk), lambda b, g, i, ws, we: (b, i, g, 0, 0))],
            out_specs=pl.BlockSpec((1, eg, cp, dm + GATE_LANES), lambda b, g, i, ws, we: (b, g, 0, 0))),
        out_shape=jax.ShapeDtypeStruct((bsz, ne, cp, dm + GATE_LANES), BF16),
        compiler_params=_cparams(("parallel", "parallel", "arbitrary")),
    )(wstart, wextent, h1, afft, slot.reshape(bsz, nc, ne // eg, eg, blk))


def _ffn_kernel(xs_ref, wg_ref, wu_ref, wd_ref, out_ref, hdn_scr, *, cap, nf, ne):
    e = pl.program_id(0)
    j = pl.program_id(1)
    bsz = xs_ref.shape[0]
    dm = wg_ref.shape[2]
    tf = wg_ref.shape[3]

    @pl.when(j < nf)
    def _():
        wg = wg_ref[0, 0].astype(BF16)
        wu = wu_ref[0, 0].astype(BF16)
        for b in range(bsz):
            x = xs_ref[b, 0, :, 0:dm]
            hg = _dot(x, wg)
            hu = _dot(x, wu)
            hdn_scr[j, b * cap:(b + 1) * cap, :] = ((hg * jax.nn.sigmoid(hg)) * hu).astype(BF16)

    @pl.when(j >= nf)
    def _():
        wd = wd_ref[0, 0].astype(BF16)
        cp = out_ref.shape[2]
        tn = out_ref.shape[3]
        lane = lax.broadcasted_iota(jnp.int32, (cap, GATE_LANES), 1)
        mine = ((lane % ne) == e) & (lane < 3 * ne)
        for b in range(bsz):
            rows = slice(b * cap, (b + 1) * cap)
            y = _dot(hdn_scr[0, rows, :], wd[0:tf])
            for f in range(1, nf):
                y = y + _dot(hdn_scr[f, rows, :], wd[f * tf:(f + 1) * tf])
            pieces = xs_ref[b, 0, :, dm:dm + GATE_LANES].astype(F32)
            gate = jnp.sum(jnp.where(mine, pieces, 0.0), axis=1, keepdims=True)
            out_ref[b, 0, 0:cap, :] = (y * gate).astype(BF16)
            out_ref[b, 0, cap:cp, :] = jnp.zeros((cp - cap, tn), BF16)


def _ffn(xs, wg, wu, wd, l, cap, tf=1024):
    bsz, ne, cp, dx = xs.shape
    dm = wg.shape[2]
    ff = wg.shape[-1]
    nf = ff // tf
    tn = dm // nf
    return pl.pallas_call(
        functools.partial(_ffn_kernel, cap=cap, nf=nf, ne=ne), grid=(ne, 2 * nf),
        in_specs=[pl.BlockSpec((bsz, 1, cap, dx), lambda e, j: (0, e, 0, 0)),
                  pl.BlockSpec((1, 1, dm, tf), lambda e, j: (l, e, 0, jnp.minimum(j, nf - 1))),
                  pl.BlockSpec((1, 1, dm, tf), lambda e, j: (l, e, 0, jnp.minimum(j, nf - 1))),
                  pl.BlockSpec((1, 1, ff, tn), lambda e, j: (l, e, 0, jnp.maximum(j - nf, 0)))],
        out_specs=pl.BlockSpec((bsz, 1, cp, tn), lambda e, j: (0, e, 0, jnp.maximum(j - nf, 0))),
        out_shape=jax.ShapeDtypeStruct((bsz, ne, cp, dm), BF16),
        scratch_shapes=[pltpu.VMEM((nf, bsz * cap, tf), BF16)],
        compiler_params=_cparams(("parallel", "arbitrary")),
    )(xs, wg, wu, wd)


def _combine_kernel(ws_ref, we_ref, out_ref, slott_ref, h1_ref, lng_ref, lnb_ref, h2_ref, rhs_scr, *, nc, ne, cc):
    b = pl.program_id(0)
    i = pl.program_id(1)
    blk = ATT_BLOCK

    def finish(k, tot):
        rows = slice(k * blk, (k + 1) * blk)
        h2_ref[rows, :] = _layer_norm(DEEPNORM_ALPHA * h1_ref[rows, :] + tot, lng_ref[0], lnb_ref[0])

    def gather_packed(k, base):
        kk = ne * SMALL_WIN
        slot_t = slott_ref[0, k].astype(F32).astype(BF16)
        expand = (lax.broadcasted_iota(jnp.int32, (ne, kk), 1) // SMALL_WIN
                  == lax.broadcasted_iota(jnp.int32, (ne, kk), 0)).astype(BF16)
        spread = _dot(slot_t, expand)
        row = (lax.broadcasted_iota(jnp.int32, (blk, kk), 1) % SMALL_WIN).astype(F32)
        onehot = (spread == row).astype(BF16)
        for e in range(ne):
            w = pl.multiple_of(ws_ref[base + e], SLOT_ALIGN)
            rhs_scr[e * SMALL_WIN:(e + 1) * SMALL_WIN, :] = out_ref[0, e, pl.ds(w, SMALL_WIN), :]
        finish(k, _dot(onehot, rhs_scr[...]))

    def gather_per_expert(k, base):
        slot_t = slott_ref[0, k]
        liota = lax.broadcasted_iota(jnp.int32, (blk, WIN), 1)
        tot = None
        for e in range(ne):
            w = pl.multiple_of(ws_ref[base + e], SLOT_ALIGN)
            onehot = (liota == slot_t[:, e:e + 1]).astype(BF16)
            y = _dot(onehot, out_ref[0, e, pl.ds(w, WIN), :])
            tot = y if tot is None else tot + y
        finish(k, tot)

    for k in range(cc):
        base = (b * nc + i * cc + k) * ne
        small = _max_extent(we_ref, base, ne) <= SMALL_WIN
        pl.when(small)(functools.partial(gather_packed, k, base))
        pl.when(jnp.logical_not(small))(functools.partial(gather_per_expert, k, base))


def _combine(out, slot, wstart, wextent, h1, lng3, lnb3, l, seq):
    bsz, ne, cp, dm = out.shape
    nc = seq // ATT_BLOCK
    blk = ATT_BLOCK
    t = h1.shape[0]
    slot_t = jnp.swapaxes(slot, 2, 3)
    cc = math.gcd(COMBINE_STEP_CHUNKS, nc)
    ns = nc // cc
    vec = pl.BlockSpec((1, 1, dm), lambda b, i, ws, we: (l, 0, 0))
    return pl.pallas_call(
        functools.partial(_combine_kernel, nc=nc, ne=ne, cc=cc),
        grid_spec=pltpu.PrefetchScalarGridSpec(
            num_scalar_prefetch=2, grid=(bsz, ns),
            in_specs=[pl.BlockSpec((1, ne, cp, dm), lambda b, i, ws, we: (b, 0, 0, 0)),
                      pl.BlockSpec((1, cc, blk, ne), lambda b, i, ws, we: (b, i, 0, 0)),
                      pl.BlockSpec((cc * blk, dm), lambda b, i, ws, we: (b * ns + i, 0)),
                      vec, vec],
            out_specs=pl.BlockSpec((cc * blk, dm), lambda b, i, ws, we: (b * ns + i, 0)),
            scratch_shapes=[pltpu.VMEM((ne * SMALL_WIN, dm), BF16)]),
        out_shape=jax.ShapeDtypeStruct((t, dm), F32),
        compiler_params=_cparams(("parallel", "arbitrary")),
    )(wstart, wextent, out, slot_t, h1, lng3, lnb3)


def kernel(x, ln_in_g, ln_in_b, w_in, ret_theta, ssm_lambda_re, ssm_lambda_im, ssm_log_step, ssm_b_re, ssm_b_im,
           ssm_c_re, ssm_c_im, ssm_d, ssm_w_glu, ssm_b_glu, attn_sink, w_out, ln1_g, ln1_b, router_w,
           exp_w_gate, exp_w_up, exp_w_down, ln2_g, ln2_b):
    bsz, seq, dm = x.shape
    depth = w_in.shape[0]
    ne = router_w.shape[-1]
    cap = EC_FACTOR * seq // ne
    t = bsz * seq
    w_in_bf = w_in.astype(BF16)
    wt_bf = jnp.swapaxes(jnp.concatenate([w_in_bf[:, :, COL_SU:COL_SU + SSM_WIDTH],
                                          w_in_bf[:, :, COL_RK:COL_RK + RET_WIDTH]], axis=2), 1, 2)
    w_out_bf = w_out.astype(BF16)
    w_glu_bf = ssm_w_glu.astype(BF16)
    lg = jax.nn.log_sigmoid(ret_theta.astype(F32))
    sink = attn_sink.astype(F32)
    ssm_par = _ssm_params(ssm_lambda_re, ssm_lambda_im, ssm_log_step, ssm_b_re, ssm_b_im, ssm_c_re, ssm_c_im)
    vec3 = lambda a: a.astype(F32).reshape(depth, 1, -1)
    d3, bglu3, ln1g3, ln1b3, ln2g3, ln2b3 = map(vec3, (ssm_d, ssm_b_glu, ln1_g, ln1_b, ln2_g, ln2_b))
    rwt = jnp.swapaxes(router_w.astype(F32), 1, 2)
    rwt_hi = rwt.astype(BF16)
    rwt = jnp.concatenate([rwt_hi, (rwt - rwt_hi.astype(F32)).astype(BF16)], axis=1)
    bias = _attention_bias(seq)

    h = x
    for l in range(depth):
        if l == 0:
            h, proj, ut, kt = _inproj(h, w_in_bf, wt_bf, l, ln=(ln_in_g, ln_in_b))
        else:
            proj, ut, kt = _inproj(h, w_in_bf, wt_bf, l)
        h2d = h.reshape(t, dm)
        proj = proj.reshape(t, D_IN)
        yt = _ssm_conv(ut, ssm_par, l, bsz)
        sf, sb = _ret_states(proj, kt, lg, l, bsz, seq)
        h1, afft = _mixer(proj, sf, sb, yt, h2d, bias, lg, sink, w_out_bf, w_glu_bf, d3, bglu3, ln1g3, ln1b3,
                          rwt, l, bsz, seq)
        slot, wstart, wextent = _route(afft, cap)
        xs = _dispatch(h1, afft, slot, wstart, wextent, cap)
        out = _ffn(xs, exp_w_gate, exp_w_up, exp_w_down, l, cap)
        h = _combine(out, slot, wstart, wextent, h1, ln2g3, ln2b3, l, seq).reshape(bsz, seq, dm)
    return h
```

```python
import functools
import math

import jax
import jax.numpy as jnp
from jax import lax
from jax.experimental import pallas as pl
from jax.experimental.pallas import tpu as pltpu

F32 = jnp.float32
BF16 = jnp.bfloat16

RET_HEADS = 4
RET_DK = 64
RET_CHUNK = 128
RET_WIDTH = RET_HEADS * RET_DK
SSM_CPG = 16
SSM_GROUPS = 16
SSM_STATE = 64
SSM_WIDTH = SSM_CPG * SSM_GROUPS
ATT_HEADS = 8
ATT_KV_HEADS = 2
ATT_GQ = ATT_HEADS // ATT_KV_HEADS
ATT_HEAD_DIM = 64
ATT_BLOCK = 128
ATT_WINDOW = 128
EC_FACTOR = 2
DEPTH = 2
DEEPNORM_ALPHA = (2.0 * DEPTH) ** 0.25
LN_EPS = 1e-5
NEG_INF = -1e30

COL_RQ, COL_RK, COL_RV, COL_RG, COL_SU, COL_AQ, COL_AK, COL_AV = 0, 256, 512, 768, 1024, 1280, 1792, 1920
D_IN = 2048

CHUNK = 128
SUBLANES = 8
SLOT_ALIGN = 16
WIN = CHUNK + SLOT_ALIGN
SMALL_WIN = 3 * SLOT_ALIGN
DISPATCH_STEP_CHUNKS = 8
COMBINE_STEP_CHUNKS = 4
MIXER_STEP_CHUNKS = 2
GATE_LANES = 128
RET_STEP_CHUNKS = 8
VMEM_LIMIT = 56 * 1024 * 1024
ROUTE_REFINE_STEPS = 10


def _cparams(sem):
    return pltpu.CompilerParams(dimension_semantics=sem, vmem_limit_bytes=VMEM_LIMIT)


def _layer_norm(x, g, b):
    mu = jnp.mean(x, axis=-1, keepdims=True)
    xc = x - mu
    var = jnp.mean(xc * xc, axis=-1, keepdims=True)
    return xc * lax.rsqrt(var + LN_EPS) * g + b


def _dot(a, b):
    return jnp.dot(a, b, preferred_element_type=F32)


def _dot_hp(a, b):
    return jnp.dot(a, b, precision=lax.Precision.HIGHEST, preferred_element_type=F32)


def _dot_nt(a, b):
    return lax.dot_general(a, b, (((1,), (1,)), ((), ())), preferred_element_type=F32)


def _cmul(ar, ai, br, bi):
    return ar * br - ai * bi, ar * bi + ai * br


def _cpow(br, bi, expo, nbits):
    shape = (br.shape[0], expo.shape[1])
    rr = jnp.ones(shape, F32)
    ri = jnp.zeros(shape, F32)
    for j in range(nbits):
        bit = jnp.broadcast_to(((expo >> j) & 1) == 1, shape)
        nr, ni = _cmul(rr, ri, br, bi)
        rr = jnp.where(bit, nr, rr)
        ri = jnp.where(bit, ni, ri)
        if j + 1 < nbits:
            br, bi = _cmul(br, bi, br, bi)
    return rr, ri


def _inproj_kernel(*refs, apply_ln):
    if apply_ln:
        x_ref, g_ref, b_ref, w_ref, wt_ref, h_ref, proj_ref, ut_ref, kt_ref = refs
    else:
        x_ref, w_ref, wt_ref, proj_ref, ut_ref, kt_ref = refs
    nb, tl, d = x_ref.shape
    h = x_ref[...].reshape(nb * tl, d)
    if apply_ln:
        h = _layer_norm(h, g_ref[...], b_ref[...])
        h_ref[...] = h.reshape(nb, tl, d)
    hb = h.astype(BF16)
    proj_ref[...] = _dot(hb, w_ref[0]).reshape(nb, tl, -1)
    tt = _dot_nt(wt_ref[0], hb)
    k = tl // CHUNK
    for j in range(k):
        for b in range(nb):
            ut_ref[:, j * nb + b, :] = tt[0:SSM_WIDTH, (b * k + j) * CHUNK:(b * k + j + 1) * CHUNK]
    for b in range(nb):
        kt_ref[b] = tt[SSM_WIDTH:SSM_WIDTH + RET_WIDTH, b * tl:(b + 1) * tl]


def _inproj(x3, w_bf, wt_bf, l, ln=None):
    bsz, seq, d = x3.shape
    n = w_bf.shape[2]
    k = SUBLANES // bsz
    tl = k * CHUNK
    nc = seq // CHUNK
    xspec = pl.BlockSpec((bsz, tl, d), lambda i: (0, i, 0))
    wspec = pl.BlockSpec((1, d, n), lambda i: (l, 0, 0))
    wtspec = pl.BlockSpec((1,) + wt_bf.shape[1:], lambda i: (l, 0, 0))
    out_specs = [pl.BlockSpec((bsz, tl, n), lambda i: (0, i, 0)),
                 pl.BlockSpec((SSM_WIDTH, k * bsz, CHUNK), lambda i: (0, i, 0)),
                 pl.BlockSpec((bsz, RET_WIDTH, tl), lambda i: (0, 0, i))]
    out_shape = [jax.ShapeDtypeStruct((bsz, seq, n), F32),
                 jax.ShapeDtypeStruct((SSM_WIDTH, nc * bsz, CHUNK), F32),
                 jax.ShapeDtypeStruct((bsz, RET_WIDTH, seq), F32)]
    if ln is None:
        return pl.pallas_call(
            functools.partial(_inproj_kernel, apply_ln=False), grid=(seq // tl,),
            in_specs=[xspec, wspec, wtspec], out_specs=out_specs, out_shape=out_shape,
            compiler_params=_cparams(("parallel",)),
        )(x3, w_bf, wt_bf)
    g, b = ln
    vec = pl.BlockSpec((1, d), lambda i: (0, 0))
    return pl.pallas_call(
        functools.partial(_inproj_kernel, apply_ln=True), grid=(seq // tl,),
        in_specs=[xspec, vec, vec, wspec, wtspec], out_specs=[xspec] + out_specs,
        out_shape=[jax.ShapeDtypeStruct(x3.shape, F32)] + out_shape,
        compiler_params=_cparams(("parallel",)),
    )(x3, g.reshape(1, d), b.reshape(1, d), w_bf, wt_bf)


def _ssm_params(lam_re, lam_im, log_step, b_re, b_im, c_re, c_im):
    lr, li = lam_re.astype(F32), lam_im.astype(F32)
    step = jnp.exp(log_step.astype(F32))[..., None]
    er = jnp.exp(lr * step)
    lbr, lbi = er * jnp.cos(li * step), er * jnp.sin(li * step)
    den = lr * lr + li * li
    fr = ((lbr - 1.0) * lr + lbi * li) / den
    fi = (lbi * lr - (lbr - 1.0) * li) / den
    br = jnp.swapaxes(b_re.astype(F32), -1, -2)[:, None]
    bi = jnp.swapaxes(b_im.astype(F32), -1, -2)[:, None]
    bbr = fr[..., None, :] * br - fi[..., None, :] * bi
    bbi = fr[..., None, :] * bi + fi[..., None, :] * br
    lcol = jnp.stack([lbr[:, 0], lbi[:, 0], lbr[:, 1], lbi[:, 1]], axis=-1)
    lrow = jnp.swapaxes(lcol, -1, -2)
    lrow = jnp.concatenate([lrow, jnp.zeros_like(lrow)], axis=2)
    bt = jnp.stack([bbr[:, 0], bbi[:, 0], bbr[:, 1], bbi[:, 1]], axis=2)
    cr, ci = c_re.astype(F32), c_im.astype(F32)
    c4 = jnp.stack([cr[:, 0], ci[:, 0], cr[:, 1], ci[:, 1]], axis=2)
    return lcol, lrow, bt, jnp.swapaxes(bt, -1, -2), c4, jnp.swapaxes(c4, -1, -2)


def _ssm_kernel(ut_ref, lcol_ref, lrow_ref, bt_ref, bcol_ref, c_ref, ct_ref, y_ref,
                w_scr, wst_scr, g_scr, vf_scr, vb_scr, acc_scr, s_scr, sp_scr, x_scr, *, nb, nchunks):
    tc, p, cpg = CHUNK, SSM_STATE, SSM_CPG
    lcol = lcol_ref[0, 0]
    lf = (lcol[:, 0:1], lcol[:, 1:2])
    lb = (lcol[:, 2:3], lcol[:, 3:4])

    m_row = lax.broadcasted_iota(jnp.int32, (1, tc), 1)
    pfr, pfi = _cpow(*lf, m_row, 7)
    pf1r, pf1i = _cmul(pfr, pfi, *lf)
    prr, pri = _cpow(*lf, tc - 1 - m_row, 7)
    pbr, pbi = _cpow(*lb, tc - m_row, 8)
    pqr, pqi = _cpow(*lb, m_row, 7)

    def rep_co(x):
        return jnp.broadcast_to(x[:, None, :], (cpg, cpg, p)).reshape(cpg * cpg, p)

    def rep_ci(x):
        return jnp.broadcast_to(x[None, :, :], (cpg, cpg, p)).reshape(cpg * cpg, p)

    cbfr, cbfi = _cmul(rep_co(c_ref[0, 0, 0]), rep_co(c_ref[0, 0, 1]), rep_ci(bt_ref[0, 0, 0]), rep_ci(bt_ref[0, 0, 1]))
    cbbr, cbbi = _cmul(rep_co(c_ref[0, 0, 2]), rep_co(c_ref[0, 0, 3]), rep_ci(bt_ref[0, 0, 2]), rep_ci(bt_ref[0, 0, 3]))
    lane = lax.broadcasted_iota(jnp.int32, (cpg * cpg, tc), 1)
    kb0 = jnp.sum(cbbr, axis=1, keepdims=True)
    vf_scr[...] = _dot_hp(cbfr, pfr) - _dot_hp(cbfi, pfi) + jnp.where(lane == 0, kb0, 0.0)
    vb_scr[...] = _dot_hp(cbbr, pbr) - _dot_hp(cbbi, pbi)

    for co in range(cpg):
        cols = slice(co * tc, (co + 1) * tc)
        for r, (ar, ai) in enumerate(((pf1r, pf1i), (pbr, pbi))):
            gr, gi = _cmul(ct_ref[0, 0, 2 * r][:, co:co + 1], ct_ref[0, 0, 2 * r + 1][:, co:co + 1], ar, ai)
            g_scr[2 * r * p:(2 * r + 1) * p, cols] = gr.astype(BF16)
            g_scr[(2 * r + 1) * p:(2 * r + 2) * p, cols] = (-gi).astype(BF16)
        for r, (ar, ai) in enumerate(((prr, pri), (pqr, pqi))):
            sr, si = _cmul(bcol_ref[0, 0, 2 * r][:, co:co + 1], bcol_ref[0, 0, 2 * r + 1][:, co:co + 1], ar, ai)
            wst_scr[co, 2 * r * p:(2 * r + 1) * p, :] = sr.astype(BF16)
            wst_scr[co, (2 * r + 1) * p:(2 * r + 2) * p, :] = si.astype(BF16)

    s_idx = lax.broadcasted_iota(jnp.int32, (tc, tc), 0)
    j_idx = lax.broadcasted_iota(jnp.int32, (tc, tc), 1)
    fwd_part = j_idx < tc - s_idx
    acc_scr[...] = jnp.zeros_like(acc_scr)
    s_scr[...] = jnp.zeros_like(s_scr)

    def per_channel(ci, carry):
        for co in range(cpg):
            rf = jnp.broadcast_to(vf_scr[pl.ds(co * cpg + ci, 1), :], (tc, tc))
            rb = jnp.broadcast_to(vb_scr[pl.ds(co * cpg + ci, 1), :], (tc, tc))
            m = pltpu.roll(jnp.where(fwd_part, rf, rb), 0, 1, stride=1, stride_axis=0)
            w_scr[ci, :, co * tc:(co + 1) * tc] = m.astype(BF16)
        u = ut_ref[ci].astype(BF16)
        acc_scr[...] += _dot(u, w_scr[ci])
        s_scr[...] += _dot_nt(u, wst_scr[ci])
        return carry

    lax.fori_loop(0, cpg, per_channel, 0)

    lrow = lrow_ref[0, 0]
    dec = []
    for r in range(2):
        dr, di = lrow[2 * r:2 * r + 1], lrow[2 * r + 1:2 * r + 2]
        for _ in range(7):
            dr, di = _cmul(dr, di, dr, di)
        dec.append((dr, di))
    for q in range(4):
        sp_scr[q] = s_scr[:, q * p:(q + 1) * p]
    xfr = xfi = xbr = xbi = jnp.zeros((nb, p), F32)
    for i in range(nchunks):
        rf = slice(i * nb, (i + 1) * nb)
        rb = slice((nchunks - 1 - i) * nb, (nchunks - i) * nb)
        x_scr[0, rf, :] = xfr
        x_scr[1, rf, :] = xfi
        x_scr[2, rb, :] = xbr
        x_scr[3, rb, :] = xbi
        xfr, xfi = _cmul(dec[0][0], dec[0][1], xfr, xfi)
        xbr, xbi = _cmul(dec[1][0], dec[1][1], xbr, xbi)
        xfr, xfi = xfr + sp_scr[0, rf, :], xfi + sp_scr[1, rf, :]
        xbr, xbi = xbr + sp_scr[2, rb, :], xbi + sp_scr[3, rb, :]
    xin = jnp.concatenate([x_scr[q] for q in range(4)], axis=1).astype(BF16)
    y = acc_scr[...] + _dot(xin, g_scr[...])
    for co in range(cpg):
        y_ref[co] = y[:, co * tc:(co + 1) * tc]


def _ssm_conv(ut, params, l, bsz):
    _, r, tc = ut.shape
    cpg, p, ng = SSM_CPG, SSM_STATE, SSM_GROUPS

    def pspec(a):
        return pl.BlockSpec((1, 1) + a.shape[2:], lambda g: (l, g) + (0,) * (a.ndim - 2))
    return pl.pallas_call(
        functools.partial(_ssm_kernel, nb=bsz, nchunks=r // bsz), grid=(ng,),
        in_specs=[pl.BlockSpec((cpg, r, tc), lambda g: (g, 0, 0))] + [pspec(a) for a in params],
        out_specs=pl.BlockSpec((cpg, r, tc), lambda g: (g, 0, 0)),
        out_shape=jax.ShapeDtypeStruct(ut.shape, F32),
        scratch_shapes=[pltpu.VMEM((cpg, tc, cpg * tc), BF16), pltpu.VMEM((cpg, 4 * p, tc), BF16),
                        pltpu.VMEM((4 * p, cpg * tc), BF16),
                        pltpu.VMEM((cpg * cpg, tc), F32), pltpu.VMEM((cpg * cpg, tc), F32),
                        pltpu.VMEM((r, cpg * tc), F32), pltpu.VMEM((r, 4 * p), F32),
                        pltpu.VMEM((4, r, p), F32), pltpu.VMEM((4, r, p), F32)],
        compiler_params=_cparams(("parallel",)),
    )(ut, *params)


def _retstate_kernel(lg_ref, ktf_ref, vf_ref, ktb_ref, vb_ref, sf_ref, sb_ref, accf, accb, *, l, cs):
    i = pl.program_id(1)

    @pl.when(i == 0)
    def _():
        accf[...] = jnp.zeros_like(accf)
        accb[...] = jnp.zeros_like(accb)

    ch = RET_CHUNK
    pos = lax.broadcasted_iota(jnp.int32, (1, ch), 1).astype(F32)
    one = jnp.ones((1, RET_DK), F32)
    wts = []
    for h in range(RET_HEADS):
        lgf = lg_ref[l, 0, h]
        lgb = lg_ref[l, 1, h]
        wts.append((jnp.exp(lgf * (ch - 1.0 - pos)) * RET_DK ** -0.5, jnp.exp(lgb * pos) * RET_DK ** -0.5,
                    jnp.exp(one * (lgf * ch)), jnp.exp(one * (lgb * ch))))
    for j in range(cs):
        jb = cs - 1 - j
        sf_ref[0, j] = accf[...]
        sb_ref[0, jb] = accb[...]
        for h in range(RET_HEADS):
            rows = slice(h * RET_DK, (h + 1) * RET_DK)
            wf, wb, decf, decb = wts[h]
            kf = (ktf_ref[0, rows, j * ch:(j + 1) * ch] * wf).astype(BF16)
            kb = (ktb_ref[0, rows, jb * ch:(jb + 1) * ch] * wb).astype(BF16)
            accf[rows, :] = decf * accf[rows, :] + _dot(kf, vf_ref[j * ch:(j + 1) * ch, rows].astype(BF16))
            accb[rows, :] = decb * accb[rows, :] + _dot(kb, vb_ref[jb * ch:(jb + 1) * ch, rows].astype(BF16))


def _ret_states(proj, kt, lg, l, bsz, seq):
    nc = seq // RET_CHUNK
    cs = min(RET_STEP_CHUNKS, nc)
    ns = nc // cs
    w = RET_WIDTH
    tl = cs * RET_CHUNK
    vcol = COL_RV // w
    st = jax.ShapeDtypeStruct((bsz, nc, w, RET_DK), F32)
    return pl.pallas_call(
        functools.partial(_retstate_kernel, l=l, cs=cs), grid=(bsz, ns),
        in_specs=[pl.BlockSpec(memory_space=pltpu.SMEM),
                  pl.BlockSpec((1, w, tl), lambda b, i: (b, 0, i)),
                  pl.BlockSpec((tl, w), lambda b, i: (b * ns + i, vcol)),
                  pl.BlockSpec((1, w, tl), lambda b, i: (b, 0, ns - 1 - i)),
                  pl.BlockSpec((tl, w), lambda b, i: (b * ns + ns - 1 - i, vcol))],
        out_specs=[pl.BlockSpec((1, cs, w, RET_DK), lambda b, i: (b, i, 0, 0)),
                   pl.BlockSpec((1, cs, w, RET_DK), lambda b, i: (b, ns - 1 - i, 0, 0))],
        out_shape=[st, st],
        scratch_shapes=[pltpu.VMEM((w, RET_DK), F32), pltpu.VMEM((w, RET_DK), F32)],
        compiler_params=_cparams(("parallel", "arbitrary")),
    )(lg, kt, proj, kt, proj)


def _gelu_tanh(x):
    return 0.5 * x * (1.0 + jnp.tanh(math.sqrt(2.0 / math.pi) * (x + 0.044715 * (x * x * x))))


def _attention_bias(seq):
    blk = ATT_BLOCK
    s_idx = jnp.arange(3 * blk)[None, :]
    t_idx = jnp.arange(blk)[:, None]
    arel = jnp.abs(s_idx - blk - t_idx)
    band = arel <= ATT_WINDOW
    slopes = jnp.exp2(-8.0 * jnp.arange(1, ATT_HEADS + 1, dtype=F32) / ATT_HEADS)
    alibi = -slopes[:, None, None] * arel.astype(F32)[None]
    variants = []
    for prev_ok, next_ok in ((False, True), (True, True), (True, False)):
        ok = band & (prev_ok | (s_idx >= blk)) & (next_ok | (s_idx < 2 * blk))
        variants.append(jnp.where(ok[None], alibi, NEG_INF).reshape(ATT_KV_HEADS, ATT_GQ * blk, 3 * blk))
    return jnp.stack(variants)


def _mixer_kernel(lg_ref, sink_ref, proj_ref, kvp_ref, kvn_ref, sf_ref, sb_ref, yt_ref, h_ref, bias0_ref, bias1_ref,
                  wout_ref, wglu_ref, d_ref, bglu_ref, lng_ref, lnb_ref, rwt_ref,
                  h1_ref, afft_ref, mix_scr, sret_scr, satt_scr, pret_scr, patt_scr, *, l, nb):
    b = pl.program_id(1)
    cm = MIXER_STEP_CHUNKS
    bias_refs = (bias0_ref, bias1_ref)
    ch = RET_CHUNK
    dk = RET_DK
    blk = ATT_BLOCK
    hd = ATT_HEAD_DIM
    kvw = ATT_KV_HEADS * hd

    def kv_block(u, cols):
        if u < 0:
            return kvp_ref[:, cols]
        if u >= cm:
            return kvn_ref[:, cols]
        return proj_ref[u * ch:(u + 1) * ch, COL_AK + cols.start:COL_AK + cols.stop]

    vbs = {}
    for u in range(cm):
        r = slice(u * ch, (u + 1) * ch)
        for h in range(RET_HEADS):
            q = proj_ref[r, COL_RQ + h * dk:COL_RQ + (h + 1) * dk]
            k = proj_ref[r, COL_RK + h * dk:COL_RK + (h + 1) * dk] * dk ** -0.5
            sret_scr[u * RET_HEADS + h] = _dot_nt(q.astype(BF16), k.astype(BF16))
        for kvh in range(ATT_KV_HEADS):
            kc = slice(kvh * hd, (kvh + 1) * hd)
            vc = slice(kvw + kvh * hd, kvw + (kvh + 1) * hd)
            kb = jnp.concatenate([kv_block(u + d, kc) for d in (-1, 0, 1)], axis=0).astype(BF16)
            vbs[u, kvh] = jnp.concatenate([kv_block(u + d, vc) for d in (-1, 0, 1)], axis=0).astype(BF16)
            q4 = jnp.concatenate([proj_ref[r, COL_AQ + (kvh * ATT_GQ + gq) * hd:COL_AQ + (kvh * ATT_GQ + gq + 1) * hd]
                                  for gq in range(ATT_GQ)], axis=0) * hd ** -0.5
            satt_scr[u * ATT_KV_HEADS + kvh] = _dot_nt(q4.astype(BF16), kb)

    pos = lax.broadcasted_iota(jnp.int32, (ch, 1), 0).astype(F32)
    dist = lax.broadcasted_iota(jnp.int32, (ch, ch), 0) - lax.broadcasted_iota(jnp.int32, (ch, ch), 1)
    adist = jnp.abs(dist).astype(F32)
    for h in range(RET_HEADS):
        lgf = lg_ref[l, 0, h]
        lgb = lg_ref[l, 1, h]
        dmat = jnp.exp(jnp.where(dist >= 0, lgf, lgb) * adist)
        wqf = jnp.exp(lgf * (pos + 1.0))
        wqb = jnp.exp(lgb * (ch - pos))
        for u in range(cm):
            q = proj_ref[u * ch:(u + 1) * ch, COL_RQ + h * dk:COL_RQ + (h + 1) * dk]
            uh = u * RET_HEADS + h
            pret_scr[uh, :, 0:ch] = (sret_scr[uh] * dmat).astype(BF16)
            pret_scr[uh, :, ch:ch + dk] = (q * wqf).astype(BF16)
            pret_scr[uh, :, ch + dk:ch + 2 * dk] = (q * wqb).astype(BF16)
    for u in range(cm):
        for kvh in range(ATT_KV_HEADS):
            uk = u * ATT_KV_HEADS + kvh
            for gq in range(ATT_GQ):
                rows = slice(gq * blk, (gq + 1) * blk)
                sink = sink_ref[l, kvh * ATT_GQ + gq]
                sc = satt_scr[uk, rows, :] + bias_refs[u][0, kvh, rows, :]
                m = jnp.maximum(jnp.max(sc, axis=-1, keepdims=True), sink)
                p = jnp.exp(sc - m)
                denom = jnp.sum(p, axis=-1, keepdims=True) + jnp.exp(sink - m)
                patt_scr[uk, rows, :] = (p * (1.0 / denom)).astype(BF16)

    att_base = RET_WIDTH + SSM_WIDTH
    w_ssm = SSM_WIDTH
    for u in range(cm):
        r = slice(u * ch, (u + 1) * ch)
        for h in range(RET_HEADS):
            rows = slice(h * dk, (h + 1) * dk)
            v = proj_ref[r, COL_RV + h * dk:COL_RV + (h + 1) * dk]
            g = proj_ref[r, COL_RG + h * dk:COL_RG + (h + 1) * dk]
            rhs = jnp.concatenate([v, sf_ref[0, u, rows, :], sb_ref[0, u, rows, :]], axis=0).astype(BF16)
            o = _dot(pret_scr[u * RET_HEADS + h], rhs)
            mu = jnp.mean(o, axis=-1, keepdims=True)
            oc = o - mu
            var = jnp.mean(oc * oc, axis=-1, keepdims=True)
            mix_scr[r, h * dk:(h + 1) * dk] = (g * jax.nn.sigmoid(g)) * (oc * lax.rsqrt(var + LN_EPS))
        for kvh in range(ATT_KV_HEADS):
            o = _dot(patt_scr[u * ATT_KV_HEADS + kvh], vbs[u, kvh])
            for gq in range(ATT_GQ):
                hh = kvh * ATT_GQ + gq
                mix_scr[r, att_base + hh * hd:att_base + (hh + 1) * hd] = o[gq * blk:(gq + 1) * blk]

        yraw = yt_ref[:, pl.ds(u * nb + b, 1), :].reshape(w_ssm, ch).T
        y = yraw + d_ref[0] * proj_ref[r, COL_SU:COL_SU + w_ssm]
        y = _gelu_tanh(y)
        gate = jax.nn.sigmoid(_dot(y.astype(BF16), wglu_ref[0]) + bglu_ref[0])
        mix_scr[r, RET_WIDTH:RET_WIDTH + w_ssm] = y * gate

    mix = _dot(mix_scr[...].astype(BF16), wout_ref[0])
    h1 = _layer_norm(DEEPNORM_ALPHA * h_ref[...] + mix, lng_ref[0], lnb_ref[0])
    h1_ref[...] = h1

    ne = rwt_ref.shape[1] // 2
    h_hi = h1.astype(BF16)
    h_lo = (h1 - h_hi.astype(F32)).astype(BF16)
    both = _dot_nt(rwt_ref[0], h_hi)
    lt = both[0:ne] + both[ne:2 * ne] + _dot_nt(rwt_ref[0, 0:ne, :], h_lo)
    lt = lt - jnp.max(lt, axis=0, keepdims=True)
    et = jnp.exp(lt)
    aff = et / jnp.sum(et, axis=0, keepdims=True)
    for u in range(cm):
        afft_ref[u, 0] = aff[:, u * ch:(u + 1) * ch]


def _mixer(proj, sf, sb, yt, h2d, bias, lg, sink, wout_bf, wglu_bf, d3, bglu3, lng3, lnb3, rwt, l, bsz, seq):
    nc = seq // ATT_BLOCK
    cm = MIXER_STEP_CHUNKS
    ns = nc // cm
    assert nc % cm == 0 and cm * bsz == SUBLANES
    t, dm = h2d.shape
    ne = rwt.shape[1] // 2
    kvw = 2 * ATT_KV_HEADS * ATT_HEAD_DIM
    sw = RET_WIDTH
    kvcol = COL_AK // kvw
    blk = ATT_BLOCK

    def layer(a):
        return pl.BlockSpec((1,) + a.shape[1:], lambda i, b: (l,) + (0,) * (a.ndim - 1))
    smem = pl.BlockSpec(memory_space=pltpu.SMEM)
    in_specs = [
        smem, smem,
        pl.BlockSpec((cm * blk, D_IN), lambda i, b: (b * ns + i, 0)),
        pl.BlockSpec((blk, kvw), lambda i, b: (b * nc + jnp.maximum(cm * i - 1, 0), kvcol)),
        pl.BlockSpec((blk, kvw), lambda i, b: (b * nc + jnp.minimum(cm * i + cm, nc - 1), kvcol)),
        pl.BlockSpec((1, cm, sw, RET_DK), lambda i, b: (b, i, 0, 0)),
        pl.BlockSpec((1, cm, sw, RET_DK), lambda i, b: (b, i, 0, 0)),
        pl.BlockSpec((SSM_WIDTH, SUBLANES, CHUNK), lambda i, b: (0, i, 0)),
        pl.BlockSpec((cm * blk, dm), lambda i, b: (b * ns + i, 0)),
        pl.BlockSpec((1,) + bias.shape[1:], lambda i, b: (jnp.where(i == 0, 0, 1), 0, 0, 0)),
        pl.BlockSpec((1,) + bias.shape[1:], lambda i, b: (jnp.where(i == ns - 1, 2, 1), 0, 0, 0)),
        layer(wout_bf), layer(wglu_bf), layer(d3), layer(bglu3), layer(lng3), layer(lnb3), layer(rwt),
    ]
    out_specs = [
        pl.BlockSpec((cm * blk, dm), lambda i, b: (b * ns + i, 0)),
        pl.BlockSpec((cm, 1, ne, blk), lambda i, b: (i, b, 0, 0)),
    ]
    out_shape = [jax.ShapeDtypeStruct((t, dm), F32),
                 jax.ShapeDtypeStruct((nc, bsz, ne, blk), F32)]
    scratch = [pltpu.VMEM((cm * blk, dm), F32),
               pltpu.VMEM((cm * RET_HEADS, blk, blk), F32),
               pltpu.VMEM((cm * ATT_KV_HEADS, ATT_GQ * blk, 3 * blk), F32),
               pltpu.VMEM((cm * RET_HEADS, blk, blk + 2 * RET_DK), BF16),
               pltpu.VMEM((cm * ATT_KV_HEADS, ATT_GQ * blk, 3 * blk), BF16)]
    return pl.pallas_call(
        functools.partial(_mixer_kernel, l=l, nb=bsz), grid=(ns, bsz),
        in_specs=in_specs, out_specs=out_specs, out_shape=out_shape, scratch_shapes=scratch,
        compiler_params=_cparams(("parallel", "parallel")),
    )(lg, sink, proj, proj, proj, sf, sb, yt, h2d, bias, bias, wout_bf, wglu_bf, d3, bglu3, lng3, lnb3, rwt)


def _route_kernel(a_ref, slot_ref, start_ref, extent_ref, *, nc, ne, cap):
    blk = ATT_BLOCK
    a = a_ref[0]

    def chunk(x, c):
        return x[c * ne:(c + 1) * ne]

    def count(pred):
        x = pred.astype(jnp.int32)
        tot = chunk(x, 0)
        for c in range(1, nc):
            tot = tot + chunk(x, c)
        return jnp.sum(tot, axis=1, keepdims=True)

    def tile_e(v):
        return jnp.concatenate([v] * nc, axis=0)

    tau = jnp.zeros((ne, 1), jnp.int32)
    for bit in range(30, -1, -1):
        cand = tau | (1 << bit)
        tau = jnp.where(count(a >= tile_e(pltpu.bitcast(cand, F32))) >= cap, cand, tau)
    lo = pltpu.bitcast(tau, F32)
    hi = pltpu.bitcast(tau + 1, F32)
    for _ in range(ROUTE_REFINE_STEPS):
        mid = lo + (hi - lo) * 0.5
        ok = count(a >= tile_e(mid)) >= cap
        lo = jnp.where(ok, mid, lo)
        hi = jnp.where(ok, hi, mid)
    gt = a >= tile_e(hi)
    eq = (a >= tile_e(lo)) & jnp.logical_not(gt)
    need = cap - count(gt)

    tri = (lax.broadcasted_iota(jnp.int32, (blk, blk), 0) <= lax.broadcasted_iota(jnp.int32, (blk, blk), 1)).astype(BF16)

    def ranks(mask):
        mf = mask.astype(F32)
        incl = _dot(mf.astype(BF16), tri)
        run = jnp.zeros((ne, 1), F32)
        offs = []
        tots = []
        for c in range(nc):
            offs.append(run)
            tots.append(chunk(incl, c)[:, blk - 1:blk])
            run = run + tots[-1]
        return incl - mf, jnp.concatenate(offs, axis=0), jnp.concatenate(tots, axis=0)

    eq_local, eq_off, _ = ranks(eq)
    mask = gt | (eq & ((eq_local + eq_off) < tile_e(need).astype(F32)))
    local, off, tot = ranks(mask)
    off_i = off.astype(jnp.int32)
    start = (off_i // SLOT_ALIGN) * SLOT_ALIGN
    slot = (off_i - start) + local.astype(jnp.int32)
    slot_ref[0] = jnp.where(mask, slot, -1)
    start_ref[0] = jnp.broadcast_to(start, (nc * ne, blk))
    extent_ref[0] = jnp.broadcast_to((off_i - start) + tot.astype(jnp.int32), (nc * ne, blk))


def _route(afft, cap):
    nc, bsz, ne, blk = afft.shape
    a2 = afft.reshape(1, nc * bsz * ne, blk)
    spec = pl.BlockSpec(a2.shape, lambda s: (0, 0, 0))
    slot, start, extent = pl.pallas_call(
        functools.partial(_route_kernel, nc=nc, ne=bsz * ne, cap=cap), grid=(1,),
        in_specs=[spec], out_specs=[spec, spec, spec],
        out_shape=[jax.ShapeDtypeStruct(a2.shape, jnp.int32)] * 3,
        compiler_params=_cparams(("arbitrary",)),
    )(a2)
    flat = lambda v: v[0, :, 0]
    return slot.reshape(nc, bsz, ne, blk), flat(start), flat(extent)


def _max_extent(we_ref, base, n):
    mx = we_ref[base]
    for j in range(1, n):
        mx = jnp.maximum(mx, we_ref[base + j])
    return mx


def _dispatch_kernel(ws_ref, we_ref, h_ref, afft_ref, slot_ref, xs_ref, *, nb, ne, eg, cd):
    b = pl.program_id(0)
    g = pl.program_id(1)
    i = pl.program_id(2)
    dm = h_ref.shape[1]
    blk = ATT_BLOCK

    @pl.when(i == 0)
    def _():
        xs_ref[...] = jnp.zeros_like(xs_ref)

    def scatter_rows(win, k, base):
        hb = h_ref[k * blk:(k + 1) * blk, :].astype(BF16)
        a = afft_ref[k, 0]
        a_hi = a.astype(BF16)
        r1 = a - a_hi.astype(F32)
        a_mid = r1.astype(BF16)
        a_lo = (r1 - a_mid.astype(F32)).astype(BF16)
        a3 = jnp.concatenate([a_hi, a_mid, a_lo, jnp.zeros((GATE_LANES - 3 * ne, blk), BF16)], axis=0)
        riota = lax.broadcasted_iota(jnp.int32, (win, blk), 0)
        onehots = jnp.concatenate([(riota == slot_ref[k, 0, 0, j:j + 1, :]).astype(BF16) for j in range(eg)], axis=0)
        res = _dot(onehots, hb)
        resg = _dot_nt(onehots, a3)
        for j in range(eg):
            w = pl.multiple_of(ws_ref[base + j], SLOT_ALIGN)
            head = pl.ds(w, SLOT_ALIGN)
            tail = pl.ds(w + SLOT_ALIGN, win - SLOT_ALIGN)
            for cols, r in ((slice(0, dm), res), (slice(dm, dm + GATE_LANES), resg)):
                rj = r[j * win:(j + 1) * win]
                xs_ref[0, j, head, cols] = (xs_ref[0, j, head, cols].astype(F32) + rj[0:SLOT_ALIGN]).astype(BF16)
                xs_ref[0, j, tail, cols] = rj[SLOT_ALIGN:].astype(BF16)

    for k in range(cd):
        base = ((i * cd + k) * nb + b) * ne + g * eg
        small = _max_extent(we_ref, base, eg) <= SMALL_WIN
        pl.when(small)(functools.partial(scatter_rows, SMALL_WIN, k, base))
        pl.when(jnp.logical_not(small))(functools.partial(scatter_rows, WIN, k, base))


def _dispatch(h1, afft, slot, wstart, wextent, cap, eg=8):
    nc, bsz, ne, blk = slot.shape
    t, dm = h1.shape
    cp = cap + WIN
    cd = math.gcd(DISPATCH_STEP_CHUNKS, nc)
    ns = nc // cd
    return pl.pallas_call(
        functools.partial(_dispatch_kernel, nb=bsz, ne=ne, eg=eg, cd=cd),
        grid_spec=pltpu.PrefetchScalarGridSpec(
            num_scalar_prefetch=2, grid=(bsz, ne // eg, ns),
            in_specs=[pl.BlockSpec((cd * blk, dm), lambda b, g, i, ws, we: (b * ns + i, 0)),
                      pl.BlockSpec((cd, 1, ne, blk), lambda b, g, i, ws, we: (i, b, 0, 0)),
                      pl.BlockSpec((cd, 1, 1, eg, blk), lambda b, g, i, ws, we: (i, b, g, 0, 0))],
            out_specs=pl.BlockSpec((1, eg, cp, dm + GATE_LANES), lambda b, g, i, ws, we: (b, g, 0, 0))),
        out_shape=jax.ShapeDtypeStruct((bsz, ne, cp, dm + GATE_LANES), BF16),
        compiler_params=_cparams(("parallel", "parallel", "arbitrary")),
    )(wstart, wextent, h1, afft, slot.reshape(nc, bsz, ne // eg, eg, blk))


def _ffn_kernel(xs_ref, wg_ref, wu_ref, wd_ref, out_ref, hdn_scr, *, cap, nf, ne):
    e = pl.program_id(0)
    j = pl.program_id(1)
    bsz = xs_ref.shape[0]
    dm = wg_ref.shape[2]
    tf = wg_ref.shape[3]

    @pl.when(j < nf)
    def _():
        wg = wg_ref[0, 0].astype(BF16)
        wu = wu_ref[0, 0].astype(BF16)
        for b in range(bsz):
            x = xs_ref[b, 0, :, 0:dm]
            hg = _dot(x, wg)
            hu = _dot(x, wu)
            hdn_scr[j, b * cap:(b + 1) * cap, :] = ((hg * jax.nn.sigmoid(hg)) * hu).astype(BF16)

    @pl.when(j >= nf)
    def _():
        wd = wd_ref[0, 0].astype(BF16)
        cp = out_ref.shape[2]
        tn = out_ref.shape[3]
        lane = lax.broadcasted_iota(jnp.int32, (cap, GATE_LANES), 1)
        mine = ((lane % ne) == e) & (lane < 3 * ne)
        for b in range(bsz):
            rows = slice(b * cap, (b + 1) * cap)
            y = _dot(hdn_scr[0, rows, :], wd[0:tf])
            for f in range(1, nf):
                y = y + _dot(hdn_scr[f, rows, :], wd[f * tf:(f + 1) * tf])
            pieces = xs_ref[b, 0, :, dm:dm + GATE_LANES].astype(F32)
            gate = jnp.sum(jnp.where(mine, pieces, 0.0), axis=1, keepdims=True)
            out_ref[b, 0, 0:cap, :] = (y * gate).astype(BF16)
            out_ref[b, 0, cap:cp, :] = jnp.zeros((cp - cap, tn), BF16)


def _ffn(xs, wg, wu, wd, l, cap, tf=1024):
    bsz, ne, cp, dx = xs.shape
    dm = wg.shape[2]
    ff = wg.shape[-1]
    nf = ff // tf
    tn = dm // nf
    return pl.pallas_call(
        functools.partial(_ffn_kernel, cap=cap, nf=nf, ne=ne), grid=(ne, 2 * nf),
        in_specs=[pl.BlockSpec((bsz, 1, cap, dx), lambda e, j: (0, e, 0, 0)),
                  pl.BlockSpec((1, 1, dm, tf), lambda e, j: (l, e, 0, jnp.minimum(j, nf - 1))),
                  pl.BlockSpec((1, 1, dm, tf), lambda e, j: (l, e, 0, jnp.minimum(j, nf - 1))),
                  pl.BlockSpec((1, 1, ff, tn), lambda e, j: (l, e, 0, jnp.maximum(j - nf, 0)))],
        out_specs=pl.BlockSpec((bsz, 1, cp, tn), lambda e, j: (0, e, 0, jnp.maximum(j - nf, 0))),
        out_shape=jax.ShapeDtypeStruct((bsz, ne, cp, dm), BF16),
        scratch_shapes=[pltpu.VMEM((nf, bsz * cap, tf), BF16)],
        compiler_params=_cparams(("parallel", "arbitrary")),
    )(xs, wg, wu, wd)


def _combine_kernel(ws_ref, we_ref, out_ref, slott_ref, h1_ref, lng_ref, lnb_ref, h2_ref, rhs_scr, *, nb, ne, cc):
    b = pl.program_id(0)
    i = pl.program_id(1)
    blk = ATT_BLOCK

    def finish(k, tot):
        rows = slice(k * blk, (k + 1) * blk)
        h2_ref[rows, :] = _layer_norm(DEEPNORM_ALPHA * h1_ref[rows, :] + tot, lng_ref[0], lnb_ref[0])

    def gather_packed(k, base):
        kk = ne * SMALL_WIN
        slot_t = slott_ref[k, 0].astype(F32).astype(BF16)
        expand = (lax.broadcasted_iota(jnp.int32, (ne, kk), 1) // SMALL_WIN
                  == lax.broadcasted_iota(jnp.int32, (ne, kk), 0)).astype(BF16)
        spread = _dot(slot_t, expand)
        row = (lax.broadcasted_iota(jnp.int32, (blk, kk), 1) % SMALL_WIN).astype(F32)
        onehot = (spread == row).astype(BF16)
        for e in range(ne):
            w = pl.multiple_of(ws_ref[base + e], SLOT_ALIGN)
            rhs_scr[e * SMALL_WIN:(e + 1) * SMALL_WIN, :] = out_ref[0, e, pl.ds(w, SMALL_WIN), :]
        finish(k, _dot(onehot, rhs_scr[...]))

    def gather_per_expert(k, base):
        slot_t = slott_ref[k, 0]
        liota = lax.broadcasted_iota(jnp.int32, (blk, WIN), 1)
        tot = None
        for e in range(ne):
            w = pl.multiple_of(ws_ref[base + e], SLOT_ALIGN)
            onehot = (liota == slot_t[:, e:e + 1]).astype(BF16)
            y = _dot(onehot, out_ref[0, e, pl.ds(w, WIN), :])
            tot = y if tot is None else tot + y
        finish(k, tot)

    for k in range(cc):
        base = ((i * cc + k) * nb + b) * ne
        small = _max_extent(we_ref, base, ne) <= SMALL_WIN
        pl.when(small)(functools.partial(gather_packed, k, base))
        pl.when(jnp.logical_not(small))(functools.partial(gather_per_expert, k, base))


def _combine(out, slot, wstart, wextent, h1, lng3, lnb3, l, seq):
    bsz, ne, cp, dm = out.shape
    nc = seq // ATT_BLOCK
    blk = ATT_BLOCK
    t = h1.shape[0]
    slot_t = jnp.swapaxes(slot, 2, 3)
    cc = math.gcd(COMBINE_STEP_CHUNKS, nc)
    ns = nc // cc
    vec = pl.BlockSpec((1, 1, dm), lambda b, i, ws, we: (l, 0, 0))
    return pl.pallas_call(
        functools.partial(_combine_kernel, nb=bsz, ne=ne, cc=cc),
        grid_spec=pltpu.PrefetchScalarGridSpec(
            num_scalar_prefetch=2, grid=(bsz, ns),
            in_specs=[pl.BlockSpec((1, ne, cp, dm), lambda b, i, ws, we: (b, 0, 0, 0)),
                      pl.BlockSpec((cc, 1, blk, ne), lambda b, i, ws, we: (i, b, 0, 0)),
                      pl.BlockSpec((cc * blk, dm), lambda b, i, ws, we: (b * ns + i, 0)),
                      vec, vec],
            out_specs=pl.BlockSpec((cc * blk, dm), lambda b, i, ws, we: (b * ns + i, 0)),
            scratch_shapes=[pltpu.VMEM((ne * SMALL_WIN, dm), BF16)]),
        out_shape=jax.ShapeDtypeStruct((t, dm), F32),
        compiler_params=_cparams(("parallel", "arbitrary")),
    )(wstart, wextent, out, slot_t, h1, lng3, lnb3)


def kernel(x, ln_in_g, ln_in_b, w_in, ret_theta, ssm_lambda_re, ssm_lambda_im, ssm_log_step, ssm_b_re, ssm_b_im,
           ssm_c_re, ssm_c_im, ssm_d, ssm_w_glu, ssm_b_glu, attn_sink, w_out, ln1_g, ln1_b, router_w,
           exp_w_gate, exp_w_up, exp_w_down, ln2_g, ln2_b):
    bsz, seq, dm = x.shape
    depth = w_in.shape[0]
    ne = router_w.shape[-1]
    cap = EC_FACTOR * seq // ne
    t = bsz * seq
    w_in_bf = w_in.astype(BF16)
    wt_bf = jnp.swapaxes(jnp.concatenate([w_in_bf[:, :, COL_SU:COL_SU + SSM_WIDTH],
                                          w_in_bf[:, :, COL_RK:COL_RK + RET_WIDTH]], axis=2), 1, 2)
    w_out_bf = w_out.astype(BF16)
    w_glu_bf = ssm_w_glu.astype(BF16)
    lg = jax.nn.log_sigmoid(ret_theta.astype(F32))
    sink = attn_sink.astype(F32)
    ssm_par = _ssm_params(ssm_lambda_re, ssm_lambda_im, ssm_log_step, ssm_b_re, ssm_b_im, ssm_c_re, ssm_c_im)
    vec3 = lambda a: a.astype(F32).reshape(depth, 1, -1)
    d3, bglu3, ln1g3, ln1b3, ln2g3, ln2b3 = map(vec3, (ssm_d, ssm_b_glu, ln1_g, ln1_b, ln2_g, ln2_b))
    rwt = jnp.swapaxes(router_w.astype(F32), 1, 2)
    rwt_hi = rwt.astype(BF16)
    rwt = jnp.concatenate([rwt_hi, (rwt - rwt_hi.astype(F32)).astype(BF16)], axis=1)
    bias = _attention_bias(seq)

    h = x
    for l in range(depth):
        if l == 0:
            h, proj, ut, kt = _inproj(h, w_in_bf, wt_bf, l, ln=(ln_in_g, ln_in_b))
        else:
            proj, ut, kt = _inproj(h, w_in_bf, wt_bf, l)
        h2d = h.reshape(t, dm)
        proj = proj.reshape(t, D_IN)
        yt = _ssm_conv(ut, ssm_par, l, bsz)
        sf, sb = _ret_states(proj, kt, lg, l, bsz, seq)
        h1, afft = _mixer(proj, sf, sb, yt, h2d, bias, lg, sink, w_out_bf, w_glu_bf, d3, bglu3, ln1g3, ln1b3,
                          rwt, l, bsz, seq)
        slot, wstart, wextent = _route(afft, cap)
        xs = _dispatch(h1, afft, slot, wstart, wextent, cap)
        out = _ffn(xs, exp_w_gate, exp_w_up, exp_w_down, l, cap)
        h = _combine(out, slot, wstart, wextent, h1, ln2g3, ln2b3, l, seq).reshape(bsz, seq, dm)
    return h
```

```python
import functools
import math

import jax
import jax.numpy as jnp
from jax import lax
from jax.experimental import pallas as pl
from jax.experimental.pallas import tpu as pltpu

F32 = jnp.float32
BF16 = jnp.bfloat16

RET_HEADS = 4
RET_DK = 64
RET_CHUNK = 128
RET_WIDTH = RET_HEADS * RET_DK
SSM_CPG = 16
SSM_GROUPS = 16
SSM_STATE = 64
SSM_WIDTH = SSM_CPG * SSM_GROUPS
ATT_HEADS = 8
ATT_KV_HEADS = 2
ATT_GQ = ATT_HEADS // ATT_KV_HEADS
ATT_HEAD_DIM = 64
ATT_BLOCK = 128
ATT_WINDOW = 128
EC_FACTOR = 2
DEPTH = 2
DEEPNORM_ALPHA = (2.0 * DEPTH) ** 0.25
LN_EPS = 1e-5
NEG_INF = -1e30

COL_RQ, COL_RK, COL_RV, COL_RG, COL_SU, COL_AQ, COL_AK, COL_AV = 0, 256, 512, 768, 1024, 1280, 1792, 1920
D_IN = 2048

CHUNK = 128
SUBLANES = 8
SLOT_ALIGN = 16
WIN = CHUNK + SLOT_ALIGN
SMALL_WIN = 3 * SLOT_ALIGN
DISPATCH_STEP_CHUNKS = 8
COMBINE_STEP_CHUNKS = 4
MIXER_STEP_CHUNKS = 2
GATE_LANES = 128
RET_STEP_CHUNKS = 8
VMEM_LIMIT = 56 * 1024 * 1024
ROUTE_REFINE_STEPS = 10


def _cparams(sem):
    return pltpu.CompilerParams(dimension_semantics=sem, vmem_limit_bytes=VMEM_LIMIT)


def _layer_norm(x, g, b):
    mu = jnp.mean(x, axis=-1, keepdims=True)
    xc = x - mu
    var = jnp.mean(xc * xc, axis=-1, keepdims=True)
    return xc * lax.rsqrt(var + LN_EPS) * g + b


def _dot(a, b):
    return jnp.dot(a, b, preferred_element_type=F32)


def _dot_hp(a, b):
    return jnp.dot(a, b, precision=lax.Precision.HIGHEST, preferred_element_type=F32)


def _dot_nt(a, b):
    return lax.dot_general(a, b, (((1,), (1,)), ((), ())), preferred_element_type=F32)


def _cmul(ar, ai, br, bi):
    return ar * br - ai * bi, ar * bi + ai * br


def _cpow(br, bi, expo, nbits):
    shape = (br.shape[0], expo.shape[1])
    rr = jnp.ones(shape, F32)
    ri = jnp.zeros(shape, F32)
    for j in range(nbits):
        bit = jnp.broadcast_to(((expo >> j) & 1) == 1, shape)
        nr, ni = _cmul(rr, ri, br, bi)
        rr = jnp.where(bit, nr, rr)
        ri = jnp.where(bit, ni, ri)
        if j + 1 < nbits:
            br, bi = _cmul(br, bi, br, bi)
    return rr, ri


def _inproj_kernel(*refs, apply_ln):
    if apply_ln:
        x_ref, g_ref, b_ref, w_ref, wt_ref, h_ref, proj_ref, ut_ref, kt_ref, rv_ref = refs
    else:
        x_ref, w_ref, wt_ref, proj_ref, ut_ref, kt_ref, rv_ref = refs
    nb, tl, d = x_ref.shape
    h = x_ref[...].reshape(nb * tl, d)
    if apply_ln:
        h = _layer_norm(h, g_ref[...], b_ref[...])
        h_ref[...] = h.reshape(nb, tl, d)
    hb = h.astype(BF16)
    proj = _dot(hb, w_ref[0])
    proj_ref[...] = proj.reshape(nb, tl, -1)
    rv_ref[...] = proj[:, COL_RV:COL_RV + RET_WIDTH].astype(BF16).reshape(nb, tl, RET_WIDTH)
    tt = _dot_nt(wt_ref[0], hb)
    k = tl // CHUNK
    for j in range(k):
        for b in range(nb):
            ut_ref[:, j * nb + b, :] = tt[0:SSM_WIDTH, (b * k + j) * CHUNK:(b * k + j + 1) * CHUNK]
    for b in range(nb):
        kt_ref[b] = tt[SSM_WIDTH:SSM_WIDTH + RET_WIDTH, b * tl:(b + 1) * tl].astype(BF16)


def _inproj(x3, w_bf, wt_bf, l, ln=None):
    bsz, seq, d = x3.shape
    n = w_bf.shape[2]
    k = SUBLANES // bsz
    tl = k * CHUNK
    nc = seq // CHUNK
    xspec = pl.BlockSpec((bsz, tl, d), lambda i: (0, i, 0))
    wspec = pl.BlockSpec((1, d, n), lambda i: (l, 0, 0))
    wtspec = pl.BlockSpec((1,) + wt_bf.shape[1:], lambda i: (l, 0, 0))
    out_specs = [pl.BlockSpec((bsz, tl, n), lambda i: (0, i, 0)),
                 pl.BlockSpec((SSM_WIDTH, k * bsz, CHUNK), lambda i: (0, i, 0)),
                 pl.BlockSpec((bsz, RET_WIDTH, tl), lambda i: (0, 0, i)),
                 pl.BlockSpec((bsz, tl, RET_WIDTH), lambda i: (0, i, 0))]
    out_shape = [jax.ShapeDtypeStruct((bsz, seq, n), F32),
                 jax.ShapeDtypeStruct((SSM_WIDTH, nc * bsz, CHUNK), F32),
                 jax.ShapeDtypeStruct((bsz, RET_WIDTH, seq), BF16),
                 jax.ShapeDtypeStruct((bsz, seq, RET_WIDTH), BF16)]
    if ln is None:
        return pl.pallas_call(
            functools.partial(_inproj_kernel, apply_ln=False), grid=(seq // tl,),
            in_specs=[xspec, wspec, wtspec], out_specs=out_specs, out_shape=out_shape,
            compiler_params=_cparams(("parallel",)),
        )(x3, w_bf, wt_bf)
    g, b = ln
    vec = pl.BlockSpec((1, d), lambda i: (0, 0))
    return pl.pallas_call(
        functools.partial(_inproj_kernel, apply_ln=True), grid=(seq // tl,),
        in_specs=[xspec, vec, vec, wspec, wtspec], out_specs=[xspec] + out_specs,
        out_shape=[jax.ShapeDtypeStruct(x3.shape, F32)] + out_shape,
        compiler_params=_cparams(("parallel",)),
    )(x3, g.reshape(1, d), b.reshape(1, d), w_bf, wt_bf)


def _ssm_params(lam_re, lam_im, log_step, b_re, b_im, c_re, c_im):
    lr, li = lam_re.astype(F32), lam_im.astype(F32)
    step = jnp.exp(log_step.astype(F32))[..., None]
    er = jnp.exp(lr * step)
    lbr, lbi = er * jnp.cos(li * step), er * jnp.sin(li * step)
    den = lr * lr + li * li
    fr = ((lbr - 1.0) * lr + lbi * li) / den
    fi = (lbi * lr - (lbr - 1.0) * li) / den
    br = jnp.swapaxes(b_re.astype(F32), -1, -2)[:, None]
    bi = jnp.swapaxes(b_im.astype(F32), -1, -2)[:, None]
    bbr = fr[..., None, :] * br - fi[..., None, :] * bi
    bbi = fr[..., None, :] * bi + fi[..., None, :] * br
    lcol = jnp.stack([lbr[:, 0], lbi[:, 0], lbr[:, 1], lbi[:, 1]], axis=-1)
    lrow = jnp.swapaxes(lcol, -1, -2)
    lrow = jnp.concatenate([lrow, jnp.zeros_like(lrow)], axis=2)
    bt = jnp.stack([bbr[:, 0], bbi[:, 0], bbr[:, 1], bbi[:, 1]], axis=2)
    cr, ci = c_re.astype(F32), c_im.astype(F32)
    c4 = jnp.stack([cr[:, 0], ci[:, 0], cr[:, 1], ci[:, 1]], axis=2)
    return lcol, lrow, bt, jnp.swapaxes(bt, -1, -2), c4, jnp.swapaxes(c4, -1, -2)


def _ssm_kernel(ut_ref, lcol_ref, lrow_ref, bt_ref, bcol_ref, c_ref, ct_ref, y_ref,
                w_scr, wst_scr, g_scr, vf_scr, vb_scr, acc_scr, s_scr, sp_scr, x_scr, *, nb, nchunks):
    tc, p, cpg = CHUNK, SSM_STATE, SSM_CPG
    lcol = lcol_ref[0, 0]
    lf = (lcol[:, 0:1], lcol[:, 1:2])
    lb = (lcol[:, 2:3], lcol[:, 3:4])

    m_row = lax.broadcasted_iota(jnp.int32, (1, tc), 1)
    pfr, pfi = _cpow(*lf, m_row, 7)
    pf1r, pf1i = _cmul(pfr, pfi, *lf)
    prr, pri = _cpow(*lf, tc - 1 - m_row, 7)
    pbr, pbi = _cpow(*lb, tc - m_row, 8)
    pqr, pqi = _cpow(*lb, m_row, 7)

    def rep_co(x):
        return jnp.broadcast_to(x[:, None, :], (cpg, cpg, p)).reshape(cpg * cpg, p)

    def rep_ci(x):
        return jnp.broadcast_to(x[None, :, :], (cpg, cpg, p)).reshape(cpg * cpg, p)

    cbfr, cbfi = _cmul(rep_co(c_ref[0, 0, 0]), rep_co(c_ref[0, 0, 1]), rep_ci(bt_ref[0, 0, 0]), rep_ci(bt_ref[0, 0, 1]))
    cbbr, cbbi = _cmul(rep_co(c_ref[0, 0, 2]), rep_co(c_ref[0, 0, 3]), rep_ci(bt_ref[0, 0, 2]), rep_ci(bt_ref[0, 0, 3]))
    lane = lax.broadcasted_iota(jnp.int32, (cpg * cpg, tc), 1)
    kb0 = jnp.sum(cbbr, axis=1, keepdims=True)
    vf_scr[...] = _dot_hp(cbfr, pfr) - _dot_hp(cbfi, pfi) + jnp.where(lane == 0, kb0, 0.0)
    vb_scr[...] = _dot_hp(cbbr, pbr) - _dot_hp(cbbi, pbi)

    for co in range(cpg):
        cols = slice(co * tc, (co + 1) * tc)
        for r, (ar, ai) in enumerate(((pf1r, pf1i), (pbr, pbi))):
            gr, gi = _cmul(ct_ref[0, 0, 2 * r][:, co:co + 1], ct_ref[0, 0, 2 * r + 1][:, co:co + 1], ar, ai)
            g_scr[2 * r * p:(2 * r + 1) * p, cols] = gr.astype(BF16)
            g_scr[(2 * r + 1) * p:(2 * r + 2) * p, cols] = (-gi).astype(BF16)
        for r, (ar, ai) in enumerate(((prr, pri), (pqr, pqi))):
            sr, si = _cmul(bcol_ref[0, 0, 2 * r][:, co:co + 1], bcol_ref[0, 0, 2 * r + 1][:, co:co + 1], ar, ai)
            wst_scr[co, 2 * r * p:(2 * r + 1) * p, :] = sr.astype(BF16)
            wst_scr[co, (2 * r + 1) * p:(2 * r + 2) * p, :] = si.astype(BF16)

    s_idx = lax.broadcasted_iota(jnp.int32, (tc, tc), 0)
    j_idx = lax.broadcasted_iota(jnp.int32, (tc, tc), 1)
    fwd_part = j_idx < tc - s_idx
    acc_scr[...] = jnp.zeros_like(acc_scr)
    s_scr[...] = jnp.zeros_like(s_scr)

    def per_channel(ci, carry):
        for co in range(cpg):
            rf = jnp.broadcast_to(vf_scr[pl.ds(co * cpg + ci, 1), :], (tc, tc))
            rb = jnp.broadcast_to(vb_scr[pl.ds(co * cpg + ci, 1), :], (tc, tc))
            m = pltpu.roll(jnp.where(fwd_part, rf, rb), 0, 1, stride=1, stride_axis=0)
            w_scr[ci, :, co * tc:(co + 1) * tc] = m.astype(BF16)
        u = ut_ref[ci].astype(BF16)
        acc_scr[...] += _dot(u, w_scr[ci])
        s_scr[...] += _dot_nt(u, wst_scr[ci])
        return carry

    lax.fori_loop(0, cpg, per_channel, 0)

    lrow = lrow_ref[0, 0]
    dec = []
    for r in range(2):
        dr, di = lrow[2 * r:2 * r + 1], lrow[2 * r + 1:2 * r + 2]
        for _ in range(7):
            dr, di = _cmul(dr, di, dr, di)
        dec.append((dr, di))
    for q in range(4):
        sp_scr[q] = s_scr[:, q * p:(q + 1) * p]
    xfr = xfi = xbr = xbi = jnp.zeros((nb, p), F32)
    for i in range(nchunks):
        rf = slice(i * nb, (i + 1) * nb)
        rb = slice((nchunks - 1 - i) * nb, (nchunks - i) * nb)
        x_scr[0, rf, :] = xfr
        x_scr[1, rf, :] = xfi
        x_scr[2, rb, :] = xbr
        x_scr[3, rb, :] = xbi
        xfr, xfi = _cmul(dec[0][0], dec[0][1], xfr, xfi)
        xbr, xbi = _cmul(dec[1][0], dec[1][1], xbr, xbi)
        xfr, xfi = xfr + sp_scr[0, rf, :], xfi + sp_scr[1, rf, :]
        xbr, xbi = xbr + sp_scr[2, rb, :], xbi + sp_scr[3, rb, :]
    xin = jnp.concatenate([x_scr[q] for q in range(4)], axis=1).astype(BF16)
    y = acc_scr[...] + _dot(xin, g_scr[...])
    for co in range(cpg):
        y_ref[co] = y[:, co * tc:(co + 1) * tc]


def _ssm_conv(ut, params, l, bsz):
    _, r, tc = ut.shape
    cpg, p, ng = SSM_CPG, SSM_STATE, SSM_GROUPS

    def pspec(a):
        return pl.BlockSpec((1, 1) + a.shape[2:], lambda g: (l, g) + (0,) * (a.ndim - 2))
    return pl.pallas_call(
        functools.partial(_ssm_kernel, nb=bsz, nchunks=r // bsz), grid=(ng,),
        in_specs=[pl.BlockSpec((cpg, r, tc), lambda g: (g, 0, 0))] + [pspec(a) for a in params],
        out_specs=pl.BlockSpec((cpg, r, tc), lambda g: (g, 0, 0)),
        out_shape=jax.ShapeDtypeStruct(ut.shape, F32),
        scratch_shapes=[pltpu.VMEM((cpg, tc, cpg * tc), BF16), pltpu.VMEM((cpg, 4 * p, tc), BF16),
                        pltpu.VMEM((4 * p, cpg * tc), BF16),
                        pltpu.VMEM((cpg * cpg, tc), F32), pltpu.VMEM((cpg * cpg, tc), F32),
                        pltpu.VMEM((r, cpg * tc), F32), pltpu.VMEM((r, 4 * p), F32),
                        pltpu.VMEM((4, r, p), F32), pltpu.VMEM((4, r, p), F32)],
        compiler_params=_cparams(("parallel",)),
    )(ut, *params)


def _retstate_kernel(lg_ref, ktf_ref, vf_ref, ktb_ref, vb_ref, sf_ref, sb_ref, accf, accb, *, l, cs):
    i = pl.program_id(1)

    @pl.when(i == 0)
    def _():
        accf[...] = jnp.zeros_like(accf)
        accb[...] = jnp.zeros_like(accb)

    ch = RET_CHUNK
    pos = lax.broadcasted_iota(jnp.int32, (1, ch), 1).astype(F32)
    one = jnp.ones((1, RET_DK), F32)
    wts = []
    for h in range(RET_HEADS):
        lgf = lg_ref[l, 0, h]
        lgb = lg_ref[l, 1, h]
        wts.append((jnp.exp(lgf * (ch - 1.0 - pos)) * RET_DK ** -0.5, jnp.exp(lgb * pos) * RET_DK ** -0.5,
                    jnp.exp(one * (lgf * ch)), jnp.exp(one * (lgb * ch))))
    for j in range(cs):
        jb = cs - 1 - j
        sf_ref[0, j] = accf[...]
        sb_ref[0, jb] = accb[...]
        for h in range(RET_HEADS):
            rows = slice(h * RET_DK, (h + 1) * RET_DK)
            wf, wb, decf, decb = wts[h]
            kf = (ktf_ref[0, rows, j * ch:(j + 1) * ch].astype(F32) * wf).astype(BF16)
            kb = (ktb_ref[0, rows, jb * ch:(jb + 1) * ch].astype(F32) * wb).astype(BF16)
            accf[rows, :] = decf * accf[rows, :] + _dot(kf, vf_ref[0, j * ch:(j + 1) * ch, rows])
            accb[rows, :] = decb * accb[rows, :] + _dot(kb, vb_ref[0, jb * ch:(jb + 1) * ch, rows])


def _ret_states(kt, rv, lg, l, bsz, seq):
    nc = seq // RET_CHUNK
    cs = min(RET_STEP_CHUNKS, nc)
    ns = nc // cs
    w = RET_WIDTH
    tl = cs * RET_CHUNK
    st = jax.ShapeDtypeStruct((bsz, nc, w, RET_DK), F32)
    return pl.pallas_call(
        functools.partial(_retstate_kernel, l=l, cs=cs), grid=(bsz, ns),
        in_specs=[pl.BlockSpec(memory_space=pltpu.SMEM),
                  pl.BlockSpec((1, w, tl), lambda b, i: (b, 0, i)),
                  pl.BlockSpec((1, tl, w), lambda b, i: (b, i, 0)),
                  pl.BlockSpec((1, w, tl), lambda b, i: (b, 0, ns - 1 - i)),
                  pl.BlockSpec((1, tl, w), lambda b, i: (b, ns - 1 - i, 0))],
        out_specs=[pl.BlockSpec((1, cs, w, RET_DK), lambda b, i: (b, i, 0, 0)),
                   pl.BlockSpec((1, cs, w, RET_DK), lambda b, i: (b, ns - 1 - i, 0, 0))],
        out_shape=[st, st],
        scratch_shapes=[pltpu.VMEM((w, RET_DK), F32), pltpu.VMEM((w, RET_DK), F32)],
        compiler_params=_cparams(("parallel", "arbitrary")),
    )(lg, kt, rv, kt, rv)


def _gelu_tanh(x):
    return 0.5 * x * (1.0 + jnp.tanh(math.sqrt(2.0 / math.pi) * (x + 0.044715 * (x * x * x))))


def _attention_bias(seq):
    blk = ATT_BLOCK
    s_idx = jnp.arange(3 * blk)[None, :]
    t_idx = jnp.arange(blk)[:, None]
    arel = jnp.abs(s_idx - blk - t_idx)
    band = arel <= ATT_WINDOW
    slopes = jnp.exp2(-8.0 * jnp.arange(1, ATT_HEADS + 1, dtype=F32) / ATT_HEADS)
    alibi = -slopes[:, None, None] * arel.astype(F32)[None]
    variants = []
    for prev_ok, next_ok in ((False, True), (True, True), (True, False)):
        ok = band & (prev_ok | (s_idx >= blk)) & (next_ok | (s_idx < 2 * blk))
        variants.append(jnp.where(ok[None], alibi, NEG_INF).reshape(ATT_KV_HEADS, ATT_GQ * blk, 3 * blk))
    return jnp.stack(variants)


def _mixer_kernel(lg_ref, sink_ref, proj_ref, kvp_ref, kvn_ref, sf_ref, sb_ref, yt_ref, h_ref, bias0_ref, bias1_ref,
                  wout_ref, wglu_ref, d_ref, bglu_ref, lng_ref, lnb_ref, rwt_ref,
                  h1_ref, afft_ref, mix_scr, sret_scr, satt_scr, pret_scr, patt_scr, *, l, nb):
    b = pl.program_id(1)
    cm = MIXER_STEP_CHUNKS
    bias_refs = (bias0_ref, bias1_ref)
    ch = RET_CHUNK
    dk = RET_DK
    blk = ATT_BLOCK
    hd = ATT_HEAD_DIM
    kvw = ATT_KV_HEADS * hd

    def kv_block(u, cols):
        if u < 0:
            return kvp_ref[:, cols]
        if u >= cm:
            return kvn_ref[:, cols]
        return proj_ref[u * ch:(u + 1) * ch, COL_AK + cols.start:COL_AK + cols.stop]

    vbs = {}
    for u in range(cm):
        r = slice(u * ch, (u + 1) * ch)
        for h in range(RET_HEADS):
            q = proj_ref[r, COL_RQ + h * dk:COL_RQ + (h + 1) * dk]
            k = proj_ref[r, COL_RK + h * dk:COL_RK + (h + 1) * dk] * dk ** -0.5
            sret_scr[u * RET_HEADS + h] = _dot_nt(q.astype(BF16), k.astype(BF16))
        for kvh in range(ATT_KV_HEADS):
            kc = slice(kvh * hd, (kvh + 1) * hd)
            vc = slice(kvw + kvh * hd, kvw + (kvh + 1) * hd)
            kb = jnp.concatenate([kv_block(u + d, kc) for d in (-1, 0, 1)], axis=0).astype(BF16)
            vbs[u, kvh] = jnp.concatenate([kv_block(u + d, vc) for d in (-1, 0, 1)], axis=0).astype(BF16)
            q4 = jnp.concatenate([proj_ref[r, COL_AQ + (kvh * ATT_GQ + gq) * hd:COL_AQ + (kvh * ATT_GQ + gq + 1) * hd]
                                  for gq in range(ATT_GQ)], axis=0) * hd ** -0.5
            satt_scr[u * ATT_KV_HEADS + kvh] = _dot_nt(q4.astype(BF16), kb)

    pos = lax.broadcasted_iota(jnp.int32, (ch, 1), 0).astype(F32)
    dist = lax.broadcasted_iota(jnp.int32, (ch, ch), 0) - lax.broadcasted_iota(jnp.int32, (ch, ch), 1)
    adist = jnp.abs(dist).astype(F32)
    for h in range(RET_HEADS):
        lgf = lg_ref[l, 0, h]
        lgb = lg_ref[l, 1, h]
        dmat = jnp.exp(jnp.where(dist >= 0, lgf, lgb) * adist)
        wqf = jnp.exp(lgf * (pos + 1.0))
        wqb = jnp.exp(lgb * (ch - pos))
        for u in range(cm):
            q = proj_ref[u * ch:(u + 1) * ch, COL_RQ + h * dk:COL_RQ + (h + 1) * dk]
            uh = u * RET_HEADS + h
            pret_scr[uh, :, 0:ch] = (sret_scr[uh] * dmat).astype(BF16)
            pret_scr[uh, :, ch:ch + dk] = (q * wqf).astype(BF16)
            pret_scr[uh, :, ch + dk:ch + 2 * dk] = (q * wqb).astype(BF16)
    for u in range(cm):
        for kvh in range(ATT_KV_HEADS):
            uk = u * ATT_KV_HEADS + kvh
            for gq in range(ATT_GQ):
                rows = slice(gq * blk, (gq + 1) * blk)
                sink = sink_ref[l, kvh * ATT_GQ + gq]
                sc = satt_scr[uk, rows, :] + bias_refs[u][0, kvh, rows, :]
                m = jnp.maximum(jnp.max(sc, axis=-1, keepdims=True), sink)
                p = jnp.exp(sc - m)
                denom = jnp.sum(p, axis=-1, keepdims=True) + jnp.exp(sink - m)
                patt_scr[uk, rows, :] = (p * (1.0 / denom)).astype(BF16)

    att_base = RET_WIDTH + SSM_WIDTH
    w_ssm = SSM_WIDTH
    for u in range(cm):
        r = slice(u * ch, (u + 1) * ch)
        for h in range(RET_HEADS):
            rows = slice(h * dk, (h + 1) * dk)
            v = proj_ref[r, COL_RV + h * dk:COL_RV + (h + 1) * dk]
            g = proj_ref[r, COL_RG + h * dk:COL_RG + (h + 1) * dk]
            rhs = jnp.concatenate([v, sf_ref[0, u, rows, :], sb_ref[0, u, rows, :]], axis=0).astype(BF16)
            o = _dot(pret_scr[u * RET_HEADS + h], rhs)
            mu = jnp.mean(o, axis=-1, keepdims=True)
            oc = o - mu
            var = jnp.mean(oc * oc, axis=-1, keepdims=True)
            mix_scr[r, h * dk:(h + 1) * dk] = (g * jax.nn.sigmoid(g)) * (oc * lax.rsqrt(var + LN_EPS))
        for kvh in range(ATT_KV_HEADS):
            o = _dot(patt_scr[u * ATT_KV_HEADS + kvh], vbs[u, kvh])
            for gq in range(ATT_GQ):
                hh = kvh * ATT_GQ + gq
                mix_scr[r, att_base + hh * hd:att_base + (hh + 1) * hd] = o[gq * blk:(gq + 1) * blk]

        yraw = yt_ref[:, pl.ds(u * nb + b, 1), :].reshape(w_ssm, ch).T
        y = yraw + d_ref[0] * proj_ref[r, COL_SU:COL_SU + w_ssm]
        y = _gelu_tanh(y)
        gate = jax.nn.sigmoid(_dot(y.astype(BF16), wglu_ref[0]) + bglu_ref[0])
        mix_scr[r, RET_WIDTH:RET_WIDTH + w_ssm] = y * gate

    mix = _dot(mix_scr[...].astype(BF16), wout_ref[0])
    h1 = _layer_norm(DEEPNORM_ALPHA * h_ref[...] + mix, lng_ref[0], lnb_ref[0])
    h1_ref[...] = h1

    ne = rwt_ref.shape[1] // 2
    h_hi = h1.astype(BF16)
    h_lo = (h1 - h_hi.astype(F32)).astype(BF16)
    both = _dot_nt(rwt_ref[0], h_hi)
    lt = both[0:ne] + both[ne:2 * ne] + _dot_nt(rwt_ref[0, 0:ne, :], h_lo)
    lt = lt - jnp.max(lt, axis=0, keepdims=True)
    et = jnp.exp(lt)
    aff = et / jnp.sum(et, axis=0, keepdims=True)
    for u in range(cm):
        afft_ref[u, 0] = aff[:, u * ch:(u + 1) * ch]


def _mixer(proj, sf, sb, yt, h2d, bias, lg, sink, wout_bf, wglu_bf, d3, bglu3, lng3, lnb3, rwt, l, bsz, seq):
    nc = seq // ATT_BLOCK
    cm = MIXER_STEP_CHUNKS
    ns = nc // cm
    assert nc % cm == 0 and cm * bsz == SUBLANES
    t, dm = h2d.shape
    ne = rwt.shape[1] // 2
    kvw = 2 * ATT_KV_HEADS * ATT_HEAD_DIM
    sw = RET_WIDTH
    kvcol = COL_AK // kvw
    blk = ATT_BLOCK

    def layer(a):
        return pl.BlockSpec((1,) + a.shape[1:], lambda i, b: (l,) + (0,) * (a.ndim - 1))
    smem = pl.BlockSpec(memory_space=pltpu.SMEM)
    in_specs = [
        smem, smem,
        pl.BlockSpec((cm * blk, D_IN), lambda i, b: (b * ns + i, 0)),
        pl.BlockSpec((blk, kvw), lambda i, b: (b * nc + jnp.maximum(cm * i - 1, 0), kvcol)),
        pl.BlockSpec((blk, kvw), lambda i, b: (b * nc + jnp.minimum(cm * i + cm, nc - 1), kvcol)),
        pl.BlockSpec((1, cm, sw, RET_DK), lambda i, b: (b, i, 0, 0)),
        pl.BlockSpec((1, cm, sw, RET_DK), lambda i, b: (b, i, 0, 0)),
        pl.BlockSpec((SSM_WIDTH, SUBLANES, CHUNK), lambda i, b: (0, i, 0)),
        pl.BlockSpec((cm * blk, dm), lambda i, b: (b * ns + i, 0)),
        pl.BlockSpec((1,) + bias.shape[1:], lambda i, b: (jnp.where(i == 0, 0, 1), 0, 0, 0)),
        pl.BlockSpec((1,) + bias.shape[1:], lambda i, b: (jnp.where(i == ns - 1, 2, 1), 0, 0, 0)),
        layer(wout_bf), layer(wglu_bf), layer(d3), layer(bglu3), layer(lng3), layer(lnb3), layer(rwt),
    ]
    out_specs = [
        pl.BlockSpec((cm * blk, dm), lambda i, b: (b * ns + i, 0)),
        pl.BlockSpec((cm, 1, ne, blk), lambda i, b: (i, b, 0, 0)),
    ]
    out_shape = [jax.ShapeDtypeStruct((t, dm), F32),
                 jax.ShapeDtypeStruct((nc, bsz, ne, blk), F32)]
    scratch = [pltpu.VMEM((cm * blk, dm), F32),
               pltpu.VMEM((cm * RET_HEADS, blk, blk), F32),
               pltpu.VMEM((cm * ATT_KV_HEADS, ATT_GQ * blk, 3 * blk), F32),
               pltpu.VMEM((cm * RET_HEADS, blk, blk + 2 * RET_DK), BF16),
               pltpu.VMEM((cm * ATT_KV_HEADS, ATT_GQ * blk, 3 * blk), BF16)]
    return pl.pallas_call(
        functools.partial(_mixer_kernel, l=l, nb=bsz), grid=(ns, bsz),
        in_specs=in_specs, out_specs=out_specs, out_shape=out_shape, scratch_shapes=scratch,
        compiler_params=_cparams(("parallel", "parallel")),
    )(lg, sink, proj, proj, proj, sf, sb, yt, h2d, bias, bias, wout_bf, wglu_bf, d3, bglu3, lng3, lnb3, rwt)


def _route_kernel(a_ref, slot_ref, start_ref, extent_ref, *, nc, ne, cap):
    blk = ATT_BLOCK
    a = a_ref[0]

    def chunk(x, c):
        return x[c * ne:(c + 1) * ne]

    def count(pred):
        x = pred.astype(jnp.int32)
        tot = chunk(x, 0)
        for c in range(1, nc):
            tot = tot + chunk(x, c)
        return jnp.sum(tot, axis=1, keepdims=True)

    def tile_e(v):
        return jnp.concatenate([v] * nc, axis=0)

    tau = jnp.zeros((ne, 1), jnp.int32)
    for bit in range(30, -1, -1):
        cand = tau | (1 << bit)
        tau = jnp.where(count(a >= tile_e(pltpu.bitcast(cand, F32))) >= cap, cand, tau)
    lo = pltpu.bitcast(tau, F32)
    hi = pltpu.bitcast(tau + 1, F32)
    for _ in range(ROUTE_REFINE_STEPS):
        mid = lo + (hi - lo) * 0.5
        ok = count(a >= tile_e(mid)) >= cap
        lo = jnp.where(ok, mid, lo)
        hi = jnp.where(ok, hi, mid)
    gt = a >= tile_e(hi)
    eq = (a >= tile_e(lo)) & jnp.logical_not(gt)
    need = cap - count(gt)

    tri = (lax.broadcasted_iota(jnp.int32, (blk, blk), 0) <= lax.broadcasted_iota(jnp.int32, (blk, blk), 1)).astype(BF16)

    def ranks(mask):
        mf = mask.astype(F32)
        incl = _dot(mf.astype(BF16), tri)
        run = jnp.zeros((ne, 1), F32)
        offs = []
        tots = []
        for c in range(nc):
            offs.append(run)
            tots.append(chunk(incl, c)[:, blk - 1:blk])
            run = run + tots[-1]
        return incl - mf, jnp.concatenate(offs, axis=0), jnp.concatenate(tots, axis=0)

    eq_local, eq_off, _ = ranks(eq)
    mask = gt | (eq & ((eq_local + eq_off) < tile_e(need).astype(F32)))
    local, off, tot = ranks(mask)
    off_i = off.astype(jnp.int32)
    start = (off_i // SLOT_ALIGN) * SLOT_ALIGN
    slot = (off_i - start) + local.astype(jnp.int32)
    slot_ref[0] = jnp.where(mask, slot, -1)
    start_ref[0] = jnp.broadcast_to(start, (nc * ne, blk))
    extent_ref[0] = jnp.broadcast_to((off_i - start) + tot.astype(jnp.int32), (nc * ne, blk))


def _route(afft, cap):
    nc, bsz, ne, blk = afft.shape
    a2 = afft.reshape(1, nc * bsz * ne, blk)
    spec = pl.BlockSpec(a2.shape, lambda s: (0, 0, 0))
    slot, start, extent = pl.pallas_call(
        functools.partial(_route_kernel, nc=nc, ne=bsz * ne, cap=cap), grid=(1,),
        in_specs=[spec], out_specs=[spec, spec, spec],
        out_shape=[jax.ShapeDtypeStruct(a2.shape, jnp.int32)] * 3,
        compiler_params=_cparams(("arbitrary",)),
    )(a2)
    flat = lambda v: v[0, :, 0]
    return slot.reshape(nc, bsz, ne, blk), flat(start), flat(extent)


def _max_extent(we_ref, base, n):
    mx = we_ref[base]
    for j in range(1, n):
        mx = jnp.maximum(mx, we_ref[base + j])
    return mx


def _dispatch_kernel(ws_ref, we_ref, h_ref, afft_ref, slot_ref, xs_ref, *, nb, ne, eg, cd):
    b = pl.program_id(0)
    g = pl.program_id(1)
    i = pl.program_id(2)
    dm = h_ref.shape[1]
    blk = ATT_BLOCK

    @pl.when(i == 0)
    def _():
        xs_ref[...] = jnp.zeros_like(xs_ref)

    def scatter_rows(win, k, base):
        hb = h_ref[k * blk:(k + 1) * blk, :].astype(BF16)
        a = afft_ref[k, 0]
        a_hi = a.astype(BF16)
        r1 = a - a_hi.astype(F32)
        a_mid = r1.astype(BF16)
        a_lo = (r1 - a_mid.astype(F32)).astype(BF16)
        a3 = jnp.concatenate([a_hi, a_mid, a_lo, jnp.zeros((GATE_LANES - 3 * ne, blk), BF16)], axis=0)
        riota = lax.broadcasted_iota(jnp.int32, (win, blk), 0)
        onehots = jnp.concatenate([(riota == slot_ref[k, 0, 0, j:j + 1, :]).astype(BF16) for j in range(eg)], axis=0)
        res = _dot(onehots, hb)
        resg = _dot_nt(onehots, a3)
        for j in range(eg):
            w = pl.multiple_of(ws_ref[base + j], SLOT_ALIGN)
            head = pl.ds(w, SLOT_ALIGN)
            tail = pl.ds(w + SLOT_ALIGN, win - SLOT_ALIGN)
            for cols, r in ((slice(0, dm), res), (slice(dm, dm + GATE_LANES), resg)):
                rj = r[j * win:(j + 1) * win]
                xs_ref[0, j, head, cols] = (xs_ref[0, j, head, cols].astype(F32) + rj[0:SLOT_ALIGN]).astype(BF16)
                xs_ref[0, j, tail, cols] = rj[SLOT_ALIGN:].astype(BF16)

    for k in range(cd):
        base = ((i * cd + k) * nb + b) * ne + g * eg
        small = _max_extent(we_ref, base, eg) <= SMALL_WIN
        pl.when(small)(functools.partial(scatter_rows, SMALL_WIN, k, base))
        pl.when(jnp.logical_not(small))(functools.partial(scatter_rows, WIN, k, base))


def _dispatch(h1, afft, slot, wstart, wextent, cap, eg=8):
    nc, bsz, ne, blk = slot.shape
    t, dm = h1.shape
    cp = cap + WIN
    cd = math.gcd(DISPATCH_STEP_CHUNKS, nc)
    ns = nc // cd
    return pl.pallas_call(
        functools.partial(_dispatch_kernel, nb=bsz, ne=ne, eg=eg, cd=cd),
        grid_spec=pltpu.PrefetchScalarGridSpec(
            num_scalar_prefetch=2, grid=(bsz, ne // eg, ns),
            in_specs=[pl.BlockSpec((cd * blk, dm), lambda b, g, i, ws, we: (b * ns + i, 0)),
                      pl.BlockSpec((cd, 1, ne, blk), lambda b, g, i, ws, we: (i, b, 0, 0)),
                      pl.BlockSpec((cd, 1, 1, eg, blk), lambda b, g, i, ws, we: (i, b, g, 0, 0))],
            out_specs=pl.BlockSpec((1, eg, cp, dm + GATE_LANES), lambda b, g, i, ws, we: (b, g, 0, 0))),
        out_shape=jax.ShapeDtypeStruct((bsz, ne, cp, dm + GATE_LANES), BF16),
        compiler_params=_cparams(("parallel", "parallel", "arbitrary")),
    )(wstart, wextent, h1, afft, slot.reshape(nc, bsz, ne // eg, eg, blk))


def _ffn_kernel(xs_ref, wg_ref, wu_ref, wd_ref, out_ref, hdn_scr, *, cap, nf, ne):
    e = pl.program_id(0)
    j = pl.program_id(1)
    bsz = xs_ref.shape[0]
    dm = wg_ref.shape[2]
    tf = wg_ref.shape[3]

    @pl.when(j < nf)
    def _():
        wg = wg_ref[0, 0].astype(BF16)
        wu = wu_ref[0, 0].astype(BF16)
        for b in range(bsz):
            x = xs_ref[b, 0, :, 0:dm]
            hg = _dot(x, wg)
            hu = _dot(x, wu)
            hdn_scr[j, b * cap:(b + 1) * cap, :] = ((hg * jax.nn.sigmoid(hg)) * hu).astype(BF16)

    @pl.when(j >= nf)
    def _():
        wd = wd_ref[0, 0].astype(BF16)
        cp = out_ref.shape[2]
        tn = out_ref.shape[3]
        lane = lax.broadcasted_iota(jnp.int32, (cap, GATE_LANES), 1)
        mine = ((lane % ne) == e) & (lane < 3 * ne)
        for b in range(bsz):
            rows = slice(b * cap, (b + 1) * cap)
            y = _dot(hdn_scr[0, rows, :], wd[0:tf])
            for f in range(1, nf):
                y = y + _dot(hdn_scr[f, rows, :], wd[f * tf:(f + 1) * tf])
            pieces = xs_ref[b, 0, :, dm:dm + GATE_LANES].astype(F32)
            gate = jnp.sum(jnp.where(mine, pieces, 0.0), axis=1, keepdims=True)
            out_ref[b, 0, 0:cap, :] = (y * gate).astype(BF16)
            out_ref[b, 0, cap:cp, :] = jnp.zeros((cp - cap, tn), BF16)


def _ffn(xs, wg, wu, wd, l, cap, tf=1024):
    bsz, ne, cp, dx = xs.shape
    dm = wg.shape[2]
    ff = wg.shape[-1]
    nf = ff // tf
    tn = dm // nf
    return pl.pallas_call(
        functools.partial(_ffn_kernel, cap=cap, nf=nf, ne=ne), grid=(ne, 2 * nf),
        in_specs=[pl.BlockSpec((bsz, 1, cap, dx), lambda e, j: (0, e, 0, 0)),
                  pl.BlockSpec((1, 1, dm, tf), lambda e, j: (l, e, 0, jnp.minimum(j, nf - 1))),
                  pl.BlockSpec((1, 1, dm, tf), lambda e, j: (l, e, 0, jnp.minimum(j, nf - 1))),
                  pl.BlockSpec((1, 1, ff, tn), lambda e, j: (l, e, 0, jnp.maximum(j - nf, 0)))],
        out_specs=pl.BlockSpec((bsz, 1, cp, tn), lambda e, j: (0, e, 0, jnp.maximum(j - nf, 0))),
        out_shape=jax.ShapeDtypeStruct((bsz, ne, cp, dm), BF16),
        scratch_shapes=[pltpu.VMEM((nf, bsz * cap, tf), BF16)],
        compiler_params=_cparams(("parallel", "arbitrary")),
    )(xs, wg, wu, wd)


def _combine_kernel(ws_ref, we_ref, out_ref, slott_ref, h1_ref, lng_ref, lnb_ref, h2_ref, rhs_scr, *, nb, ne, cc):
    b = pl.program_id(0)
    i = pl.program_id(1)
    blk = ATT_BLOCK

    def finish(k, tot):
        rows = slice(k * blk, (k + 1) * blk)
        h2_ref[rows, :] = _layer_norm(DEEPNORM_ALPHA * h1_ref[rows, :] + tot, lng_ref[0], lnb_ref[0])

    def gather_packed(k, base):
        kk = ne * SMALL_WIN
        slot_t = slott_ref[k, 0].astype(F32).astype(BF16)
        expand = (lax.broadcasted_iota(jnp.int32, (ne, kk), 1) // SMALL_WIN
                  == lax.broadcasted_iota(jnp.int32, (ne, kk), 0)).astype(BF16)
        spread = _dot(slot_t, expand)
        row = (lax.broadcasted_iota(jnp.int32, (blk, kk), 1) % SMALL_WIN).astype(F32)
        onehot = (spread == row).astype(BF16)
        for e in range(ne):
            w = pl.multiple_of(ws_ref[base + e], SLOT_ALIGN)
            rhs_scr[e * SMALL_WIN:(e + 1) * SMALL_WIN, :] = out_ref[0, e, pl.ds(w, SMALL_WIN), :]
        finish(k, _dot(onehot, rhs_scr[...]))

    def gather_per_expert(k, base):
        slot_t = slott_ref[k, 0]
        liota = lax.broadcasted_iota(jnp.int32, (blk, WIN), 1)
        tot = None
        for e in range(ne):
            w = pl.multiple_of(ws_ref[base + e], SLOT_ALIGN)
            onehot = (liota == slot_t[:, e:e + 1]).astype(BF16)
            y = _dot(onehot, out_ref[0, e, pl.ds(w, WIN), :])
            tot = y if tot is None else tot + y
        finish(k, tot)

    for k in range(cc):
        base = ((i * cc + k) * nb + b) * ne
        small = _max_extent(we_ref, base, ne) <= SMALL_WIN
        pl.when(small)(functools.partial(gather_packed, k, base))
        pl.when(jnp.logical_not(small))(functools.partial(gather_per_expert, k, base))


def _combine(out, slot, wstart, wextent, h1, lng3, lnb3, l, seq):
    bsz, ne, cp, dm = out.shape
    nc = seq // ATT_BLOCK
    blk = ATT_BLOCK
    t = h1.shape[0]
    slot_t = jnp.swapaxes(slot, 2, 3)
    cc = math.gcd(COMBINE_STEP_CHUNKS, nc)
    ns = nc // cc
    vec = pl.BlockSpec((1, 1, dm), lambda b, i, ws, we: (l, 0, 0))
    return pl.pallas_call(
        functools.partial(_combine_kernel, nb=bsz, ne=ne, cc=cc),
        grid_spec=pltpu.PrefetchScalarGridSpec(
            num_scalar_prefetch=2, grid=(bsz, ns),
            in_specs=[pl.BlockSpec((1, ne, cp, dm), lambda b, i, ws, we: (b, 0, 0, 0)),
                      pl.BlockSpec((cc, 1, blk, ne), lambda b, i, ws, we: (i, b, 0, 0)),
                      pl.BlockSpec((cc * blk, dm), lambda b, i, ws, we: (b * ns + i, 0)),
                      vec, vec],
            out_specs=pl.BlockSpec((cc * blk, dm), lambda b, i, ws, we: (b * ns + i, 0)),
            scratch_shapes=[pltpu.VMEM((ne * SMALL_WIN, dm), BF16)]),
        out_shape=jax.ShapeDtypeStruct((t, dm), F32),
        compiler_params=_cparams(("parallel", "arbitrary")),
    )(wstart, wextent, out, slot_t, h1, lng3, lnb3)


def kernel(x, ln_in_g, ln_in_b, w_in, ret_theta, ssm_lambda_re, ssm_lambda_im, ssm_log_step, ssm_b_re, ssm_b_im,
           ssm_c_re, ssm_c_im, ssm_d, ssm_w_glu, ssm_b_glu, attn_sink, w_out, ln1_g, ln1_b, router_w,
           exp_w_gate, exp_w_up, exp_w_down, ln2_g, ln2_b):
    bsz, seq, dm = x.shape
    depth = w_in.shape[0]
    ne = router_w.shape[-1]
    cap = EC_FACTOR * seq // ne
    t = bsz * seq
    w_in_bf = w_in.astype(BF16)
    wt_bf = jnp.swapaxes(jnp.concatenate([w_in_bf[:, :, COL_SU:COL_SU + SSM_WIDTH],
                                          w_in_bf[:, :, COL_RK:COL_RK + RET_WIDTH]], axis=2), 1, 2)
    w_out_bf = w_out.astype(BF16)
    w_glu_bf = ssm_w_glu.astype(BF16)
    lg = jax.nn.log_sigmoid(ret_theta.astype(F32))
    sink = attn_sink.astype(F32)
    ssm_par = _ssm_params(ssm_lambda_re, ssm_lambda_im, ssm_log_step, ssm_b_re, ssm_b_im, ssm_c_re, ssm_c_im)
    vec3 = lambda a: a.astype(F32).reshape(depth, 1, -1)
    d3, bglu3, ln1g3, ln1b3, ln2g3, ln2b3 = map(vec3, (ssm_d, ssm_b_glu, ln1_g, ln1_b, ln2_g, ln2_b))
    rwt = jnp.swapaxes(router_w.astype(F32), 1, 2)
    rwt_hi = rwt.astype(BF16)
    rwt = jnp.concatenate([rwt_hi, (rwt - rwt_hi.astype(F32)).astype(BF16)], axis=1)
    bias = _attention_bias(seq)

    h = x
    for l in range(depth):
        if l == 0:
            h, proj, ut, kt, rv = _inproj(h, w_in_bf, wt_bf, l, ln=(ln_in_g, ln_in_b))
        else:
            proj, ut, kt, rv = _inproj(h, w_in_bf, wt_bf, l)
        h2d = h.reshape(t, dm)
        proj = proj.reshape(t, D_IN)
        yt = _ssm_conv(ut, ssm_par, l, bsz)
        sf, sb = _ret_states(kt, rv, lg, l, bsz, seq)
        h1, afft = _mixer(proj, sf, sb, yt, h2d, bias, lg, sink, w_out_bf, w_glu_bf, d3, bglu3, ln1g3, ln1b3,
                          rwt, l, bsz, seq)
        slot, wstart, wextent = _route(afft, cap)
        xs = _dispatch(h1, afft, slot, wstart, wextent, cap)
        out = _ffn(xs, exp_w_gate, exp_w_up, exp_w_down, l, cap)
        h = _combine(out, slot, wstart, wextent, h1, ln2g3, ln2b3, l, seq).reshape(bsz, seq, dm)
    return h
```

```python
import functools
import math

import jax
import jax.numpy as jnp
from jax import lax
from jax.experimental import pallas as pl
from jax.experimental.pallas import tpu as pltpu

F32 = jnp.float32
BF16 = jnp.bfloat16

RET_HEADS = 4
RET_DK = 64
RET_CHUNK = 128
RET_WIDTH = RET_HEADS * RET_DK
SSM_CPG = 16
SSM_GROUPS = 16
SSM_STATE = 64
SSM_WIDTH = SSM_CPG * SSM_GROUPS
ATT_HEADS = 8
ATT_KV_HEADS = 2
ATT_GQ = ATT_HEADS // ATT_KV_HEADS
ATT_HEAD_DIM = 64
ATT_BLOCK = 128
ATT_WINDOW = 128
EC_FACTOR = 2
DEPTH = 2
DEEPNORM_ALPHA = (2.0 * DEPTH) ** 0.25
LN_EPS = 1e-5
NEG_INF = -1e30

COL_RQ, COL_RK, COL_RV, COL_RG, COL_SU, COL_AQ, COL_AK, COL_AV = 0, 256, 512, 768, 1024, 1280, 1792, 1920
D_IN = 2048

CHUNK = 128
SUBLANES = 8
SLOT_ALIGN = 16
WIN = CHUNK + SLOT_ALIGN
SMALL_WIN = 3 * SLOT_ALIGN
DISPATCH_STEP_CHUNKS = 8
COMBINE_STEP_CHUNKS = 4
MIXER_STEP_CHUNKS = 2
GATE_LANES = 128
RET_STEP_CHUNKS = 8
VMEM_LIMIT = 56 * 1024 * 1024
ROUTE_REFINE_STEPS = 10


def _cparams(sem):
    return pltpu.CompilerParams(dimension_semantics=sem, vmem_limit_bytes=VMEM_LIMIT)


def _layer_norm(x, g, b):
    mu = jnp.mean(x, axis=-1, keepdims=True)
    xc = x - mu
    var = jnp.mean(xc * xc, axis=-1, keepdims=True)
    return xc * lax.rsqrt(var + LN_EPS) * g + b


def _dot(a, b):
    return jnp.dot(a, b, preferred_element_type=F32)


def _dot_hp(a, b):
    return jnp.dot(a, b, precision=lax.Precision.HIGHEST, preferred_element_type=F32)


def _dot_nt(a, b):
    return lax.dot_general(a, b, (((1,), (1,)), ((), ())), preferred_element_type=F32)


def _cmul(ar, ai, br, bi):
    return ar * br - ai * bi, ar * bi + ai * br


def _cpow(br, bi, expo, nbits):
    shape = (br.shape[0], expo.shape[1])
    rr = jnp.ones(shape, F32)
    ri = jnp.zeros(shape, F32)
    for j in range(nbits):
        bit = jnp.broadcast_to(((expo >> j) & 1) == 1, shape)
        nr, ni = _cmul(rr, ri, br, bi)
        rr = jnp.where(bit, nr, rr)
        ri = jnp.where(bit, ni, ri)
        if j + 1 < nbits:
            br, bi = _cmul(br, bi, br, bi)
    return rr, ri


def _inproj_kernel(*refs, apply_ln):
    if apply_ln:
        x_ref, g_ref, b_ref, w_ref, wt_ref, h_ref, proj_ref, ut_ref, kt_ref, rv_ref = refs
    else:
        x_ref, w_ref, wt_ref, proj_ref, ut_ref, kt_ref, rv_ref = refs
    nb, tl, d = x_ref.shape
    h = x_ref[...].reshape(nb * tl, d)
    if apply_ln:
        h = _layer_norm(h, g_ref[...], b_ref[...])
        h_ref[...] = h.reshape(nb, tl, d)
    hb = h.astype(BF16)
    proj = _dot(hb, w_ref[0])
    proj_ref[...] = proj.reshape(nb, tl, -1)
    rv_ref[...] = proj[:, COL_RV:COL_RV + RET_WIDTH].astype(BF16).reshape(nb, tl, RET_WIDTH)
    tt = _dot_nt(wt_ref[0], hb)
    k = tl // CHUNK
    for j in range(k):
        for b in range(nb):
            ut_ref[:, j * nb + b, :] = tt[0:SSM_WIDTH, (b * k + j) * CHUNK:(b * k + j + 1) * CHUNK]
    for b in range(nb):
        kt_ref[b] = tt[SSM_WIDTH:SSM_WIDTH + RET_WIDTH, b * tl:(b + 1) * tl].astype(BF16)


def _inproj(x3, w_bf, wt_bf, l, ln=None):
    bsz, seq, d = x3.shape
    n = w_bf.shape[2]
    k = SUBLANES // bsz
    tl = k * CHUNK
    nc = seq // CHUNK
    xspec = pl.BlockSpec((bsz, tl, d), lambda i: (0, i, 0))
    wspec = pl.BlockSpec((1, d, n), lambda i: (l, 0, 0))
    wtspec = pl.BlockSpec((1,) + wt_bf.shape[1:], lambda i: (l, 0, 0))
    out_specs = [pl.BlockSpec((bsz, tl, n), lambda i: (0, i, 0)),
                 pl.BlockSpec((SSM_WIDTH, k * bsz, CHUNK), lambda i: (0, i, 0)),
                 pl.BlockSpec((bsz, RET_WIDTH, tl), lambda i: (0, 0, i)),
                 pl.BlockSpec((bsz, tl, RET_WIDTH), lambda i: (0, i, 0))]
    out_shape = [jax.ShapeDtypeStruct((bsz, seq, n), F32),
                 jax.ShapeDtypeStruct((SSM_WIDTH, nc * bsz, CHUNK), F32),
                 jax.ShapeDtypeStruct((bsz, RET_WIDTH, seq), BF16),
                 jax.ShapeDtypeStruct((bsz, seq, RET_WIDTH), BF16)]
    if ln is None:
        return pl.pallas_call(
            functools.partial(_inproj_kernel, apply_ln=False), grid=(seq // tl,),
            in_specs=[xspec, wspec, wtspec], out_specs=out_specs, out_shape=out_shape,
            compiler_params=_cparams(("parallel",)),
        )(x3, w_bf, wt_bf)
    g, b = ln
    vec = pl.BlockSpec((1, d), lambda i: (0, 0))
    return pl.pallas_call(
        functools.partial(_inproj_kernel, apply_ln=True), grid=(seq // tl,),
        in_specs=[xspec, vec, vec, wspec, wtspec], out_specs=[xspec] + out_specs,
        out_shape=[jax.ShapeDtypeStruct(x3.shape, F32)] + out_shape,
        compiler_params=_cparams(("parallel",)),
    )(x3, g.reshape(1, d), b.reshape(1, d), w_bf, wt_bf)


def _ssm_params(lam_re, lam_im, log_step, b_re, b_im, c_re, c_im):
    lr, li = lam_re.astype(F32), lam_im.astype(F32)
    step = jnp.exp(log_step.astype(F32))[..., None]
    er = jnp.exp(lr * step)
    lbr, lbi = er * jnp.cos(li * step), er * jnp.sin(li * step)
    den = lr * lr + li * li
    fr = ((lbr - 1.0) * lr + lbi * li) / den
    fi = (lbi * lr - (lbr - 1.0) * li) / den
    br = jnp.swapaxes(b_re.astype(F32), -1, -2)[:, None]
    bi = jnp.swapaxes(b_im.astype(F32), -1, -2)[:, None]
    bbr = fr[..., None, :] * br - fi[..., None, :] * bi
    bbi = fr[..., None, :] * bi + fi[..., None, :] * br
    lcol = jnp.stack([lbr[:, 0], lbi[:, 0], lbr[:, 1], lbi[:, 1]], axis=-1)
    lrow = jnp.swapaxes(lcol, -1, -2)
    lrow = jnp.concatenate([lrow, jnp.zeros_like(lrow)], axis=2)
    bt = jnp.stack([bbr[:, 0], bbi[:, 0], bbr[:, 1], bbi[:, 1]], axis=2)
    cr, ci = c_re.astype(F32), c_im.astype(F32)
    c4 = jnp.stack([cr[:, 0], ci[:, 0], cr[:, 1], ci[:, 1]], axis=2)
    return lcol, lrow, bt, jnp.swapaxes(bt, -1, -2), c4, jnp.swapaxes(c4, -1, -2)


def _ssm_kernel(ut_ref, lcol_ref, lrow_ref, bt_ref, bcol_ref, c_ref, ct_ref, y_ref,
                w_scr, wst_scr, g_scr, vf_scr, vb_scr, acc_scr, s_scr, sp_scr, x_scr, *, nb, nchunks):
    tc, p, cpg = CHUNK, SSM_STATE, SSM_CPG
    lcol = lcol_ref[0, 0]
    lf = (lcol[:, 0:1], lcol[:, 1:2])
    lb = (lcol[:, 2:3], lcol[:, 3:4])

    m_row = lax.broadcasted_iota(jnp.int32, (1, tc), 1)
    pfr, pfi = _cpow(*lf, m_row, 7)
    pf1r, pf1i = _cmul(pfr, pfi, *lf)
    prr, pri = _cpow(*lf, tc - 1 - m_row, 7)
    pbr, pbi = _cpow(*lb, tc - m_row, 8)
    pqr, pqi = _cpow(*lb, m_row, 7)

    def rep_co(x):
        return jnp.broadcast_to(x[:, None, :], (cpg, cpg, p)).reshape(cpg * cpg, p)

    def rep_ci(x):
        return jnp.broadcast_to(x[None, :, :], (cpg, cpg, p)).reshape(cpg * cpg, p)

    cbfr, cbfi = _cmul(rep_co(c_ref[0, 0, 0]), rep_co(c_ref[0, 0, 1]), rep_ci(bt_ref[0, 0, 0]), rep_ci(bt_ref[0, 0, 1]))
    cbbr, cbbi = _cmul(rep_co(c_ref[0, 0, 2]), rep_co(c_ref[0, 0, 3]), rep_ci(bt_ref[0, 0, 2]), rep_ci(bt_ref[0, 0, 3]))
    lane = lax.broadcasted_iota(jnp.int32, (cpg * cpg, tc), 1)
    kb0 = jnp.sum(cbbr, axis=1, keepdims=True)
    vf_scr[...] = _dot_hp(cbfr, pfr) - _dot_hp(cbfi, pfi) + jnp.where(lane == 0, kb0, 0.0)
    vb_scr[...] = _dot_hp(cbbr, pbr) - _dot_hp(cbbi, pbi)

    for co in range(cpg):
        cols = slice(co * tc, (co + 1) * tc)
        for r, (ar, ai) in enumerate(((pf1r, pf1i), (pbr, pbi))):
            gr, gi = _cmul(ct_ref[0, 0, 2 * r][:, co:co + 1], ct_ref[0, 0, 2 * r + 1][:, co:co + 1], ar, ai)
            g_scr[2 * r * p:(2 * r + 1) * p, cols] = gr.astype(BF16)
            g_scr[(2 * r + 1) * p:(2 * r + 2) * p, cols] = (-gi).astype(BF16)
        for r, (ar, ai) in enumerate(((prr, pri), (pqr, pqi))):
            sr, si = _cmul(bcol_ref[0, 0, 2 * r][:, co:co + 1], bcol_ref[0, 0, 2 * r + 1][:, co:co + 1], ar, ai)
            wst_scr[co, 2 * r * p:(2 * r + 1) * p, :] = sr.astype(BF16)
            wst_scr[co, (2 * r + 1) * p:(2 * r + 2) * p, :] = si.astype(BF16)

    s_idx = lax.broadcasted_iota(jnp.int32, (tc, tc), 0)
    j_idx = lax.broadcasted_iota(jnp.int32, (tc, tc), 1)
    fwd_part = j_idx < tc - s_idx
    acc_scr[...] = jnp.zeros_like(acc_scr)
    s_scr[...] = jnp.zeros_like(s_scr)

    def per_channel(ci, carry):
        for co in range(cpg):
            rf = jnp.broadcast_to(vf_scr[pl.ds(co * cpg + ci, 1), :], (tc, tc))
            rb = jnp.broadcast_to(vb_scr[pl.ds(co * cpg + ci, 1), :], (tc, tc))
            m = pltpu.roll(jnp.where(fwd_part, rf, rb), 0, 1, stride=1, stride_axis=0)
            w_scr[ci, :, co * tc:(co + 1) * tc] = m.astype(BF16)
        u = ut_ref[ci].astype(BF16)
        acc_scr[...] += _dot(u, w_scr[ci])
        s_scr[...] += _dot_nt(u, wst_scr[ci])
        return carry

    lax.fori_loop(0, cpg, per_channel, 0)

    lrow = lrow_ref[0, 0]
    dec = []
    for r in range(2):
        dr, di = lrow[2 * r:2 * r + 1], lrow[2 * r + 1:2 * r + 2]
        for _ in range(7):
            dr, di = _cmul(dr, di, dr, di)
        dec.append((dr, di))
    for q in range(4):
        sp_scr[q] = s_scr[:, q * p:(q + 1) * p]
    xfr = xfi = xbr = xbi = jnp.zeros((nb, p), F32)
    for i in range(nchunks):
        rf = slice(i * nb, (i + 1) * nb)
        rb = slice((nchunks - 1 - i) * nb, (nchunks - i) * nb)
        x_scr[0, rf, :] = xfr
        x_scr[1, rf, :] = xfi
        x_scr[2, rb, :] = xbr
        x_scr[3, rb, :] = xbi
        xfr, xfi = _cmul(dec[0][0], dec[0][1], xfr, xfi)
        xbr, xbi = _cmul(dec[1][0], dec[1][1], xbr, xbi)
        xfr, xfi = xfr + sp_scr[0, rf, :], xfi + sp_scr[1, rf, :]
        xbr, xbi = xbr + sp_scr[2, rb, :], xbi + sp_scr[3, rb, :]
    xin = jnp.concatenate([x_scr[q] for q in range(4)], axis=1).astype(BF16)
    y = acc_scr[...] + _dot(xin, g_scr[...])
    for co in range(cpg):
        y_ref[co] = y[:, co * tc:(co + 1) * tc]


def _ssm_conv(ut, params, l, bsz):
    _, r, tc = ut.shape
    cpg, p, ng = SSM_CPG, SSM_STATE, SSM_GROUPS

    def pspec(a):
        return pl.BlockSpec((1, 1) + a.shape[2:], lambda g: (l, g) + (0,) * (a.ndim - 2))
    return pl.pallas_call(
        functools.partial(_ssm_kernel, nb=bsz, nchunks=r // bsz), grid=(ng,),
        in_specs=[pl.BlockSpec((cpg, r, tc), lambda g: (g, 0, 0))] + [pspec(a) for a in params],
        out_specs=pl.BlockSpec((cpg, r, tc), lambda g: (g, 0, 0)),
        out_shape=jax.ShapeDtypeStruct(ut.shape, F32),
        scratch_shapes=[pltpu.VMEM((cpg, tc, cpg * tc), BF16), pltpu.VMEM((cpg, 4 * p, tc), BF16),
                        pltpu.VMEM((4 * p, cpg * tc), BF16),
                        pltpu.VMEM((cpg * cpg, tc), F32), pltpu.VMEM((cpg * cpg, tc), F32),
                        pltpu.VMEM((r, cpg * tc), F32), pltpu.VMEM((r, 4 * p), F32),
                        pltpu.VMEM((4, r, p), F32), pltpu.VMEM((4, r, p), F32)],
        compiler_params=_cparams(("parallel",)),
    )(ut, *params)


def _retstate_kernel(lg_ref, ktf_ref, vf_ref, ktb_ref, vb_ref, sf_ref, sb_ref, accf, accb, *, l, cs):
    i = pl.program_id(1)

    @pl.when(i == 0)
    def _():
        accf[...] = jnp.zeros_like(accf)
        accb[...] = jnp.zeros_like(accb)

    ch = RET_CHUNK
    pos = lax.broadcasted_iota(jnp.int32, (1, ch), 1).astype(F32)
    one = jnp.ones((1, RET_DK), F32)
    wts = []
    for h in range(RET_HEADS):
        lgf = lg_ref[l, 0, h]
        lgb = lg_ref[l, 1, h]
        wts.append((jnp.exp(lgf * (ch - 1.0 - pos)) * RET_DK ** -0.5, jnp.exp(lgb * pos) * RET_DK ** -0.5,
                    jnp.exp(one * (lgf * ch)), jnp.exp(one * (lgb * ch))))
    for j in range(cs):
        jb = cs - 1 - j
        sf_ref[0, j] = accf[...].astype(BF16)
        sb_ref[0, jb] = accb[...].astype(BF16)
        for h in range(RET_HEADS):
            rows = slice(h * RET_DK, (h + 1) * RET_DK)
            wf, wb, decf, decb = wts[h]
            kf = (ktf_ref[0, rows, j * ch:(j + 1) * ch].astype(F32) * wf).astype(BF16)
            kb = (ktb_ref[0, rows, jb * ch:(jb + 1) * ch].astype(F32) * wb).astype(BF16)
            accf[rows, :] = decf * accf[rows, :] + _dot(kf, vf_ref[0, j * ch:(j + 1) * ch, rows])
            accb[rows, :] = decb * accb[rows, :] + _dot(kb, vb_ref[0, jb * ch:(jb + 1) * ch, rows])


def _ret_states(kt, rv, lg, l, bsz, seq):
    nc = seq // RET_CHUNK
    cs = min(RET_STEP_CHUNKS, nc)
    ns = nc // cs
    w = RET_WIDTH
    tl = cs * RET_CHUNK
    st = jax.ShapeDtypeStruct((bsz, nc, w, RET_DK), BF16)
    return pl.pallas_call(
        functools.partial(_retstate_kernel, l=l, cs=cs), grid=(bsz, ns),
        in_specs=[pl.BlockSpec(memory_space=pltpu.SMEM),
                  pl.BlockSpec((1, w, tl), lambda b, i: (b, 0, i)),
                  pl.BlockSpec((1, tl, w), lambda b, i: (b, i, 0)),
                  pl.BlockSpec((1, w, tl), lambda b, i: (b, 0, ns - 1 - i)),
                  pl.BlockSpec((1, tl, w), lambda b, i: (b, ns - 1 - i, 0))],
        out_specs=[pl.BlockSpec((1, cs, w, RET_DK), lambda b, i: (b, i, 0, 0)),
                   pl.BlockSpec((1, cs, w, RET_DK), lambda b, i: (b, ns - 1 - i, 0, 0))],
        out_shape=[st, st],
        scratch_shapes=[pltpu.VMEM((w, RET_DK), F32), pltpu.VMEM((w, RET_DK), F32)],
        compiler_params=_cparams(("parallel", "arbitrary")),
    )(lg, kt, rv, kt, rv)


def _gelu_tanh(x):
    return 0.5 * x * (1.0 + jnp.tanh(math.sqrt(2.0 / math.pi) * (x + 0.044715 * (x * x * x))))


def _attention_bias(seq):
    blk = ATT_BLOCK
    s_idx = jnp.arange(3 * blk)[None, :]
    t_idx = jnp.arange(blk)[:, None]
    arel = jnp.abs(s_idx - blk - t_idx)
    band = arel <= ATT_WINDOW
    slopes = jnp.exp2(-8.0 * jnp.arange(1, ATT_HEADS + 1, dtype=F32) / ATT_HEADS)
    alibi = -slopes[:, None, None] * arel.astype(F32)[None]
    variants = []
    for prev_ok, next_ok in ((False, True), (True, True), (True, False)):
        ok = band & (prev_ok | (s_idx >= blk)) & (next_ok | (s_idx < 2 * blk))
        variants.append(jnp.where(ok[None], alibi, NEG_INF).reshape(ATT_KV_HEADS, ATT_GQ * blk, 3 * blk))
    return jnp.stack(variants)


def _mixer_kernel(lg_ref, sink_ref, proj_ref, kvp_ref, kvn_ref, sf_ref, sb_ref, yt_ref, h_ref, bias0_ref, bias1_ref,
                  wout_ref, wglu_ref, d_ref, bglu_ref, lng_ref, lnb_ref, rwt_ref,
                  h1_ref, h1b_ref, afft_ref, mix_scr, sret_scr, satt_scr, pret_scr, patt_scr, *, l, nb):
    b = pl.program_id(1)
    cm = MIXER_STEP_CHUNKS
    bias_refs = (bias0_ref, bias1_ref)
    ch = RET_CHUNK
    dk = RET_DK
    blk = ATT_BLOCK
    hd = ATT_HEAD_DIM
    kvw = ATT_KV_HEADS * hd

    def kv_block(u, cols):
        if u < 0:
            return kvp_ref[:, cols]
        if u >= cm:
            return kvn_ref[:, cols]
        return proj_ref[u * ch:(u + 1) * ch, COL_AK + cols.start:COL_AK + cols.stop]

    vbs = {}
    for u in range(cm):
        r = slice(u * ch, (u + 1) * ch)
        for h in range(RET_HEADS):
            q = proj_ref[r, COL_RQ + h * dk:COL_RQ + (h + 1) * dk]
            k = proj_ref[r, COL_RK + h * dk:COL_RK + (h + 1) * dk] * dk ** -0.5
            sret_scr[u * RET_HEADS + h] = _dot_nt(q.astype(BF16), k.astype(BF16))
        for kvh in range(ATT_KV_HEADS):
            kc = slice(kvh * hd, (kvh + 1) * hd)
            vc = slice(kvw + kvh * hd, kvw + (kvh + 1) * hd)
            kb = jnp.concatenate([kv_block(u + d, kc) for d in (-1, 0, 1)], axis=0).astype(BF16)
            vbs[u, kvh] = jnp.concatenate([kv_block(u + d, vc) for d in (-1, 0, 1)], axis=0).astype(BF16)
            q4 = jnp.concatenate([proj_ref[r, COL_AQ + (kvh * ATT_GQ + gq) * hd:COL_AQ + (kvh * ATT_GQ + gq + 1) * hd]
                                  for gq in range(ATT_GQ)], axis=0) * hd ** -0.5
            satt_scr[u * ATT_KV_HEADS + kvh] = _dot_nt(q4.astype(BF16), kb)

    pos = lax.broadcasted_iota(jnp.int32, (ch, 1), 0).astype(F32)
    dist = lax.broadcasted_iota(jnp.int32, (ch, ch), 0) - lax.broadcasted_iota(jnp.int32, (ch, ch), 1)
    adist = jnp.abs(dist).astype(F32)
    for h in range(RET_HEADS):
        lgf = lg_ref[l, 0, h]
        lgb = lg_ref[l, 1, h]
        dmat = jnp.exp(jnp.where(dist >= 0, lgf, lgb) * adist)
        wqf = jnp.exp(lgf * (pos + 1.0))
        wqb = jnp.exp(lgb * (ch - pos))
        for u in range(cm):
            q = proj_ref[u * ch:(u + 1) * ch, COL_RQ + h * dk:COL_RQ + (h + 1) * dk]
            uh = u * RET_HEADS + h
            pret_scr[uh, :, 0:ch] = (sret_scr[uh] * dmat).astype(BF16)
            pret_scr[uh, :, ch:ch + dk] = (q * wqf).astype(BF16)
            pret_scr[uh, :, ch + dk:ch + 2 * dk] = (q * wqb).astype(BF16)
    for u in range(cm):
        for kvh in range(ATT_KV_HEADS):
            uk = u * ATT_KV_HEADS + kvh
            for gq in range(ATT_GQ):
                rows = slice(gq * blk, (gq + 1) * blk)
                sink = sink_ref[l, kvh * ATT_GQ + gq]
                sc = satt_scr[uk, rows, :] + bias_refs[u][0, kvh, rows, :]
                m = jnp.maximum(jnp.max(sc, axis=-1, keepdims=True), sink)
                p = jnp.exp(sc - m)
                denom = jnp.sum(p, axis=-1, keepdims=True) + jnp.exp(sink - m)
                patt_scr[uk, rows, :] = (p * (1.0 / denom)).astype(BF16)

    att_base = RET_WIDTH + SSM_WIDTH
    w_ssm = SSM_WIDTH
    for u in range(cm):
        r = slice(u * ch, (u + 1) * ch)
        for h in range(RET_HEADS):
            rows = slice(h * dk, (h + 1) * dk)
            v = proj_ref[r, COL_RV + h * dk:COL_RV + (h + 1) * dk]
            g = proj_ref[r, COL_RG + h * dk:COL_RG + (h + 1) * dk]
            rhs = jnp.concatenate([v.astype(BF16), sf_ref[0, u, rows, :], sb_ref[0, u, rows, :]], axis=0)
            o = _dot(pret_scr[u * RET_HEADS + h], rhs)
            mu = jnp.mean(o, axis=-1, keepdims=True)
            oc = o - mu
            var = jnp.mean(oc * oc, axis=-1, keepdims=True)
            mix_scr[r, h * dk:(h + 1) * dk] = (g * jax.nn.sigmoid(g)) * (oc * lax.rsqrt(var + LN_EPS))
        for kvh in range(ATT_KV_HEADS):
            o = _dot(patt_scr[u * ATT_KV_HEADS + kvh], vbs[u, kvh])
            for gq in range(ATT_GQ):
                hh = kvh * ATT_GQ + gq
                mix_scr[r, att_base + hh * hd:att_base + (hh + 1) * hd] = o[gq * blk:(gq + 1) * blk]

        yraw = yt_ref[:, pl.ds(u * nb + b, 1), :].reshape(w_ssm, ch).T
        y = yraw + d_ref[0] * proj_ref[r, COL_SU:COL_SU + w_ssm]
        y = _gelu_tanh(y)
        gate = jax.nn.sigmoid(_dot(y.astype(BF16), wglu_ref[0]) + bglu_ref[0])
        mix_scr[r, RET_WIDTH:RET_WIDTH + w_ssm] = y * gate

    mix = _dot(mix_scr[...].astype(BF16), wout_ref[0])
    h1 = _layer_norm(DEEPNORM_ALPHA * h_ref[...] + mix, lng_ref[0], lnb_ref[0])
    h1_ref[...] = h1
    h1b_ref[...] = h1.astype(BF16)

    ne = rwt_ref.shape[1] // 2
    h_hi = h1.astype(BF16)
    h_lo = (h1 - h_hi.astype(F32)).astype(BF16)
    both = _dot_nt(rwt_ref[0], h_hi)
    lt = both[0:ne] + both[ne:2 * ne] + _dot_nt(rwt_ref[0, 0:ne, :], h_lo)
    lt = lt - jnp.max(lt, axis=0, keepdims=True)
    et = jnp.exp(lt)
    aff = et / jnp.sum(et, axis=0, keepdims=True)
    for u in range(cm):
        afft_ref[u, 0] = aff[:, u * ch:(u + 1) * ch]


def _mixer(proj, sf, sb, yt, h2d, bias, lg, sink, wout_bf, wglu_bf, d3, bglu3, lng3, lnb3, rwt, l, bsz, seq):
    nc = seq // ATT_BLOCK
    cm = MIXER_STEP_CHUNKS
    ns = nc // cm
    assert nc % cm == 0 and cm * bsz == SUBLANES
    t, dm = h2d.shape
    ne = rwt.shape[1] // 2
    kvw = 2 * ATT_KV_HEADS * ATT_HEAD_DIM
    sw = RET_WIDTH
    kvcol = COL_AK // kvw
    blk = ATT_BLOCK

    def layer(a):
        return pl.BlockSpec((1,) + a.shape[1:], lambda i, b: (l,) + (0,) * (a.ndim - 1))
    smem = pl.BlockSpec(memory_space=pltpu.SMEM)
    in_specs = [
        smem, smem,
        pl.BlockSpec((cm * blk, D_IN), lambda i, b: (b * ns + i, 0)),
        pl.BlockSpec((blk, kvw), lambda i, b: (b * nc + jnp.maximum(cm * i - 1, 0), kvcol)),
        pl.BlockSpec((blk, kvw), lambda i, b: (b * nc + jnp.minimum(cm * i + cm, nc - 1), kvcol)),
        pl.BlockSpec((1, cm, sw, RET_DK), lambda i, b: (b, i, 0, 0)),
        pl.BlockSpec((1, cm, sw, RET_DK), lambda i, b: (b, i, 0, 0)),
        pl.BlockSpec((SSM_WIDTH, SUBLANES, CHUNK), lambda i, b: (0, i, 0)),
        pl.BlockSpec((cm * blk, dm), lambda i, b: (b * ns + i, 0)),
        pl.BlockSpec((1,) + bias.shape[1:], lambda i, b: (jnp.where(i == 0, 0, 1), 0, 0, 0)),
        pl.BlockSpec((1,) + bias.shape[1:], lambda i, b: (jnp.where(i == ns - 1, 2, 1), 0, 0, 0)),
        layer(wout_bf), layer(wglu_bf), layer(d3), layer(bglu3), layer(lng3), layer(lnb3), layer(rwt),
    ]
    out_specs = [
        pl.BlockSpec((cm * blk, dm), lambda i, b: (b * ns + i, 0)),
        pl.BlockSpec((cm * blk, dm), lambda i, b: (b * ns + i, 0)),
        pl.BlockSpec((cm, 1, ne, blk), lambda i, b: (i, b, 0, 0)),
    ]
    out_shape = [jax.ShapeDtypeStruct((t, dm), F32),
                 jax.ShapeDtypeStruct((t, dm), BF16),
                 jax.ShapeDtypeStruct((nc, bsz, ne, blk), F32)]
    scratch = [pltpu.VMEM((cm * blk, dm), F32),
               pltpu.VMEM((cm * RET_HEADS, blk, blk), F32),
               pltpu.VMEM((cm * ATT_KV_HEADS, ATT_GQ * blk, 3 * blk), F32),
               pltpu.VMEM((cm * RET_HEADS, blk, blk + 2 * RET_DK), BF16),
               pltpu.VMEM((cm * ATT_KV_HEADS, ATT_GQ * blk, 3 * blk), BF16)]
    return pl.pallas_call(
        functools.partial(_mixer_kernel, l=l, nb=bsz), grid=(ns, bsz),
        in_specs=in_specs, out_specs=out_specs, out_shape=out_shape, scratch_shapes=scratch,
        compiler_params=_cparams(("parallel", "parallel")),
    )(lg, sink, proj, proj, proj, sf, sb, yt, h2d, bias, bias, wout_bf, wglu_bf, d3, bglu3, lng3, lnb3, rwt)


def _route_kernel(a_ref, slot_ref, start_ref, extent_ref, *, nc, ne, cap):
    blk = ATT_BLOCK
    a = a_ref[0]

    def chunk(x, c):
        return x[c * ne:(c + 1) * ne]

    def count(pred):
        x = pred.astype(jnp.int32)
        tot = chunk(x, 0)
        for c in range(1, nc):
            tot = tot + chunk(x, c)
        return jnp.sum(tot, axis=1, keepdims=True)

    def tile_e(v):
        return jnp.concatenate([v] * nc, axis=0)

    tau = jnp.zeros((ne, 1), jnp.int32)
    for bit in range(30, -1, -1):
        cand = tau | (1 << bit)
        tau = jnp.where(count(a >= tile_e(pltpu.bitcast(cand, F32))) >= cap, cand, tau)
    lo = pltpu.bitcast(tau, F32)
    hi = pltpu.bitcast(tau + 1, F32)
    for _ in range(ROUTE_REFINE_STEPS):
        mid = lo + (hi - lo) * 0.5
        ok = count(a >= tile_e(mid)) >= cap
        lo = jnp.where(ok, mid, lo)
        hi = jnp.where(ok, hi, mid)
    gt = a >= tile_e(hi)
    eq = (a >= tile_e(lo)) & jnp.logical_not(gt)
    need = cap - count(gt)

    tri = (lax.broadcasted_iota(jnp.int32, (blk, blk), 0) <= lax.broadcasted_iota(jnp.int32, (blk, blk), 1)).astype(BF16)

    def ranks(mask):
        mf = mask.astype(F32)
        incl = _dot(mf.astype(BF16), tri)
        run = jnp.zeros((ne, 1), F32)
        offs = []
        tots = []
        for c in range(nc):
            offs.append(run)
            tots.append(chunk(incl, c)[:, blk - 1:blk])
            run = run + tots[-1]
        return incl - mf, jnp.concatenate(offs, axis=0), jnp.concatenate(tots, axis=0)

    eq_local, eq_off, _ = ranks(eq)
    mask = gt | (eq & ((eq_local + eq_off) < tile_e(need).astype(F32)))
    local, off, tot = ranks(mask)
    off_i = off.astype(jnp.int32)
    start = (off_i // SLOT_ALIGN) * SLOT_ALIGN
    slot = (off_i - start) + local.astype(jnp.int32)
    slot_ref[0] = jnp.where(mask, slot, -1)
    start_ref[0] = jnp.broadcast_to(start, (nc * ne, blk))
    extent_ref[0] = jnp.broadcast_to((off_i - start) + tot.astype(jnp.int32), (nc * ne, blk))


def _route(afft, cap):
    nc, bsz, ne, blk = afft.shape
    a2 = afft.reshape(1, nc * bsz * ne, blk)
    spec = pl.BlockSpec(a2.shape, lambda s: (0, 0, 0))
    slot, start, extent = pl.pallas_call(
        functools.partial(_route_kernel, nc=nc, ne=bsz * ne, cap=cap), grid=(1,),
        in_specs=[spec], out_specs=[spec, spec, spec],
        out_shape=[jax.ShapeDtypeStruct(a2.shape, jnp.int32)] * 3,
        compiler_params=_cparams(("arbitrary",)),
    )(a2)
    flat = lambda v: v[0, :, 0]
    return slot.reshape(nc, bsz, ne, blk), flat(start), flat(extent)


def _max_extent(we_ref, base, n):
    mx = we_ref[base]
    for j in range(1, n):
        mx = jnp.maximum(mx, we_ref[base + j])
    return mx


def _dispatch_kernel(ws_ref, we_ref, h_ref, afft_ref, slot_ref, xs_ref, *, nb, ne, eg, cd):
    b = pl.program_id(0)
    g = pl.program_id(1)
    i = pl.program_id(2)
    dm = h_ref.shape[1]
    blk = ATT_BLOCK

    @pl.when(i == 0)
    def _():
        xs_ref[...] = jnp.zeros_like(xs_ref)

    def scatter_rows(win, k, base):
        hb = h_ref[k * blk:(k + 1) * blk, :]
        a = afft_ref[k, 0]
        a_hi = a.astype(BF16)
        r1 = a - a_hi.astype(F32)
        a_mid = r1.astype(BF16)
        a_lo = (r1 - a_mid.astype(F32)).astype(BF16)
        a3 = jnp.concatenate([a_hi, a_mid, a_lo, jnp.zeros((GATE_LANES - 3 * ne, blk), BF16)], axis=0)
        riota = lax.broadcasted_iota(jnp.int32, (win, blk), 0)
        onehots = jnp.concatenate([(riota == slot_ref[k, 0, 0, j:j + 1, :]).astype(BF16) for j in range(eg)], axis=0)
        res = _dot(onehots, hb)
        resg = _dot_nt(onehots, a3)
        for j in range(eg):
            w = pl.multiple_of(ws_ref[base + j], SLOT_ALIGN)
            head = pl.ds(w, SLOT_ALIGN)
            tail = pl.ds(w + SLOT_ALIGN, win - SLOT_ALIGN)
            for cols, r in ((slice(0, dm), res), (slice(dm, dm + GATE_LANES), resg)):
                rj = r[j * win:(j + 1) * win]
                xs_ref[0, j, head, cols] = (xs_ref[0, j, head, cols].astype(F32) + rj[0:SLOT_ALIGN]).astype(BF16)
                xs_ref[0, j, tail, cols] = rj[SLOT_ALIGN:].astype(BF16)

    for k in range(cd):
        base = ((i * cd + k) * nb + b) * ne + g * eg
        small = _max_extent(we_ref, base, eg) <= SMALL_WIN
        pl.when(small)(functools.partial(scatter_rows, SMALL_WIN, k, base))
        pl.when(jnp.logical_not(small))(functools.partial(scatter_rows, WIN, k, base))


def _dispatch(h1, afft, slot, wstart, wextent, cap, eg=8):
    nc, bsz, ne, blk = slot.shape
    t, dm = h1.shape
    cp = cap + WIN
    cd = math.gcd(DISPATCH_STEP_CHUNKS, nc)
    ns = nc // cd
    return pl.pallas_call(
        functools.partial(_dispatch_kernel, nb=bsz, ne=ne, eg=eg, cd=cd),
        grid_spec=pltpu.PrefetchScalarGridSpec(
            num_scalar_prefetch=2, grid=(bsz, ne // eg, ns),
            in_specs=[pl.BlockSpec((cd * blk, dm), lambda b, g, i, ws, we: (b * ns + i, 0)),
                      pl.BlockSpec((cd, 1, ne, blk), lambda b, g, i, ws, we: (i, b, 0, 0)),
                      pl.BlockSpec((cd, 1, 1, eg, blk), lambda b, g, i, ws, we: (i, b, g, 0, 0))],
            out_specs=pl.BlockSpec((1, eg, cp, dm + GATE_LANES), lambda b, g, i, ws, we: (b, g, 0, 0))),
        out_shape=jax.ShapeDtypeStruct((bsz, ne, cp, dm + GATE_LANES), BF16),
        compiler_params=_cparams(("parallel", "parallel", "arbitrary")),
    )(wstart, wextent, h1, afft, slot.reshape(nc, bsz, ne // eg, eg, blk))


def _ffn_kernel(xs_ref, wg_ref, wu_ref, wd_ref, out_ref, hdn_scr, *, cap, nf, ne):
    e = pl.program_id(0)
    j = pl.program_id(1)
    bsz = xs_ref.shape[0]
    dm = wg_ref.shape[2]
    tf = wg_ref.shape[3]

    @pl.when(j < nf)
    def _():
        wg = wg_ref[0, 0].astype(BF16)
        wu = wu_ref[0, 0].astype(BF16)
        for b in range(bsz):
            x = xs_ref[b, 0, :, 0:dm]
            hg = _dot(x, wg)
            hu = _dot(x, wu)
            hdn_scr[j, b * cap:(b + 1) * cap, :] = ((hg * jax.nn.sigmoid(hg)) * hu).astype(BF16)

    @pl.when(j >= nf)
    def _():
        wd = wd_ref[0, 0].astype(BF16)
        cp = out_ref.shape[2]
        tn = out_ref.shape[3]
        lane = lax.broadcasted_iota(jnp.int32, (cap, GATE_LANES), 1)
        mine = ((lane % ne) == e) & (lane < 3 * ne)
        for b in range(bsz):
            rows = slice(b * cap, (b + 1) * cap)
            y = _dot(hdn_scr[0, rows, :], wd[0:tf])
            for f in range(1, nf):
                y = y + _dot(hdn_scr[f, rows, :], wd[f * tf:(f + 1) * tf])
            pieces = xs_ref[b, 0, :, dm:dm + GATE_LANES].astype(F32)
            gate = jnp.sum(jnp.where(mine, pieces, 0.0), axis=1, keepdims=True)
            out_ref[b, 0, 0:cap, :] = (y * gate).astype(BF16)
            out_ref[b, 0, cap:cp, :] = jnp.zeros((cp - cap, tn), BF16)


def _ffn(xs, wg, wu, wd, l, cap, tf=1024):
    bsz, ne, cp, dx = xs.shape
    dm = wg.shape[2]
    ff = wg.shape[-1]
    nf = ff // tf
    tn = dm // nf
    return pl.pallas_call(
        functools.partial(_ffn_kernel, cap=cap, nf=nf, ne=ne), grid=(ne, 2 * nf),
        in_specs=[pl.BlockSpec((bsz, 1, cap, dx), lambda e, j: (0, e, 0, 0)),
                  pl.BlockSpec((1, 1, dm, tf), lambda e, j: (l, e, 0, jnp.minimum(j, nf - 1))),
                  pl.BlockSpec((1, 1, dm, tf), lambda e, j: (l, e, 0, jnp.minimum(j, nf - 1))),
                  pl.BlockSpec((1, 1, ff, tn), lambda e, j: (l, e, 0, jnp.maximum(j - nf, 0)))],
        out_specs=pl.BlockSpec((bsz, 1, cp, tn), lambda e, j: (0, e, 0, jnp.maximum(j - nf, 0))),
        out_shape=jax.ShapeDtypeStruct((bsz, ne, cp, dm), BF16),
        scratch_shapes=[pltpu.VMEM((nf, bsz * cap, tf), BF16)],
        compiler_params=_cparams(("parallel", "arbitrary")),
    )(xs, wg, wu, wd)


def _combine_kernel(ws_ref, we_ref, out_ref, slott_ref, h1_ref, lng_ref, lnb_ref, h2_ref, rhs_scr, *, nb, ne, cc):
    b = pl.program_id(0)
    i = pl.program_id(1)
    blk = ATT_BLOCK

    def finish(k, tot):
        rows = slice(k * blk, (k + 1) * blk)
        h2_ref[rows, :] = _layer_norm(DEEPNORM_ALPHA * h1_ref[rows, :] + tot, lng_ref[0], lnb_ref[0])

    def gather_packed(k, base):
        kk = ne * SMALL_WIN
        slot_t = slott_ref[k, 0].astype(F32).astype(BF16)
        expand = (lax.broadcasted_iota(jnp.int32, (ne, kk), 1) // SMALL_WIN
                  == lax.broadcasted_iota(jnp.int32, (ne, kk), 0)).astype(BF16)
        spread = _dot(slot_t, expand)
        row = (lax.broadcasted_iota(jnp.int32, (blk, kk), 1) % SMALL_WIN).astype(F32)
        onehot = (spread == row).astype(BF16)
        for e in range(ne):
            w = pl.multiple_of(ws_ref[base + e], SLOT_ALIGN)
            rhs_scr[e * SMALL_WIN:(e + 1) * SMALL_WIN, :] = out_ref[0, e, pl.ds(w, SMALL_WIN), :]
        finish(k, _dot(onehot, rhs_scr[...]))

    def gather_per_expert(k, base):
        slot_t = slott_ref[k, 0]
        liota = lax.broadcasted_iota(jnp.int32, (blk, WIN), 1)
        tot = None
        for e in range(ne):
            w = pl.multiple_of(ws_ref[base + e], SLOT_ALIGN)
            onehot = (liota == slot_t[:, e:e + 1]).astype(BF16)
            y = _dot(onehot, out_ref[0, e, pl.ds(w, WIN), :])
            tot = y if tot is None else tot + y
        finish(k, tot)

    for k in range(cc):
        base = ((i * cc + k) * nb + b) * ne
        small = _max_extent(we_ref, base, ne) <= SMALL_WIN
        pl.when(small)(functools.partial(gather_packed, k, base))
        pl.when(jnp.logical_not(small))(functools.partial(gather_per_expert, k, base))


def _combine(out, slot, wstart, wextent, h1, lng3, lnb3, l, seq):
    bsz, ne, cp, dm = out.shape
    nc = seq // ATT_BLOCK
    blk = ATT_BLOCK
    t = h1.shape[0]
    slot_t = jnp.swapaxes(slot, 2, 3)
    cc = math.gcd(COMBINE_STEP_CHUNKS, nc)
    ns = nc // cc
    vec = pl.BlockSpec((1, 1, dm), lambda b, i, ws, we: (l, 0, 0))
    return pl.pallas_call(
        functools.partial(_combine_kernel, nb=bsz, ne=ne, cc=cc),
        grid_spec=pltpu.PrefetchScalarGridSpec(
            num_scalar_prefetch=2, grid=(bsz, ns),
            in_specs=[pl.BlockSpec((1, ne, cp, dm), lambda b, i, ws, we: (b, 0, 0, 0)),
                      pl.BlockSpec((cc, 1, blk, ne), lambda b, i, ws, we: (i, b, 0, 0)),
                      pl.BlockSpec((cc * blk, dm), lambda b, i, ws, we: (b * ns + i, 0)),
                      vec, vec],
            out_specs=pl.BlockSpec((cc * blk, dm), lambda b, i, ws, we: (b * ns + i, 0)),
            scratch_shapes=[pltpu.VMEM((ne * SMALL_WIN, dm), BF16)]),
        out_shape=jax.ShapeDtypeStruct((t, dm), F32),
        compiler_params=_cparams(("parallel", "arbitrary")),
    )(wstart, wextent, out, slot_t, h1, lng3, lnb3)


def kernel(x, ln_in_g, ln_in_b, w_in, ret_theta, ssm_lambda_re, ssm_lambda_im, ssm_log_step, ssm_b_re, ssm_b_im,
           ssm_c_re, ssm_c_im, ssm_d, ssm_w_glu, ssm_b_glu, attn_sink, w_out, ln1_g, ln1_b, router_w,
           exp_w_gate, exp_w_up, exp_w_down, ln2_g, ln2_b):
    bsz, seq, dm = x.shape
    depth = w_in.shape[0]
    ne = router_w.shape[-1]
    cap = EC_FACTOR * seq // ne
    t = bsz * seq
    w_in_bf = w_in.astype(BF16)
    wt_bf = jnp.swapaxes(jnp.concatenate([w_in_bf[:, :, COL_SU:COL_SU + SSM_WIDTH],
                                          w_in_bf[:, :, COL_RK:COL_RK + RET_WIDTH]], axis=2), 1, 2)
    w_out_bf = w_out.astype(BF16)
    w_glu_bf = ssm_w_glu.astype(BF16)
    lg = jax.nn.log_sigmoid(ret_theta.astype(F32))
    sink = attn_sink.astype(F32)
    ssm_par = _ssm_params(ssm_lambda_re, ssm_lambda_im, ssm_log_step, ssm_b_re, ssm_b_im, ssm_c_re, ssm_c_im)
    vec3 = lambda a: a.astype(F32).reshape(depth, 1, -1)
    d3, bglu3, ln1g3, ln1b3, ln2g3, ln2b3 = map(vec3, (ssm_d, ssm_b_glu, ln1_g, ln1_b, ln2_g, ln2_b))
    rwt = jnp.swapaxes(router_w.astype(F32), 1, 2)
    rwt_hi = rwt.astype(BF16)
    rwt = jnp.concatenate([rwt_hi, (rwt - rwt_hi.astype(F32)).astype(BF16)], axis=1)
    bias = _attention_bias(seq)

    h = x
    for l in range(depth):
        if l == 0:
            h, proj, ut, kt, rv = _inproj(h, w_in_bf, wt_bf, l, ln=(ln_in_g, ln_in_b))
        else:
            proj, ut, kt, rv = _inproj(h, w_in_bf, wt_bf, l)
        h2d = h.reshape(t, dm)
        proj = proj.reshape(t, D_IN)
        yt = _ssm_conv(ut, ssm_par, l, bsz)
        sf, sb = _ret_states(kt, rv, lg, l, bsz, seq)
        h1, h1b, afft = _mixer(proj, sf, sb, yt, h2d, bias, lg, sink, w_out_bf, w_glu_bf, d3, bglu3, ln1g3, ln1b3,
                               rwt, l, bsz, seq)
        slot, wstart, wextent = _route(afft, cap)
        xs = _dispatch(h1b, afft, slot, wstart, wextent, cap)
        out = _ffn(xs, exp_w_gate, exp_w_up, exp_w_down, l, cap)
        h = _combine(out, slot, wstart, wextent, h1, ln2g3, ln2b3, l, seq).reshape(bsz, seq, dm)
    return h
```

```python
import functools
import math

import jax
import jax.numpy as jnp
from jax import lax
from jax.experimental import pallas as pl
from jax.experimental.pallas import tpu as pltpu

F32 = jnp.float32
BF16 = jnp.bfloat16

RET_HEADS = 4
RET_DK = 64
RET_CHUNK = 128
RET_WIDTH = RET_HEADS * RET_DK
SSM_CPG = 16
SSM_GROUPS = 16
SSM_STATE = 64
SSM_WIDTH = SSM_CPG * SSM_GROUPS
ATT_HEADS = 8
ATT_KV_HEADS = 2
ATT_GQ = ATT_HEADS // ATT_KV_HEADS
ATT_HEAD_DIM = 64
ATT_BLOCK = 128
ATT_WINDOW = 128
EC_FACTOR = 2
DEPTH = 2
DEEPNORM_ALPHA = (2.0 * DEPTH) ** 0.25
LN_EPS = 1e-5
NEG_INF = -1e30

COL_RQ, COL_RK, COL_RV, COL_RG, COL_SU, COL_AQ, COL_AK, COL_AV = 0, 256, 512, 768, 1024, 1280, 1792, 1920
D_IN = 2048

CHUNK = 128
SUBLANES = 8
SLOT_ALIGN = 16
WIN = CHUNK + SLOT_ALIGN
SMALL_WIN = 3 * SLOT_ALIGN
DISPATCH_STEP_CHUNKS = 16
COMBINE_STEP_CHUNKS = 4
MIXER_STEP_CHUNKS = 2
GATE_LANES = 128
RET_STEP_CHUNKS = 8
VMEM_LIMIT = 56 * 1024 * 1024
ROUTE_REFINE_STEPS = 10


def _cparams(sem):
    return pltpu.CompilerParams(dimension_semantics=sem, vmem_limit_bytes=VMEM_LIMIT)


def _layer_norm(x, g, b):
    mu = jnp.mean(x, axis=-1, keepdims=True)
    xc = x - mu
    var = jnp.mean(xc * xc, axis=-1, keepdims=True)
    return xc * lax.rsqrt(var + LN_EPS) * g + b


def _dot(a, b):
    return jnp.dot(a, b, preferred_element_type=F32)


def _dot_hp(a, b):
    return jnp.dot(a, b, precision=lax.Precision.HIGHEST, preferred_element_type=F32)


def _dot_nt(a, b):
    return lax.dot_general(a, b, (((1,), (1,)), ((), ())), preferred_element_type=F32)


def _cmul(ar, ai, br, bi):
    return ar * br - ai * bi, ar * bi + ai * br


def _cpow(br, bi, expo, nbits):
    shape = (br.shape[0], expo.shape[1])
    rr = jnp.ones(shape, F32)
    ri = jnp.zeros(shape, F32)
    for j in range(nbits):
        bit = jnp.broadcast_to(((expo >> j) & 1) == 1, shape)
        nr, ni = _cmul(rr, ri, br, bi)
        rr = jnp.where(bit, nr, rr)
        ri = jnp.where(bit, ni, ri)
        if j + 1 < nbits:
            br, bi = _cmul(br, bi, br, bi)
    return rr, ri


def _inproj_kernel(*refs, apply_ln):
    if apply_ln:
        x_ref, g_ref, b_ref, w_ref, wt_ref, h_ref, proj_ref, ut_ref, kt_ref, rv_ref = refs
    else:
        x_ref, w_ref, wt_ref, proj_ref, ut_ref, kt_ref, rv_ref = refs
    nb, tl, d = x_ref.shape
    h = x_ref[...].reshape(nb * tl, d)
    if apply_ln:
        h = _layer_norm(h, g_ref[...], b_ref[...])
        h_ref[...] = h.reshape(nb, tl, d)
    hb = h.astype(BF16)
    proj = _dot(hb, w_ref[0])
    proj_ref[...] = proj.reshape(nb, tl, -1)
    rv_ref[...] = proj[:, COL_RV:COL_RV + RET_WIDTH].astype(BF16).reshape(nb, tl, RET_WIDTH)
    tt = _dot_nt(wt_ref[0], hb)
    k = tl // CHUNK
    for j in range(k):
        for b in range(nb):
            ut_ref[:, j * nb + b, :] = tt[0:SSM_WIDTH, (b * k + j) * CHUNK:(b * k + j + 1) * CHUNK]
    for b in range(nb):
        kt_ref[b] = tt[SSM_WIDTH:SSM_WIDTH + RET_WIDTH, b * tl:(b + 1) * tl].astype(BF16)


def _inproj(x3, w_bf, wt_bf, l, ln=None):
    bsz, seq, d = x3.shape
    n = w_bf.shape[2]
    k = SUBLANES // bsz
    tl = k * CHUNK
    nc = seq // CHUNK
    xspec = pl.BlockSpec((bsz, tl, d), lambda i: (0, i, 0))
    wspec = pl.BlockSpec((1, d, n), lambda i: (l, 0, 0))
    wtspec = pl.BlockSpec((1,) + wt_bf.shape[1:], lambda i: (l, 0, 0))
    out_specs = [pl.BlockSpec((bsz, tl, n), lambda i: (0, i, 0)),
                 pl.BlockSpec((SSM_WIDTH, k * bsz, CHUNK), lambda i: (0, i, 0)),
                 pl.BlockSpec((bsz, RET_WIDTH, tl), lambda i: (0, 0, i)),
                 pl.BlockSpec((bsz, tl, RET_WIDTH), lambda i: (0, i, 0))]
    out_shape = [jax.ShapeDtypeStruct((bsz, seq, n), F32),
                 jax.ShapeDtypeStruct((SSM_WIDTH, nc * bsz, CHUNK), F32),
                 jax.ShapeDtypeStruct((bsz, RET_WIDTH, seq), BF16),
                 jax.ShapeDtypeStruct((bsz, seq, RET_WIDTH), BF16)]
    if ln is None:
        return pl.pallas_call(
            functools.partial(_inproj_kernel, apply_ln=False), grid=(seq // tl,),
            in_specs=[xspec, wspec, wtspec], out_specs=out_specs, out_shape=out_shape,
            compiler_params=_cparams(("parallel",)),
        )(x3, w_bf, wt_bf)
    g, b = ln
    vec = pl.BlockSpec((1, d), lambda i: (0, 0))
    return pl.pallas_call(
        functools.partial(_inproj_kernel, apply_ln=True), grid=(seq // tl,),
        in_specs=[xspec, vec, vec, wspec, wtspec], out_specs=[xspec] + out_specs,
        out_shape=[jax.ShapeDtypeStruct(x3.shape, F32)] + out_shape,
        compiler_params=_cparams(("parallel",)),
    )(x3, g.reshape(1, d), b.reshape(1, d), w_bf, wt_bf)


def _ssm_params(lam_re, lam_im, log_step, b_re, b_im, c_re, c_im):
    lr, li = lam_re.astype(F32), lam_im.astype(F32)
    step = jnp.exp(log_step.astype(F32))[..., None]
    er = jnp.exp(lr * step)
    lbr, lbi = er * jnp.cos(li * step), er * jnp.sin(li * step)
    den = lr * lr + li * li
    fr = ((lbr - 1.0) * lr + lbi * li) / den
    fi = (lbi * lr - (lbr - 1.0) * li) / den
    br = jnp.swapaxes(b_re.astype(F32), -1, -2)[:, None]
    bi = jnp.swapaxes(b_im.astype(F32), -1, -2)[:, None]
    bbr = fr[..., None, :] * br - fi[..., None, :] * bi
    bbi = fr[..., None, :] * bi + fi[..., None, :] * br
    lcol = jnp.stack([lbr[:, 0], lbi[:, 0], lbr[:, 1], lbi[:, 1]], axis=-1)
    lrow = jnp.swapaxes(lcol, -1, -2)
    lrow = jnp.concatenate([lrow, jnp.zeros_like(lrow)], axis=2)
    bt = jnp.stack([bbr[:, 0], bbi[:, 0], bbr[:, 1], bbi[:, 1]], axis=2)
    cr, ci = c_re.astype(F32), c_im.astype(F32)
    c4 = jnp.stack([cr[:, 0], ci[:, 0], cr[:, 1], ci[:, 1]], axis=2)
    return lcol, lrow, bt, jnp.swapaxes(bt, -1, -2), c4, jnp.swapaxes(c4, -1, -2)


def _ssm_kernel(ut_ref, lcol_ref, lrow_ref, bt_ref, bcol_ref, c_ref, ct_ref, y_ref,
                w_scr, wst_scr, g_scr, vf_scr, vb_scr, acc_scr, s_scr, sp_scr, x_scr, *, nb, nchunks):
    tc, p, cpg = CHUNK, SSM_STATE, SSM_CPG
    lcol = lcol_ref[0, 0]
    lf = (lcol[:, 0:1], lcol[:, 1:2])
    lb = (lcol[:, 2:3], lcol[:, 3:4])

    m_row = lax.broadcasted_iota(jnp.int32, (1, tc), 1)
    pfr, pfi = _cpow(*lf, m_row, 7)
    pf1r, pf1i = _cmul(pfr, pfi, *lf)
    prr, pri = _cpow(*lf, tc - 1 - m_row, 7)
    pbr, pbi = _cpow(*lb, tc - m_row, 8)
    pqr, pqi = _cpow(*lb, m_row, 7)

    def rep_co(x):
        return jnp.broadcast_to(x[:, None, :], (cpg, cpg, p)).reshape(cpg * cpg, p)

    def rep_ci(x):
        return jnp.broadcast_to(x[None, :, :], (cpg, cpg, p)).reshape(cpg * cpg, p)

    cbfr, cbfi = _cmul(rep_co(c_ref[0, 0, 0]), rep_co(c_ref[0, 0, 1]), rep_ci(bt_ref[0, 0, 0]), rep_ci(bt_ref[0, 0, 1]))
    cbbr, cbbi = _cmul(rep_co(c_ref[0, 0, 2]), rep_co(c_ref[0, 0, 3]), rep_ci(bt_ref[0, 0, 2]), rep_ci(bt_ref[0, 0, 3]))
    lane = lax.broadcasted_iota(jnp.int32, (cpg * cpg, tc), 1)
    kb0 = jnp.sum(cbbr, axis=1, keepdims=True)
    vf_scr[...] = _dot_hp(cbfr, pfr) - _dot_hp(cbfi, pfi) + jnp.where(lane == 0, kb0, 0.0)
    vb_scr[...] = _dot_hp(cbbr, pbr) - _dot_hp(cbbi, pbi)

    for co in range(cpg):
        cols = slice(co * tc, (co + 1) * tc)
        for r, (ar, ai) in enumerate(((pf1r, pf1i), (pbr, pbi))):
            gr, gi = _cmul(ct_ref[0, 0, 2 * r][:, co:co + 1], ct_ref[0, 0, 2 * r + 1][:, co:co + 1], ar, ai)
            g_scr[2 * r * p:(2 * r + 1) * p, cols] = gr.astype(BF16)
            g_scr[(2 * r + 1) * p:(2 * r + 2) * p, cols] = (-gi).astype(BF16)
        for r, (ar, ai) in enumerate(((prr, pri), (pqr, pqi))):
            sr, si = _cmul(bcol_ref[0, 0, 2 * r][:, co:co + 1], bcol_ref[0, 0, 2 * r + 1][:, co:co + 1], ar, ai)
            wst_scr[co, 2 * r * p:(2 * r + 1) * p, :] = sr.astype(BF16)
            wst_scr[co, (2 * r + 1) * p:(2 * r + 2) * p, :] = si.astype(BF16)

    s_idx = lax.broadcasted_iota(jnp.int32, (tc, tc), 0)
    j_idx = lax.broadcasted_iota(jnp.int32, (tc, tc), 1)
    fwd_part = j_idx < tc - s_idx
    acc_scr[...] = jnp.zeros_like(acc_scr)
    s_scr[...] = jnp.zeros_like(s_scr)

    def per_channel(ci, carry):
        for co in range(cpg):
            rf = jnp.broadcast_to(vf_scr[pl.ds(co * cpg + ci, 1), :], (tc, tc))
            rb = jnp.broadcast_to(vb_scr[pl.ds(co * cpg + ci, 1), :], (tc, tc))
            m = pltpu.roll(jnp.where(fwd_part, rf, rb), 0, 1, stride=1, stride_axis=0)
            w_scr[ci, :, co * tc:(co + 1) * tc] = m.astype(BF16)
        u = ut_ref[ci].astype(BF16)
        acc_scr[...] += _dot(u, w_scr[ci])
        s_scr[...] += _dot_nt(u, wst_scr[ci])
        return carry

    lax.fori_loop(0, cpg, per_channel, 0)

    lrow = lrow_ref[0, 0]
    dec = []
    for r in range(2):
        dr, di = lrow[2 * r:2 * r + 1], lrow[2 * r + 1:2 * r + 2]
        for _ in range(7):
            dr, di = _cmul(dr, di, dr, di)
        dec.append((dr, di))
    for q in range(4):
        sp_scr[q] = s_scr[:, q * p:(q + 1) * p]
    xfr = xfi = xbr = xbi = jnp.zeros((nb, p), F32)
    for i in range(nchunks):
        rf = slice(i * nb, (i + 1) * nb)
        rb = slice((nchunks - 1 - i) * nb, (nchunks - i) * nb)
        x_scr[0, rf, :] = xfr
        x_scr[1, rf, :] = xfi
        x_scr[2, rb, :] = xbr
        x_scr[3, rb, :] = xbi
        xfr, xfi = _cmul(dec[0][0], dec[0][1], xfr, xfi)
        xbr, xbi = _cmul(dec[1][0], dec[1][1], xbr, xbi)
        xfr, xfi = xfr + sp_scr[0, rf, :], xfi + sp_scr[1, rf, :]
        xbr, xbi = xbr + sp_scr[2, rb, :], xbi + sp_scr[3, rb, :]
    xin = jnp.concatenate([x_scr[q] for q in range(4)], axis=1).astype(BF16)
    y = acc_scr[...] + _dot(xin, g_scr[...])
    for co in range(cpg):
        y_ref[co] = y[:, co * tc:(co + 1) * tc]


def _ssm_conv(ut, params, l, bsz):
    _, r, tc = ut.shape
    cpg, p, ng = SSM_CPG, SSM_STATE, SSM_GROUPS

    def pspec(a):
        return pl.BlockSpec((1, 1) + a.shape[2:], lambda g: (l, g) + (0,) * (a.ndim - 2))
    return pl.pallas_call(
        functools.partial(_ssm_kernel, nb=bsz, nchunks=r // bsz), grid=(ng,),
        in_specs=[pl.BlockSpec((cpg, r, tc), lambda g: (g, 0, 0))] + [pspec(a) for a in params],
        out_specs=pl.BlockSpec((cpg, r, tc), lambda g: (g, 0, 0)),
        out_shape=jax.ShapeDtypeStruct(ut.shape, F32),
        scratch_shapes=[pltpu.VMEM((cpg, tc, cpg * tc), BF16), pltpu.VMEM((cpg, 4 * p, tc), BF16),
                        pltpu.VMEM((4 * p, cpg * tc), BF16),
                        pltpu.VMEM((cpg * cpg, tc), F32), pltpu.VMEM((cpg * cpg, tc), F32),
                        pltpu.VMEM((r, cpg * tc), F32), pltpu.VMEM((r, 4 * p), F32),
                        pltpu.VMEM((4, r, p), F32), pltpu.VMEM((4, r, p), F32)],
        compiler_params=_cparams(("parallel",)),
    )(ut, *params)


def _retstate_kernel(lg_ref, ktf_ref, vf_ref, ktb_ref, vb_ref, sf_ref, sb_ref, accf, accb, *, l, cs):
    i = pl.program_id(1)

    @pl.when(i == 0)
    def _():
        accf[...] = jnp.zeros_like(accf)
        accb[...] = jnp.zeros_like(accb)

    ch = RET_CHUNK
    pos = lax.broadcasted_iota(jnp.int32, (1, ch), 1).astype(F32)
    one = jnp.ones((1, RET_DK), F32)
    wts = []
    for h in range(RET_HEADS):
        lgf = lg_ref[l, 0, h]
        lgb = lg_ref[l, 1, h]
        wts.append((jnp.exp(lgf * (ch - 1.0 - pos)) * RET_DK ** -0.5, jnp.exp(lgb * pos) * RET_DK ** -0.5,
                    jnp.exp(one * (lgf * ch)), jnp.exp(one * (lgb * ch))))
    for j in range(cs):
        jb = cs - 1 - j
        sf_ref[0, j] = accf[...].astype(BF16)
        sb_ref[0, jb] = accb[...].astype(BF16)
        for h in range(RET_HEADS):
            rows = slice(h * RET_DK, (h + 1) * RET_DK)
            wf, wb, decf, decb = wts[h]
            kf = (ktf_ref[0, rows, j * ch:(j + 1) * ch].astype(F32) * wf).astype(BF16)
            kb = (ktb_ref[0, rows, jb * ch:(jb + 1) * ch].astype(F32) * wb).astype(BF16)
            accf[rows, :] = decf * accf[rows, :] + _dot(kf, vf_ref[0, j * ch:(j + 1) * ch, rows])
            accb[rows, :] = decb * accb[rows, :] + _dot(kb, vb_ref[0, jb * ch:(jb + 1) * ch, rows])


def _ret_states(kt, rv, lg, l, bsz, seq):
    nc = seq // RET_CHUNK
    cs = min(RET_STEP_CHUNKS, nc)
    ns = nc // cs
    w = RET_WIDTH
    tl = cs * RET_CHUNK
    st = jax.ShapeDtypeStruct((bsz, nc, w, RET_DK), BF16)
    return pl.pallas_call(
        functools.partial(_retstate_kernel, l=l, cs=cs), grid=(bsz, ns),
        in_specs=[pl.BlockSpec(memory_space=pltpu.SMEM),
                  pl.BlockSpec((1, w, tl), lambda b, i: (b, 0, i)),
                  pl.BlockSpec((1, tl, w), lambda b, i: (b, i, 0)),
                  pl.BlockSpec((1, w, tl), lambda b, i: (b, 0, ns - 1 - i)),
                  pl.BlockSpec((1, tl, w), lambda b, i: (b, ns - 1 - i, 0))],
        out_specs=[pl.BlockSpec((1, cs, w, RET_DK), lambda b, i: (b, i, 0, 0)),
                   pl.BlockSpec((1, cs, w, RET_DK), lambda b, i: (b, ns - 1 - i, 0, 0))],
        out_shape=[st, st],
        scratch_shapes=[pltpu.VMEM((w, RET_DK), F32), pltpu.VMEM((w, RET_DK), F32)],
        compiler_params=_cparams(("parallel", "arbitrary")),
    )(lg, kt, rv, kt, rv)


def _gelu_tanh(x):
    return 0.5 * x * (1.0 + jnp.tanh(math.sqrt(2.0 / math.pi) * (x + 0.044715 * (x * x * x))))


def _attention_bias(seq):
    blk = ATT_BLOCK
    s_idx = jnp.arange(3 * blk)[None, :]
    t_idx = jnp.arange(blk)[:, None]
    arel = jnp.abs(s_idx - blk - t_idx)
    band = arel <= ATT_WINDOW
    slopes = jnp.exp2(-8.0 * jnp.arange(1, ATT_HEADS + 1, dtype=F32) / ATT_HEADS)
    alibi = -slopes[:, None, None] * arel.astype(F32)[None]
    variants = []
    for prev_ok, next_ok in ((False, True), (True, True), (True, False)):
        ok = band & (prev_ok | (s_idx >= blk)) & (next_ok | (s_idx < 2 * blk))
        variants.append(jnp.where(ok[None], alibi, NEG_INF).reshape(ATT_KV_HEADS, ATT_GQ * blk, 3 * blk))
    return jnp.stack(variants)


def _mixer_kernel(lg_ref, sink_ref, proj_ref, kvp_ref, kvn_ref, sf_ref, sb_ref, yt_ref, h_ref, bias0_ref, bias1_ref,
                  wout_ref, wglu_ref, d_ref, bglu_ref, lng_ref, lnb_ref, rwt_ref,
                  h1_ref, h1b_ref, afft_ref, mix_scr, sret_scr, satt_scr, pret_scr, patt_scr, *, l, nb):
    b = pl.program_id(1)
    cm = MIXER_STEP_CHUNKS
    bias_refs = (bias0_ref, bias1_ref)
    ch = RET_CHUNK
    dk = RET_DK
    blk = ATT_BLOCK
    hd = ATT_HEAD_DIM
    kvw = ATT_KV_HEADS * hd

    def kv_block(u, cols):
        if u < 0:
            return kvp_ref[:, cols]
        if u >= cm:
            return kvn_ref[:, cols]
        return proj_ref[u * ch:(u + 1) * ch, COL_AK + cols.start:COL_AK + cols.stop]

    vbs = {}
    for u in range(cm):
        r = slice(u * ch, (u + 1) * ch)
        for h in range(RET_HEADS):
            q = proj_ref[r, COL_RQ + h * dk:COL_RQ + (h + 1) * dk]
            k = proj_ref[r, COL_RK + h * dk:COL_RK + (h + 1) * dk] * dk ** -0.5
            sret_scr[u * RET_HEADS + h] = _dot_nt(q.astype(BF16), k.astype(BF16))
        for kvh in range(ATT_KV_HEADS):
            kc = slice(kvh * hd, (kvh + 1) * hd)
            vc = slice(kvw + kvh * hd, kvw + (kvh + 1) * hd)
            kb = jnp.concatenate([kv_block(u + d, kc) for d in (-1, 0, 1)], axis=0).astype(BF16)
            vbs[u, kvh] = jnp.concatenate([kv_block(u + d, vc) for d in (-1, 0, 1)], axis=0).astype(BF16)
            q4 = jnp.concatenate([proj_ref[r, COL_AQ + (kvh * ATT_GQ + gq) * hd:COL_AQ + (kvh * ATT_GQ + gq + 1) * hd]
                                  for gq in range(ATT_GQ)], axis=0) * hd ** -0.5
            satt_scr[u * ATT_KV_HEADS + kvh] = _dot_nt(q4.astype(BF16), kb)

    pos = lax.broadcasted_iota(jnp.int32, (ch, 1), 0).astype(F32)
    dist = lax.broadcasted_iota(jnp.int32, (ch, ch), 0) - lax.broadcasted_iota(jnp.int32, (ch, ch), 1)
    adist = jnp.abs(dist).astype(F32)
    for h in range(RET_HEADS):
        lgf = lg_ref[l, 0, h]
        lgb = lg_ref[l, 1, h]
        dmat = jnp.exp(jnp.where(dist >= 0, lgf, lgb) * adist)
        wqf = jnp.exp(lgf * (pos + 1.0))
        wqb = jnp.exp(lgb * (ch - pos))
        for u in range(cm):
            q = proj_ref[u * ch:(u + 1) * ch, COL_RQ + h * dk:COL_RQ + (h + 1) * dk]
            uh = u * RET_HEADS + h
            pret_scr[uh, :, 0:ch] = (sret_scr[uh] * dmat).astype(BF16)
            pret_scr[uh, :, ch:ch + dk] = (q * wqf).astype(BF16)
            pret_scr[uh, :, ch + dk:ch + 2 * dk] = (q * wqb).astype(BF16)
    for u in range(cm):
        for kvh in range(ATT_KV_HEADS):
            uk = u * ATT_KV_HEADS + kvh
            for gq in range(ATT_GQ):
                rows = slice(gq * blk, (gq + 1) * blk)
                sink = sink_ref[l, kvh * ATT_GQ + gq]
                sc = satt_scr[uk, rows, :] + bias_refs[u][0, kvh, rows, :]
                m = jnp.maximum(jnp.max(sc, axis=-1, keepdims=True), sink)
                p = jnp.exp(sc - m)
                denom = jnp.sum(p, axis=-1, keepdims=True) + jnp.exp(sink - m)
                patt_scr[uk, rows, :] = (p * (1.0 / denom)).astype(BF16)

    att_base = RET_WIDTH + SSM_WIDTH
    w_ssm = SSM_WIDTH
    for u in range(cm):
        r = slice(u * ch, (u + 1) * ch)
        for h in range(RET_HEADS):
            rows = slice(h * dk, (h + 1) * dk)
            v = proj_ref[r, COL_RV + h * dk:COL_RV + (h + 1) * dk]
            g = proj_ref[r, COL_RG + h * dk:COL_RG + (h + 1) * dk]
            rhs = jnp.concatenate([v.astype(BF16), sf_ref[0, u, rows, :], sb_ref[0, u, rows, :]], axis=0)
            o = _dot(pret_scr[u * RET_HEADS + h], rhs)
            mu = jnp.mean(o, axis=-1, keepdims=True)
            oc = o - mu
            var = jnp.mean(oc * oc, axis=-1, keepdims=True)
            mix_scr[r, h * dk:(h + 1) * dk] = (g * jax.nn.sigmoid(g)) * (oc * lax.rsqrt(var + LN_EPS))
        for kvh in range(ATT_KV_HEADS):
            o = _dot(patt_scr[u * ATT_KV_HEADS + kvh], vbs[u, kvh])
            for gq in range(ATT_GQ):
                hh = kvh * ATT_GQ + gq
                mix_scr[r, att_base + hh * hd:att_base + (hh + 1) * hd] = o[gq * blk:(gq + 1) * blk]

        yraw = yt_ref[:, pl.ds(u * nb + b, 1), :].reshape(w_ssm, ch).T
        y = yraw + d_ref[0] * proj_ref[r, COL_SU:COL_SU + w_ssm]
        y = _gelu_tanh(y)
        gate = jax.nn.sigmoid(_dot(y.astype(BF16), wglu_ref[0]) + bglu_ref[0])
        mix_scr[r, RET_WIDTH:RET_WIDTH + w_ssm] = y * gate

    mix = _dot(mix_scr[...].astype(BF16), wout_ref[0])
    h1 = _layer_norm(DEEPNORM_ALPHA * h_ref[...] + mix, lng_ref[0], lnb_ref[0])
    h1_ref[...] = h1
    h1b_ref[...] = h1.astype(BF16)

    ne = rwt_ref.shape[1] // 2
    h_hi = h1.astype(BF16)
    h_lo = (h1 - h_hi.astype(F32)).astype(BF16)
    both = _dot_nt(rwt_ref[0], h_hi)
    lt = both[0:ne] + both[ne:2 * ne] + _dot_nt(rwt_ref[0, 0:ne, :], h_lo)
    lt = lt - jnp.max(lt, axis=0, keepdims=True)
    et = jnp.exp(lt)
    aff = et / jnp.sum(et, axis=0, keepdims=True)
    for u in range(cm):
        afft_ref[u, 0] = aff[:, u * ch:(u + 1) * ch]


def _mixer(proj, sf, sb, yt, h2d, bias, lg, sink, wout_bf, wglu_bf, d3, bglu3, lng3, lnb3, rwt, l, bsz, seq):
    nc = seq // ATT_BLOCK
    cm = MIXER_STEP_CHUNKS
    ns = nc // cm
    assert nc % cm == 0 and cm * bsz == SUBLANES
    t, dm = h2d.shape
    ne = rwt.shape[1] // 2
    kvw = 2 * ATT_KV_HEADS * ATT_HEAD_DIM
    sw = RET_WIDTH
    kvcol = COL_AK // kvw
    blk = ATT_BLOCK

    def layer(a):
        return pl.BlockSpec((1,) + a.shape[1:], lambda i, b: (l,) + (0,) * (a.ndim - 1))
    smem = pl.BlockSpec(memory_space=pltpu.SMEM)
    in_specs = [
        smem, smem,
        pl.BlockSpec((cm * blk, D_IN), lambda i, b: (b * ns + i, 0)),
        pl.BlockSpec((blk, kvw), lambda i, b: (b * nc + jnp.maximum(cm * i - 1, 0), kvcol)),
        pl.BlockSpec((blk, kvw), lambda i, b: (b * nc + jnp.minimum(cm * i + cm, nc - 1), kvcol)),
        pl.BlockSpec((1, cm, sw, RET_DK), lambda i, b: (b, i, 0, 0)),
        pl.BlockSpec((1, cm, sw, RET_DK), lambda i, b: (b, i, 0, 0)),
        pl.BlockSpec((SSM_WIDTH, SUBLANES, CHUNK), lambda i, b: (0, i, 0)),
        pl.BlockSpec((cm * blk, dm), lambda i, b: (b * ns + i, 0)),
        pl.BlockSpec((1,) + bias.shape[1:], lambda i, b: (jnp.where(i == 0, 0, 1), 0, 0, 0)),
        pl.BlockSpec((1,) + bias.shape[1:], lambda i, b: (jnp.where(i == ns - 1, 2, 1), 0, 0, 0)),
        layer(wout_bf), layer(wglu_bf), layer(d3), layer(bglu3), layer(lng3), layer(lnb3), layer(rwt),
    ]
    out_specs = [
        pl.BlockSpec((cm * blk, dm), lambda i, b: (b * ns + i, 0)),
        pl.BlockSpec((cm * blk, dm), lambda i, b: (b * ns + i, 0)),
        pl.BlockSpec((cm, 1, ne, blk), lambda i, b: (i, b, 0, 0)),
    ]
    out_shape = [jax.ShapeDtypeStruct((t, dm), F32),
                 jax.ShapeDtypeStruct((t, dm), BF16),
                 jax.ShapeDtypeStruct((nc, bsz, ne, blk), F32)]
    scratch = [pltpu.VMEM((cm * blk, dm), F32),
               pltpu.VMEM((cm * RET_HEADS, blk, blk), F32),
               pltpu.VMEM((cm * ATT_KV_HEADS, ATT_GQ * blk, 3 * blk), F32),
               pltpu.VMEM((cm * RET_HEADS, blk, blk + 2 * RET_DK), BF16),
               pltpu.VMEM((cm * ATT_KV_HEADS, ATT_GQ * blk, 3 * blk), BF16)]
    return pl.pallas_call(
        functools.partial(_mixer_kernel, l=l, nb=bsz), grid=(ns, bsz),
        in_specs=in_specs, out_specs=out_specs, out_shape=out_shape, scratch_shapes=scratch,
        compiler_params=_cparams(("parallel", "parallel")),
    )(lg, sink, proj, proj, proj, sf, sb, yt, h2d, bias, bias, wout_bf, wglu_bf, d3, bglu3, lng3, lnb3, rwt)


def _route_kernel(a_ref, slot_ref, start_ref, extent_ref, *, nc, ne, cap):
    blk = ATT_BLOCK
    a = a_ref[0]

    def chunk(x, c):
        return x[c * ne:(c + 1) * ne]

    def count(pred):
        x = pred.astype(jnp.int32)
        tot = chunk(x, 0)
        for c in range(1, nc):
            tot = tot + chunk(x, c)
        return jnp.sum(tot, axis=1, keepdims=True)

    def tile_e(v):
        return jnp.concatenate([v] * nc, axis=0)

    tau = jnp.zeros((ne, 1), jnp.int32)
    for bit in range(30, -1, -1):
        cand = tau | (1 << bit)
        tau = jnp.where(count(a >= tile_e(pltpu.bitcast(cand, F32))) >= cap, cand, tau)
    lo = pltpu.bitcast(tau, F32)
    hi = pltpu.bitcast(tau + 1, F32)
    for _ in range(ROUTE_REFINE_STEPS):
        mid = lo + (hi - lo) * 0.5
        ok = count(a >= tile_e(mid)) >= cap
        lo = jnp.where(ok, mid, lo)
        hi = jnp.where(ok, hi, mid)
    gt = a >= tile_e(hi)
    eq = (a >= tile_e(lo)) & jnp.logical_not(gt)
    need = cap - count(gt)

    tri = (lax.broadcasted_iota(jnp.int32, (blk, blk), 0) <= lax.broadcasted_iota(jnp.int32, (blk, blk), 1)).astype(BF16)

    def ranks(mask):
        mf = mask.astype(F32)
        incl = _dot(mf.astype(BF16), tri)
        run = jnp.zeros((ne, 1), F32)
        offs = []
        tots = []
        for c in range(nc):
            offs.append(run)
            tots.append(chunk(incl, c)[:, blk - 1:blk])
            run = run + tots[-1]
        return incl - mf, jnp.concatenate(offs, axis=0), jnp.concatenate(tots, axis=0)

    eq_local, eq_off, _ = ranks(eq)
    mask = gt | (eq & ((eq_local + eq_off) < tile_e(need).astype(F32)))
    local, off, tot = ranks(mask)
    off_i = off.astype(jnp.int32)
    start = (off_i // SLOT_ALIGN) * SLOT_ALIGN
    slot = (off_i - start) + local.astype(jnp.int32)
    slot_ref[0] = jnp.where(mask, slot, -1)
    start_ref[0] = jnp.broadcast_to(start, (nc * ne, blk))
    extent_ref[0] = jnp.broadcast_to((off_i - start) + tot.astype(jnp.int32), (nc * ne, blk))


def _route(afft, cap):
    nc, bsz, ne, blk = afft.shape
    a2 = afft.reshape(1, nc * bsz * ne, blk)
    spec = pl.BlockSpec(a2.shape, lambda s: (0, 0, 0))
    slot, start, extent = pl.pallas_call(
        functools.partial(_route_kernel, nc=nc, ne=bsz * ne, cap=cap), grid=(1,),
        in_specs=[spec], out_specs=[spec, spec, spec],
        out_shape=[jax.ShapeDtypeStruct(a2.shape, jnp.int32)] * 3,
        compiler_params=_cparams(("arbitrary",)),
    )(a2)
    flat = lambda v: v[0, :, 0]
    return slot.reshape(nc, bsz, ne, blk), flat(start), flat(extent)


def _max_extent(we_ref, base, n):
    mx = we_ref[base]
    for j in range(1, n):
        mx = jnp.maximum(mx, we_ref[base + j])
    return mx


def _dispatch_kernel(ws_ref, we_ref, h_ref, afft_ref, slot_ref, xs_ref, *, nb, ne, eg, cd):
    b = pl.program_id(0)
    g = pl.program_id(1)
    i = pl.program_id(2)
    dm = h_ref.shape[1]
    blk = ATT_BLOCK

    @pl.when(i == 0)
    def _():
        xs_ref[...] = jnp.zeros_like(xs_ref)

    def scatter_rows(win, k, base):
        hb = h_ref[k * blk:(k + 1) * blk, :]
        a = afft_ref[k, 0]
        a_hi = a.astype(BF16)
        r1 = a - a_hi.astype(F32)
        a_mid = r1.astype(BF16)
        a_lo = (r1 - a_mid.astype(F32)).astype(BF16)
        a3 = jnp.concatenate([a_hi, a_mid, a_lo, jnp.zeros((GATE_LANES - 3 * ne, blk), BF16)], axis=0)
        riota = lax.broadcasted_iota(jnp.int32, (win, blk), 0)
        onehots = jnp.concatenate([(riota == slot_ref[k, 0, 0, j:j + 1, :]).astype(BF16) for j in range(eg)], axis=0)
        res = _dot(onehots, hb)
        resg = _dot_nt(onehots, a3)
        for j in range(eg):
            w = pl.multiple_of(ws_ref[base + j], SLOT_ALIGN)
            head = pl.ds(w, SLOT_ALIGN)
            tail = pl.ds(w + SLOT_ALIGN, win - SLOT_ALIGN)
            for cols, r in ((slice(0, dm), res), (slice(dm, dm + GATE_LANES), resg)):
                rj = r[j * win:(j + 1) * win]
                xs_ref[0, j, head, cols] = (xs_ref[0, j, head, cols].astype(F32) + rj[0:SLOT_ALIGN]).astype(BF16)
                xs_ref[0, j, tail, cols] = rj[SLOT_ALIGN:].astype(BF16)

    for k in range(cd):
        base = ((i * cd + k) * nb + b) * ne + g * eg
        small = _max_extent(we_ref, base, eg) <= SMALL_WIN
        pl.when(small)(functools.partial(scatter_rows, SMALL_WIN, k, base))
        pl.when(jnp.logical_not(small))(functools.partial(scatter_rows, WIN, k, base))


def _dispatch(h1, afft, slot, wstart, wextent, cap, eg=8):
    nc, bsz, ne, blk = slot.shape
    t, dm = h1.shape
    cp = cap + WIN
    cd = math.gcd(DISPATCH_STEP_CHUNKS, nc)
    ns = nc // cd
    return pl.pallas_call(
        functools.partial(_dispatch_kernel, nb=bsz, ne=ne, eg=eg, cd=cd),
        grid_spec=pltpu.PrefetchScalarGridSpec(
            num_scalar_prefetch=2, grid=(bsz, ne // eg, ns),
            in_specs=[pl.BlockSpec((cd * blk, dm), lambda b, g, i, ws, we: (b * ns + i, 0)),
                      pl.BlockSpec((cd, 1, ne, blk), lambda b, g, i, ws, we: (i, b, 0, 0)),
                      pl.BlockSpec((cd, 1, 1, eg, blk), lambda b, g, i, ws, we: (i, b, g, 0, 0))],
            out_specs=pl.BlockSpec((1, eg, cp, dm + GATE_LANES), lambda b, g, i, ws, we: (b, g, 0, 0))),
        out_shape=jax.ShapeDtypeStruct((bsz, ne, cp, dm + GATE_LANES), BF16),
        compiler_params=_cparams(("parallel", "parallel", "arbitrary")),
    )(wstart, wextent, h1, afft, slot.reshape(nc, bsz, ne // eg, eg, blk))


def _ffn_kernel(xs_ref, wg_ref, wu_ref, wd_ref, out_ref, hdn_scr, *, cap, nf, ne):
    e = pl.program_id(0)
    j = pl.program_id(1)
    bsz = xs_ref.shape[0]
    dm = wg_ref.shape[2]
    tf = wg_ref.shape[3]

    @pl.when(j < nf)
    def _():
        wg = wg_ref[0, 0].astype(BF16)
        wu = wu_ref[0, 0].astype(BF16)
        for b in range(bsz):
            x = xs_ref[b, 0, :, 0:dm]
            hg = _dot(x, wg)
            hu = _dot(x, wu)
            hdn_scr[j, b * cap:(b + 1) * cap, :] = ((hg * jax.nn.sigmoid(hg)) * hu).astype(BF16)

    @pl.when(j >= nf)
    def _():
        wd = wd_ref[0, 0].astype(BF16)
        cp = out_ref.shape[2]
        tn = out_ref.shape[3]
        lane = lax.broadcasted_iota(jnp.int32, (cap, GATE_LANES), 1)
        mine = ((lane % ne) == e) & (lane < 3 * ne)
        for b in range(bsz):
            rows = slice(b * cap, (b + 1) * cap)
            y = _dot(hdn_scr[0, rows, :], wd[0:tf])
            for f in range(1, nf):
                y = y + _dot(hdn_scr[f, rows, :], wd[f * tf:(f + 1) * tf])
            pieces = xs_ref[b, 0, :, dm:dm + GATE_LANES].astype(F32)
            gate = jnp.sum(jnp.where(mine, pieces, 0.0), axis=1, keepdims=True)
            out_ref[b, 0, 0:cap, :] = (y * gate).astype(BF16)
            out_ref[b, 0, cap:cp, :] = jnp.zeros((cp - cap, tn), BF16)


def _ffn(xs, wg, wu, wd, l, cap, tf=1024):
    bsz, ne, cp, dx = xs.shape
    dm = wg.shape[2]
    ff = wg.shape[-1]
    nf = ff // tf
    tn = dm // nf
    return pl.pallas_call(
        functools.partial(_ffn_kernel, cap=cap, nf=nf, ne=ne), grid=(ne, 2 * nf),
        in_specs=[pl.BlockSpec((bsz, 1, cap, dx), lambda e, j: (0, e, 0, 0)),
                  pl.BlockSpec((1, 1, dm, tf), lambda e, j: (l, e, 0, jnp.minimum(j, nf - 1))),
                  pl.BlockSpec((1, 1, dm, tf), lambda e, j: (l, e, 0, jnp.minimum(j, nf - 1))),
                  pl.BlockSpec((1, 1, ff, tn), lambda e, j: (l, e, 0, jnp.maximum(j - nf, 0)))],
        out_specs=pl.BlockSpec((bsz, 1, cp, tn), lambda e, j: (0, e, 0, jnp.maximum(j - nf, 0))),
        out_shape=jax.ShapeDtypeStruct((bsz, ne, cp, dm), BF16),
        scratch_shapes=[pltpu.VMEM((nf, bsz * cap, tf), BF16)],
        compiler_params=_cparams(("parallel", "arbitrary")),
    )(xs, wg, wu, wd)


def _combine_kernel(ws_ref, we_ref, out_ref, slott_ref, h1_ref, lng_ref, lnb_ref, h2_ref, rhs_scr, *, nb, ne, cc):
    b = pl.program_id(0)
    i = pl.program_id(1)
    blk = ATT_BLOCK

    def finish(k, tot):
        rows = slice(k * blk, (k + 1) * blk)
        h2_ref[rows, :] = _layer_norm(DEEPNORM_ALPHA * h1_ref[rows, :] + tot, lng_ref[0], lnb_ref[0])

    def gather_packed(k, base):
        kk = ne * SMALL_WIN
        slot_t = slott_ref[k, 0].astype(F32).astype(BF16)
        expand = (lax.broadcasted_iota(jnp.int32, (ne, kk), 1) // SMALL_WIN
                  == lax.broadcasted_iota(jnp.int32, (ne, kk), 0)).astype(BF16)
        spread = _dot(slot_t, expand)
        row = (lax.broadcasted_iota(jnp.int32, (blk, kk), 1) % SMALL_WIN).astype(F32)
        onehot = (spread == row).astype(BF16)
        for e in range(ne):
            w = pl.multiple_of(ws_ref[base + e], SLOT_ALIGN)
            rhs_scr[e * SMALL_WIN:(e + 1) * SMALL_WIN, :] = out_ref[0, e, pl.ds(w, SMALL_WIN), :]
        finish(k, _dot(onehot, rhs_scr[...]))

    def gather_per_expert(k, base):
        slot_t = slott_ref[k, 0]
        liota = lax.broadcasted_iota(jnp.int32, (blk, WIN), 1)
        tot = None
        for e in range(ne):
            w = pl.multiple_of(ws_ref[base + e], SLOT_ALIGN)
            onehot = (liota == slot_t[:, e:e + 1]).astype(BF16)
            y = _dot(onehot, out_ref[0, e, pl.ds(w, WIN), :])
            tot = y if tot is None else tot + y
        finish(k, tot)

    for k in range(cc):
        base = ((i * cc + k) * nb + b) * ne
        small = _max_extent(we_ref, base, ne) <= SMALL_WIN
        pl.when(small)(functools.partial(gather_packed, k, base))
        pl.when(jnp.logical_not(small))(functools.partial(gather_per_expert, k, base))


def _combine(out, slot, wstart, wextent, h1, lng3, lnb3, l, seq):
    bsz, ne, cp, dm = out.shape
    nc = seq // ATT_BLOCK
    blk = ATT_BLOCK
    t = h1.shape[0]
    slot_t = jnp.swapaxes(slot, 2, 3)
    cc = math.gcd(COMBINE_STEP_CHUNKS, nc)
    ns = nc // cc
    vec = pl.BlockSpec((1, 1, dm), lambda b, i, ws, we: (l, 0, 0))
    return pl.pallas_call(
        functools.partial(_combine_kernel, nb=bsz, ne=ne, cc=cc),
        grid_spec=pltpu.PrefetchScalarGridSpec(
            num_scalar_prefetch=2, grid=(bsz, ns),
            in_specs=[pl.BlockSpec((1, ne, cp, dm), lambda b, i, ws, we: (b, 0, 0, 0)),
                      pl.BlockSpec((cc, 1, blk, ne), lambda b, i, ws, we: (i, b, 0, 0)),
                      pl.BlockSpec((cc * blk, dm), lambda b, i, ws, we: (b * ns + i, 0)),
                      vec, vec],
            out_specs=pl.BlockSpec((cc * blk, dm), lambda b, i, ws, we: (b * ns + i, 0)),
            scratch_shapes=[pltpu.VMEM((ne * SMALL_WIN, dm), BF16)]),
        out_shape=jax.ShapeDtypeStruct((t, dm), F32),
        compiler_params=_cparams(("parallel", "arbitrary")),
    )(wstart, wextent, out, slot_t, h1, lng3, lnb3)


def kernel(x, ln_in_g, ln_in_b, w_in, ret_theta, ssm_lambda_re, ssm_lambda_im, ssm_log_step, ssm_b_re, ssm_b_im,
           ssm_c_re, ssm_c_im, ssm_d, ssm_w_glu, ssm_b_glu, attn_sink, w_out, ln1_g, ln1_b, router_w,
           exp_w_gate, exp_w_up, exp_w_down, ln2_g, ln2_b):
    bsz, seq, dm = x.shape
    depth = w_in.shape[0]
    ne = router_w.shape[-1]
    cap = EC_FACTOR * seq // ne
    t = bsz * seq
    w_in_bf = w_in.astype(BF16)
    wt_bf = jnp.swapaxes(jnp.concatenate([w_in_bf[:, :, COL_SU:COL_SU + SSM_WIDTH],
                                          w_in_bf[:, :, COL_RK:COL_RK + RET_WIDTH]], axis=2), 1, 2)
    w_out_bf = w_out.astype(BF16)
    w_glu_bf = ssm_w_glu.astype(BF16)
    lg = jax.nn.log_sigmoid(ret_theta.astype(F32))
    sink = attn_sink.astype(F32)
    ssm_par = _ssm_params(ssm_lambda_re, ssm_lambda_im, ssm_log_step, ssm_b_re, ssm_b_im, ssm_c_re, ssm_c_im)
    vec3 = lambda a: a.astype(F32).reshape(depth, 1, -1)
    d3, bglu3, ln1g3, ln1b3, ln2g3, ln2b3 = map(vec3, (ssm_d, ssm_b_glu, ln1_g, ln1_b, ln2_g, ln2_b))
    rwt = jnp.swapaxes(router_w.astype(F32), 1, 2)
    rwt_hi = rwt.astype(BF16)
    rwt = jnp.concatenate([rwt_hi, (rwt - rwt_hi.astype(F32)).astype(BF16)], axis=1)
    bias = _attention_bias(seq)

    h = x
    for l in range(depth):
        if l == 0:
            h, proj, ut, kt, rv = _inproj(h, w_in_bf, wt_bf, l, ln=(ln_in_g, ln_in_b))
        else:
            proj, ut, kt, rv = _inproj(h, w_in_bf, wt_bf, l)
        h2d = h.reshape(t, dm)
        proj = proj.reshape(t, D_IN)
        yt = _ssm_conv(ut, ssm_par, l, bsz)
        sf, sb = _ret_states(kt, rv, lg, l, bsz, seq)
        h1, h1b, afft = _mixer(proj, sf, sb, yt, h2d, bias, lg, sink, w_out_bf, w_glu_bf, d3, bglu3, ln1g3, ln1b3,
                               rwt, l, bsz, seq)
        slot, wstart, wextent = _route(afft, cap)
        xs = _dispatch(h1b, afft, slot, wstart, wextent, cap)
        out = _ffn(xs, exp_w_gate, exp_w_up, exp_w_down, l, cap)
        h = _combine(out, slot, wstart, wextent, h1, ln2g3, ln2b3, l, seq).reshape(bsz, seq, dm)
    return h
```
